```python
import math
import jax
import jax.numpy as jnp
from jax import lax
import numpy as np

D_MODEL = 1024
BATCH = 4
SEQ = 4096
DEPTH = 2

GRID_W = 64
CTX_LEN = 256
EPS = 1e-6

ATT_HEADS = 6
ATT_KV_HEADS = 2
HEAD_DIM = 64
ATT_W = ATT_HEADS * HEAD_DIM
KV_W = ATT_KV_HEADS * HEAD_DIM
ROPE_AXIS_DIM = HEAD_DIM // 2
ROPE_THETA = 10000.0
Q_BLOCK = 128

SSD_HEADS = 6
SSD_HEAD_DIM = 64
SSD_W = SSD_HEADS * SSD_HEAD_DIM
SSD_GROUPS = 2
SSD_STATE = 64
SSD_CONV = 3
SSD_CHUNK = 128
SSD_CONV_CH = SSD_W + 2 * SSD_GROUPS * SSD_STATE

HY_W = 256
HY_ORDER = 2
HY_SHORT = 3
HY_BANDS = 16
HY_POS_DIM = 1 + 2 * HY_BANDS
HY_FILTER_HID = 64
HY_FAST_DECAY = 0.3
HY_SLOW_DECAY = 1.5
HY_TARGET = 1e-2

MIX_W = ATT_W + SSD_W + HY_W
D_IN = ATT_W + 2 * KV_W + SSD_W + SSD_CONV_CH + 2 * SSD_HEADS + (HY_ORDER + 1) * HY_W

N_EXPERTS = 16
N_GROUPS = 4
EXPERTS_PER_GROUP = N_EXPERTS // N_GROUPS
GROUP_SCORE_TOPK = 2
TOP_K = 2
D_FF = 256

kernel_name = 'hybrid_ssd_hyena_gqa_moe_prefix_dit'


def rmsnorm(x, g):
    xf = x.astype(jnp.float32)
    y = xf * lax.rsqrt(jnp.mean(xf * xf, axis=-1, keepdims=True) + EPS)
    return (y * g.astype(jnp.float32)).astype(x.dtype)


def modulate(h, shift, scale):
    return h * (1 + scale[:, None, :]) + shift[:, None, :]


def adaln(cvec, w_mod, b_mod):
    m = jax.nn.silu(cvec) @ w_mod + b_mod
    return jnp.split(m, 6, axis=-1)


def split_proj(u):
    cuts = np.cumsum([ATT_W, KV_W, KV_W, SSD_W, SSD_CONV_CH, 2 * SSD_HEADS])
    return jnp.split(u, [int(i) for i in cuts], axis=-1)


def rev(t):
    return jnp.flip(t, axis=1)


def dwconv_centred(u, w, b):
    k = w.shape[0]
    out = lax.conv_general_dilated(u, w[:, None, :].astype(u.dtype), (1,), [(k // 2, k // 2)],
                                   dimension_numbers=('NWC', 'WIO', 'NWC'),
                                   feature_group_count=u.shape[-1])
    return out + b.astype(u.dtype)


def axial_rope(rows):
    row = jnp.broadcast_to(jnp.arange(rows)[:, None], (rows, GRID_W)).reshape(-1).astype(jnp.float32)
    col = jnp.broadcast_to(jnp.arange(GRID_W)[None, :], (rows, GRID_W)).reshape(-1).astype(jnp.float32)
    inv = ROPE_THETA ** (-jnp.arange(0, ROPE_AXIS_DIM, 2, dtype=jnp.float32) / ROPE_AXIS_DIM)
    ang = jnp.concatenate([row[:, None] * inv, col[:, None] * inv], axis=-1)
    return jnp.cos(ang), jnp.sin(ang)


def apply_rope(x, cos, sin):
    half = HEAD_DIM // 2
    xf = x.astype(jnp.float32)
    x1, x2 = xf[..., :half], xf[..., half:]
    cs, sn = cos[None, :, None, :], sin[None, :, None, :]
    return jnp.concatenate([x1 * cs - x2 * sn, x2 * cs + x1 * sn], axis=-1).astype(x.dtype)


def heads(t, n_heads):
    return t.reshape(t.shape[0], t.shape[1], n_heads, HEAD_DIM)


def gqa_blocks(q, k, v):
    bsz, n = q.shape[:2]
    rep = ATT_HEADS // ATT_KV_HEADS
    qb = q.reshape(bsz, n // Q_BLOCK, Q_BLOCK, ATT_KV_HEADS, rep, HEAD_DIM).swapaxes(0, 1)
    scale = HEAD_DIM ** -0.5

    def one_block(qi):
        s = jnp.einsum('bqgrd,bkgd->bgrqk', qi, k).astype(jnp.float32) * scale
        p = jax.nn.softmax(s, axis=-1).astype(v.dtype)
        return jnp.einsum('bgrqk,bkgd->bqgrd', p, v)

    o = lax.map(one_block, qb)
    return o.swapaxes(0, 1).reshape(bsz, n, ATT_W)


def ssd_chunked(x, dt, a, bmat, cmat, h0):
    bsz, L = x.shape[:2]
    nc = L // SSD_CHUNK
    rep = SSD_HEADS // SSD_GROUPS
    f = jnp.float32
    xc = x.astype(f).reshape(bsz, nc, SSD_CHUNK, SSD_HEADS, SSD_HEAD_DIM)
    dtc = dt.astype(f).reshape(bsz, nc, SSD_CHUNK, SSD_HEADS)
    bc = jnp.repeat(bmat.astype(f), rep, axis=2).reshape(bsz, nc, SSD_CHUNK, SSD_HEADS, SSD_STATE)
    cc = jnp.repeat(cmat.astype(f), rep, axis=2).reshape(bsz, nc, SSD_CHUNK, SSD_HEADS, SSD_STATE)
    acum = jnp.cumsum(dtc * a, axis=2)
    seg = acum[:, :, :, None, :] - acum[:, :, None, :, :]
    earlier = jnp.tril(jnp.ones((SSD_CHUNK, SSD_CHUNK), dtype=bool))[None, None, :, :, None]
    decay = jnp.exp(jnp.where(earlier, seg, -jnp.inf))
    cb = jnp.einsum('bcthn,bcshn->bctsh', cc, bc)
    y_in = jnp.einsum('bctsh,bcsh,bcshp->bcthp', cb * decay, dtc, xc)
    w_end = jnp.exp(acum[:, :, -1:, :] - acum) * dtc
    states = jnp.einsum('bcshn,bcsh,bcshp->bchpn', bc, w_end, xc)
    chunk_decay = jnp.exp(acum[:, :, -1, :])

    def step(hprev, inp):
        st, dec = inp
        return hprev * dec[:, :, None, None] + st, hprev

    h_last, h_enter = lax.scan(step, h0.astype(f),
                               (jnp.moveaxis(states, 1, 0), jnp.moveaxis(chunk_decay, 1, 0)))
    h_enter = jnp.moveaxis(h_enter, 0, 1)
    y_out = jnp.einsum('bcthn,bchpn,bcth->bcthp', cc, h_enter, jnp.exp(acum))
    return (y_in + y_out).reshape(bsz, L, SSD_HEADS, SSD_HEAD_DIM), h_last


def ssd_last_state(x, dt, a, bmat):
    rep = SSD_HEADS // SSD_GROUPS
    f = jnp.float32
    dtf = dt.astype(f)
    acum = jnp.cumsum(dtf * a, axis=1)
    w = jnp.exp(acum[:, -1:, :] - acum) * dtf
    bh = jnp.repeat(bmat.astype(f), rep, axis=2)
    return jnp.einsum('blhn,blh,blhp->bhpn', bh, w, x.astype(f))


def bidir_ssd(xs, dt, a, bm, cm, h0_f, h0_b):
    y_f, h_f = ssd_chunked(xs, dt[:, :, 0], a[0], bm, cm, h0_f)
    y_b, h_b = ssd_chunked(rev(xs), rev(dt[:, :, 1]), a[1], rev(bm), rev(cm), h0_b)
    return y_f + rev(y_b), h_f, h_b


def ssd_prep(xbc, dt_raw, conv_w, conv_b, dt_bias):
    bsz, L = xbc.shape[:2]
    xbc = jax.nn.silu(dwconv_centred(xbc, conv_w, conv_b))
    xs, bm, cm = jnp.split(xbc, [SSD_W, SSD_W + SSD_GROUPS * SSD_STATE], axis=-1)
    dt = jax.nn.softplus(dt_raw.astype(jnp.float32).reshape(bsz, L, 2, SSD_HEADS) + dt_bias.astype(jnp.float32))
    return (xs.reshape(bsz, L, SSD_HEADS, SSD_HEAD_DIM),
            bm.reshape(bsz, L, SSD_GROUPS, SSD_STATE),
            cm.reshape(bsz, L, SSD_GROUPS, SSD_STATE), dt)


def ssd_out(y, xs, z, d_skip, norm_w):
    bsz, L = y.shape[:2]
    y = y + xs.astype(jnp.float32) * d_skip.astype(jnp.float32)[:, None]
    g = y.reshape(bsz, L, SSD_W) * jax.nn.silu(z.astype(jnp.float32))
    g = rmsnorm(g.reshape(bsz, L, SSD_GROUPS, SSD_W // SSD_GROUPS), norm_w.reshape(SSD_GROUPS, -1))
    return g.reshape(bsz, L, SSD_W).astype(z.dtype)


def hyena_filters(L, w1, b1, freq, w2, b2, w3):
    f = jnp.float32
    n = jnp.arange(L, dtype=f)
    t = n / max(L - 1, 1)
    bands = jnp.linspace(1e-4, HY_BANDS - 1, HY_BANDS, dtype=f)
    wpos = (2 * math.pi / L) * n
    feats = jnp.concatenate([t[:, None], jnp.cos(wpos[:, None] * bands), -jnp.sin(wpos[:, None] * bands)], axis=-1)
    fr = freq.astype(f)
    hid = jnp.sin(fr * (feats @ w1.astype(f) + b1.astype(f)))
    hid = jnp.sin(fr * (hid @ w2.astype(f) + b2.astype(f)))
    h = (hid @ w3.astype(f)).reshape(L, 2, HY_ORDER, HY_W)
    deltas = jnp.abs(jnp.linspace(math.log(HY_TARGET) / HY_SLOW_DECAY, math.log(HY_TARGET) / HY_FAST_DECAY, HY_W, dtype=f))
    window = jnp.exp(-t[:, None] * deltas)
    return h * window[:, None, None, :]


def bidir_long_conv(z, hf, hb, bias):
    L = z.shape[1]
    kern = jnp.concatenate([hf, jnp.zeros_like(hf[:1]), hb[:0:-1]], axis=0)
    zf = z.astype(jnp.float32)
    y = jnp.fft.irfft(jnp.fft.rfft(zf, n=2 * L, axis=1) * jnp.fft.rfft(kern, axis=0)[None], n=2 * L, axis=1)[:, :L]
    return y + zf * bias.astype(jnp.float32)


def hyena_mixer(u, conv_w, conv_b, w1, b1, freq, w2, b2, w3, bias):
    u = dwconv_centred(u, conv_w, conv_b)
    v, x1, x2 = jnp.split(u, 3, axis=-1)
    h = hyena_filters(u.shape[1], w1, b1, freq, w2, b2, w3)
    zz = x1.astype(jnp.float32) * bidir_long_conv(v, h[:, 0, 0], h[:, 1, 0], bias[0])
    zz = x2.astype(jnp.float32) * bidir_long_conv(zz, h[:, 0, 1], h[:, 1, 1], bias[1])
    return zz.astype(u.dtype)


def grouped_moe(h, w_router, router_bias, w_gate, w_up, w_down):
    bsz, n, d = h.shape
    t = h.reshape(bsz * n, d)
    scores = jax.nn.sigmoid((t @ w_router).astype(jnp.float32))
    sel = scores + router_bias.astype(jnp.float32)
    grp = lax.top_k(sel.reshape(-1, N_GROUPS, EXPERTS_PER_GROUP), GROUP_SCORE_TOPK)[0].sum(-1)
    best = jnp.argmax(grp, axis=-1)
    in_grp = (jnp.arange(N_EXPERTS) // EXPERTS_PER_GROUP)[None, :] == best[:, None]
    _, idx = lax.top_k(jnp.where(in_grp, sel, -jnp.inf), TOP_K)
    wts = jnp.take_along_axis(scores, idx, axis=-1)
    wts = wts / jnp.sum(wts, axis=-1, keepdims=True)
    combine = jnp.sum(jax.nn.one_hot(idx, N_EXPERTS, dtype=jnp.float32) * wts[..., None], axis=1).astype(t.dtype)
    out = jnp.zeros_like(t)
    for e in range(N_EXPERTS):
        y = (jax.nn.silu(t @ w_gate[e]) * (t @ w_up[e])) @ w_down[e]
        out = out + combine[:, e:e + 1] * y
    return out.reshape(bsz, n, d)


def setup_inputs(seed: int = 0) -> dict:
    key = jax.random.key(seed)
    ks = iter(jax.random.split(key, 48))

    def nrm(shape, s):
        return jax.random.normal(next(ks), shape, jnp.float32) * s

    def gain(shape):
        return 1.0 + nrm(shape, 0.01)

    dt0 = jnp.exp(jax.random.uniform(next(ks), (DEPTH, 2, SSD_HEADS), jnp.float32, math.log(1e-3), math.log(1e-1)))
    return {
        'x': nrm((BATCH, SEQ, D_MODEL), 1.0),
        'c': nrm((BATCH, D_MODEL), 1.0),
        'ctx': nrm((BATCH, CTX_LEN, D_MODEL), 1.0),
        'c_ctx': nrm((D_MODEL,), 1.0),
        'w_mod': nrm((DEPTH, D_MODEL, 6 * D_MODEL), 0.5 * D_MODEL ** -0.5),
        'b_mod': nrm((DEPTH, 6 * D_MODEL), 0.01),
        'g_mix': gain((DEPTH, D_MODEL)),
        'g_ffn': gain((DEPTH, D_MODEL)),
        'w_in': nrm((DEPTH, D_MODEL, D_IN), D_MODEL ** -0.5),
        'q_norm': gain((DEPTH, HEAD_DIM)),
        'k_norm': gain((DEPTH, HEAD_DIM)),
        'ssd_conv_w': nrm((DEPTH, SSD_CONV, SSD_CONV_CH), SSD_CONV ** -0.5),
        'ssd_conv_b': nrm((DEPTH, SSD_CONV_CH), 0.01),
        'ssd_dt_bias': dt0 + jnp.log(-jnp.expm1(-dt0)),
        'ssd_a_log': jnp.log(jax.random.uniform(next(ks), (DEPTH, 2, SSD_HEADS), jnp.float32, 1.0, 16.0)),
        'ssd_d': gain((DEPTH, SSD_HEADS)),
        'ssd_norm': gain((DEPTH, SSD_W)),
        'hy_conv_w': nrm((DEPTH, HY_SHORT, (HY_ORDER + 1) * HY_W), HY_SHORT ** -0.5),
        'hy_conv_b': nrm((DEPTH, (HY_ORDER + 1) * HY_W), 0.01),
        'hy_w1': nrm((DEPTH, HY_POS_DIM, HY_FILTER_HID), HY_POS_DIM ** -0.5),
        'hy_b1': nrm((DEPTH, HY_FILTER_HID), 0.1),
        'hy_freq': gain((DEPTH, HY_FILTER_HID)),
        'hy_w2': nrm((DEPTH, HY_FILTER_HID, HY_FILTER_HID), HY_FILTER_HID ** -0.5),
        'hy_b2': nrm((DEPTH, HY_FILTER_HID), 0.1),
        'hy_w3': nrm((DEPTH, HY_FILTER_HID, 2 * HY_ORDER * HY_W), 0.05 * HY_FILTER_HID ** -0.5),
        'hy_bias': nrm((DEPTH, HY_ORDER, HY_W), 1.0),
        'w_out': nrm((DEPTH, MIX_W, D_MODEL), MIX_W ** -0.5),
        'w_router': nrm((D_MODEL, N_EXPERTS), D_MODEL ** -0.5),
        'router_bias': nrm((N_EXPERTS,), 0.01),
        'w_gate': nrm((DEPTH, N_EXPERTS, D_MODEL, D_FF), D_MODEL ** -0.5),
        'w_up': nrm((DEPTH, N_EXPERTS, D_MODEL, D_FF), D_MODEL ** -0.5),
        'w_down': nrm((DEPTH, N_EXPERTS, D_FF, D_MODEL), D_FF ** -0.5),
        'g_final': gain((D_MODEL,)),
    }


def reference(x, c, ctx, c_ctx, w_mod, b_mod, g_mix, g_ffn, w_in, q_norm, k_norm,
              ssd_conv_w, ssd_conv_b, ssd_dt_bias, ssd_a_log, ssd_d, ssd_norm,
              hy_conv_w, hy_conv_b, hy_w1, hy_b1, hy_freq, hy_w2, hy_b2, hy_w3, hy_bias,
              w_out, w_router, router_bias, w_gate, w_up, w_down, g_final):
    bsz, n_lat = x.shape[0], x.shape[1]
    rows = n_lat // GRID_W
    cos, sin = axial_rope(rows)
    xl, xc = x, ctx
    for i in range(DEPTH):
        need_ctx = i < DEPTH - 1
        sh1, sc1, gt1, sh2, sc2, gt2 = adaln(c, w_mod[i], b_mod[i])
        csh1, csc1, cgt1, csh2, csc2, cgt2 = adaln(c_ctx[None], w_mod[i], b_mod[i])

        hl = modulate(rmsnorm(xl, g_mix[i]), sh1, sc1)
        hc = modulate(rmsnorm(xc, g_mix[i]), csh1, csc1)
        q_l, k_l, v_l, z_l, xbc_l, dt_l, hy_l = split_proj(hl @ w_in[i])
        q_c, k_c, v_c, z_c, xbc_c, dt_c, hy_c = split_proj(hc @ w_in[i])

        ql = apply_rope(rmsnorm(heads(q_l, ATT_HEADS), q_norm[i]), cos, sin)
        kl = apply_rope(rmsnorm(heads(k_l, ATT_KV_HEADS), k_norm[i]), cos, sin)
        kc = rmsnorm(heads(k_c, ATT_KV_HEADS), k_norm[i])
        vl, vc = heads(v_l, ATT_KV_HEADS), heads(v_c, ATT_KV_HEADS)
        att_l = gqa_blocks(ql, jnp.concatenate([kc, kl], axis=1), jnp.concatenate([vc, vl], axis=1))

        a = -jnp.exp(ssd_a_log[i].astype(jnp.float32))
        xs_c, b_c, c_c, dtc = ssd_prep(xbc_c, dt_c, ssd_conv_w[i], ssd_conv_b[i], ssd_dt_bias[i])
        xs_l, b_l, c_l, dtl = ssd_prep(xbc_l, dt_l, ssd_conv_w[i], ssd_conv_b[i], ssd_dt_bias[i])
        if need_ctx:
            zeros = jnp.zeros((bsz, SSD_HEADS, SSD_HEAD_DIM, SSD_STATE), jnp.float32)
            yc, hc_f, hc_b = bidir_ssd(xs_c, dtc, a, b_c, c_c, zeros, zeros)
        else:
            hc_f = ssd_last_state(xs_c, dtc[:, :, 0], a[0], b_c)
            hc_b = ssd_last_state(rev(xs_c), rev(dtc[:, :, 1]), a[1], rev(b_c))
        yl, _, _ = bidir_ssd(xs_l, dtl, a, b_l, c_l, hc_f, hc_b)
        ssd_l = ssd_out(yl, xs_l, z_l, ssd_d[i], ssd_norm[i])

        hyn_l = hyena_mixer(hy_l, hy_conv_w[i], hy_conv_b[i], hy_w1[i], hy_b1[i], hy_freq[i],
                            hy_w2[i], hy_b2[i], hy_w3[i], hy_bias[i])

        mix_l = jnp.concatenate([att_l, ssd_l, hyn_l], axis=-1) @ w_out[i]
        xl = xl + gt1[:, None, :] * mix_l
        if need_ctx:
            qc = rmsnorm(heads(q_c, ATT_HEADS), q_norm[i])
            att_c = gqa_blocks(qc, kc, vc)
            ssd_c = ssd_out(yc, xs_c, z_c, ssd_d[i], ssd_norm[i])
            hyn_c = hyena_mixer(hy_c, hy_conv_w[i], hy_conv_b[i], hy_w1[i], hy_b1[i], hy_freq[i],
                                hy_w2[i], hy_b2[i], hy_w3[i], hy_bias[i])
            mix_c = jnp.concatenate([att_c, ssd_c, hyn_c], axis=-1) @ w_out[i]
            xc = xc + cgt1[:, None, :] * mix_c

        hl2 = modulate(rmsnorm(xl, g_ffn[i]), sh2, sc2)
        if need_ctx:
            hc2 = modulate(rmsnorm(xc, g_ffn[i]), csh2, csc2)
            n_ctx = hc2.shape[1]
            hc2 = jnp.broadcast_to(hc2, (bsz,) + hc2.shape[1:])
            moe_all = grouped_moe(jnp.concatenate([hc2, hl2], axis=1), w_router, router_bias,
                                  w_gate[i], w_up[i], w_down[i])
            xc = xc + cgt2[:, None, :] * moe_all[:, :n_ctx]
            xl = xl + gt2[:, None, :] * moe_all[:, n_ctx:]
        else:
            xl = xl + gt2[:, None, :] * grouped_moe(hl2, w_router, router_bias, w_gate[i], w_up[i], w_down[i])
    return rmsnorm(xl, g_final)
```

```python
import functools
import math

import jax
import jax.numpy as jnp
import numpy as np
from jax import lax
from jax.experimental import pallas as pl
from jax.experimental.pallas import tpu as pltpu

F32 = jnp.float32
BF16 = jnp.bfloat16

D_MODEL = 1024
GRID_W = 64
EPS = 1e-6

ATT_HEADS = 6
ATT_KV_HEADS = 2
HEAD_DIM = 64
ATT_W = ATT_HEADS * HEAD_DIM
KV_W = ATT_KV_HEADS * HEAD_DIM
ROPE_AXIS_DIM = HEAD_DIM // 2
ROPE_THETA = 10000.0

SSD_HEADS = 6
SSD_HEAD_DIM = 64
SSD_W = SSD_HEADS * SSD_HEAD_DIM
SSD_GROUPS = 2
SSD_STATE = 64
SSD_CHUNK = 128
SSD_CONV_CH = SSD_W + 2 * SSD_GROUPS * SSD_STATE
SSD_HEADS_PER_GROUP = SSD_HEADS // SSD_GROUPS

HY_W = 256
HY_ORDER = 2
HY_BANDS = 16
HY_POS_DIM = 1 + 2 * HY_BANDS
HY_FILTER_HID = 64
HY_FAST_DECAY = 0.3
HY_SLOW_DECAY = 1.5
HY_TARGET = 1e-2
HY_N2 = 64

MIX_W = ATT_W + SSD_W + HY_W
N_EXPERTS = 16
N_GROUPS = 4
EXPERTS_PER_GROUP = N_EXPERTS // N_GROUPS
D_FF = 256

LANES = 128
TOKEN_TILE = 256
MOD_ROWS = 8
VMEM_LIMIT = 56 * 1024 * 1024

COL_Q = 0
COL_K = COL_Q + ATT_W
COL_V = COL_K + KV_W
COL_Z = COL_V + KV_W
COL_XBC = COL_Z + SSD_W
COL_DT = COL_XBC + SSD_CONV_CH
COL_HY = COL_DT + LANES
COL_END = COL_HY + 3 * HY_W


def _params(*sem):
    return pltpu.CompilerParams(dimension_semantics=sem, vmem_limit_bytes=VMEM_LIMIT)


def _silu(x):
    return x * jax.nn.sigmoid(x)


def _softplus(x):
    return jnp.maximum(x, 0.0) + jnp.log1p(jnp.exp(-jnp.abs(x)))


def _split3(x):
    hi = x.astype(BF16)
    r1 = x - hi.astype(F32)
    mid = r1.astype(BF16)
    lo = (r1 - mid.astype(F32)).astype(BF16)
    return hi, mid, lo


def _dot(a, b):
    return jnp.dot(a, b, preferred_element_type=F32)


def _dot3_right(m_bf16, x):
    hi, mid, lo = _split3(x)
    return _dot(m_bf16, hi) + _dot(m_bf16, mid) + _dot(m_bf16, lo)


def _dot3_left(x, m_bf16):
    hi, mid, lo = _split3(x)
    return _dot(hi, m_bf16) + _dot(mid, m_bf16) + _dot(lo, m_bf16)


def _adaln_kernel(c_ref, w_ref, b_ref, o_ref):
    s = _silu(c_ref[...]).astype(BF16)
    o_ref[0] = _dot(s, w_ref[0].astype(BF16)) + b_ref[0]


def _adaln(cvec, w_mod, b_mod):
    depth, d, n = w_mod.shape
    bn = n // 4
    return pl.pallas_call(
        _adaln_kernel,
        grid=(depth, n // bn),
        in_specs=[
            pl.BlockSpec((MOD_ROWS, d), lambda i, j: (0, 0)),
            pl.BlockSpec((1, d, bn), lambda i, j: (i, 0, j)),
            pl.BlockSpec((1, 1, bn), lambda i, j: (i, 0, j)),
        ],
        out_specs=pl.BlockSpec((1, MOD_ROWS, bn), lambda i, j: (i, 0, j)),
        out_shape=jax.ShapeDtypeStruct((depth, MOD_ROWS, n), F32),
        compiler_params=_params("arbitrary", "arbitrary"),
        name="adaln",
    )(cvec, w_mod, b_mod.reshape(depth, 1, n))


def _head_rms_rope(x, bd, gain, cs, sn):
    width = x.shape[-1]
    ms = _dot((x * x).astype(BF16), bd) * (1.0 / HEAD_DIM)
    xn = x * lax.rsqrt(ms + EPS) * gain
    lane = lax.broadcasted_iota(jnp.int32, xn.shape, 1)
    first_half = (lane % HEAD_DIM) < (HEAD_DIM // 2)
    partner = jnp.where(first_half,
                        pltpu.roll(xn, width - HEAD_DIM // 2, 1),
                        pltpu.roll(xn, HEAD_DIM // 2, 1))
    return xn * cs + partner * sn


def _inproj_kernel(x_ref, sh_ref, sc_ref, g_ref, w_ref, gq_ref, gk_ref, cs_ref, sn_ref, bd_ref,
                   q_ref, kt_ref, v_ref, z_ref, xbc_ref, dt_ref, hy_ref):
    x = x_ref[0]
    ms = jnp.mean(x * x, axis=-1, keepdims=True)
    h = x * lax.rsqrt(ms + EPS) * g_ref[...]
    h = h * (1.0 + sc_ref[0]) + sh_ref[0]
    acc = _dot(h.astype(BF16), w_ref[...])

    cs = cs_ref[...]
    sn = sn_ref[...]
    bd = bd_ref[...]
    q = _head_rms_rope(acc[:, COL_Q:COL_K], bd, gq_ref[...], cs, sn)
    q_ref[0] = q.astype(q_ref.dtype)
    k = _head_rms_rope(acc[:, COL_K:COL_V], bd[:KV_W, :KV_W], gk_ref[...],
                       cs[:, :KV_W] * 8.0, sn[:, :KV_W] * 8.0)
    kt_ref[0] = k.T.astype(kt_ref.dtype)
    v_ref[0] = acc[:, COL_V:COL_Z].astype(v_ref.dtype)
    z_ref[0] = acc[:, COL_Z:COL_XBC]
    xbc_ref[0] = acc[:, COL_XBC:COL_DT]
    dt_ref[0] = acc[:, COL_DT:COL_HY]
    hy_ref[0] = acc[:, COL_HY:COL_END]


def _inproj(x, mods, g_mix, w_cat, gq, gk, cs, sn, bd, n_lat_tiles):
    bsz, t, d = x.shape
    tm = TOKEN_TILE
    nt = t // tm

    def mod_row(j, b):
        return jnp.where(j >= n_lat_tiles, bsz, b)

    tok = lambda w: pl.BlockSpec((1, tm, w), lambda j, b: (b, j, 0))
    const = lambda shape: pl.BlockSpec(shape, lambda j, b: tuple(0 for _ in shape))
    outs = pl.pallas_call(
        _inproj_kernel,
        grid=(nt, bsz),
        in_specs=[
            tok(d),
            pl.BlockSpec((1, 1, d), lambda j, b: (mod_row(j, b), 0, 0)),
            pl.BlockSpec((1, 1, d), lambda j, b: (mod_row(j, b), 0, 1)),
            const((1, d)),
            const((d, COL_END)),
            const((1, ATT_W)),
            const((1, KV_W)),
            pl.BlockSpec((tm, ATT_W), lambda j, b: (j, 0)),
            pl.BlockSpec((tm, ATT_W), lambda j, b: (j, 0)),
            const((ATT_W, ATT_W)),
        ],
        out_specs=[
            tok(ATT_W),
            pl.BlockSpec((1, KV_W, tm), lambda j, b: (b, 0, j)),
            tok(KV_W),
            tok(SSD_W),
            tok(SSD_CONV_CH),
            tok(LANES),
            tok(3 * HY_W),
        ],
        out_shape=[
            jax.ShapeDtypeStruct((bsz, t, ATT_W), BF16),
            jax.ShapeDtypeStruct((bsz, KV_W, t), BF16),
            jax.ShapeDtypeStruct((bsz, t, KV_W), BF16),
            jax.ShapeDtypeStruct((bsz, t, SSD_W), F32),
            jax.ShapeDtypeStruct((bsz, t, SSD_CONV_CH), F32),
            jax.ShapeDtypeStruct((bsz, t, LANES), F32),
            jax.ShapeDtypeStruct((bsz, t, 3 * HY_W), F32),
        ],
        compiler_params=_params("arbitrary", "arbitrary"),
        name="inproj",
    )(x, mods, mods, g_mix, w_cat, gq, gk, cs, sn, bd)
    return outs


def _attn_kernel(q_ref, kt_ref, v_ref, o_ref):
    q = q_ref[0]
    kt = kt_ref[0]
    v = v_ref[0]
    rep = ATT_HEADS // ATT_KV_HEADS
    outs = []
    for g in range(ATT_KV_HEADS):
        ktg = kt[g * HEAD_DIM:(g + 1) * HEAD_DIM, :]
        for r in range(rep):
            hd = g * rep + r
            s = _dot(q[:, hd * HEAD_DIM:(hd + 1) * HEAD_DIM], ktg)
            m = jnp.max(s, axis=-1, keepdims=True)
            p = jnp.exp(s - m)
            l = jnp.sum(p, axis=-1, keepdims=True)
            o = _dot(p.astype(BF16), v)
            outs.append(o[:, g * HEAD_DIM:(g + 1) * HEAD_DIM] / l)
    o_ref[0] = jnp.concatenate(outs, axis=-1).astype(o_ref.dtype)


def _attention(q, kt, v, q_tile0, n_q_tiles, k_tile0, n_k_tiles):
    bsz = q.shape[0]
    tq = TOKEN_TILE
    s = n_k_tiles * tq
    kblk = k_tile0 // n_k_tiles
    assert kblk * n_k_tiles == k_tile0
    return pl.pallas_call(
        _attn_kernel,
        grid=(bsz, n_q_tiles),
        in_specs=[
            pl.BlockSpec((1, tq, ATT_W), lambda b, j: (b, q_tile0 + j, 0)),
            pl.BlockSpec((1, KV_W, s), lambda b, j: (b, 0, kblk)),
            pl.BlockSpec((1, s, KV_W), lambda b, j: (b, kblk, 0)),
        ],
        out_specs=pl.BlockSpec((1, tq, ATT_W), lambda b, j: (b, j, 0)),
        out_shape=jax.ShapeDtypeStruct((bsz, n_q_tiles * tq, ATT_W), BF16),
        compiler_params=_params("arbitrary", "arbitrary"),
        name="attention",
    )(q, kt, v)


def _dwconv_kernel(u_ref, w_ref, b_ref, o_ref, *, n_lat, act):
    u = u_ref[0]
    t = u.shape[0]
    row = lax.broadcasted_iota(jnp.int32, u.shape, 0)
    prev = jnp.where((row == 0) | (row == n_lat), 0.0, pltpu.roll(u, 1, 0))
    nxt = jnp.where((row == n_lat - 1) | (row == t - 1), 0.0, pltpu.roll(u, t - 1, 0))
    w = w_ref[...]
    y = prev * w[0:1] + u * w[1:2] + nxt * w[2:3] + b_ref[...]
    if act:
        y = _silu(y)
    o_ref[0, 0] = y


def _dwconv(u, w, b, n_lat, act, parts):
    bsz, t, c = u.shape
    cpp = c // parts
    nb = cpp // LANES
    return pl.pallas_call(
        functools.partial(_dwconv_kernel, n_lat=n_lat, act=act),
        grid=(bsz, parts * nb),
        in_specs=[
            pl.BlockSpec((1, t, LANES), lambda b_, j: (b_, 0, j)),
            pl.BlockSpec((3, LANES), lambda b_, j: (0, j)),
            pl.BlockSpec((1, LANES), lambda b_, j: (0, j)),
        ],
        out_specs=pl.BlockSpec((1, 1, t, LANES), lambda b_, j: (j // nb, b_, 0, j % nb)),
        out_shape=jax.ShapeDtypeStruct((parts, bsz, t, cpp), F32),
        compiler_params=_params("arbitrary", "arbitrary"),
        name="dwconv",
    )(u, w, b.reshape(1, c))


def _ssd_kernel(xbc_ref, z_ref, dt_ref, dtb_ref, alog_ref, dsk_ref, nw_ref, bd_ref,
                o_ref, hf_ref, hb_ref, hbe_ref, *, n_lat_chunks):
    q = SSD_CHUNK
    nh = SSD_HEADS
    phase = pl.program_id(1)
    step = pl.program_id(2)
    n_chunks = n_lat_chunks + 2
    chunk = jnp.where(phase == 0, n_chunks - 1 - step,
                      jnp.where(step < 2, n_lat_chunks + step, step - 2))

    xbc = xbc_ref[0, 0]
    x = xbc[:, :SSD_W]
    bmat = xbc[:, SSD_W:SSD_W + SSD_GROUPS * SSD_STATE]
    cmat = xbc[:, SSD_W + SSD_GROUPS * SSD_STATE:]

    lane = lax.broadcasted_iota(jnp.int32, (1, LANES), 1)
    a_row = jnp.where(lane < 2 * nh, -jnp.exp(alog_ref[...]), 0.0)
    dt = _softplus(dt_ref[0] + dtb_ref[...])
    a = dt * a_row
    a_t = a.T
    dt_t = dt.T

    tt = lax.broadcasted_iota(jnp.int32, (q, q), 0)
    ss = lax.broadcasted_iota(jnp.int32, (q, q), 1)
    lower = (ss <= tt)
    upper = (ss >= tt)
    lmat = jnp.where(lower, 1.0, 0.0).astype(BF16)
    umat = jnp.where(upper, 1.0, 0.0).astype(BF16)

    bt = bmat.T

    @pl.when(step == 0)
    def _():
        hf_ref[...] = jnp.zeros_like(hf_ref)
        hb_ref[...] = jnp.zeros_like(hb_ref)

    @pl.when(phase == 0)
    def _():
        suf_t = _dot3_left(a_t, lmat)
        for hd in range(nh):
            g = hd // SSD_HEADS_PER_GROUP
            col = nh + hd
            row_b = suf_t[col:col + 1, :]
            total = row_b[:, 0:1]
            w_t = jnp.exp(total - row_b) * dt_t[col:col + 1, :]
            btg = bt[g * SSD_STATE:(g + 1) * SSD_STATE, :]
            xh = x[:, hd * SSD_HEAD_DIM:(hd + 1) * SSD_HEAD_DIM].astype(BF16)
            prev = hb_ref[hd]
            hbe_ref[chunk, hd] = prev
            hb_ref[hd] = prev * jnp.exp(total) + _dot((btg * w_t).astype(BF16), xh)

    @pl.when(phase == 1)
    def _():
        pre = _dot3_right(lmat, a)
        suf = _dot3_right(umat, a)
        pre_t = _dot3_left(a_t, umat)
        suf_t = _dot3_left(a_t, lmat)
        neg_inf = jnp.float32(-jnp.inf)
        ys = []
        for g in range(SSD_GROUPS):
            cg = cmat[:, g * SSD_STATE:(g + 1) * SSD_STATE].astype(BF16)
            btg = bt[g * SSD_STATE:(g + 1) * SSD_STATE, :]
            cb = _dot(cg, btg.astype(BF16))
            for r in range(SSD_HEADS_PER_GROUP):
                hd = g * SSD_HEADS_PER_GROUP + r
                colf = pre[:, hd:hd + 1]
                rowf = pre_t[hd:hd + 1, :]
                colb = suf[:, nh + hd:nh + hd + 1]
                rowb = suf_t[nh + hd:nh + hd + 1, :]
                wf = jnp.exp(jnp.where(lower, colf - rowf, neg_inf)) * dt_t[hd:hd + 1, :]
                wb = jnp.exp(jnp.where(upper, colb - rowb, neg_inf)) * dt_t[nh + hd:nh + hd + 1, :]
                xh = x[:, hd * SSD_HEAD_DIM:(hd + 1) * SSD_HEAD_DIM].astype(BF16)
                y = _dot((cb * (wf + wb)).astype(BF16), xh)
                hf = hf_ref[hd]
                y = y + _dot(cg, hf.astype(BF16)) * jnp.exp(colf)
                y = y + _dot(cg, hbe_ref[chunk, hd].astype(BF16)) * jnp.exp(colb)
                total = rowf[:, q - 1:q]
                w_t = jnp.exp(total - rowf) * dt_t[hd:hd + 1, :]
                hf_ref[hd] = hf * jnp.exp(total) + _dot((btg * w_t).astype(BF16), xh)
                ys.append(y)
        y = jnp.concatenate(ys, axis=-1) + x * dsk_ref[...]
        gz = y * _silu(z_ref[0])
        ms = _dot((gz * gz).astype(BF16), bd_ref[...]) * (1.0 / (SSD_W // SSD_GROUPS))
        o_ref[0] = (gz * lax.rsqrt(ms + EPS) * nw_ref[...]).astype(o_ref.dtype)


def _ssd(xbc, z, dt_raw, dt_bias, a_log, d_skip, norm_w, bd, n_lat_chunks):
    bsz, t, _ = z.shape
    q = SSD_CHUNK
    n_chunks = t // q
    assert n_chunks == n_lat_chunks + 2

    def chunk_of(p, s):
        return jnp.where(p == 0, n_chunks - 1 - s, jnp.where(s < 2, n_lat_chunks + s, s - 2))

    first_fwd = n_lat_chunks

    def out_chunk(p, s):
        return jnp.where(p == 0, first_fwd, chunk_of(p, s))

    const = lambda shape: pl.BlockSpec(shape, lambda b, p, s: tuple(0 for _ in shape))
    return pl.pallas_call(
        functools.partial(_ssd_kernel, n_lat_chunks=n_lat_chunks),
        grid=(bsz, 2, n_chunks),
        in_specs=[
            pl.BlockSpec((1, 1, q, SSD_CONV_CH), lambda b, p, s: (0, b, chunk_of(p, s), 0)),
            pl.BlockSpec((1, q, SSD_W), lambda b, p, s: (b, chunk_of(p, s), 0)),
            pl.BlockSpec((1, q, LANES), lambda b, p, s: (b, chunk_of(p, s), 0)),
            const((1, LANES)),
            const((1, LANES)),
            const((1, SSD_W)),
            const((1, SSD_W)),
            const((SSD_W, SSD_W)),
        ],
        out_specs=pl.BlockSpec((1, q, SSD_W), lambda b, p, s: (b, out_chunk(p, s), 0)),
        out_shape=jax.ShapeDtypeStruct((bsz, t, SSD_W), BF16),
        scratch_shapes=[
            pltpu.VMEM((SSD_HEADS, SSD_STATE, SSD_HEAD_DIM), F32),
            pltpu.VMEM((SSD_HEADS, SSD_STATE, SSD_HEAD_DIM), F32),
            pltpu.VMEM((n_chunks, SSD_HEADS, SSD_STATE, SSD_HEAD_DIM), F32),
        ],
        compiler_params=_params("arbitrary", "arbitrary", "arbitrary"),
        name="ssd",
    )(xbc, z, dt_raw, dt_bias, a_log, d_skip, norm_w, bd)


def _hyfilt_kernel(feat_ref, win_ref, w1_ref, b1_ref, fr_ref, w2_ref, b2_ref, w3_ref, o_ref):
    hp = lax.Precision.HIGHEST
    fr = fr_ref[...]
    h1 = jnp.sin(fr * (jnp.dot(feat_ref[...], w1_ref[...], precision=hp, preferred_element_type=F32) + b1_ref[...]))
    h2 = jnp.sin(fr * (jnp.dot(h1, w2_ref[...], precision=hp, preferred_element_type=F32) + b2_ref[...]))
    h = jnp.dot(h2, w3_ref[...], precision=hp, preferred_element_type=F32)
    win = win_ref[...]
    first_tile = pl.program_id(0) == 0
    row = lax.broadcasted_iota(jnp.int32, win.shape, 0)
    for order in range(HY_ORDER):
        for direction in range(2):
            c0 = (direction * HY_ORDER + order) * HY_W
            f = h[:, c0:c0 + HY_W] * win
            if direction == 1:
                f = jnp.where(first_tile & (row == 0), 0.0, f)
            o_ref[order * 2 + direction] = f


def _hy_positional(length):
    n = np.arange(length, dtype=np.float64)
    t = n / max(length - 1, 1)
    bands = np.linspace(1e-4, HY_BANDS - 1, HY_BANDS)
    wpos = (2 * math.pi / length) * n
    feats = np.concatenate([t[:, None], np.cos(wpos[:, None] * bands), -np.sin(wpos[:, None] * bands)], axis=-1)
    feats = np.pad(feats, ((0, 0), (0, LANES - HY_POS_DIM)))
    deltas = np.abs(np.linspace(math.log(HY_TARGET) / HY_SLOW_DECAY, math.log(HY_TARGET) / HY_FAST_DECAY, HY_W))
    window = np.exp(-t[:, None] * deltas)
    return jnp.asarray(feats, F32), jnp.asarray(window, F32)


def _hyena_filters(length, w1, b1, freq, w2, b2, w3):
    feats, window = _hy_positional(length)
    tl = min(length, 512)
    hid = HY_FILTER_HID
    w1p = jnp.pad(w1, ((0, LANES - HY_POS_DIM), (0, 0)))
    const = lambda shape: pl.BlockSpec(shape, lambda i: tuple(0 for _ in shape))
    return pl.pallas_call(
        _hyfilt_kernel,
        grid=(length // tl,),
        in_specs=[
            pl.BlockSpec((tl, LANES), lambda i: (i, 0)),
            pl.BlockSpec((tl, HY_W), lambda i: (i, 0)),
            const((LANES, hid)), const((1, hid)), const((1, hid)),
            const((hid, hid)), const((1, hid)), const((hid, 2 * HY_ORDER * HY_W)),
        ],
        out_specs=pl.BlockSpec((2 * HY_ORDER, tl, HY_W), lambda i: (0, i, 0)),
        out_shape=jax.ShapeDtypeStruct((2 * HY_ORDER, length, HY_W), F32),
        compiler_params=_params("arbitrary"),
        name="hyena_filters",
    )(feats, window, w1p, b1.reshape(1, hid), freq.reshape(1, hid), w2, b2.reshape(1, hid), w3)


def _dft_tables(length):
    n = 2 * length
    n2 = HY_N2
    n1 = n // n2
    half = n1 // 2
    k1 = np.arange(n1, dtype=np.float64)
    idx = (n2 * np.arange(half)[None, None, :] + np.arange(n2)[:, None, None])
    ang = 2 * np.pi * k1[None, :, None] * idx / n
    fa = np.concatenate([np.cos(ang), -np.sin(ang)], axis=1)
    kk = np.arange(n2, dtype=np.float64)
    angb = 2 * np.pi * kk[:, None] * kk[None, :] / n2
    cr, sr = np.cos(angb), np.sin(angb)
    fb = np.block([[cr, sr], [-sr, cr]])
    fbi = np.block([[cr, -sr], [sr, cr]])
    idxo = (n2 * np.arange(half)[None, :, None] + np.arange(n2)[:, None, None])
    ango = 2 * np.pi * k1[None, None, :] * idxo / n
    ga = np.concatenate([np.cos(ango), -np.sin(ango)], axis=2) / n
    f32 = lambda a: jnp.asarray(a, F32).astype(BF16)
    return f32(fa), f32(fb), f32(fbi), f32(ga)


def _hy_stage_a_kernel(z_ref, fa_ref, o_ref, *, group):
    for g in range(group):
        zg = z_ref[0, 0, :, g * HY_W:(g + 1) * HY_W].astype(BF16)
        o_ref[0, g] = _dot(fa_ref[g], zg).astype(o_ref.dtype)


def _hy_stage_a(z4, part, fa, group=8):
    _, bsz, _, _ = z4.shape
    n2, two_n1, half = fa.shape
    return pl.pallas_call(
        functools.partial(_hy_stage_a_kernel, group=group),
        grid=(bsz, n2 // group),
        in_specs=[
            pl.BlockSpec((1, 1, half, group * HY_W), lambda b, j: (part, b, 0, j)),
            pl.BlockSpec((group, two_n1, half), lambda b, j: (j, 0, 0)),
        ],
        out_specs=pl.BlockSpec((1, group, two_n1, HY_W), lambda b, j: (b, j, 0, 0)),
        out_shape=jax.ShapeDtypeStruct((bsz, n2, two_n1, HY_W), BF16),
        compiler_params=_params("arbitrary", "arbitrary"),
        name="hyena_stage_a",
    )(z4, fa)


def _hy_filter_spec_kernel(fr_ref, fi_ref, br_ref, bi_ref, fb_ref, o_ref, *, group):
    n2 = HY_N2
    for g in range(group):
        sl = slice(g * HY_W, (g + 1) * HY_W)
        xf = _dot(fb_ref[...], jnp.concatenate([fr_ref[0, :, sl], fi_ref[0, :, sl]], axis=0))
        xb = _dot(fb_ref[...], jnp.concatenate([br_ref[0, :, sl], bi_ref[0, :, sl]], axis=0))
        o_ref[0, g] = jnp.concatenate([xf[:n2] + xb[:n2], xf[n2:] - xb[n2:]], axis=0)


def _hy_filter_spectra(a_f, fb, group=8):
    nf, n2, two_n1, _ = a_f.shape
    n1 = two_n1 // 2
    nblk = n1 // group
    a2 = a_f.reshape(nf, n2, two_n1 * HY_W)
    spec = lambda f_off, im: pl.BlockSpec((1, n2, group * HY_W), lambda o, j: (2 * o + f_off, 0, im * nblk + j))
    return pl.pallas_call(
        functools.partial(_hy_filter_spec_kernel, group=group),
        grid=(HY_ORDER, nblk),
        in_specs=[spec(0, 0), spec(0, 1), spec(1, 0), spec(1, 1),
                  pl.BlockSpec((2 * n2, 2 * n2), lambda o, j: (0, 0))],
        out_specs=pl.BlockSpec((1, group, 2 * n2, HY_W), lambda o, j: (o, j, 0, 0)),
        out_shape=jax.ShapeDtypeStruct((HY_ORDER, n1, 2 * n2, HY_W), F32),
        compiler_params=_params("arbitrary", "arbitrary"),
        name="hyena_filter_spectra",
    )(a2, a2, a2, a2, fb)


def _hy_stage_b_kernel(ar_ref, ai_ref, k_ref, fb_ref, fbi_ref, o_ref, *, group):
    n2 = HY_N2
    for g in range(group):
        sl = slice(g * HY_W, (g + 1) * HY_W)
        x = _dot(fb_ref[...], jnp.concatenate([ar_ref[0, :, sl], ai_ref[0, :, sl]], axis=0))
        kk = k_ref[0, g]
        xr, xi = x[:n2], x[n2:]
        kr, ki = kk[:n2], kk[n2:]
        y = jnp.concatenate([xr * kr - xi * ki, xr * ki + xi * kr], axis=0).astype(BF16)
        o_ref[0, g] = _dot(fbi_ref[...], y).astype(o_ref.dtype)


def _hy_stage_b(a, kspec, order, fb, fbi, group=8):
    bsz, n2, two_n1, _ = a.shape
    n1 = two_n1 // 2
    nblk = n1 // group
    a2 = a.reshape(bsz, n2, two_n1 * HY_W)
    return pl.pallas_call(
        functools.partial(_hy_stage_b_kernel, group=group),
        grid=(nblk, bsz),
        in_specs=[
            pl.BlockSpec((1, n2, group * HY_W), lambda j, b: (b, 0, j)),
            pl.BlockSpec((1, n2, group * HY_W), lambda j, b: (b, 0, nblk + j)),
            pl.BlockSpec((1, group, 2 * n2, HY_W), lambda j, b: (order, j, 0, 0)),
            pl.BlockSpec((2 * n2, 2 * n2), lambda j, b: (0, 0)),
            pl.BlockSpec((2 * n2, 2 * n2), lambda j, b: (0, 0)),
        ],
        out_specs=pl.BlockSpec((1, group, 2 * n2, HY_W), lambda j, b: (b, j, 0, 0)),
        out_shape=jax.ShapeDtypeStruct((bsz, n1, 2 * n2, HY_W), BF16),
        compiler_params=_params("arbitrary", "arbitrary"),
        name="hyena_stage_b",
    )(a2, a2, kspec, fb, fbi)


def _hy_stage_c_kernel(cr_ref, ci_ref, ga_ref, z_ref, gate_ref, bias_ref, o_ref, *, group):
    for g in range(group):
        sl = slice(g * HY_W, (g + 1) * HY_W)
        c = jnp.concatenate([cr_ref[0, :, sl], ci_ref[0, :, sl]], axis=0)
        y = _dot(ga_ref[g], c)
        y = y + z_ref[0, 0, :, sl] * bias_ref[0]
        o_ref[0, 0, :, sl] = gate_ref[0, 0, :, sl] * y


def _hy_stage_c(c, ga, z4, z_part, gate4, gate_part, bias, order, group=8):
    bsz, n1, two_n2, _ = c.shape
    n2 = two_n2 // 2
    half = n1 // 2
    nblk = n2 // group
    c2 = c.reshape(bsz, n1, two_n2 * HY_W)
    strided = lambda part: pl.BlockSpec((1, 1, half, group * HY_W), lambda b, j: (part, b, 0, j))
    out = pl.pallas_call(
        functools.partial(_hy_stage_c_kernel, group=group),
        grid=(bsz, nblk),
        in_specs=[
            pl.BlockSpec((1, n1, group * HY_W), lambda b, j: (b, 0, j)),
            pl.BlockSpec((1, n1, group * HY_W), lambda b, j: (b, 0, nblk + j)),
            pl.BlockSpec((group, half, 2 * n1), lambda b, j: (j, 0, 0)),
            strided(z_part),
            strided(gate_part),
            pl.BlockSpec((1, 1, HY_W), lambda b, j: (order, 0, 0)),
        ],
        out_specs=pl.BlockSpec((1, 1, half, group * HY_W), lambda b, j: (0, b, 0, j)),
        out_shape=jax.ShapeDtypeStruct((1, bsz, half, n2 * HY_W), F32),
        compiler_params=_params("arbitrary", "arbitrary"),
        name="hyena_stage_c",
    )(c2, c2, ga, z4, gate4, bias)
    return out


def _hyena_ctx_kernel(u_ref, filt_ref, bias_ref, fd_ref, gd_ref, o_ref, *, n_ctx):
    fd = fd_ref[...]
    gd = gd_ref[...]
    nn = 2 * n_ctx
    v, x1, x2 = u_ref[0, 0], u_ref[1, 0], u_ref[2, 0]

    def conv(zin, order):
        hf = _dot(fd, filt_ref[2 * order].astype(BF16))
        hb = _dot(fd, filt_ref[2 * order + 1].astype(BF16))
        kr, ki = hf[:nn] + hb[:nn], hf[nn:] - hb[nn:]
        zz = _dot(fd, zin.astype(BF16))
        zr, zi = zz[:nn], zz[nn:]
        y = jnp.concatenate([zr * kr - zi * ki, zr * ki + zi * kr], axis=0).astype(BF16)
        return _dot(gd, y) + zin * bias_ref[order]

    y1 = x1 * conv(v, 0)
    o_ref[0] = x2 * conv(y1, 1)


def _hyena_ctx(hy3, filt, bias, n_lat, n_ctx):
    bsz = hy3.shape[1]
    nn = 2 * n_ctx
    k = np.arange(nn, dtype=np.float64)
    t = np.arange(n_ctx, dtype=np.float64)
    ang = 2 * np.pi * k[:, None] * t[None, :] / nn
    fd = jnp.asarray(np.concatenate([np.cos(ang), -np.sin(ang)], axis=0), F32).astype(BF16)
    gd = jnp.asarray(np.concatenate([np.cos(ang.T), -np.sin(ang.T)], axis=1) / nn, F32).astype(BF16)
    blk = n_lat // n_ctx
    assert blk * n_ctx == n_lat
    return pl.pallas_call(
        functools.partial(_hyena_ctx_kernel, n_ctx=n_ctx),
        grid=(bsz,),
        in_specs=[
            pl.BlockSpec((3, 1, n_ctx, HY_W), lambda b: (0, b, blk, 0)),
            pl.BlockSpec((2 * HY_ORDER, n_ctx, HY_W), lambda b: (0, 0, 0)),
            pl.BlockSpec((HY_ORDER, 1, HY_W), lambda b: (0, 0, 0)),
            pl.BlockSpec((2 * nn, n_ctx), lambda b: (0, 0)),
            pl.BlockSpec((n_ctx, 2 * nn), lambda b: (0, 0)),
        ],
        out_specs=pl.BlockSpec((1, n_ctx, HY_W), lambda b: (b, 0, 0)),
        out_shape=jax.ShapeDtypeStruct((bsz, n_ctx, HY_W), F32),
        compiler_params=_params("arbitrary"),
        name="hyena_ctx",
    )(hy3, filt, bias, fd, gd)


def _hyena_latent(hy3, filt, bias, n_lat, tables):
    fa, fb, fbi, ga = tables
    _, bsz, t, _ = hy3.shape
    n2 = HY_N2
    rows = t // n2
    hy4 = hy3.reshape(3, bsz, rows, n2 * HY_W)
    filt4 = filt.reshape(1, 2 * HY_ORDER, n_lat // n2, n2 * HY_W)
    kspec = _hy_filter_spectra(_hy_stage_a(filt4, 0, fa), fb)
    bias3 = bias.reshape(HY_ORDER, 1, HY_W)
    c1 = _hy_stage_b(_hy_stage_a(hy4, 0, fa), kspec, 0, fb, fbi)
    y1 = _hy_stage_c(c1, ga, hy4, 0, hy4, 1, bias3, 0)
    c2 = _hy_stage_b(_hy_stage_a(y1, 0, fa), kspec, 1, fb, fbi)
    y2 = _hy_stage_c(c2, ga, y1, 0, hy4, 2, bias3, 1)
    return y2.reshape(bsz, n_lat, HY_W)


def _outproj_kernel(x_ref, att_ref, ssd_ref, hyl_ref, hyc_ref, gt_ref, wa_ref, ws_ref, wh_ref, o_ref,
                    *, n_lat_tiles):
    j = pl.program_id(0)
    hy = jnp.where(j >= n_lat_tiles, hyc_ref[0], hyl_ref[0]).astype(BF16)
    mix = _dot(att_ref[0], wa_ref[...]) + _dot(ssd_ref[0], ws_ref[...]) + _dot(hy, wh_ref[...])
    o_ref[0] = x_ref[0] + gt_ref[0] * mix


def _outproj(x, att, ssd, hy_l, hy_c, mods, w_out, n_lat_tiles, n_tiles):
    bsz, t, d = x.shape
    tm = TOKEN_TILE

    def mod_row(j, b):
        return jnp.where(j >= n_lat_tiles, bsz, b)

    tok = lambda w: pl.BlockSpec((1, tm, w), lambda j, b: (b, j, 0))
    return pl.pallas_call(
        functools.partial(_outproj_kernel, n_lat_tiles=n_lat_tiles),
        grid=(n_tiles, bsz),
        in_specs=[
            tok(d), tok(ATT_W), tok(SSD_W),
            pl.BlockSpec((1, tm, HY_W), lambda j, b: (b, jnp.minimum(j, n_lat_tiles - 1), 0)),
            pl.BlockSpec((1, tm, HY_W), lambda j, b: (b, 0, 0)),
            pl.BlockSpec((1, 1, d), lambda j, b: (mod_row(j, b), 0, 2)),
            pl.BlockSpec((ATT_W, d), lambda j, b: (0, 0)),
            pl.BlockSpec((SSD_W, d), lambda j, b: (0, 0)),
            pl.BlockSpec((HY_W, d), lambda j, b: (0, 0)),
        ],
        out_specs=tok(d),
        out_shape=jax.ShapeDtypeStruct((bsz, n_tiles * tm, d), F32),
        compiler_params=_params("arbitrary", "arbitrary"),
        name="outproj",
    )(x, att, ssd, hy_l, hy_c, mods, w_out[:ATT_W], w_out[ATT_W:ATT_W + SSD_W], w_out[ATT_W + SSD_W:])


def _route(logits_t, bias_col):
    scores = jax.nn.sigmoid(logits_t)
    sel = scores + bias_col
    neg_inf = jnp.float32(-jnp.inf)
    rows = [sel[e:e + 1, :] for e in range(N_EXPERTS)]
    grp = []
    for g in range(N_GROUPS):
        r = rows[g * EXPERTS_PER_GROUP:(g + 1) * EXPERTS_PER_GROUP]
        top = functools.reduce(jnp.maximum, r)
        taken = None
        rest = []
        for ri in r:
            is_top = (ri == top) if taken is None else (ri == top) & jnp.logical_not(taken)
            rest.append(jnp.where(is_top, neg_inf, ri))
            taken = is_top if taken is None else taken | is_top
        grp.append(top + functools.reduce(jnp.maximum, rest))
    best = jnp.zeros(grp[0].shape, jnp.int32)
    cur = grp[0]
    for g in range(1, N_GROUPS):
        upd = grp[g] > cur
        best = jnp.where(upd, g, best)
        cur = jnp.where(upd, grp[g], cur)
    picked = []
    for e in range(N_EXPERTS):
        g, i = divmod(e, EXPERTS_PER_GROUP)
        rank = jnp.zeros(best.shape, jnp.int32)
        for j in range(EXPERTS_PER_GROUP):
            if j == i:
                continue
            other = rows[g * EXPERTS_PER_GROUP + j]
            ahead = (other > rows[e]) | ((other == rows[e]) & (j < i))
            rank = rank + ahead.astype(jnp.int32)
        keep = (best == g) & (rank < 2)
        picked.append(jnp.where(keep, scores[e:e + 1, :], 0.0))
    total = functools.reduce(lambda u, w: u + w, picked)
    return jnp.concatenate(picked, axis=0) / total


def _moe_kernel(x_ref, sh_ref, sc_ref, gt_ref, g_ref, wr_ref, rb_ref, wgu_ref, wd_ref, gf_ref, o_ref, *, final):
    x = x_ref[0]
    ms = jnp.mean(x * x, axis=-1, keepdims=True)
    t = x * lax.rsqrt(ms + EPS) * g_ref[...]
    t = t * (1.0 + sc_ref[0]) + sh_ref[0]
    tm = t.shape[0]

    logits = jnp.dot(t, wr_ref[...], precision=lax.Precision.HIGHEST, preferred_element_type=F32)
    comb_t = _route(logits.T[:N_EXPERTS, :], rb_ref[...])
    comb = jnp.concatenate([comb_t, jnp.zeros((LANES - N_EXPERTS, tm), F32)], axis=0).T

    tb = t.astype(BF16)
    acc = jnp.zeros((tm, D_MODEL), F32)
    for e in range(N_EXPERTS):
        gu = _dot(tb, wgu_ref[e])
        hid = _silu(gu[:, :D_FF]) * gu[:, D_FF:] * comb[:, e:e + 1]
        acc = acc + _dot(hid.astype(BF16), wd_ref[e])
    y = x + gt_ref[0] * acc
    if final:
        y = y * lax.rsqrt(jnp.mean(y * y, axis=-1, keepdims=True) + EPS) * gf_ref[...]
    o_ref[0] = y


def _moe(x, mods, g_ffn, w_router, router_bias, wgu, wd, g_final, n_lat_tiles, n_tiles, final):
    bsz, _, d = x.shape
    tm = TOKEN_TILE

    def mod_row(j, b):
        return jnp.where(j >= n_lat_tiles, bsz, b)

    tok = pl.BlockSpec((1, tm, d), lambda j, b: (b, j, 0))
    mod = lambda col: pl.BlockSpec((1, 1, d), lambda j, b: (mod_row(j, b), 0, col))
    resident = lambda shape: pl.BlockSpec(shape, lambda j, b: tuple(0 for _ in shape),
                                          pipeline_mode=pl.Buffered(1))
    const = lambda shape: pl.BlockSpec(shape, lambda j, b: tuple(0 for _ in shape))
    return pl.pallas_call(
        functools.partial(_moe_kernel, final=final),
        grid=(n_tiles, bsz),
        in_specs=[
            tok, mod(3), mod(4), mod(5),
            const((1, d)),
            const((d, LANES)),
            const((N_EXPERTS, 1)),
            resident((N_EXPERTS, d, 2 * D_FF)),
            resident((N_EXPERTS, D_FF, d)),
            const((1, d)),
        ],
        out_specs=tok,
        out_shape=jax.ShapeDtypeStruct((bsz, n_tiles * tm, d), F32),
        compiler_params=_params("arbitrary", "arbitrary"),
        name="moe",
    )(x, mods, mods, mods, g_ffn, w_router, router_bias, wgu, wd, g_final)


def _block_ones(width, block):
    idx = np.arange(width) // block
    return jnp.asarray(idx[:, None] == idx[None, :], F32).astype(BF16)


def _rope_tables(n_lat, n_ctx):
    rows = n_lat // GRID_W
    row = np.repeat(np.arange(rows), GRID_W).astype(np.float64)
    col = np.tile(np.arange(GRID_W), rows).astype(np.float64)
    inv = ROPE_THETA ** (-np.arange(0, ROPE_AXIS_DIM, 2, dtype=np.float64) / ROPE_AXIS_DIM)
    ang = np.concatenate([row[:, None] * inv, col[:, None] * inv], axis=-1)
    ang = np.concatenate([ang, np.zeros((n_ctx, ang.shape[1]))], axis=0)
    cos = np.concatenate([np.cos(ang), np.cos(ang)], axis=-1)
    sin = np.concatenate([-np.sin(ang), np.sin(ang)], axis=-1)
    scale = HEAD_DIM ** -0.5
    cs = np.tile(cos, (1, ATT_HEADS)) * scale
    sn = np.tile(sin, (1, ATT_HEADS)) * scale
    return jnp.asarray(cs, F32), jnp.asarray(sn, F32)


def _pack_w_in(w):
    cuts = np.cumsum([ATT_W, KV_W, KV_W, SSD_W, SSD_CONV_CH, 2 * SSD_HEADS])
    q, k, v, z, xbc, dt, hy = jnp.split(w, [int(c) for c in cuts], axis=-1)
    dt = jnp.pad(dt, ((0, 0), (0, LANES - 2 * SSD_HEADS)))
    return jnp.concatenate([q, k, v, z, xbc, dt, hy], axis=-1).astype(BF16)


def kernel(x, c, ctx, c_ctx, w_mod, b_mod, g_mix, g_ffn, w_in, q_norm, k_norm, ssd_conv_w, ssd_conv_b,
           ssd_dt_bias, ssd_a_log, ssd_d, ssd_norm, hy_conv_w, hy_conv_b, hy_w1, hy_b1, hy_freq, hy_w2, hy_b2,
           hy_w3, hy_bias, w_out, w_router, router_bias, w_gate, w_up, w_down, g_final):
    bsz, n_lat, d = x.shape
    n_ctx = ctx.shape[1]
    depth = w_mod.shape[0]
    t = n_lat + n_ctx
    tm = TOKEN_TILE
    n_lat_tiles = n_lat // tm
    n_tiles = t // tm
    assert n_ctx == tm and n_lat % (HY_N2 * 8) == 0 and bsz < MOD_ROWS

    cvec = jnp.concatenate([c, c_ctx[None], jnp.zeros((MOD_ROWS - bsz - 1, d), F32)], axis=0)
    mods_all = _adaln(cvec, w_mod, b_mod)

    cs, sn = _rope_tables(n_lat, n_ctx)
    bd_head = _block_ones(ATT_W, HEAD_DIM)
    bd_ssd = _block_ones(SSD_W, SSD_W // SSD_GROUPS)
    tables = _dft_tables(n_lat)
    pad_row = lambda v: jnp.pad(v.reshape(1, -1), ((0, 0), (0, LANES - v.size)))

    xs = jnp.concatenate([x, ctx], axis=1)
    for i in range(depth):
        last = i == depth - 1
        mods = mods_all[i].reshape(MOD_ROWS, 1, 6 * d)
        q, kt, v, z, xbc, dt_raw, hy = _inproj(
            xs, mods, g_mix[i].reshape(1, d), _pack_w_in(w_in[i]),
            jnp.tile(q_norm[i], ATT_HEADS).reshape(1, ATT_W), jnp.tile(k_norm[i], ATT_KV_HEADS).reshape(1, KV_W),
            cs, sn, bd_head, n_lat_tiles)

        att = _attention(q, kt, v, 0, n_lat_tiles, 0, n_tiles)
        if not last:
            att_c = _attention(q, kt, v, n_lat_tiles, 1, n_lat_tiles, 1)
            att = jnp.concatenate([att, att_c], axis=1)

        xbc_c = _dwconv(xbc, ssd_conv_w[i], ssd_conv_b[i], n_lat, True, 1)
        ssd = _ssd(xbc_c, z, dt_raw, pad_row(ssd_dt_bias[i]), pad_row(ssd_a_log[i]),
                   jnp.repeat(ssd_d[i], SSD_HEAD_DIM).reshape(1, SSD_W), ssd_norm[i].reshape(1, SSD_W),
                   bd_ssd, n_lat // SSD_CHUNK)

        hy3 = _dwconv(hy, hy_conv_w[i], hy_conv_b[i], n_lat, False, 3)
        hyp = (hy_w1[i], hy_b1[i], hy_freq[i], hy_w2[i], hy_b2[i], hy_w3[i])
        hy_l = _hyena_latent(hy3, _hyena_filters(n_lat, *hyp), hy_bias[i], n_lat, tables)
        if not last:
            hy_c = _hyena_ctx(hy3, _hyena_filters(n_ctx, *hyp), hy_bias[i].reshape(HY_ORDER, 1, HY_W), n_lat, n_ctx)
        else:
            hy_c = jnp.zeros((bsz, n_ctx, HY_W), F32)

        live_tiles = n_lat_tiles if last else n_tiles
        xs = _outproj(xs, att, ssd, hy_l, hy_c, mods, w_out[i].astype(BF16), n_lat_tiles, live_tiles)

        wgu = jnp.concatenate([w_gate[i], w_up[i]], axis=-1).astype(BF16)
        xs = _moe(xs, mods, g_ffn[i].reshape(1, d), jnp.pad(w_router, ((0, 0), (0, LANES - N_EXPERTS))),
                  router_bias.reshape(N_EXPERTS, 1), wgu, w_down[i].astype(BF16), g_final.reshape(1, d),
                  n_lat_tiles, live_tiles, last)
    return xs
```

```python
import functools
import math

import jax
import jax.numpy as jnp
import numpy as np
from jax import lax
from jax.experimental import pallas as pl
from jax.experimental.pallas import tpu as pltpu

F32 = jnp.float32
BF16 = jnp.bfloat16

D_MODEL = 1024
GRID_W = 64
EPS = 1e-6

ATT_HEADS = 6
ATT_KV_HEADS = 2
HEAD_DIM = 64
ATT_W = ATT_HEADS * HEAD_DIM
KV_W = ATT_KV_HEADS * HEAD_DIM
ROPE_AXIS_DIM = HEAD_DIM // 2
ROPE_THETA = 10000.0

SSD_HEADS = 6
SSD_HEAD_DIM = 64
SSD_W = SSD_HEADS * SSD_HEAD_DIM
SSD_GROUPS = 2
SSD_STATE = 64
SSD_CHUNK = 128
SSD_CONV_CH = SSD_W + 2 * SSD_GROUPS * SSD_STATE
SSD_HEADS_PER_GROUP = SSD_HEADS // SSD_GROUPS

HY_W = 256
HY_ORDER = 2
HY_BANDS = 16
HY_POS_DIM = 1 + 2 * HY_BANDS
HY_FILTER_HID = 64
HY_FAST_DECAY = 0.3
HY_SLOW_DECAY = 1.5
HY_TARGET = 1e-2
HY_N2 = 64
HY_UNROLL = 8

MIX_W = ATT_W + SSD_W + HY_W
N_EXPERTS = 16
N_GROUPS = 4
EXPERTS_PER_GROUP = N_EXPERTS // N_GROUPS
D_FF = 256

LANES = 128
SUBLANES = 8
TOKEN_TILE = 256
MOE_TILE = 512
MOE_CHUNK = 128
MOE_CHUNKS = MOE_TILE // MOE_CHUNK + N_GROUPS - 1
MOD_ROWS = 8
VMEM_LIMIT = 56 * 1024 * 1024

COL_Q = 0
COL_K = COL_Q + ATT_W
COL_V = COL_K + KV_W
COL_Z = COL_V + KV_W
COL_XBC = COL_Z + SSD_W
COL_DT = COL_XBC + SSD_CONV_CH
COL_HY = COL_DT + LANES
COL_END = COL_HY + 3 * HY_W


def _params(*sem):
    return pltpu.CompilerParams(dimension_semantics=sem, vmem_limit_bytes=VMEM_LIMIT)


def _silu(x):
    return x * jax.nn.sigmoid(x)


def _softplus(x):
    return jnp.maximum(x, 0.0) + jnp.log1p(jnp.exp(-jnp.abs(x)))


def _split3(x):
    hi = x.astype(BF16)
    r1 = x - hi.astype(F32)
    mid = r1.astype(BF16)
    lo = (r1 - mid.astype(F32)).astype(BF16)
    return hi, mid, lo


def _dot(a, b):
    return jnp.dot(a, b, preferred_element_type=F32)


def _dot3_right(m_bf16, x):
    hi, mid, lo = _split3(x)
    return _dot(m_bf16, hi) + _dot(m_bf16, mid) + _dot(m_bf16, lo)


def _dot3_left(x, m_bf16):
    hi, mid, lo = _split3(x)
    return _dot(hi, m_bf16) + _dot(mid, m_bf16) + _dot(lo, m_bf16)


def _adaln_kernel(c_ref, w_ref, b_ref, o_ref):
    s = _silu(c_ref[...]).astype(BF16)
    o_ref[0] = _dot(s, w_ref[0].astype(BF16)) + b_ref[0]


def _adaln(cvec, w_mod, b_mod):
    depth, d, n = w_mod.shape
    bn = n // 4
    return pl.pallas_call(
        _adaln_kernel,
        grid=(depth, n // bn),
        in_specs=[
            pl.BlockSpec((MOD_ROWS, d), lambda i, j: (0, 0)),
            pl.BlockSpec((1, d, bn), lambda i, j: (i, 0, j)),
            pl.BlockSpec((1, 1, bn), lambda i, j: (i, 0, j)),
        ],
        out_specs=pl.BlockSpec((1, MOD_ROWS, bn), lambda i, j: (i, 0, j)),
        out_shape=jax.ShapeDtypeStruct((depth, MOD_ROWS, n), F32),
        compiler_params=_params("arbitrary", "arbitrary"),
        name="adaln",
    )(cvec, w_mod, b_mod.reshape(depth, 1, n))


def _head_rms_rope(x, bd, gain, cs, sn):
    width = x.shape[-1]
    ms = _dot((x * x).astype(BF16), bd) * (1.0 / HEAD_DIM)
    xn = x * lax.rsqrt(ms + EPS) * gain
    lane = lax.broadcasted_iota(jnp.int32, xn.shape, 1)
    first_half = (lane % HEAD_DIM) < (HEAD_DIM // 2)
    partner = jnp.where(first_half,
                        pltpu.roll(xn, width - HEAD_DIM // 2, 1),
                        pltpu.roll(xn, HEAD_DIM // 2, 1))
    return xn * cs + partner * sn


def _inproj_kernel(x_ref, sh_ref, sc_ref, g_ref, w_ref, gq_ref, gk_ref, cs_ref, sn_ref, bd_ref,
                   q_ref, kt_ref, v_ref, z_ref, xbc_ref, dt_ref, hy_ref):
    x = x_ref[0]
    ms = jnp.mean(x * x, axis=-1, keepdims=True)
    h = x * lax.rsqrt(ms + EPS) * g_ref[...]
    h = h * (1.0 + sc_ref[0]) + sh_ref[0]
    acc = _dot(h.astype(BF16), w_ref[...])

    cs = cs_ref[...]
    sn = sn_ref[...]
    bd = bd_ref[...]
    q = _head_rms_rope(acc[:, COL_Q:COL_K], bd, gq_ref[...], cs, sn)
    q_ref[0] = q.astype(q_ref.dtype)
    k = _head_rms_rope(acc[:, COL_K:COL_V], bd[:KV_W, :KV_W], gk_ref[...],
                       cs[:, :KV_W] * 8.0, sn[:, :KV_W] * 8.0)
    kt_ref[0] = k.T.astype(kt_ref.dtype)
    v_ref[0] = acc[:, COL_V:COL_Z].astype(v_ref.dtype)
    z_ref[0] = acc[:, COL_Z:COL_XBC]
    xbc_ref[0] = acc[:, COL_XBC:COL_DT]
    dt_ref[0] = acc[:, COL_DT:COL_HY]
    hy_ref[0] = acc[:, COL_HY:COL_END]


def _inproj(x, mods, g_mix, w_cat, gq, gk, cs, sn, bd, n_lat_tiles):
    bsz, t, d = x.shape
    tm = TOKEN_TILE
    nt = t // tm

    def mod_row(j, b):
        return jnp.where(j >= n_lat_tiles, bsz, b)

    tok = lambda w: pl.BlockSpec((1, tm, w), lambda j, b: (b, j, 0))
    const = lambda shape: pl.BlockSpec(shape, lambda j, b: tuple(0 for _ in shape))
    outs = pl.pallas_call(
        _inproj_kernel,
        grid=(nt, bsz),
        in_specs=[
            tok(d),
            pl.BlockSpec((1, 1, d), lambda j, b: (mod_row(j, b), 0, 0)),
            pl.BlockSpec((1, 1, d), lambda j, b: (mod_row(j, b), 0, 1)),
            const((1, d)),
            const((d, COL_END)),
            const((1, ATT_W)),
            const((1, KV_W)),
            pl.BlockSpec((tm, ATT_W), lambda j, b: (j, 0)),
            pl.BlockSpec((tm, ATT_W), lambda j, b: (j, 0)),
            const((ATT_W, ATT_W)),
        ],
        out_specs=[
            tok(ATT_W),
            pl.BlockSpec((1, KV_W, tm), lambda j, b: (b, 0, j)),
            tok(KV_W),
            tok(SSD_W),
            tok(SSD_CONV_CH),
            tok(LANES),
            tok(3 * HY_W),
        ],
        out_shape=[
            jax.ShapeDtypeStruct((bsz, t, ATT_W), BF16),
            jax.ShapeDtypeStruct((bsz, KV_W, t), BF16),
            jax.ShapeDtypeStruct((bsz, t, KV_W), BF16),
            jax.ShapeDtypeStruct((bsz, t, SSD_W), F32),
            jax.ShapeDtypeStruct((bsz, t, SSD_CONV_CH), F32),
            jax.ShapeDtypeStruct((bsz, t, LANES), F32),
            jax.ShapeDtypeStruct((bsz, t, 3 * HY_W), F32),
        ],
        compiler_params=_params("arbitrary", "arbitrary"),
        name="inproj",
    )(x, mods, mods, g_mix, w_cat, gq, gk, cs, sn, bd)
    return outs


def _attn_kernel(q_ref, kt_ref, v_ref, o_ref):
    q = q_ref[0]
    kt = kt_ref[0]
    v = v_ref[0]
    rep = ATT_HEADS // ATT_KV_HEADS
    outs = []
    for g in range(ATT_KV_HEADS):
        ktg = kt[g * HEAD_DIM:(g + 1) * HEAD_DIM, :]
        for r in range(rep):
            hd = g * rep + r
            s = _dot(q[:, hd * HEAD_DIM:(hd + 1) * HEAD_DIM], ktg)
            m = jnp.max(s, axis=-1, keepdims=True)
            p = jnp.exp(s - m)
            l = jnp.sum(p, axis=-1, keepdims=True)
            o = _dot(p.astype(BF16), v)
            outs.append(o[:, g * HEAD_DIM:(g + 1) * HEAD_DIM] / l)
    o_ref[0] = jnp.concatenate(outs, axis=-1).astype(o_ref.dtype)


def _attention(q, kt, v, q_tile0, n_q_tiles, k_tile0, n_k_tiles):
    bsz = q.shape[0]
    tq = TOKEN_TILE
    s = n_k_tiles * tq
    kblk = k_tile0 // n_k_tiles
    assert kblk * n_k_tiles == k_tile0
    return pl.pallas_call(
        _attn_kernel,
        grid=(bsz, n_q_tiles),
        in_specs=[
            pl.BlockSpec((1, tq, ATT_W), lambda b, j: (b, q_tile0 + j, 0)),
            pl.BlockSpec((1, KV_W, s), lambda b, j: (b, 0, kblk)),
            pl.BlockSpec((1, s, KV_W), lambda b, j: (b, kblk, 0)),
        ],
        out_specs=pl.BlockSpec((1, tq, ATT_W), lambda b, j: (b, j, 0)),
        out_shape=jax.ShapeDtypeStruct((bsz, n_q_tiles * tq, ATT_W), BF16),
        compiler_params=_params("arbitrary", "arbitrary"),
        name="attention",
    )(q, kt, v)


def _dwconv_kernel(u_ref, w_ref, b_ref, o_ref, *, n_lat, act):
    u = u_ref[0]
    t = u.shape[0]
    row = lax.broadcasted_iota(jnp.int32, u.shape, 0)
    prev = jnp.where((row == 0) | (row == n_lat), 0.0, pltpu.roll(u, 1, 0))
    nxt = jnp.where((row == n_lat - 1) | (row == t - 1), 0.0, pltpu.roll(u, t - 1, 0))
    w = w_ref[...]
    y = prev * w[0:1] + u * w[1:2] + nxt * w[2:3] + b_ref[...]
    if act:
        y = _silu(y)
    o_ref[0, 0] = y


def _dwconv(u, w, b, n_lat, act, parts):
    bsz, t, c = u.shape
    cpp = c // parts
    nb = cpp // LANES
    return pl.pallas_call(
        functools.partial(_dwconv_kernel, n_lat=n_lat, act=act),
        grid=(bsz, parts * nb),
        in_specs=[
            pl.BlockSpec((1, t, LANES), lambda b_, j: (b_, 0, j)),
            pl.BlockSpec((3, LANES), lambda b_, j: (0, j)),
            pl.BlockSpec((1, LANES), lambda b_, j: (0, j)),
        ],
        out_specs=pl.BlockSpec((1, 1, t, LANES), lambda b_, j: (j // nb, b_, 0, j % nb)),
        out_shape=jax.ShapeDtypeStruct((parts, bsz, t, cpp), F32),
        compiler_params=_params("arbitrary", "arbitrary"),
        name="dwconv",
    )(u, w, b.reshape(1, c))


def _ssd_kernel(xbc_ref, z_ref, dt_ref, dtb_ref, alog_ref, dsk_ref, nw_ref, bd_ref,
                o_ref, hf_ref, hb_ref, hbe_ref, *, n_lat_chunks):
    q = SSD_CHUNK
    nh = SSD_HEADS
    phase = pl.program_id(1)
    step = pl.program_id(2)
    n_chunks = n_lat_chunks + 2
    chunk = jnp.where(phase == 0, n_chunks - 1 - step,
                      jnp.where(step < 2, n_lat_chunks + step, step - 2))

    xbc = xbc_ref[0, 0]
    x = xbc[:, :SSD_W]
    bmat = xbc[:, SSD_W:SSD_W + SSD_GROUPS * SSD_STATE]
    cmat = xbc[:, SSD_W + SSD_GROUPS * SSD_STATE:]

    lane = lax.broadcasted_iota(jnp.int32, (1, LANES), 1)
    a_row = jnp.where(lane < 2 * nh, -jnp.exp(alog_ref[...]), 0.0)
    dt = _softplus(dt_ref[0] + dtb_ref[...])
    a = dt * a_row
    a_t = a.T
    dt_t = dt.T

    tt = lax.broadcasted_iota(jnp.int32, (q, q), 0)
    ss = lax.broadcasted_iota(jnp.int32, (q, q), 1)
    lower = (ss <= tt)
    upper = (ss >= tt)
    lmat = jnp.where(lower, 1.0, 0.0).astype(BF16)
    umat = jnp.where(upper, 1.0, 0.0).astype(BF16)

    bt = bmat.T

    @pl.when(step == 0)
    def _():
        hf_ref[...] = jnp.zeros_like(hf_ref)
        hb_ref[...] = jnp.zeros_like(hb_ref)

    @pl.when(phase == 0)
    def _():
        suf_t = _dot3_left(a_t, lmat)
        for hd in range(nh):
            g = hd // SSD_HEADS_PER_GROUP
            col = nh + hd
            row_b = suf_t[col:col + 1, :]
            total = row_b[:, 0:1]
            w_t = jnp.exp(total - row_b) * dt_t[col:col + 1, :]
            btg = bt[g * SSD_STATE:(g + 1) * SSD_STATE, :]
            xh = x[:, hd * SSD_HEAD_DIM:(hd + 1) * SSD_HEAD_DIM].astype(BF16)
            prev = hb_ref[hd]
            hbe_ref[chunk, hd] = prev
            hb_ref[hd] = prev * jnp.exp(total) + _dot((btg * w_t).astype(BF16), xh)

    @pl.when(phase == 1)
    def _():
        pre = _dot3_right(lmat, a)
        suf = _dot3_right(umat, a)
        pre_t = _dot3_left(a_t, umat)
        suf_t = _dot3_left(a_t, lmat)
        neg_inf = jnp.float32(-jnp.inf)
        ys = []
        for g in range(SSD_GROUPS):
            cg = cmat[:, g * SSD_STATE:(g + 1) * SSD_STATE].astype(BF16)
            btg = bt[g * SSD_STATE:(g + 1) * SSD_STATE, :]
            cb = _dot(cg, btg.astype(BF16))
            for r in range(SSD_HEADS_PER_GROUP):
                hd = g * SSD_HEADS_PER_GROUP + r
                colf = pre[:, hd:hd + 1]
                rowf = pre_t[hd:hd + 1, :]
                colb = suf[:, nh + hd:nh + hd + 1]
                rowb = suf_t[nh + hd:nh + hd + 1, :]
                wf = jnp.exp(jnp.where(lower, colf - rowf, neg_inf)) * dt_t[hd:hd + 1, :]
                wb = jnp.exp(jnp.where(upper, colb - rowb, neg_inf)) * dt_t[nh + hd:nh + hd + 1, :]
                xh = x[:, hd * SSD_HEAD_DIM:(hd + 1) * SSD_HEAD_DIM].astype(BF16)
                y = _dot((cb * (wf + wb)).astype(BF16), xh)
                hf = hf_ref[hd]
                y = y + _dot(cg, hf.astype(BF16)) * jnp.exp(colf)
                y = y + _dot(cg, hbe_ref[chunk, hd].astype(BF16)) * jnp.exp(colb)
                total = rowf[:, q - 1:q]
                w_t = jnp.exp(total - rowf) * dt_t[hd:hd + 1, :]
                hf_ref[hd] = hf * jnp.exp(total) + _dot((btg * w_t).astype(BF16), xh)
                ys.append(y)
        y = jnp.concatenate(ys, axis=-1) + x * dsk_ref[...]
        gz = y * _silu(z_ref[0])
        ms = _dot((gz * gz).astype(BF16), bd_ref[...]) * (1.0 / (SSD_W // SSD_GROUPS))
        o_ref[0] = (gz * lax.rsqrt(ms + EPS) * nw_ref[...]).astype(o_ref.dtype)


def _ssd(xbc, z, dt_raw, dt_bias, a_log, d_skip, norm_w, bd, n_lat_chunks):
    bsz, t, _ = z.shape
    q = SSD_CHUNK
    n_chunks = t // q
    assert n_chunks == n_lat_chunks + 2

    def chunk_of(p, s):
        return jnp.where(p == 0, n_chunks - 1 - s, jnp.where(s < 2, n_lat_chunks + s, s - 2))

    first_fwd = n_lat_chunks

    def out_chunk(p, s):
        return jnp.where(p == 0, first_fwd, chunk_of(p, s))

    const = lambda shape: pl.BlockSpec(shape, lambda b, p, s: tuple(0 for _ in shape))
    return pl.pallas_call(
        functools.partial(_ssd_kernel, n_lat_chunks=n_lat_chunks),
        grid=(bsz, 2, n_chunks),
        in_specs=[
            pl.BlockSpec((1, 1, q, SSD_CONV_CH), lambda b, p, s: (0, b, chunk_of(p, s), 0)),
            pl.BlockSpec((1, q, SSD_W), lambda b, p, s: (b, chunk_of(p, s), 0)),
            pl.BlockSpec((1, q, LANES), lambda b, p, s: (b, chunk_of(p, s), 0)),
            const((1, LANES)),
            const((1, LANES)),
            const((1, SSD_W)),
            const((1, SSD_W)),
            const((SSD_W, SSD_W)),
        ],
        out_specs=pl.BlockSpec((1, q, SSD_W), lambda b, p, s: (b, out_chunk(p, s), 0)),
        out_shape=jax.ShapeDtypeStruct((bsz, t, SSD_W), BF16),
        scratch_shapes=[
            pltpu.VMEM((SSD_HEADS, SSD_STATE, SSD_HEAD_DIM), F32),
            pltpu.VMEM((SSD_HEADS, SSD_STATE, SSD_HEAD_DIM), F32),
            pltpu.VMEM((n_chunks, SSD_HEADS, SSD_STATE, SSD_HEAD_DIM), F32),
        ],
        compiler_params=_params("arbitrary", "arbitrary", "arbitrary"),
        name="ssd",
    )(xbc, z, dt_raw, dt_bias, a_log, d_skip, norm_w, bd)


def _hyfilt_kernel(feat_ref, win_ref, w1_ref, b1_ref, fr_ref, w2_ref, b2_ref, w3_ref, o_ref):
    hp = lax.Precision.HIGHEST
    fr = fr_ref[...]
    h1 = jnp.sin(fr * (jnp.dot(feat_ref[...], w1_ref[...], precision=hp, preferred_element_type=F32) + b1_ref[...]))
    h2 = jnp.sin(fr * (jnp.dot(h1, w2_ref[...], precision=hp, preferred_element_type=F32) + b2_ref[...]))
    h = jnp.dot(h2, w3_ref[...], precision=hp, preferred_element_type=F32)
    win = win_ref[...]
    first_tile = pl.program_id(0) == 0
    row = lax.broadcasted_iota(jnp.int32, win.shape, 0)
    for order in range(HY_ORDER):
        for direction in range(2):
            c0 = (direction * HY_ORDER + order) * HY_W
            f = h[:, c0:c0 + HY_W] * win
            if direction == 1:
                f = jnp.where(first_tile & (row == 0), 0.0, f)
            o_ref[order * 2 + direction] = f


def _hy_positional(length):
    n = np.arange(length, dtype=np.float64)
    t = n / max(length - 1, 1)
    bands = np.linspace(1e-4, HY_BANDS - 1, HY_BANDS)
    wpos = (2 * math.pi / length) * n
    feats = np.concatenate([t[:, None], np.cos(wpos[:, None] * bands), -np.sin(wpos[:, None] * bands)], axis=-1)
    feats = np.pad(feats, ((0, 0), (0, LANES - HY_POS_DIM)))
    deltas = np.abs(np.linspace(math.log(HY_TARGET) / HY_SLOW_DECAY, math.log(HY_TARGET) / HY_FAST_DECAY, HY_W))
    window = np.exp(-t[:, None] * deltas)
    return jnp.asarray(feats, F32), jnp.asarray(window, F32)


def _hyena_filters(length, w1, b1, freq, w2, b2, w3):
    feats, window = _hy_positional(length)
    tl = min(length, 512)
    hid = HY_FILTER_HID
    w1p = jnp.pad(w1, ((0, LANES - HY_POS_DIM), (0, 0)))
    const = lambda shape: pl.BlockSpec(shape, lambda i: tuple(0 for _ in shape))
    return pl.pallas_call(
        _hyfilt_kernel,
        grid=(length // tl,),
        in_specs=[
            pl.BlockSpec((tl, LANES), lambda i: (i, 0)),
            pl.BlockSpec((tl, HY_W), lambda i: (i, 0)),
            const((LANES, hid)), const((1, hid)), const((1, hid)),
            const((hid, hid)), const((1, hid)), const((hid, 2 * HY_ORDER * HY_W)),
        ],
        out_specs=pl.BlockSpec((2 * HY_ORDER, tl, HY_W), lambda i: (0, i, 0)),
        out_shape=jax.ShapeDtypeStruct((2 * HY_ORDER, length, HY_W), F32),
        compiler_params=_params("arbitrary"),
        name="hyena_filters",
    )(feats, window, w1p, b1.reshape(1, hid), freq.reshape(1, hid), w2, b2.reshape(1, hid), w3)


def _dft_tables(length):
    n = 2 * length
    n2 = HY_N2
    n1 = n // n2
    half = n1 // 2
    k1 = np.arange(n1, dtype=np.float64)
    idx = (n2 * np.arange(half)[None, None, :] + np.arange(n2)[:, None, None])
    ang = 2 * np.pi * k1[None, :, None] * idx / n
    fa = np.concatenate([np.cos(ang), -np.sin(ang)], axis=1)
    kk = np.arange(n2, dtype=np.float64)
    angb = 2 * np.pi * kk[:, None] * kk[None, :] / n2
    cr, sr = np.cos(angb), np.sin(angb)
    fb = np.block([[cr, sr], [-sr, cr]])
    fbi = np.block([[cr, -sr], [sr, cr]])
    idxo = (n2 * np.arange(half)[None, :, None] + np.arange(n2)[:, None, None])
    ango = 2 * np.pi * k1[None, None, :] * idxo / n
    ga = np.concatenate([np.cos(ango), -np.sin(ango)], axis=2) / n
    f32 = lambda a: jnp.asarray(a, F32).astype(BF16)
    return f32(fa), f32(fb), f32(fbi), f32(ga)


def _conv3_rows(u, w, b):
    n = u.shape[0]
    row = lax.broadcasted_iota(jnp.int32, u.shape, 0)
    prev = jnp.where(row == 0, 0.0, pltpu.roll(u, 1, 0))
    nxt = jnp.where(row == n - 1, 0.0, pltpu.roll(u, n - 1, 0))
    return prev * w[0:1] + u * w[1:2] + nxt * w[2:3] + b


def _seq_pitch(n2n):
    return n2n + SUBLANES


def _to_pitched(dst_ref, val, n2n):
    pitch = _seq_pitch(n2n)
    for i in range(val.shape[0] // n2n):
        dst_ref[i * pitch:i * pitch + n2n, :] = val[i * n2n:(i + 1) * n2n, :]


def _hy_forward_a(src_ref, a_ref, fa_ref):
    n2n, two_n1, half = fa_ref.shape
    zp, ap = _seq_pitch(n2n), _seq_pitch(two_n1)

    def body(n2, carry):
        zs = src_ref[pl.ds(n2, half, stride=zp), :].astype(BF16)
        a_ref[pl.ds(pl.multiple_of(n2 * ap, SUBLANES), two_n1), :] = _dot(fa_ref[n2], zs)
        return carry

    lax.fori_loop(0, n2n, body, 0, unroll=HY_UNROLL)


def _hy_kspec_kernel(hf_ref, hb_ref, fa_ref, fb_ref, o_ref, af_ref, ab_ref, hfp_ref, hbp_ref):
    n2n, two_n1, _ = fa_ref.shape
    n1 = two_n1 // 2
    ap = _seq_pitch(two_n1)
    _to_pitched(hfp_ref, hf_ref[0], n2n)
    _to_pitched(hbp_ref, hb_ref[0], n2n)
    _hy_forward_a(hfp_ref, af_ref, fa_ref)
    _hy_forward_a(hbp_ref, ab_ref, fa_ref)
    fb = fb_ref[...]

    def body(k1, carry):
        def spectrum(a_ref):
            ar = a_ref[pl.ds(k1, n2n, stride=ap), :]
            ai = a_ref[pl.ds(n1 + k1, n2n, stride=ap), :]
            return _dot(fb, jnp.concatenate([ar, ai], axis=0).astype(BF16))

        xf, xb = spectrum(af_ref), spectrum(ab_ref)
        o_ref[0, k1] = jnp.concatenate([xf[:n2n] + xb[:n2n], xf[n2n:] - xb[n2n:]], axis=0)
        return carry

    lax.fori_loop(0, n1, body, 0, unroll=HY_UNROLL)


def _hy_kspec(filt, tables):
    fa, fb, _, _ = tables
    _, length, _ = filt.shape
    n2n, two_n1, _ = fa.shape
    n1 = two_n1 // 2
    nh = HY_W // LANES
    return pl.pallas_call(
        _hy_kspec_kernel,
        grid=(HY_ORDER, nh),
        in_specs=[
            pl.BlockSpec((1, length, LANES), lambda o, h: (2 * o, 0, h)),
            pl.BlockSpec((1, length, LANES), lambda o, h: (2 * o + 1, 0, h)),
            pl.BlockSpec(fa.shape, lambda o, h: (0, 0, 0), pipeline_mode=pl.Buffered(1)),
            pl.BlockSpec(fb.shape, lambda o, h: (0, 0)),
        ],
        out_specs=pl.BlockSpec((1, n1, 2 * n2n, LANES), lambda o, h: (o, 0, 0, h)),
        out_shape=jax.ShapeDtypeStruct((HY_ORDER, n1, 2 * n2n, HY_W), F32),
        scratch_shapes=[pltpu.VMEM((n2n * _seq_pitch(two_n1), LANES), F32),
                        pltpu.VMEM((n2n * _seq_pitch(two_n1), LANES), F32),
                        pltpu.VMEM((length // n2n * _seq_pitch(n2n), LANES), F32),
                        pltpu.VMEM((length // n2n * _seq_pitch(n2n), LANES), F32)],
        compiler_params=_params("arbitrary", "arbitrary"),
        name="hyena_kspec",
    )(filt, filt, fa, fb)


def _hy_conv_kernel(z_ref, g_ref, wz_ref, bz_ref, wg_ref, bg_ref, fa_ref, fb_ref, fbi_ref, ga_ref, k_ref, bias_ref,
                    o_ref, a_ref, c_ref, zc_ref, gc_ref, *, z_conv):
    n2n, two_n1, half = fa_ref.shape
    n1 = two_n1 // 2
    zp, ap = _seq_pitch(n2n), _seq_pitch(two_n1)
    _to_pitched(zc_ref, _conv3_rows(z_ref[0], wz_ref[...], bz_ref[...]) if z_conv else z_ref[0], n2n)
    _to_pitched(gc_ref, _conv3_rows(g_ref[0], wg_ref[...], bg_ref[...]), n2n)
    _hy_forward_a(zc_ref, a_ref, fa_ref)
    fb = fb_ref[...]
    fbi = fbi_ref[...]

    def body_b(k1, carry):
        re = pl.ds(k1, n2n, stride=ap)
        im = pl.ds(n1 + k1, n2n, stride=ap)
        x = _dot(fb, jnp.concatenate([a_ref[re, :], a_ref[im, :]], axis=0).astype(BF16))
        kk = k_ref[0, k1]
        xr, xi = x[:n2n], x[n2n:]
        kr, ki = kk[:n2n], kk[n2n:]
        y = jnp.concatenate([xr * kr - xi * ki, xr * ki + xi * kr], axis=0).astype(BF16)
        c = _dot(fbi, y)
        c_ref[re, :] = c[:n2n]
        c_ref[im, :] = c[n2n:]
        return carry

    lax.fori_loop(0, n1, body_b, 0, unroll=HY_UNROLL)
    bias = bias_ref[0]

    def body_c(m2, carry):
        rows = pl.ds(m2, half, stride=zp)
        c = c_ref[pl.ds(pl.multiple_of(m2 * ap, SUBLANES), two_n1), :].astype(BF16)
        y = _dot(ga_ref[m2], c)
        a_ref[rows, :] = gc_ref[rows, :] * (y + zc_ref[rows, :] * bias)
        return carry

    lax.fori_loop(0, n2n, body_c, 0, unroll=HY_UNROLL)
    for i in range(half):
        o_ref[0, i * n2n:(i + 1) * n2n, :] = a_ref[i * zp:i * zp + n2n, :]


def _hy_conv(z_arr, z_blk, g_arr, g_blk, w, b, kspec, order, bias3, tables, n_lat, z_conv):
    fa, fb, fbi, ga = tables
    bsz = z_arr.shape[0]
    n2n, two_n1, half = fa.shape
    n1 = two_n1 // 2
    nh = HY_W // LANES
    zw = z_blk if z_conv else g_blk
    resident = lambda shape: pl.BlockSpec(shape, lambda h, b_: tuple(0 for _ in shape), pipeline_mode=pl.Buffered(1))
    return pl.pallas_call(
        functools.partial(_hy_conv_kernel, z_conv=z_conv),
        grid=(nh, bsz),
        in_specs=[
            pl.BlockSpec((1, n_lat, LANES), lambda h, b_: (b_, 0, z_blk + h)),
            pl.BlockSpec((1, n_lat, LANES), lambda h, b_: (b_, 0, g_blk + h)),
            pl.BlockSpec((3, LANES), lambda h, b_: (0, zw + h)),
            pl.BlockSpec((1, LANES), lambda h, b_: (0, zw + h)),
            pl.BlockSpec((3, LANES), lambda h, b_: (0, g_blk + h)),
            pl.BlockSpec((1, LANES), lambda h, b_: (0, g_blk + h)),
            resident(fa.shape), resident(fb.shape), resident(fbi.shape), resident(ga.shape),
            pl.BlockSpec((1, n1, 2 * n2n, LANES), lambda h, b_: (order, 0, 0, h), pipeline_mode=pl.Buffered(1)),
            pl.BlockSpec((1, 1, LANES), lambda h, b_: (order, 0, h)),
        ],
        out_specs=pl.BlockSpec((1, n_lat, LANES), lambda h, b_: (b_, 0, h)),
        out_shape=jax.ShapeDtypeStruct((bsz, n_lat, HY_W), F32),
        scratch_shapes=[pltpu.VMEM((n2n * _seq_pitch(two_n1), LANES), F32),
                        pltpu.VMEM((n2n * _seq_pitch(two_n1), LANES), F32),
                        pltpu.VMEM((half * _seq_pitch(n2n), LANES), F32),
                        pltpu.VMEM((half * _seq_pitch(n2n), LANES), F32)],
        compiler_params=_params("arbitrary", "arbitrary"),
        name="hyena_conv",
    )(z_arr, g_arr, w, b, w, b, fa, fb, fbi, ga, kspec, bias3)


def _hyena_ctx_kernel(u_ref, w_ref, b_ref, filt_ref, bias_ref, fd_ref, gd_ref, o_ref, *, n_ctx):
    fd = fd_ref[...]
    gd = gd_ref[...]
    nn = 2 * n_ctx
    u = _conv3_rows(u_ref[0], w_ref[...], b_ref[...])
    v, x1, x2 = u[:, :HY_W], u[:, HY_W:2 * HY_W], u[:, 2 * HY_W:]

    def conv(zin, order):
        hf = _dot(fd, filt_ref[2 * order].astype(BF16))
        hb = _dot(fd, filt_ref[2 * order + 1].astype(BF16))
        kr, ki = hf[:nn] + hb[:nn], hf[nn:] - hb[nn:]
        zz = _dot(fd, zin.astype(BF16))
        zr, zi = zz[:nn], zz[nn:]
        y = jnp.concatenate([zr * kr - zi * ki, zr * ki + zi * kr], axis=0).astype(BF16)
        return _dot(gd, y) + zin * bias_ref[order]

    y1 = x1 * conv(v, 0)
    o_ref[0] = x2 * conv(y1, 1)


def _hyena_ctx(hy, conv_w, conv_b, filt, bias, n_lat, n_ctx):
    bsz = hy.shape[0]
    nn = 2 * n_ctx
    k = np.arange(nn, dtype=np.float64)
    t = np.arange(n_ctx, dtype=np.float64)
    ang = 2 * np.pi * k[:, None] * t[None, :] / nn
    fd = jnp.asarray(np.concatenate([np.cos(ang), -np.sin(ang)], axis=0), F32).astype(BF16)
    gd = jnp.asarray(np.concatenate([np.cos(ang.T), -np.sin(ang.T)], axis=1) / nn, F32).astype(BF16)
    blk = n_lat // n_ctx
    assert blk * n_ctx == n_lat
    return pl.pallas_call(
        functools.partial(_hyena_ctx_kernel, n_ctx=n_ctx),
        grid=(bsz,),
        in_specs=[
            pl.BlockSpec((1, n_ctx, 3 * HY_W), lambda b: (b, blk, 0)),
            pl.BlockSpec((3, 3 * HY_W), lambda b: (0, 0)),
            pl.BlockSpec((1, 3 * HY_W), lambda b: (0, 0)),
            pl.BlockSpec((2 * HY_ORDER, n_ctx, HY_W), lambda b: (0, 0, 0)),
            pl.BlockSpec((HY_ORDER, 1, HY_W), lambda b: (0, 0, 0)),
            pl.BlockSpec((2 * nn, n_ctx), lambda b: (0, 0)),
            pl.BlockSpec((n_ctx, 2 * nn), lambda b: (0, 0)),
        ],
        out_specs=pl.BlockSpec((1, n_ctx, HY_W), lambda b: (b, 0, 0)),
        out_shape=jax.ShapeDtypeStruct((bsz, n_ctx, HY_W), F32),
        compiler_params=_params("arbitrary"),
        name="hyena_ctx",
    )(hy, conv_w, conv_b.reshape(1, -1), filt, bias, fd, gd)


def _hyena_latent(hy, conv_w, conv_b, filt, bias, n_lat, tables):
    kspec = _hy_kspec(filt, tables)
    bias3 = bias.reshape(HY_ORDER, 1, HY_W)
    cb = conv_b.reshape(1, -1)
    nb = HY_W // LANES
    y1 = _hy_conv(hy, 0, hy, nb, conv_w, cb, kspec, 0, bias3, tables, n_lat, True)
    return _hy_conv(y1, 0, hy, 2 * nb, conv_w, cb, kspec, 1, bias3, tables, n_lat, False)


def _outproj_kernel(x_ref, att_ref, ssd_ref, hyl_ref, hyc_ref, gt_ref, wa_ref, ws_ref, wh_ref, o_ref,
                    *, n_lat_tiles):
    j = pl.program_id(0)
    hy = jnp.where(j >= n_lat_tiles, hyc_ref[0], hyl_ref[0]).astype(BF16)
    mix = _dot(att_ref[0], wa_ref[...]) + _dot(ssd_ref[0], ws_ref[...]) + _dot(hy, wh_ref[...])
    o_ref[0] = x_ref[0] + gt_ref[0] * mix


def _outproj(x, att, ssd, hy_l, hy_c, mods, w_out, n_lat_tiles, n_tiles):
    bsz, t, d = x.shape
    tm = TOKEN_TILE

    def mod_row(j, b):
        return jnp.where(j >= n_lat_tiles, bsz, b)

    tok = lambda w: pl.BlockSpec((1, tm, w), lambda j, b: (b, j, 0))
    return pl.pallas_call(
        functools.partial(_outproj_kernel, n_lat_tiles=n_lat_tiles),
        grid=(n_tiles, bsz),
        in_specs=[
            tok(d), tok(ATT_W), tok(SSD_W),
            pl.BlockSpec((1, tm, HY_W), lambda j, b: (b, jnp.minimum(j, n_lat_tiles - 1), 0)),
            pl.BlockSpec((1, tm, HY_W), lambda j, b: (b, 0, 0)),
            pl.BlockSpec((1, 1, d), lambda j, b: (mod_row(j, b), 0, 2)),
            pl.BlockSpec((ATT_W, d), lambda j, b: (0, 0)),
            pl.BlockSpec((SSD_W, d), lambda j, b: (0, 0)),
            pl.BlockSpec((HY_W, d), lambda j, b: (0, 0)),
        ],
        out_specs=tok(d),
        out_shape=jax.ShapeDtypeStruct((bsz, n_tiles * tm, d), F32),
        compiler_params=_params("arbitrary", "arbitrary"),
        name="outproj",
    )(x, att, ssd, hy_l, hy_c, mods, w_out[:ATT_W], w_out[ATT_W:ATT_W + SSD_W], w_out[ATT_W + SSD_W:])


def _route(logits_t, bias_col):
    scores = jax.nn.sigmoid(logits_t)
    sel = scores + bias_col
    neg_inf = jnp.float32(-jnp.inf)
    rows = [sel[e:e + 1, :] for e in range(N_EXPERTS)]
    grp = []
    for g in range(N_GROUPS):
        r = rows[g * EXPERTS_PER_GROUP:(g + 1) * EXPERTS_PER_GROUP]
        top = functools.reduce(jnp.maximum, r)
        taken = None
        rest = []
        for ri in r:
            is_top = (ri == top) if taken is None else (ri == top) & jnp.logical_not(taken)
            rest.append(jnp.where(is_top, neg_inf, ri))
            taken = is_top if taken is None else taken | is_top
        grp.append(top + functools.reduce(jnp.maximum, rest))
    best = jnp.zeros(grp[0].shape, jnp.int32)
    cur = grp[0]
    for g in range(1, N_GROUPS):
        upd = grp[g] > cur
        best = jnp.where(upd, g, best)
        cur = jnp.where(upd, grp[g], cur)
    picked = []
    for e in range(N_EXPERTS):
        g, i = divmod(e, EXPERTS_PER_GROUP)
        rank = jnp.zeros(best.shape, jnp.int32)
        for j in range(EXPERTS_PER_GROUP):
            if j == i:
                continue
            other = rows[g * EXPERTS_PER_GROUP + j]
            ahead = (other > rows[e]) | ((other == rows[e]) & (j < i))
            rank = rank + ahead.astype(jnp.int32)
        keep = (best == g) & (rank < 2)
        picked.append(jnp.where(keep, scores[e:e + 1, :], 0.0))
    total = functools.reduce(lambda u, w: u + w, picked)
    return jnp.concatenate(picked, axis=0) / total, best


def _ffn_input(x_ref, sh_ref, sc_ref, g_ref):
    x = x_ref[...].reshape(MOE_TILE, D_MODEL)
    ms = jnp.mean(x * x, axis=-1, keepdims=True)
    t = x * lax.rsqrt(ms + EPS) * g_ref[...]
    return x, t * (1.0 + sc_ref[0]) + sh_ref[0]


def _route_kernel(x_ref, sh_ref, sc_ref, g_ref, wr_ref, rb_ref, rt_ref, srow_ref, tbl_ref):
    tm = MOE_TILE
    _, t = _ffn_input(x_ref, sh_ref, sc_ref, g_ref)
    logits = jnp.dot(t, wr_ref[...], precision=lax.Precision.HIGHEST, preferred_element_type=F32)
    comb_t, best = _route(logits.T[:N_EXPERTS, :], rb_ref[...])

    member = [jnp.where(best == g, 1.0, 0.0) for g in range(N_GROUPS)]
    comb4 = functools.reduce(
        lambda u, w: u + w,
        [member[g] * comb_t[g * EXPERTS_PER_GROUP:(g + 1) * EXPERTS_PER_GROUP, :] for g in range(N_GROUPS)])
    masks = jnp.concatenate(member + [jnp.zeros((8 - N_GROUPS, tm), F32)], axis=0)
    earlier = jnp.where(lax.broadcasted_iota(jnp.int32, (tm, tm), 0) < lax.broadcasted_iota(jnp.int32, (tm, tm), 1),
                        1.0, 0.0).astype(BF16)
    ranks = _dot(masks.astype(BF16), earlier)
    cnt = jnp.sum(masks, axis=1, keepdims=True)
    padded = jnp.ceil(cnt * (1.0 / MOE_CHUNK)) * MOE_CHUNK
    lane = lax.broadcasted_iota(jnp.int32, (1, LANES), 1).astype(F32) * MOE_CHUNK
    start = jnp.zeros((1, 1), F32)
    slot = jnp.zeros((1, tm), F32)
    gid = jnp.zeros((1, LANES), jnp.int32)
    for g in range(N_GROUPS):
        slot = slot + member[g] * (start + ranks[g:g + 1, :])
        start = start + padded[g:g + 1, :]
        gid = gid + jnp.where(lane >= start, 1, 0)
    tbl_ref[0] = gid
    srow_ref[0] = slot.astype(jnp.int32)
    rt_t = jnp.concatenate([comb4, slot, jnp.zeros((LANES - EXPERTS_PER_GROUP - 1, tm), F32)], axis=0)
    rt_ref[...] = rt_t.T


def _experts_kernel(tbl_ref, x_ref, sh_ref, sc_ref, gt_ref, g_ref, rt_ref, srow_ref, wgu_ref, wd_ref, gf_ref,
                    *rest, final, tile_of, n_grid):
    o_ref, xp_ref, yp_ref = rest[-3:]
    tm, ch = MOE_TILE, MOE_CHUNK
    n_slots = MOE_CHUNKS * ch
    tile = tile_of(*[pl.program_id(a) for a in range(n_grid)])
    x, t = _ffn_input(x_ref, sh_ref, sc_ref, g_ref)
    rt = rt_ref[...]
    slot_col = rt[:, EXPERTS_PER_GROUP:EXPERTS_PER_GROUP + 1].astype(jnp.int32)
    gather = jnp.where(lax.broadcasted_iota(jnp.int32, (n_slots, tm), 0) == srow_ref[0], 1.0, 0.0).astype(BF16)
    scatter = jnp.where(lax.broadcasted_iota(jnp.int32, (tm, n_slots), 1) == slot_col, 1.0, 0.0).astype(BF16)
    xp_ref[...] = _dot(gather, t.astype(BF16)).astype(BF16)
    w_slot = _dot3_right(gather, rt)

    for c in range(MOE_CHUNKS):
        rows = slice(c * ch, (c + 1) * ch)
        grp = tbl_ref[tile, c]

        @pl.when(grp < N_GROUPS)
        def _():
            xc = xp_ref[rows, :]
            acc = jnp.zeros((ch, D_MODEL), F32)
            for j in range(EXPERTS_PER_GROUP):
                e = grp * EXPERTS_PER_GROUP + j
                gu = _dot(xc, wgu_ref[e])
                hid = _silu(gu[:, :D_FF]) * gu[:, D_FF:] * w_slot[rows, j:j + 1]
                acc = acc + _dot(hid.astype(BF16), wd_ref[e])
            yp_ref[rows, :] = acc.astype(BF16)

        @pl.when(grp >= N_GROUPS)
        def _():
            yp_ref[rows, :] = jnp.zeros((ch, D_MODEL), BF16)

    y = x + gt_ref[0] * _dot(scatter, yp_ref[...])
    if final:
        y = y * lax.rsqrt(jnp.mean(y * y, axis=-1, keepdims=True) + EPS) * gf_ref[...]
    o_ref[...] = y.reshape(o_ref.shape)


def _moe_tiles(x, prev, mods, mod_row, g_ffn, w_router, router_bias, wgu, wd, g_final, final, out_rows,
               grid, x_block, x_index, tile_of, n_tiles):
    bsz, _, d = x.shape
    tm = MOE_TILE
    ng = len(grid)
    const = lambda shape: pl.BlockSpec(shape, lambda *a: tuple(0 for _ in shape))
    mod = lambda col: pl.BlockSpec((1, 1, d), lambda *a: (mod_row(*a[:ng]), 0, col))
    xspec = pl.BlockSpec(x_block, lambda *a: x_index(*a[:ng]))
    rt, srow, tbl = pl.pallas_call(
        _route_kernel,
        grid=grid,
        in_specs=[xspec, mod(3), mod(4), const((1, d)), const((d, LANES)), const((N_EXPERTS, 1))],
        out_specs=[
            pl.BlockSpec((tm, LANES), lambda *a: (tile_of(*a), 0)),
            pl.BlockSpec((1, 1, tm), lambda *a: (tile_of(*a), 0, 0)),
            pl.BlockSpec((1, 1, LANES), lambda *a: (tile_of(*a), 0, 0)),
        ],
        out_shape=[
            jax.ShapeDtypeStruct((n_tiles * tm, LANES), F32),
            jax.ShapeDtypeStruct((n_tiles, 1, tm), jnp.int32),
            jax.ShapeDtypeStruct((n_tiles, 1, LANES), jnp.int32),
        ],
        compiler_params=_params(*["arbitrary"] * ng),
        name="moe_route",
    )(x, mods, mods, g_ffn, w_router, router_bias)

    resident = lambda shape: pl.BlockSpec(shape, lambda *a: tuple(0 for _ in shape), pipeline_mode=pl.Buffered(1))
    in_specs = [
        xspec, mod(3), mod(4), mod(5), const((1, d)),
        pl.BlockSpec((tm, LANES), lambda *a: (tile_of(*a[:ng]), 0)),
        pl.BlockSpec((1, 1, tm), lambda *a: (tile_of(*a[:ng]), 0, 0)),
        resident((N_EXPERTS, d, 2 * D_FF)),
        resident((N_EXPERTS, D_FF, d)),
        const((1, d)),
    ]
    args = [tbl.reshape(n_tiles, LANES), x, mods, mods, mods, g_ffn, rt, srow, wgu, wd, g_final]
    aliases = {}
    if prev is not None:
        in_specs.append(pl.BlockSpec(memory_space=pl.ANY))
        args.append(prev)
        aliases = {len(args) - 1: 0}
    return pl.pallas_call(
        functools.partial(_experts_kernel, final=final, tile_of=tile_of, n_grid=ng),
        grid_spec=pltpu.PrefetchScalarGridSpec(
            num_scalar_prefetch=1,
            grid=grid,
            in_specs=in_specs,
            out_specs=xspec,
            scratch_shapes=[pltpu.VMEM((MOE_CHUNKS * MOE_CHUNK, d), BF16), pltpu.VMEM((MOE_CHUNKS * MOE_CHUNK, d), BF16)],
        ),
        out_shape=jax.ShapeDtypeStruct((bsz, out_rows, d), F32),
        input_output_aliases=aliases,
        compiler_params=_params(*["arbitrary"] * ng),
        name="moe_experts",
    )(*args)


def _moe(x, mods, g_ffn, w_router, router_bias, wgu, wd, g_final, n_lat, n_ctx, with_ctx, final):
    bsz, _, d = x.shape
    tm = MOE_TILE
    per_b = n_lat // tm
    common = (g_ffn, w_router, router_bias, wgu, wd, g_final, final)
    out = _moe_tiles(x, None, mods, lambda b, j: b, *common, n_lat + (n_ctx if with_ctx else 0),
                     (bsz, per_b), (1, tm, d), lambda b, j: (b, j, 0), lambda b, j: b * per_b + j, bsz * per_b)
    if with_ctx:
        nb = tm // n_ctx
        out = _moe_tiles(x, out, mods, lambda i: bsz, *common, n_lat + n_ctx,
                         (bsz // nb,), (nb, n_ctx, d), lambda i: (i, n_lat // n_ctx, 0), lambda i: i, bsz // nb)
    return out


def _block_ones(width, block):
    idx = np.arange(width) // block
    return jnp.asarray(idx[:, None] == idx[None, :], F32).astype(BF16)


def _rope_tables(n_lat, n_ctx):
    rows = n_lat // GRID_W
    row = np.repeat(np.arange(rows), GRID_W).astype(np.float64)
    col = np.tile(np.arange(GRID_W), rows).astype(np.float64)
    inv = ROPE_THETA ** (-np.arange(0, ROPE_AXIS_DIM, 2, dtype=np.float64) / ROPE_AXIS_DIM)
    ang = np.concatenate([row[:, None] * inv, col[:, None] * inv], axis=-1)
    ang = np.concatenate([ang, np.zeros((n_ctx, ang.shape[1]))], axis=0)
    cos = np.concatenate([np.cos(ang), np.cos(ang)], axis=-1)
    sin = np.concatenate([-np.sin(ang), np.sin(ang)], axis=-1)
    scale = HEAD_DIM ** -0.5
    cs = np.tile(cos, (1, ATT_HEADS)) * scale
    sn = np.tile(sin, (1, ATT_HEADS)) * scale
    return jnp.asarray(cs, F32), jnp.asarray(sn, F32)


def _pack_w_in(w):
    cuts = np.cumsum([ATT_W, KV_W, KV_W, SSD_W, SSD_CONV_CH, 2 * SSD_HEADS])
    q, k, v, z, xbc, dt, hy = jnp.split(w, [int(c) for c in cuts], axis=-1)
    dt = jnp.pad(dt, ((0, 0), (0, LANES - 2 * SSD_HEADS)))
    return jnp.concatenate([q, k, v, z, xbc, dt, hy], axis=-1).astype(BF16)


def kernel(x, c, ctx, c_ctx, w_mod, b_mod, g_mix, g_ffn, w_in, q_norm, k_norm, ssd_conv_w, ssd_conv_b,
           ssd_dt_bias, ssd_a_log, ssd_d, ssd_norm, hy_conv_w, hy_conv_b, hy_w1, hy_b1, hy_freq, hy_w2, hy_b2,
           hy_w3, hy_bias, w_out, w_router, router_bias, w_gate, w_up, w_down, g_final):
    bsz, n_lat, d = x.shape
    n_ctx = ctx.shape[1]
    depth = w_mod.shape[0]
    t = n_lat + n_ctx
    tm = TOKEN_TILE
    n_lat_tiles = n_lat // tm
    n_tiles = t // tm
    assert n_ctx == tm and n_lat % (HY_N2 * 8) == 0 and bsz < MOD_ROWS

    cvec = jnp.concatenate([c, c_ctx[None], jnp.zeros((MOD_ROWS - bsz - 1, d), F32)], axis=0)
    mods_all = _adaln(cvec, w_mod, b_mod)

    cs, sn = _rope_tables(n_lat, n_ctx)
    bd_head = _block_ones(ATT_W, HEAD_DIM)
    bd_ssd = _block_ones(SSD_W, SSD_W // SSD_GROUPS)
    tables = _dft_tables(n_lat)
    pad_row = lambda v: jnp.pad(v.reshape(1, -1), ((0, 0), (0, LANES - v.size)))

    xs = jnp.concatenate([x, ctx], axis=1)
    for i in range(depth):
        last = i == depth - 1
        mods = mods_all[i].reshape(MOD_ROWS, 1, 6 * d)
        q, kt, v, z, xbc, dt_raw, hy = _inproj(
            xs, mods, g_mix[i].reshape(1, d), _pack_w_in(w_in[i]),
            jnp.tile(q_norm[i], ATT_HEADS).reshape(1, ATT_W), jnp.tile(k_norm[i], ATT_KV_HEADS).reshape(1, KV_W),
            cs, sn, bd_head, n_lat_tiles)

        att = _attention(q, kt, v, 0, n_lat_tiles, 0, n_tiles)
        if not last:
            att_c = _attention(q, kt, v, n_lat_tiles, 1, n_lat_tiles, 1)
            att = jnp.concatenate([att, att_c], axis=1)

        xbc_c = _dwconv(xbc, ssd_conv_w[i], ssd_conv_b[i], n_lat, True, 1)
        ssd = _ssd(xbc_c, z, dt_raw, pad_row(ssd_dt_bias[i]), pad_row(ssd_a_log[i]),
                   jnp.repeat(ssd_d[i], SSD_HEAD_DIM).reshape(1, SSD_W), ssd_norm[i].reshape(1, SSD_W),
                   bd_ssd, n_lat // SSD_CHUNK)

        hyp = (hy_w1[i], hy_b1[i], hy_freq[i], hy_w2[i], hy_b2[i], hy_w3[i])
        hy_l = _hyena_latent(hy, hy_conv_w[i], hy_conv_b[i], _hyena_filters(n_lat, *hyp), hy_bias[i], n_lat, tables)
        if not last:
            hy_c = _hyena_ctx(hy, hy_conv_w[i], hy_conv_b[i], _hyena_filters(n_ctx, *hyp),
                              hy_bias[i].reshape(HY_ORDER, 1, HY_W), n_lat, n_ctx)
        else:
            hy_c = jnp.zeros((bsz, n_ctx, HY_W), F32)

        live_tiles = n_lat_tiles if last else n_tiles
        xs = _outproj(xs, att, ssd, hy_l, hy_c, mods, w_out[i].astype(BF16), n_lat_tiles, live_tiles)

        wgu = jnp.concatenate([w_gate[i], w_up[i]], axis=-1).astype(BF16)
        xs = _moe(xs, mods, g_ffn[i].reshape(1, d), jnp.pad(w_router, ((0, 0), (0, LANES - N_EXPERTS))),
                  router_bias.reshape(N_EXPERTS, 1), wgu, w_down[i].astype(BF16), g_final.reshape(1, d),
                  n_lat, n_ctx, not last, last)
    return xs
```

```python
import functools
import math

import jax
import jax.numpy as jnp
import numpy as np
from jax import lax
from jax.experimental import pallas as pl
from jax.experimental.pallas import tpu as pltpu

F32 = jnp.float32
BF16 = jnp.bfloat16

D_MODEL = 1024
GRID_W = 64
EPS = 1e-6

ATT_HEADS = 6
ATT_KV_HEADS = 2
HEAD_DIM = 64
ATT_W = ATT_HEADS * HEAD_DIM
KV_W = ATT_KV_HEADS * HEAD_DIM
ROPE_AXIS_DIM = HEAD_DIM // 2
ROPE_THETA = 10000.0
Q_SCALE = HEAD_DIM ** -0.5 * math.log2(math.e)

SSD_HEADS = 6
SSD_HEAD_DIM = 64
SSD_W = SSD_HEADS * SSD_HEAD_DIM
SSD_GROUPS = 2
SSD_STATE = 64
SSD_CHUNK = 128
SSD_CONV_CH = SSD_W + 2 * SSD_GROUPS * SSD_STATE
SSD_HEADS_PER_GROUP = SSD_HEADS // SSD_GROUPS
SSD_BLOCK = 2

HY_W = 256
HY_ORDER = 2
HY_BANDS = 16
HY_POS_DIM = 1 + 2 * HY_BANDS
HY_FILTER_HID = 64
HY_FAST_DECAY = 0.3
HY_SLOW_DECAY = 1.5
HY_TARGET = 1e-2
HY_N2 = 64
HY_UNROLL = 8

MIX_W = ATT_W + SSD_W + HY_W
N_EXPERTS = 16
N_GROUPS = 4
EXPERTS_PER_GROUP = N_EXPERTS // N_GROUPS
D_FF = 256

LANES = 128
SUBLANES = 8
TOKEN_TILE = 256
ATT_TILE = 512
ATT_SUB = 512
MOE_TILE = 512
MOE_CHUNK = 128
MOE_CHUNKS = MOE_TILE // MOE_CHUNK + N_GROUPS - 1
MOD_ROWS = 8
VMEM_LIMIT = 56 * 1024 * 1024

COL_Q = 0
COL_K = COL_Q + ATT_W
COL_V = COL_K + KV_W
COL_Z = COL_V + KV_W
COL_XBC = COL_Z + SSD_W
COL_DT = COL_XBC + SSD_CONV_CH
COL_HY = COL_DT + LANES
COL_END = COL_HY + 3 * HY_W


def _params(*sem):
    return pltpu.CompilerParams(dimension_semantics=sem, vmem_limit_bytes=VMEM_LIMIT)


def _silu(x):
    return x * jax.nn.sigmoid(x)


def _softplus(x):
    return jnp.maximum(x, 0.0) + jnp.log1p(jnp.exp(-jnp.abs(x)))


def _split3(x):
    hi = x.astype(BF16)
    r1 = x - hi.astype(F32)
    mid = r1.astype(BF16)
    lo = (r1 - mid.astype(F32)).astype(BF16)
    return hi, mid, lo


def _dot(a, b):
    return jnp.dot(a, b, preferred_element_type=F32)


def _dot3_right(m_bf16, x):
    hi, mid, lo = _split3(x)
    return _dot(m_bf16, hi) + _dot(m_bf16, mid) + _dot(m_bf16, lo)


def _dot3_left(x, m_bf16):
    hi, mid, lo = _split3(x)
    return _dot(hi, m_bf16) + _dot(mid, m_bf16) + _dot(lo, m_bf16)


def _adaln_kernel(c_ref, w_ref, b_ref, o_ref):
    s = _silu(c_ref[...]).astype(BF16)
    o_ref[0] = _dot(s, w_ref[0].astype(BF16)) + b_ref[0]


def _adaln(cvec, w_mod, b_mod):
    depth, d, n = w_mod.shape
    bn = n // 4
    return pl.pallas_call(
        _adaln_kernel,
        grid=(depth, n // bn),
        in_specs=[
            pl.BlockSpec((MOD_ROWS, d), lambda i, j: (0, 0)),
            pl.BlockSpec((1, d, bn), lambda i, j: (i, 0, j)),
            pl.BlockSpec((1, 1, bn), lambda i, j: (i, 0, j)),
        ],
        out_specs=pl.BlockSpec((1, MOD_ROWS, bn), lambda i, j: (i, 0, j)),
        out_shape=jax.ShapeDtypeStruct((depth, MOD_ROWS, n), F32),
        compiler_params=_params("arbitrary", "arbitrary"),
        name="adaln",
    )(cvec, w_mod, b_mod.reshape(depth, 1, n))


def _head_rms_rope(x, bd, gain, cs, sn):
    width = x.shape[-1]
    ms = _dot((x * x).astype(BF16), bd) * (1.0 / HEAD_DIM)
    xn = x * lax.rsqrt(ms + EPS) * gain
    lane = lax.broadcasted_iota(jnp.int32, xn.shape, 1)
    first_half = (lane % HEAD_DIM) < (HEAD_DIM // 2)
    partner = jnp.where(first_half,
                        pltpu.roll(xn, width - HEAD_DIM // 2, 1),
                        pltpu.roll(xn, HEAD_DIM // 2, 1))
    return xn * cs + partner * sn


def _stream_specs(n_lat_tiles, ctx_blk, d):
    lat = pl.BlockSpec((1, TOKEN_TILE, d), lambda j, b: (b, jnp.minimum(j, n_lat_tiles - 1), 0))
    ctx = pl.BlockSpec((1, TOKEN_TILE, d), lambda j, b: (jnp.where(j >= n_lat_tiles, b, 0), ctx_blk, 0))
    return lat, ctx


def _stream_tile(xl_ref, xc_ref, n_lat_tiles):
    return jnp.where(pl.program_id(0) >= n_lat_tiles, xc_ref[0], xl_ref[0])


def _inproj_kernel(xl_ref, xc_ref, sh_ref, sc_ref, g_ref, w_ref, gq_ref, gk_ref, cs_ref, sn_ref, bd_ref,
                   qt_ref, k_ref, vt_ref, z_ref, xbc_ref, dt_ref, hy_ref, *, n_lat_tiles):
    x = _stream_tile(xl_ref, xc_ref, n_lat_tiles)
    ms = jnp.mean(x * x, axis=-1, keepdims=True)
    h = x * lax.rsqrt(ms + EPS) * g_ref[...]
    h = h * (1.0 + sc_ref[0]) + sh_ref[0]
    acc = _dot(h.astype(BF16), w_ref[...])

    cs = cs_ref[...]
    sn = sn_ref[...]
    bd = bd_ref[...]
    q = _head_rms_rope(acc[:, COL_Q:COL_K], bd, gq_ref[...], cs, sn)
    qt_ref[0] = q.T.astype(qt_ref.dtype)
    k = _head_rms_rope(acc[:, COL_K:COL_V], bd[:KV_W, :KV_W], gk_ref[...],
                       cs[:, :KV_W] * (1.0 / Q_SCALE), sn[:, :KV_W] * (1.0 / Q_SCALE))
    k_ref[0] = k.astype(k_ref.dtype)
    vt_ref[0] = acc[:, COL_V:COL_Z].T.astype(vt_ref.dtype)
    z_ref[0] = acc[:, COL_Z:COL_XBC]
    xbc_ref[0] = acc[:, COL_XBC:COL_DT]
    dt_ref[0] = acc[:, COL_DT:COL_HY]
    hy_ref[0] = acc[:, COL_HY:COL_END]


def _inproj(x_lat, x_ctx, ctx_blk, mods, g_mix, w_cat, gq, gk, cs, sn, bd, n_lat_tiles):
    bsz, _, d = x_lat.shape
    tm = TOKEN_TILE
    nt = n_lat_tiles + 1
    t = nt * tm

    def mod_row(j, b):
        return jnp.where(j >= n_lat_tiles, bsz, b)

    tok = lambda w: pl.BlockSpec((1, tm, w), lambda j, b: (b, j, 0))
    const = lambda shape: pl.BlockSpec(shape, lambda j, b: tuple(0 for _ in shape))
    outs = pl.pallas_call(
        functools.partial(_inproj_kernel, n_lat_tiles=n_lat_tiles),
        grid=(nt, bsz),
        in_specs=[
            *_stream_specs(n_lat_tiles, ctx_blk, d),
            pl.BlockSpec((1, 1, d), lambda j, b: (mod_row(j, b), 0, 0)),
            pl.BlockSpec((1, 1, d), lambda j, b: (mod_row(j, b), 0, 1)),
            const((1, d)),
            const((d, COL_END)),
            const((1, ATT_W)),
            const((1, KV_W)),
            pl.BlockSpec((tm, ATT_W), lambda j, b: (j, 0)),
            pl.BlockSpec((tm, ATT_W), lambda j, b: (j, 0)),
            const((ATT_W, ATT_W)),
        ],
        out_specs=[
            pl.BlockSpec((1, ATT_W, tm), lambda j, b: (b, 0, j)),
            tok(KV_W),
            pl.BlockSpec((1, KV_W, tm), lambda j, b: (b, 0, j)),
            tok(SSD_W),
            tok(SSD_CONV_CH),
            tok(LANES),
            tok(3 * HY_W),
        ],
        out_shape=[
            jax.ShapeDtypeStruct((bsz, ATT_W, t), BF16),
            jax.ShapeDtypeStruct((bsz, t, KV_W), BF16),
            jax.ShapeDtypeStruct((bsz, KV_W, t), BF16),
            jax.ShapeDtypeStruct((bsz, t, SSD_W), F32),
            jax.ShapeDtypeStruct((bsz, t, SSD_CONV_CH), F32),
            jax.ShapeDtypeStruct((bsz, t, LANES), F32),
            jax.ShapeDtypeStruct((bsz, t, 3 * HY_W), F32),
        ],
        compiler_params=_params("arbitrary", "arbitrary"),
        name="inproj",
    )(x_lat, x_ctx, mods, mods, g_mix, w_cat, gq, gk, cs, sn, bd)
    return outs


def _attn_kernel(qt_ref, k_ref, vt_ref, o_ref, st_ref, pt_ref):
    k = k_ref[0]
    vt = vt_ref[0]
    rep = ATT_HEADS // ATT_KV_HEADS
    tq = qt_ref.shape[2]
    sub = min(tq, ATT_SUB)
    ones = jnp.ones((2 * SUBLANES, vt.shape[1]), vt.dtype)
    vtg = [jnp.concatenate([vt[g * HEAD_DIM:(g + 1) * HEAD_DIM, :], ones], axis=0) for g in range(ATT_KV_HEADS)]
    units = [(c0, hd) for c0 in range(0, tq, sub) for hd in range(ATT_HEADS)]

    def scores(u):
        c0, hd = units[u]
        qh = qt_ref[0, hd * HEAD_DIM:(hd + 1) * HEAD_DIM, c0:c0 + sub]
        zero = jnp.zeros_like(qh)
        w = jnp.concatenate([qh, zero] if hd < rep else [zero, qh], axis=0)
        st_ref[u % 2] = _dot(k, w)

    scores(0)
    outs = []
    for u, (c0, hd) in enumerate(units):
        if u + 1 < len(units):
            scores(u + 1)
        st = st_ref[u % 2]
        m = jnp.max(st, axis=0, keepdims=True)
        pt_ref[u % 2] = jnp.exp2(st - m).astype(BF16)
        ot = _dot(vtg[hd // rep], pt_ref[u % 2])
        outs.append(ot[:HEAD_DIM] / ot[HEAD_DIM:HEAD_DIM + 1])
        if hd == ATT_HEADS - 1:
            o_ref[0, c0:c0 + sub, :] = jnp.concatenate(outs, axis=0).T.astype(o_ref.dtype)
            outs = []


def _attention(qt, k, vt, q_row0, n_q, k_row0, n_k, tq):
    bsz = qt.shape[0]
    kblk = k_row0 // n_k
    q_tile0 = q_row0 // tq
    n_q_tiles = n_q // tq
    assert kblk * n_k == k_row0 and q_tile0 * tq == q_row0 and n_q_tiles * tq == n_q
    assert ATT_KV_HEADS == 2
    return pl.pallas_call(
        _attn_kernel,
        grid=(bsz, n_q_tiles),
        in_specs=[
            pl.BlockSpec((1, ATT_W, tq), lambda b, j: (b, 0, q_tile0 + j)),
            pl.BlockSpec((1, n_k, KV_W), lambda b, j: (b, kblk, 0)),
            pl.BlockSpec((1, KV_W, n_k), lambda b, j: (b, 0, kblk)),
        ],
        out_specs=pl.BlockSpec((1, tq, ATT_W), lambda b, j: (b, j, 0)),
        out_shape=jax.ShapeDtypeStruct((bsz, n_q_tiles * tq, ATT_W), BF16),
        scratch_shapes=[pltpu.VMEM((2, n_k, min(tq, ATT_SUB)), F32), pltpu.VMEM((2, n_k, min(tq, ATT_SUB)), BF16)],
        compiler_params=_params("arbitrary", "arbitrary"),
        name="attention",
    )(qt, k, vt)


def _dwconv_kernel(u_ref, w_ref, b_ref, o_ref, *, n_lat, act):
    u = u_ref[0]
    t = u.shape[0]
    row = lax.broadcasted_iota(jnp.int32, u.shape, 0)
    prev = jnp.where((row == 0) | (row == n_lat), 0.0, pltpu.roll(u, 1, 0))
    nxt = jnp.where((row == n_lat - 1) | (row == t - 1), 0.0, pltpu.roll(u, t - 1, 0))
    w = w_ref[...]
    y = prev * w[0:1] + u * w[1:2] + nxt * w[2:3] + b_ref[...]
    if act:
        y = _silu(y)
    o_ref[0, 0] = y


def _dwconv(u, w, b, n_lat, act, parts):
    bsz, t, c = u.shape
    cpp = c // parts
    nb = cpp // LANES
    return pl.pallas_call(
        functools.partial(_dwconv_kernel, n_lat=n_lat, act=act),
        grid=(bsz, parts * nb),
        in_specs=[
            pl.BlockSpec((1, t, LANES), lambda b_, j: (b_, 0, j)),
            pl.BlockSpec((3, LANES), lambda b_, j: (0, j)),
            pl.BlockSpec((1, LANES), lambda b_, j: (0, j)),
        ],
        out_specs=pl.BlockSpec((1, 1, t, LANES), lambda b_, j: (j // nb, b_, 0, j % nb)),
        out_shape=jax.ShapeDtypeStruct((parts, bsz, t, cpp), F32),
        compiler_params=_params("arbitrary", "arbitrary"),
        name="dwconv",
    )(u, w, b.reshape(1, c))


def _ssd_kernel(xbc_ref, z_ref, dt_ref, dtb_ref, alog_ref, dsk_ref, nw_ref, bd_ref,
                o_ref, hf_ref, hb_ref, hbe_ref, *, n_lat_blocks):
    q = SSD_CHUNK
    nh = SSD_HEADS
    phase = pl.program_id(1)
    step = pl.program_id(2)
    block = jnp.where(phase == 0, n_lat_blocks - step, jnp.where(step < 1, n_lat_blocks, step - 1))

    lane = lax.broadcasted_iota(jnp.int32, (1, LANES), 1)
    a_row = jnp.where(lane < 2 * nh, -jnp.exp(alog_ref[...]), 0.0)
    tt = lax.broadcasted_iota(jnp.int32, (q, q), 0)
    ss = lax.broadcasted_iota(jnp.int32, (q, q), 1)
    lower = (ss <= tt)
    upper = (ss >= tt)
    lmat = jnp.where(lower, 1.0, 0.0).astype(BF16)
    umat = jnp.where(upper, 1.0, 0.0).astype(BF16)

    @pl.when(step == 0)
    def _():
        hf_ref[...] = jnp.zeros_like(hf_ref)
        hb_ref[...] = jnp.zeros_like(hb_ref)

    def load_chunk(ci):
        rows = slice(ci * q, (ci + 1) * q)
        xbc = xbc_ref[0, 0, rows, :]
        dt = _softplus(dt_ref[0, rows, :] + dtb_ref[...])
        return rows, xbc, dt, dt * a_row

    def backward_chunk(ci):
        _, xbc, dt, a = load_chunk(ci)
        chunk = block * SSD_BLOCK + ci
        x = xbc[:, :SSD_W]
        bt = xbc[:, SSD_W:SSD_W + SSD_GROUPS * SSD_STATE].T
        a_t = a.T
        dt_t = dt.T
        suf_t = _dot3_left(a_t, lmat)
        for hd in range(nh):
            g = hd // SSD_HEADS_PER_GROUP
            col = nh + hd
            row_b = suf_t[col:col + 1, :]
            total = row_b[:, 0:1]
            w_t = jnp.exp(total - row_b) * dt_t[col:col + 1, :]
            btg = bt[g * SSD_STATE:(g + 1) * SSD_STATE, :]
            xh = x[:, hd * SSD_HEAD_DIM:(hd + 1) * SSD_HEAD_DIM].astype(BF16)
            prev = hb_ref[hd]
            hbe_ref[chunk, hd] = prev
            hb_ref[hd] = prev * jnp.exp(total) + _dot((btg * w_t).astype(BF16), xh)

    def forward_chunk(ci):
        rows, xbc, dt, a = load_chunk(ci)
        chunk = block * SSD_BLOCK + ci
        x = xbc[:, :SSD_W]
        bt = xbc[:, SSD_W:SSD_W + SSD_GROUPS * SSD_STATE].T
        cmat = xbc[:, SSD_W + SSD_GROUPS * SSD_STATE:]
        a_t = a.T
        dt_t = dt.T
        pre = _dot3_right(lmat, a)
        suf = _dot3_right(umat, a)
        pre_t = _dot3_left(a_t, umat)
        suf_t = _dot3_left(a_t, lmat)
        neg_inf = jnp.float32(-jnp.inf)
        ys = []
        for g in range(SSD_GROUPS):
            cg = cmat[:, g * SSD_STATE:(g + 1) * SSD_STATE].astype(BF16)
            btg = bt[g * SSD_STATE:(g + 1) * SSD_STATE, :]
            cb = _dot(cg, btg.astype(BF16))
            for r in range(SSD_HEADS_PER_GROUP):
                hd = g * SSD_HEADS_PER_GROUP + r
                colf = pre[:, hd:hd + 1]
                rowf = pre_t[hd:hd + 1, :]
                colb = suf[:, nh + hd:nh + hd + 1]
                rowb = suf_t[nh + hd:nh + hd + 1, :]
                wf = jnp.exp(jnp.where(lower, colf - rowf, neg_inf)) * dt_t[hd:hd + 1, :]
                wb = jnp.exp(jnp.where(upper, colb - rowb, neg_inf)) * dt_t[nh + hd:nh + hd + 1, :]
                xh = x[:, hd * SSD_HEAD_DIM:(hd + 1) * SSD_HEAD_DIM].astype(BF16)
                y = _dot((cb * (wf + wb)).astype(BF16), xh)
                hf = hf_ref[hd]
                y = y + _dot(cg, hf.astype(BF16)) * jnp.exp(colf)
                y = y + _dot(cg, hbe_ref[chunk, hd].astype(BF16)) * jnp.exp(colb)
                total = rowf[:, q - 1:q]
                w_t = jnp.exp(total - rowf) * dt_t[hd:hd + 1, :]
                hf_ref[hd] = hf * jnp.exp(total) + _dot((btg * w_t).astype(BF16), xh)
                ys.append(y)
        y = jnp.concatenate(ys, axis=-1) + x * dsk_ref[...]
        gz = y * _silu(z_ref[0, rows, :])
        ms = _dot((gz * gz).astype(BF16), bd_ref[...]) * (1.0 / (SSD_W // SSD_GROUPS))
        o_ref[0, rows, :] = (gz * lax.rsqrt(ms + EPS) * nw_ref[...]).astype(o_ref.dtype)

    @pl.when(phase == 0)
    def _():
        for ci in reversed(range(SSD_BLOCK)):
            backward_chunk(ci)

    @pl.when(phase == 1)
    def _():
        for ci in range(SSD_BLOCK):
            forward_chunk(ci)


def _ssd(xbc, z, dt_raw, dt_bias, a_log, d_skip, norm_w, bd, n_lat):
    bsz, t, _ = z.shape
    rows = SSD_BLOCK * SSD_CHUNK
    n_lat_blocks = n_lat // rows
    n_blocks = t // rows
    assert n_blocks == n_lat_blocks + 1 and n_lat_blocks * rows == n_lat

    def block_of(p, s):
        return jnp.where(p == 0, n_lat_blocks - s, jnp.where(s < 1, n_lat_blocks, s - 1))

    def out_block(p, s):
        return jnp.where(p == 0, n_lat_blocks, block_of(p, s))

    const = lambda shape: pl.BlockSpec(shape, lambda b, p, s: tuple(0 for _ in shape))
    return pl.pallas_call(
        functools.partial(_ssd_kernel, n_lat_blocks=n_lat_blocks),
        grid=(bsz, 2, n_blocks),
        in_specs=[
            pl.BlockSpec((1, 1, rows, SSD_CONV_CH), lambda b, p, s: (0, b, block_of(p, s), 0)),
            pl.BlockSpec((1, rows, SSD_W), lambda b, p, s: (b, block_of(p, s), 0)),
            pl.BlockSpec((1, rows, LANES), lambda b, p, s: (b, block_of(p, s), 0)),
            const((1, LANES)),
            const((1, LANES)),
            const((1, SSD_W)),
            const((1, SSD_W)),
            const((SSD_W, SSD_W)),
        ],
        out_specs=pl.BlockSpec((1, rows, SSD_W), lambda b, p, s: (b, out_block(p, s), 0)),
        out_shape=jax.ShapeDtypeStruct((bsz, t, SSD_W), BF16),
        scratch_shapes=[
            pltpu.VMEM((SSD_HEADS, SSD_STATE, SSD_HEAD_DIM), F32),
            pltpu.VMEM((SSD_HEADS, SSD_STATE, SSD_HEAD_DIM), F32),
            pltpu.VMEM((n_blocks * SSD_BLOCK, SSD_HEADS, SSD_STATE, SSD_HEAD_DIM), F32),
        ],
        compiler_params=_params("arbitrary", "arbitrary", "arbitrary"),
        name="ssd",
    )(xbc, z, dt_raw, dt_bias, a_log, d_skip, norm_w, bd)


def _hyfilt_kernel(feat_ref, win_ref, w1_ref, b1_ref, fr_ref, w2_ref, b2_ref, w3_ref, o_ref):
    hp = lax.Precision.HIGHEST
    fr = fr_ref[...]
    h1 = jnp.sin(fr * (jnp.dot(feat_ref[...], w1_ref[...], precision=hp, preferred_element_type=F32) + b1_ref[...]))
    h2 = jnp.sin(fr * (jnp.dot(h1, w2_ref[...], precision=hp, preferred_element_type=F32) + b2_ref[...]))
    h = jnp.dot(h2, w3_ref[...], precision=hp, preferred_element_type=F32)
    win = win_ref[...]
    first_tile = pl.program_id(0) == 0
    row = lax.broadcasted_iota(jnp.int32, win.shape, 0)
    for order in range(HY_ORDER):
        for direction in range(2):
            c0 = (direction * HY_ORDER + order) * HY_W
            f = h[:, c0:c0 + HY_W] * win
            if direction == 1:
                f = jnp.where(first_tile & (row == 0), 0.0, f)
            o_ref[order * 2 + direction] = f


def _hy_positional(length):
    n = np.arange(length, dtype=np.float64)
    t = n / max(length - 1, 1)
    bands = np.linspace(1e-4, HY_BANDS - 1, HY_BANDS)
    wpos = (2 * math.pi / length) * n
    feats = np.concatenate([t[:, None], np.cos(wpos[:, None] * bands), -np.sin(wpos[:, None] * bands)], axis=-1)
    feats = np.pad(feats, ((0, 0), (0, LANES - HY_POS_DIM)))
    deltas = np.abs(np.linspace(math.log(HY_TARGET) / HY_SLOW_DECAY, math.log(HY_TARGET) / HY_FAST_DECAY, HY_W))
    window = np.exp(-t[:, None] * deltas)
    return jnp.asarray(feats, F32), jnp.asarray(window, F32)


def _hyena_filters(length, w1, b1, freq, w2, b2, w3):
    feats, window = _hy_positional(length)
    tl = min(length, 512)
    hid = HY_FILTER_HID
    w1p = jnp.pad(w1, ((0, LANES - HY_POS_DIM), (0, 0)))
    const = lambda shape: pl.BlockSpec(shape, lambda i: tuple(0 for _ in shape))
    return pl.pallas_call(
        _hyfilt_kernel,
        grid=(length // tl,),
        in_specs=[
            pl.BlockSpec((tl, LANES), lambda i: (i, 0)),
            pl.BlockSpec((tl, HY_W), lambda i: (i, 0)),
            const((LANES, hid)), const((1, hid)), const((1, hid)),
            const((hid, hid)), const((1, hid)), const((hid, 2 * HY_ORDER * HY_W)),
        ],
        out_specs=pl.BlockSpec((2 * HY_ORDER, tl, HY_W), lambda i: (0, i, 0)),
        out_shape=jax.ShapeDtypeStruct((2 * HY_ORDER, length, HY_W), F32),
        compiler_params=_params("arbitrary"),
        name="hyena_filters",
    )(feats, window, w1p, b1.reshape(1, hid), freq.reshape(1, hid), w2, b2.reshape(1, hid), w3)


def _dft_tables(length):
    n = 2 * length
    n2 = HY_N2
    n1 = n // n2
    half = n1 // 2
    k1 = np.arange(half, dtype=np.float64) + 0.5
    idx = (n2 * np.arange(half)[None, None, :] + np.arange(n2)[:, None, None])
    ang = 2 * np.pi * k1[None, :, None] * idx / n
    fa = np.concatenate([np.cos(ang), -np.sin(ang)], axis=1)
    kk = np.arange(n2, dtype=np.float64)
    angb = 2 * np.pi * kk[:, None] * kk[None, :] / n2
    cr, sr = np.cos(angb), np.sin(angb)
    fb = np.block([[cr, sr], [-sr, cr]])
    fbi = np.block([[cr, -sr], [sr, cr]])
    idxo = (n2 * np.arange(half)[None, :, None] + np.arange(n2)[:, None, None])
    ango = 2 * np.pi * k1[None, None, :] * idxo / n
    ga = np.concatenate([np.cos(ango), -np.sin(ango)], axis=2) * (2.0 / n)
    f32 = lambda a: jnp.asarray(a, F32).astype(BF16)
    return f32(fa), f32(fb), f32(fbi), f32(ga)


def _conv3_rows(u, w, b):
    n = u.shape[0]
    row = lax.broadcasted_iota(jnp.int32, u.shape, 0)
    prev = jnp.where(row == 0, 0.0, pltpu.roll(u, 1, 0))
    nxt = jnp.where(row == n - 1, 0.0, pltpu.roll(u, n - 1, 0))
    return prev * w[0:1] + u * w[1:2] + nxt * w[2:3] + b


def _seq_pitch(n2n):
    return n2n + SUBLANES


def _to_pitched(dst_ref, val, n2n):
    pitch = _seq_pitch(n2n)
    for i in range(val.shape[0] // n2n):
        dst_ref[i * pitch:i * pitch + n2n, :] = val[i * n2n:(i + 1) * n2n, :]


def _hy_forward_a(src_ref, a_ref, fa_ref):
    n2n, two_n1, half = fa_ref.shape
    zp, ap = _seq_pitch(n2n), _seq_pitch(two_n1)

    def body(n2, carry):
        zs = src_ref[pl.ds(n2, half, stride=zp), :].astype(BF16)
        a_ref[pl.ds(pl.multiple_of(n2 * ap, SUBLANES), two_n1), :] = _dot(fa_ref[n2], zs)
        return carry

    lax.fori_loop(0, n2n, body, 0, unroll=HY_UNROLL)


def _hy_kspec_kernel(hf_ref, hb_ref, fa_ref, fb_ref, o_ref, af_ref, ab_ref, hfp_ref, hbp_ref):
    n2n, two_n1, _ = fa_ref.shape
    n1 = two_n1 // 2
    ap = _seq_pitch(two_n1)
    _to_pitched(hfp_ref, hf_ref[0], n2n)
    _to_pitched(hbp_ref, hb_ref[0], n2n)
    _hy_forward_a(hfp_ref, af_ref, fa_ref)
    _hy_forward_a(hbp_ref, ab_ref, fa_ref)
    fb = fb_ref[...]

    def body(k1, carry):
        def spectrum(a_ref):
            ar = a_ref[pl.ds(k1, n2n, stride=ap), :]
            ai = a_ref[pl.ds(n1 + k1, n2n, stride=ap), :]
            return _dot(fb, jnp.concatenate([ar, ai], axis=0).astype(BF16))

        xf, xb = spectrum(af_ref), spectrum(ab_ref)
        o_ref[0, k1] = jnp.concatenate([xf[:n2n] + xb[:n2n], xf[n2n:] - xb[n2n:]], axis=0)
        return carry

    lax.fori_loop(0, n1, body, 0, unroll=HY_UNROLL)


def _hy_kspec(filt, tables):
    fa, fb, _, _ = tables
    _, length, _ = filt.shape
    n2n, two_n1, _ = fa.shape
    n1 = two_n1 // 2
    nh = HY_W // LANES
    return pl.pallas_call(
        _hy_kspec_kernel,
        grid=(HY_ORDER, nh),
        in_specs=[
            pl.BlockSpec((1, length, LANES), lambda o, h: (2 * o, 0, h)),
            pl.BlockSpec((1, length, LANES), lambda o, h: (2 * o + 1, 0, h)),
            pl.BlockSpec(fa.shape, lambda o, h: (0, 0, 0), pipeline_mode=pl.Buffered(1)),
            pl.BlockSpec(fb.shape, lambda o, h: (0, 0)),
        ],
        out_specs=pl.BlockSpec((1, n1, 2 * n2n, LANES), lambda o, h: (o, 0, 0, h)),
        out_shape=jax.ShapeDtypeStruct((HY_ORDER, n1, 2 * n2n, HY_W), F32),
        scratch_shapes=[pltpu.VMEM((n2n * _seq_pitch(two_n1), LANES), F32),
                        pltpu.VMEM((n2n * _seq_pitch(two_n1), LANES), F32),
                        pltpu.VMEM((length // n2n * _seq_pitch(n2n), LANES), F32),
                        pltpu.VMEM((length // n2n * _seq_pitch(n2n), LANES), F32)],
        compiler_params=_params("arbitrary", "arbitrary"),
        name="hyena_kspec",
    )(filt, filt, fa, fb)


def _hy_conv_kernel(z_ref, g_ref, wz_ref, bz_ref, wg_ref, bg_ref, fa_ref, fb_ref, fbi_ref, ga_ref, k_ref, bias_ref,
                    o_ref, a_ref, c_ref, zc_ref, gc_ref, *, z_conv):
    n2n, two_n1, half = fa_ref.shape
    n1 = two_n1 // 2
    zp, ap = _seq_pitch(n2n), _seq_pitch(two_n1)
    _to_pitched(zc_ref, _conv3_rows(z_ref[0], wz_ref[...], bz_ref[...]) if z_conv else z_ref[0], n2n)
    _to_pitched(gc_ref, _conv3_rows(g_ref[0], wg_ref[...], bg_ref[...]), n2n)
    _hy_forward_a(zc_ref, a_ref, fa_ref)
    fb = fb_ref[...]
    fbi = fbi_ref[...]

    def body_b(grp, carry):
        k1s = [grp * HY_UNROLL + i for i in range(HY_UNROLL)]
        sl = [(pl.ds(k1, n2n, stride=ap), pl.ds(n1 + k1, n2n, stride=ap)) for k1 in k1s]
        xs = [_dot(fb, jnp.concatenate([a_ref[re, :], a_ref[im, :]], axis=0).astype(BF16)) for re, im in sl]
        ys = []
        for k1, x in zip(k1s, xs):
            kk = k_ref[0, k1]
            xr, xi = x[:n2n], x[n2n:]
            kr, ki = kk[:n2n], kk[n2n:]
            ys.append(jnp.concatenate([xr * kr - xi * ki, xr * ki + xi * kr], axis=0).astype(BF16))
        for (re, im), y in zip(sl, ys):
            c = _dot(fbi, y)
            c_ref[re, :] = c[:n2n]
            c_ref[im, :] = c[n2n:]
        return carry

    lax.fori_loop(0, n1 // HY_UNROLL, body_b, 0)
    bias = bias_ref[0]

    def body_c(m2, carry):
        rows = pl.ds(m2, half, stride=zp)
        c = c_ref[pl.ds(pl.multiple_of(m2 * ap, SUBLANES), two_n1), :].astype(BF16)
        y = _dot(ga_ref[m2], c)
        a_ref[rows, :] = gc_ref[rows, :] * (y + zc_ref[rows, :] * bias)
        return carry

    lax.fori_loop(0, n2n, body_c, 0, unroll=HY_UNROLL)
    for i in range(half):
        o_ref[0, i * n2n:(i + 1) * n2n, :] = a_ref[i * zp:i * zp + n2n, :]


def _hy_conv(z_arr, z_blk, g_arr, g_blk, w, b, kspec, order, bias3, tables, n_lat, z_conv):
    fa, fb, fbi, ga = tables
    bsz = z_arr.shape[0]
    n2n, two_n1, half = fa.shape
    n1 = two_n1 // 2
    nh = HY_W // LANES
    zw = z_blk if z_conv else g_blk
    resident = lambda shape: pl.BlockSpec(shape, lambda h, b_: tuple(0 for _ in shape), pipeline_mode=pl.Buffered(1))
    return pl.pallas_call(
        functools.partial(_hy_conv_kernel, z_conv=z_conv),
        grid=(nh, bsz),
        in_specs=[
            pl.BlockSpec((1, n_lat, LANES), lambda h, b_: (b_, 0, z_blk + h)),
            pl.BlockSpec((1, n_lat, LANES), lambda h, b_: (b_, 0, g_blk + h)),
            pl.BlockSpec((3, LANES), lambda h, b_: (0, zw + h)),
            pl.BlockSpec((1, LANES), lambda h, b_: (0, zw + h)),
            pl.BlockSpec((3, LANES), lambda h, b_: (0, g_blk + h)),
            pl.BlockSpec((1, LANES), lambda h, b_: (0, g_blk + h)),
            resident(fa.shape), resident(fb.shape), resident(fbi.shape), resident(ga.shape),
            pl.BlockSpec((1, n1, 2 * n2n, LANES), lambda h, b_: (order, 0, 0, h), pipeline_mode=pl.Buffered(1)),
            pl.BlockSpec((1, 1, LANES), lambda h, b_: (order, 0, h)),
        ],
        out_specs=pl.BlockSpec((1, n_lat, LANES), lambda h, b_: (b_, 0, h)),
        out_shape=jax.ShapeDtypeStruct((bsz, n_lat, HY_W), F32),
        scratch_shapes=[pltpu.VMEM((n2n * _seq_pitch(two_n1), LANES), F32),
                        pltpu.VMEM((n2n * _seq_pitch(two_n1), LANES), F32),
                        pltpu.VMEM((half * _seq_pitch(n2n), LANES), F32),
                        pltpu.VMEM((half * _seq_pitch(n2n), LANES), F32)],
        compiler_params=_params("arbitrary", "arbitrary"),
        name="hyena_conv",
    )(z_arr, g_arr, w, b, w, b, fa, fb, fbi, ga, kspec, bias3)


def _hyena_ctx_kernel(u_ref, w_ref, b_ref, filt_ref, bias_ref, fd_ref, gd_ref, o_ref, *, n_ctx):
    fd = fd_ref[...]
    gd = gd_ref[...]
    nn = 2 * n_ctx
    u = _conv3_rows(u_ref[0], w_ref[...], b_ref[...])
    v, x1, x2 = u[:, :HY_W], u[:, HY_W:2 * HY_W], u[:, 2 * HY_W:]

    def conv(zin, order):
        hf = _dot(fd, filt_ref[2 * order].astype(BF16))
        hb = _dot(fd, filt_ref[2 * order + 1].astype(BF16))
        kr, ki = hf[:nn] + hb[:nn], hf[nn:] - hb[nn:]
        zz = _dot(fd, zin.astype(BF16))
        zr, zi = zz[:nn], zz[nn:]
        y = jnp.concatenate([zr * kr - zi * ki, zr * ki + zi * kr], axis=0).astype(BF16)
        return _dot(gd, y) + zin * bias_ref[order]

    y1 = x1 * conv(v, 0)
    o_ref[0] = x2 * conv(y1, 1)


def _hyena_ctx(hy, conv_w, conv_b, filt, bias, n_lat, n_ctx):
    bsz = hy.shape[0]
    nn = 2 * n_ctx
    k = np.arange(nn, dtype=np.float64)
    t = np.arange(n_ctx, dtype=np.float64)
    ang = 2 * np.pi * k[:, None] * t[None, :] / nn
    fd = jnp.asarray(np.concatenate([np.cos(ang), -np.sin(ang)], axis=0), F32).astype(BF16)
    gd = jnp.asarray(np.concatenate([np.cos(ang.T), -np.sin(ang.T)], axis=1) / nn, F32).astype(BF16)
    blk = n_lat // n_ctx
    assert blk * n_ctx == n_lat
    return pl.pallas_call(
        functools.partial(_hyena_ctx_kernel, n_ctx=n_ctx),
        grid=(bsz,),
        in_specs=[
            pl.BlockSpec((1, n_ctx, 3 * HY_W), lambda b: (b, blk, 0)),
            pl.BlockSpec((3, 3 * HY_W), lambda b: (0, 0)),
            pl.BlockSpec((1, 3 * HY_W), lambda b: (0, 0)),
            pl.BlockSpec((2 * HY_ORDER, n_ctx, HY_W), lambda b: (0, 0, 0)),
            pl.BlockSpec((HY_ORDER, 1, HY_W), lambda b: (0, 0, 0)),
            pl.BlockSpec((2 * nn, n_ctx), lambda b: (0, 0)),
            pl.BlockSpec((n_ctx, 2 * nn), lambda b: (0, 0)),
        ],
        out_specs=pl.BlockSpec((1, n_ctx, HY_W), lambda b: (b, 0, 0)),
        out_shape=jax.ShapeDtypeStruct((bsz, n_ctx, HY_W), F32),
        compiler_params=_params("arbitrary"),
        name="hyena_ctx",
    )(hy, conv_w, conv_b.reshape(1, -1), filt, bias, fd, gd)


def _hyena_latent(hy, conv_w, conv_b, filt, bias, n_lat, tables):
    kspec = _hy_kspec(filt, tables)
    bias3 = bias.reshape(HY_ORDER, 1, HY_W)
    cb = conv_b.reshape(1, -1)
    nb = HY_W // LANES
    y1 = _hy_conv(hy, 0, hy, nb, conv_w, cb, kspec, 0, bias3, tables, n_lat, True)
    return _hy_conv(y1, 0, hy, 2 * nb, conv_w, cb, kspec, 1, bias3, tables, n_lat, False)


def _outproj_kernel(xl_ref, xc_ref, attl_ref, attc_ref, ssd_ref, hyl_ref, hyc_ref, gt_ref, wa_ref, ws_ref, wh_ref,
                    o_ref, *, n_lat_tiles):
    x = _stream_tile(xl_ref, xc_ref, n_lat_tiles)
    att = _stream_tile(attl_ref, attc_ref, n_lat_tiles)
    hy = _stream_tile(hyl_ref, hyc_ref, n_lat_tiles).astype(BF16)
    mix = _dot(att, wa_ref[...]) + _dot(ssd_ref[0], ws_ref[...]) + _dot(hy, wh_ref[...])
    o_ref[0] = x + gt_ref[0] * mix


def _outproj(x_lat, x_ctx, ctx_blk, att_l, att_c, ssd, hy_l, hy_c, mods, w_out, n_lat_tiles, n_tiles):
    bsz, _, d = x_lat.shape
    tm = TOKEN_TILE

    def mod_row(j, b):
        return jnp.where(j >= n_lat_tiles, bsz, b)

    tok = lambda w: pl.BlockSpec((1, tm, w), lambda j, b: (b, j, 0))
    return pl.pallas_call(
        functools.partial(_outproj_kernel, n_lat_tiles=n_lat_tiles),
        grid=(n_tiles, bsz),
        in_specs=[
            *_stream_specs(n_lat_tiles, ctx_blk, d),
            *_stream_specs(n_lat_tiles, 0, ATT_W),
            tok(SSD_W),
            *_stream_specs(n_lat_tiles, 0, HY_W),
            pl.BlockSpec((1, 1, d), lambda j, b: (mod_row(j, b), 0, 2)),
            pl.BlockSpec((ATT_W, d), lambda j, b: (0, 0)),
            pl.BlockSpec((SSD_W, d), lambda j, b: (0, 0)),
            pl.BlockSpec((HY_W, d), lambda j, b: (0, 0)),
        ],
        out_specs=tok(d),
        out_shape=jax.ShapeDtypeStruct((bsz, n_tiles * tm, d), F32),
        compiler_params=_params("arbitrary", "arbitrary"),
        name="outproj",
    )(x_lat, x_ctx, att_l, att_c, ssd, hy_l, hy_c, mods,
      w_out[:ATT_W], w_out[ATT_W:ATT_W + SSD_W], w_out[ATT_W + SSD_W:])


def _route(logits_t, bias_col):
    scores = jax.nn.sigmoid(logits_t)
    sel = scores + bias_col
    neg_inf = jnp.float32(-jnp.inf)
    rows = [sel[e:e + 1, :] for e in range(N_EXPERTS)]
    grp = []
    for g in range(N_GROUPS):
        r = rows[g * EXPERTS_PER_GROUP:(g + 1) * EXPERTS_PER_GROUP]
        top = functools.reduce(jnp.maximum, r)
        taken = None
        rest = []
        for ri in r:
            is_top = (ri == top) if taken is None else (ri == top) & jnp.logical_not(taken)
            rest.append(jnp.where(is_top, neg_inf, ri))
            taken = is_top if taken is None else taken | is_top
        grp.append(top + functools.reduce(jnp.maximum, rest))
    best = jnp.zeros(grp[0].shape, jnp.int32)
    cur = grp[0]
    for g in range(1, N_GROUPS):
        upd = grp[g] > cur
        best = jnp.where(upd, g, best)
        cur = jnp.where(upd, grp[g], cur)
    picked = []
    for e in range(N_EXPERTS):
        g, i = divmod(e, EXPERTS_PER_GROUP)
        rank = jnp.zeros(best.shape, jnp.int32)
        for j in range(EXPERTS_PER_GROUP):
            if j == i:
                continue
            other = rows[g * EXPERTS_PER_GROUP + j]
            ahead = (other > rows[e]) | ((other == rows[e]) & (j < i))
            rank = rank + ahead.astype(jnp.int32)
        keep = (best == g) & (rank < 2)
        picked.append(jnp.where(keep, scores[e:e + 1, :], 0.0))
    total = functools.reduce(lambda u, w: u + w, picked)
    return jnp.concatenate(picked, axis=0) / total, best


def _ffn_input(x_ref, sh_ref, sc_ref, g_ref):
    x = x_ref[...].reshape(MOE_TILE, D_MODEL)
    ms = jnp.mean(x * x, axis=-1, keepdims=True)
    t = x * lax.rsqrt(ms + EPS) * g_ref[...]
    return x, t * (1.0 + sc_ref[0]) + sh_ref[0]


def _route_kernel(x_ref, sh_ref, sc_ref, g_ref, wr_ref, rb_ref, rt_ref, srow_ref, tbl_ref):
    tm = MOE_TILE
    _, t = _ffn_input(x_ref, sh_ref, sc_ref, g_ref)
    t_hi, t_mid, _ = _split3(t)
    w_hi, w_mid, _ = _split3(wr_ref[...])
    logits = _dot(t_hi, w_hi) + (_dot(t_hi, w_mid) + _dot(t_mid, w_hi))
    comb_t, best = _route(logits.T[:N_EXPERTS, :], rb_ref[...])

    member = [jnp.where(best == g, 1.0, 0.0) for g in range(N_GROUPS)]
    comb4 = functools.reduce(
        lambda u, w: u + w,
        [member[g] * comb_t[g * EXPERTS_PER_GROUP:(g + 1) * EXPERTS_PER_GROUP, :] for g in range(N_GROUPS)])
    masks = jnp.concatenate(member + [jnp.zeros((8 - N_GROUPS, tm), F32)], axis=0)
    earlier = jnp.where(lax.broadcasted_iota(jnp.int32, (tm, tm), 0) < lax.broadcasted_iota(jnp.int32, (tm, tm), 1),
                        1.0, 0.0).astype(BF16)
    ranks = _dot(masks.astype(BF16), earlier)
    cnt = jnp.sum(masks, axis=1, keepdims=True)
    padded = jnp.ceil(cnt * (1.0 / MOE_CHUNK)) * MOE_CHUNK
    lane = lax.broadcasted_iota(jnp.int32, (1, LANES), 1).astype(F32) * MOE_CHUNK
    start = jnp.zeros((1, 1), F32)
    slot = jnp.zeros((1, tm), F32)
    gid = jnp.zeros((1, LANES), jnp.int32)
    for g in range(N_GROUPS):
        slot = slot + member[g] * (start + ranks[g:g + 1, :])
        start = start + padded[g:g + 1, :]
        gid = gid + jnp.where(lane >= start, 1, 0)
    tbl_ref[0] = gid
    srow_ref[0] = slot.astype(jnp.int32)
    rt_t = jnp.concatenate([comb4, slot, jnp.zeros((LANES - EXPERTS_PER_GROUP - 1, tm), F32)], axis=0)
    rt_ref[...] = rt_t.T


def _experts_kernel(tbl_ref, x_ref, sh_ref, sc_ref, gt_ref, g_ref, rt_ref, srow_ref, wgu_ref, wd_ref, gf_ref,
                    *rest, final, tile_of, n_grid):
    o_ref, xp_ref, yp_ref = rest[-3:]
    tm, ch = MOE_TILE, MOE_CHUNK
    n_slots = MOE_CHUNKS * ch
    tile = tile_of(*[pl.program_id(a) for a in range(n_grid)])
    x, t = _ffn_input(x_ref, sh_ref, sc_ref, g_ref)
    rt = rt_ref[...]
    slot_col = rt[:, EXPERTS_PER_GROUP:EXPERTS_PER_GROUP + 1].astype(jnp.int32)
    gather = jnp.where(lax.broadcasted_iota(jnp.int32, (n_slots, tm), 0) == srow_ref[0], 1.0, 0.0).astype(BF16)
    scatter = jnp.where(lax.broadcasted_iota(jnp.int32, (tm, n_slots), 1) == slot_col, 1.0, 0.0).astype(BF16)
    xp_ref[...] = _dot(gather, t.astype(BF16)).astype(BF16)
    w_slot = _dot3_right(gather, rt)

    def run_chunk(rows, grp):
        xc = xp_ref[rows, :]
        e0 = grp * EXPERTS_PER_GROUP
        gu_next = _dot(xc, wgu_ref[e0])
        acc = jnp.zeros((ch, D_MODEL), F32)
        for j in range(EXPERTS_PER_GROUP):
            gu = gu_next
            if j + 1 < EXPERTS_PER_GROUP:
                gu_next = _dot(xc, wgu_ref[e0 + j + 1])
            hid = _silu(gu[:, :D_FF]) * gu[:, D_FF:] * w_slot[rows, j:j + 1]
            acc = acc + _dot(hid.astype(BF16), wd_ref[e0 + j])
        yp_ref[rows, :] = acc.astype(BF16)

    for c in range(MOE_CHUNKS):
        rows = slice(c * ch, (c + 1) * ch)
        grp = tbl_ref[tile, c]
        pl.when(grp < N_GROUPS)(functools.partial(run_chunk, rows, grp))

        @pl.when(grp >= N_GROUPS)
        def _():
            yp_ref[rows, :] = jnp.zeros((ch, D_MODEL), BF16)

    y = x + gt_ref[0] * _dot(scatter, yp_ref[...])
    if final:
        y = y * lax.rsqrt(jnp.mean(y * y, axis=-1, keepdims=True) + EPS) * gf_ref[...]
    o_ref[...] = y.reshape(o_ref.shape)


def _moe_tiles(x, mods, mod_row, g_ffn, w_router, router_bias, wgu, wd, g_final, final, out_rows,
               grid, x_block, x_index, out_index, tile_of, n_tiles):
    bsz, _, d = x.shape
    tm = MOE_TILE
    ng = len(grid)
    const = lambda shape: pl.BlockSpec(shape, lambda *a: tuple(0 for _ in shape))
    mod = lambda col: pl.BlockSpec((1, 1, d), lambda *a: (mod_row(*a[:ng]), 0, col))
    xspec = pl.BlockSpec(x_block, lambda *a: x_index(*a[:ng]))
    rt, srow, tbl = pl.pallas_call(
        _route_kernel,
        grid=grid,
        in_specs=[xspec, mod(3), mod(4), const((1, d)), const((d, LANES)), const((N_EXPERTS, 1))],
        out_specs=[
            pl.BlockSpec((tm, LANES), lambda *a: (tile_of(*a), 0)),
            pl.BlockSpec((1, 1, tm), lambda *a: (tile_of(*a), 0, 0)),
            pl.BlockSpec((1, 1, LANES), lambda *a: (tile_of(*a), 0, 0)),
        ],
        out_shape=[
            jax.ShapeDtypeStruct((n_tiles * tm, LANES), F32),
            jax.ShapeDtypeStruct((n_tiles, 1, tm), jnp.int32),
            jax.ShapeDtypeStruct((n_tiles, 1, LANES), jnp.int32),
        ],
        compiler_params=_params(*["arbitrary"] * ng),
        name="moe_route",
    )(x, mods, mods, g_ffn, w_router, router_bias)

    resident = lambda shape: pl.BlockSpec(shape, lambda *a: tuple(0 for _ in shape), pipeline_mode=pl.Buffered(1))
    in_specs = [
        xspec, mod(3), mod(4), mod(5), const((1, d)),
        pl.BlockSpec((tm, LANES), lambda *a: (tile_of(*a[:ng]), 0)),
        pl.BlockSpec((1, 1, tm), lambda *a: (tile_of(*a[:ng]), 0, 0)),
        resident((N_EXPERTS, d, 2 * D_FF)),
        resident((N_EXPERTS, D_FF, d)),
        const((1, d)),
    ]
    args = [tbl.reshape(n_tiles, LANES), x, mods, mods, mods, g_ffn, rt, srow, wgu, wd, g_final]
    return pl.pallas_call(
        functools.partial(_experts_kernel, final=final, tile_of=tile_of, n_grid=ng),
        grid_spec=pltpu.PrefetchScalarGridSpec(
            num_scalar_prefetch=1,
            grid=grid,
            in_specs=in_specs,
            out_specs=pl.BlockSpec(x_block, lambda *a: out_index(*a[:ng])),
            scratch_shapes=[pltpu.VMEM((MOE_CHUNKS * MOE_CHUNK, d), BF16), pltpu.VMEM((MOE_CHUNKS * MOE_CHUNK, d), BF16)],
        ),
        out_shape=jax.ShapeDtypeStruct((bsz, out_rows, d), F32),
        compiler_params=_params(*["arbitrary"] * ng),
        name="moe_experts",
    )(*args)


def _moe(x, mods, g_ffn, w_router, router_bias, wgu, wd, g_final, n_lat, n_ctx, with_ctx, final):
    bsz, _, d = x.shape
    tm = MOE_TILE
    per_b = n_lat // tm
    common = (g_ffn, w_router, router_bias, wgu, wd, g_final, final)
    lat_index = lambda b, j: (b, j, 0)
    out_lat = _moe_tiles(x, mods, lambda b, j: b, *common, n_lat,
                         (bsz, per_b), (1, tm, d), lat_index, lat_index, lambda b, j: b * per_b + j, bsz * per_b)
    if not with_ctx:
        return out_lat, None
    nb = tm // n_ctx
    out_ctx = _moe_tiles(x, mods, lambda i: bsz, *common, n_ctx,
                         (bsz // nb,), (nb, n_ctx, d), lambda i: (i, n_lat // n_ctx, 0), lambda i: (i, 0, 0),
                         lambda i: i, bsz // nb)
    return out_lat, out_ctx


def _block_ones(width, block):
    idx = np.arange(width) // block
    return jnp.asarray(idx[:, None] == idx[None, :], F32).astype(BF16)


def _rope_tables(n_lat, n_ctx):
    rows = n_lat // GRID_W
    row = np.repeat(np.arange(rows), GRID_W).astype(np.float64)
    col = np.tile(np.arange(GRID_W), rows).astype(np.float64)
    inv = ROPE_THETA ** (-np.arange(0, ROPE_AXIS_DIM, 2, dtype=np.float64) / ROPE_AXIS_DIM)
    ang = np.concatenate([row[:, None] * inv, col[:, None] * inv], axis=-1)
    ang = np.concatenate([ang, np.zeros((n_ctx, ang.shape[1]))], axis=0)
    cos = np.concatenate([np.cos(ang), np.cos(ang)], axis=-1)
    sin = np.concatenate([-np.sin(ang), np.sin(ang)], axis=-1)
    scale = Q_SCALE
    cs = np.tile(cos, (1, ATT_HEADS)) * scale
    sn = np.tile(sin, (1, ATT_HEADS)) * scale
    return jnp.asarray(cs, F32), jnp.asarray(sn, F32)


def _pack_w_in(w):
    cuts = np.cumsum([ATT_W, KV_W, KV_W, SSD_W, SSD_CONV_CH, 2 * SSD_HEADS])
    q, k, v, z, xbc, dt, hy = jnp.split(w, [int(c) for c in cuts], axis=-1)
    dt = jnp.pad(dt, ((0, 0), (0, LANES - 2 * SSD_HEADS)))
    return jnp.concatenate([q, k, v, z, xbc, dt, hy], axis=-1).astype(BF16)


def kernel(x, c, ctx, c_ctx, w_mod, b_mod, g_mix, g_ffn, w_in, q_norm, k_norm, ssd_conv_w, ssd_conv_b,
           ssd_dt_bias, ssd_a_log, ssd_d, ssd_norm, hy_conv_w, hy_conv_b, hy_w1, hy_b1, hy_freq, hy_w2, hy_b2,
           hy_w3, hy_bias, w_out, w_router, router_bias, w_gate, w_up, w_down, g_final):
    bsz, n_lat, d = x.shape
    n_ctx = ctx.shape[1]
    depth = w_mod.shape[0]
    t = n_lat + n_ctx
    tm = TOKEN_TILE
    n_lat_tiles = n_lat // tm
    n_tiles = t // tm
    assert n_ctx == tm and n_lat % (HY_N2 * 8) == 0 and bsz < MOD_ROWS

    cvec = jnp.concatenate([c, c_ctx[None], jnp.zeros((MOD_ROWS - bsz - 1, d), F32)], axis=0)
    mods_all = _adaln(cvec, w_mod, b_mod)

    cs, sn = _rope_tables(n_lat, n_ctx)
    bd_head = _block_ones(ATT_W, HEAD_DIM)
    bd_ssd = _block_ones(SSD_W, SSD_W // SSD_GROUPS)
    tables = _dft_tables(n_lat)
    pad_row = lambda v: jnp.pad(v.reshape(1, -1), ((0, 0), (0, LANES - v.size)))

    x_lat, x_ctx, ctx_blk = x, ctx, 0
    for i in range(depth):
        last = i == depth - 1
        mods = mods_all[i].reshape(MOD_ROWS, 1, 6 * d)
        q, kt, v, z, xbc, dt_raw, hy = _inproj(
            x_lat, x_ctx, ctx_blk, mods, g_mix[i].reshape(1, d), _pack_w_in(w_in[i]),
            jnp.tile(q_norm[i], ATT_HEADS).reshape(1, ATT_W), jnp.tile(k_norm[i], ATT_KV_HEADS).reshape(1, KV_W),
            cs, sn, bd_head, n_lat_tiles)

        att = _attention(q, kt, v, 0, n_lat, 0, t, ATT_TILE)
        att_c = _attention(q, kt, v, n_lat, n_ctx, n_lat, n_ctx, n_ctx) if not last else att

        xbc_c = _dwconv(xbc, ssd_conv_w[i], ssd_conv_b[i], n_lat, True, 1)
        ssd = _ssd(xbc_c, z, dt_raw, pad_row(ssd_dt_bias[i]), pad_row(ssd_a_log[i]),
                   jnp.repeat(ssd_d[i], SSD_HEAD_DIM).reshape(1, SSD_W), ssd_norm[i].reshape(1, SSD_W),
                   bd_ssd, n_lat)

        hyp = (hy_w1[i], hy_b1[i], hy_freq[i], hy_w2[i], hy_b2[i], hy_w3[i])
        hy_l = _hyena_latent(hy, hy_conv_w[i], hy_conv_b[i], _hyena_filters(n_lat, *hyp), hy_bias[i], n_lat, tables)
        hy_c = hy_l if last else _hyena_ctx(hy, hy_conv_w[i], hy_conv_b[i], _hyena_filters(n_ctx, *hyp),
                                            hy_bias[i].reshape(HY_ORDER, 1, HY_W), n_lat, n_ctx)

        live_tiles = n_lat_tiles if last else n_tiles
        xs = _outproj(x_lat, x_ctx, ctx_blk, att, att_c, ssd, hy_l, hy_c, mods, w_out[i].astype(BF16),
                      n_lat_tiles, live_tiles)

        wgu = jnp.concatenate([w_gate[i], w_up[i]], axis=-1).astype(BF16)
        x_lat, x_ctx = _moe(xs, mods, g_ffn[i].reshape(1, d), jnp.pad(w_router, ((0, 0), (0, LANES - N_EXPERTS))),
                            router_bias.reshape(N_EXPERTS, 1), wgu, w_down[i].astype(BF16), g_final.reshape(1, d),
                            n_lat, n_ctx, not last, last)
    return x_lat
```

```python
import functools
import math

import jax
import jax.numpy as jnp
import numpy as np
from jax import lax
from jax.experimental import pallas as pl
from jax.experimental.pallas import tpu as pltpu

F32 = jnp.float32
BF16 = jnp.bfloat16

D_MODEL = 1024
GRID_W = 64
EPS = 1e-6

ATT_HEADS = 6
ATT_KV_HEADS = 2
HEAD_DIM = 64
ATT_W = ATT_HEADS * HEAD_DIM
KV_W = ATT_KV_HEADS * HEAD_DIM
ROPE_AXIS_DIM = HEAD_DIM // 2
ROPE_THETA = 10000.0
Q_SCALE = HEAD_DIM ** -0.5 * math.log2(math.e)

SSD_HEADS = 6
SSD_HEAD_DIM = 64
SSD_W = SSD_HEADS * SSD_HEAD_DIM
SSD_GROUPS = 2
SSD_STATE = 64
SSD_CHUNK = 128
SSD_CONV_CH = SSD_W + 2 * SSD_GROUPS * SSD_STATE
SSD_HEADS_PER_GROUP = SSD_HEADS // SSD_GROUPS
SSD_BLOCK = 2

HY_W = 256
HY_ORDER = 2
HY_BANDS = 16
HY_POS_DIM = 1 + 2 * HY_BANDS
HY_FILTER_HID = 64
HY_FAST_DECAY = 0.3
HY_SLOW_DECAY = 1.5
HY_TARGET = 1e-2
HY_N2 = 64
HY_UNROLL = 8

MIX_W = ATT_W + SSD_W + HY_W
N_EXPERTS = 16
N_GROUPS = 4
EXPERTS_PER_GROUP = N_EXPERTS // N_GROUPS
D_FF = 256

LANES = 128
SUBLANES = 8
TOKEN_TILE = 256
ATT_TILE = 1024
ATT_SUB = 512
MOE_TILE = 512
MOE_CHUNK = 128
MOE_CHUNKS = MOE_TILE // MOE_CHUNK + N_GROUPS - 1
MOD_ROWS = 8
VMEM_LIMIT = 56 * 1024 * 1024

COL_Q = 0
COL_K = COL_Q + ATT_W
COL_V = COL_K + KV_W
COL_Z = COL_V + KV_W
COL_XBC = COL_Z + SSD_W
COL_DT = COL_XBC + SSD_CONV_CH
COL_HY = COL_DT + LANES
COL_END = COL_HY + 3 * HY_W


def _params(*sem):
    return pltpu.CompilerParams(dimension_semantics=sem, vmem_limit_bytes=VMEM_LIMIT)


def _silu(x):
    return x * jax.nn.sigmoid(x)


def _softplus(x):
    return jnp.maximum(x, 0.0) + jnp.log1p(jnp.exp(-jnp.abs(x)))


def _split3(x):
    hi = x.astype(BF16)
    r1 = x - hi.astype(F32)
    mid = r1.astype(BF16)
    lo = (r1 - mid.astype(F32)).astype(BF16)
    return hi, mid, lo


def _dot(a, b):
    return jnp.dot(a, b, preferred_element_type=F32)


def _dot_x3(a, b):
    a_hi, a_mid, _ = _split3(a)
    b_hi, b_mid, _ = _split3(b)
    return _dot(a_hi, b_hi) + (_dot(a_hi, b_mid) + _dot(a_mid, b_hi))


def _dot3_right(m_bf16, x):
    hi, mid, lo = _split3(x)
    return _dot(m_bf16, hi) + _dot(m_bf16, mid) + _dot(m_bf16, lo)


def _dot3_left(x, m_bf16):
    hi, mid, lo = _split3(x)
    return _dot(hi, m_bf16) + _dot(mid, m_bf16) + _dot(lo, m_bf16)


def _adaln_kernel(c_ref, w_ref, b_ref, o_ref):
    s = _silu(c_ref[...]).astype(BF16)
    o_ref[0] = _dot(s, w_ref[0].astype(BF16)) + b_ref[0]


def _adaln(cvec, w_mod, b_mod):
    depth, d, n = w_mod.shape
    bn = n // 4
    return pl.pallas_call(
        _adaln_kernel,
        grid=(depth, n // bn),
        in_specs=[
            pl.BlockSpec((MOD_ROWS, d), lambda i, j: (0, 0)),
            pl.BlockSpec((1, d, bn), lambda i, j: (i, 0, j)),
            pl.BlockSpec((1, 1, bn), lambda i, j: (i, 0, j)),
        ],
        out_specs=pl.BlockSpec((1, MOD_ROWS, bn), lambda i, j: (i, 0, j)),
        out_shape=jax.ShapeDtypeStruct((depth, MOD_ROWS, n), F32),
        compiler_params=_params("arbitrary", "arbitrary"),
        name="adaln",
    )(cvec, w_mod, b_mod.reshape(depth, 1, n))


def _head_rope(xn, cs, sn):
    width = xn.shape[-1]
    lane = lax.broadcasted_iota(jnp.int32, xn.shape, 1)
    first_half = (lane % HEAD_DIM) < (HEAD_DIM // 2)
    partner = jnp.where(first_half,
                        pltpu.roll(xn, width - HEAD_DIM // 2, 1),
                        pltpu.roll(xn, HEAD_DIM // 2, 1))
    return xn * cs + partner * sn


def _stream_specs(n_lat_tiles, ctx_blk, d):
    lat = pl.BlockSpec((1, TOKEN_TILE, d), lambda j, b: (b, jnp.minimum(j, n_lat_tiles - 1), 0))
    ctx = pl.BlockSpec((1, TOKEN_TILE, d), lambda j, b: (jnp.where(j >= n_lat_tiles, b, 0), ctx_blk, 0))
    return lat, ctx


def _stream_tile(xl_ref, xc_ref, n_lat_tiles):
    return jnp.where(pl.program_id(0) >= n_lat_tiles, xc_ref[0], xl_ref[0])


def _inproj_kernel(xl_ref, xc_ref, sh_ref, sc_ref, g_ref, w_ref, gq_ref, gk_ref, cs_ref, sn_ref, bd_ref,
                   qt_ref, k_ref, vt_ref, z_ref, xbc_ref, dt_ref, hy_ref, *, n_lat_tiles):
    x = _stream_tile(xl_ref, xc_ref, n_lat_tiles)
    ms = jnp.mean(x * x, axis=-1, keepdims=True)
    h = x * lax.rsqrt(ms + EPS) * g_ref[...]
    h = h * (1.0 + sc_ref[0]) + sh_ref[0]
    hb = h.astype(BF16)
    qkv = _dot(hb, w_ref[:, COL_Q:COL_Z])
    bd = bd_ref[...]
    q, k = qkv[:, COL_Q:COL_K], qkv[:, COL_K:COL_V]
    ms_q = _dot((q * q).astype(BF16), bd) * (1.0 / HEAD_DIM)
    ms_k = _dot((k * k).astype(BF16), bd[:KV_W, :KV_W]) * (1.0 / HEAD_DIM)
    rest = _dot(hb, w_ref[:, COL_Z:COL_END])

    cs = cs_ref[...]
    sn = sn_ref[...]
    qt_ref[0] = _head_rope(q * lax.rsqrt(ms_q + EPS) * gq_ref[...], cs, sn).T.astype(qt_ref.dtype)
    k_ref[0] = _head_rope(k * lax.rsqrt(ms_k + EPS) * gk_ref[...],
                          cs[:, :KV_W] * (1.0 / Q_SCALE), sn[:, :KV_W] * (1.0 / Q_SCALE)).astype(k_ref.dtype)
    vt_ref[0] = qkv[:, COL_V:COL_Z].T.astype(vt_ref.dtype)
    z_ref[0] = rest[:, :COL_XBC - COL_Z]
    xbc_ref[0] = rest[:, COL_XBC - COL_Z:COL_DT - COL_Z]
    dt_ref[0] = rest[:, COL_DT - COL_Z:COL_HY - COL_Z]
    hy_ref[0] = rest[:, COL_HY - COL_Z:]


def _inproj(x_lat, x_ctx, ctx_blk, mods, g_mix, w_cat, gq, gk, cs, sn, bd, n_lat_tiles):
    bsz, _, d = x_lat.shape
    tm = TOKEN_TILE
    nt = n_lat_tiles + 1
    t = nt * tm

    def mod_row(j, b):
        return jnp.where(j >= n_lat_tiles, bsz, b)

    tok = lambda w: pl.BlockSpec((1, tm, w), lambda j, b: (b, j, 0))
    const = lambda shape: pl.BlockSpec(shape, lambda j, b: tuple(0 for _ in shape))
    outs = pl.pallas_call(
        functools.partial(_inproj_kernel, n_lat_tiles=n_lat_tiles),
        grid=(nt, bsz),
        in_specs=[
            *_stream_specs(n_lat_tiles, ctx_blk, d),
            pl.BlockSpec((1, 1, d), lambda j, b: (mod_row(j, b), 0, 0)),
            pl.BlockSpec((1, 1, d), lambda j, b: (mod_row(j, b), 0, 1)),
            const((1, d)),
            const((d, COL_END)),
            const((1, ATT_W)),
            const((1, KV_W)),
            pl.BlockSpec((tm, ATT_W), lambda j, b: (j, 0)),
            pl.BlockSpec((tm, ATT_W), lambda j, b: (j, 0)),
            const((ATT_W, ATT_W)),
        ],
        out_specs=[
            pl.BlockSpec((1, ATT_W, tm), lambda j, b: (b, 0, j)),
            tok(KV_W),
            pl.BlockSpec((1, KV_W, tm), lambda j, b: (b, 0, j)),
            tok(SSD_W),
            tok(SSD_CONV_CH),
            tok(LANES),
            tok(3 * HY_W),
        ],
        out_shape=[
            jax.ShapeDtypeStruct((bsz, ATT_W, t), BF16),
            jax.ShapeDtypeStruct((bsz, t, KV_W), BF16),
            jax.ShapeDtypeStruct((bsz, KV_W, t), BF16),
            jax.ShapeDtypeStruct((bsz, t, SSD_W), F32),
            jax.ShapeDtypeStruct((bsz, t, SSD_CONV_CH), F32),
            jax.ShapeDtypeStruct((bsz, t, LANES), F32),
            jax.ShapeDtypeStruct((bsz, t, 3 * HY_W), F32),
        ],
        compiler_params=_params("arbitrary", "arbitrary"),
        name="inproj",
    )(x_lat, x_ctx, mods, mods, g_mix, w_cat, gq, gk, cs, sn, bd)
    return outs


def _attn_kernel(qt_ref, k_ref, vt_ref, o_ref, st_ref, pt_ref):
    k = k_ref[0]
    vt = vt_ref[0]
    rep = ATT_HEADS // ATT_KV_HEADS
    tq = qt_ref.shape[2]
    sub = min(tq, ATT_SUB)
    ones = jnp.ones((2 * SUBLANES, vt.shape[1]), vt.dtype)
    vtg = [jnp.concatenate([vt[g * HEAD_DIM:(g + 1) * HEAD_DIM, :], ones], axis=0) for g in range(ATT_KV_HEADS)]
    units = [(c0, hd) for c0 in range(0, tq, sub) for hd in range(ATT_HEADS)]

    def scores(u):
        c0, hd = units[u]
        qh = qt_ref[0, hd * HEAD_DIM:(hd + 1) * HEAD_DIM, c0:c0 + sub]
        zero = jnp.zeros_like(qh)
        w = jnp.concatenate([qh, zero] if hd < rep else [zero, qh], axis=0)
        st = _dot(k, w)
        st_ref[u % 2] = st
        return jnp.max(st, axis=0, keepdims=True)

    m_next = scores(0)
    outs = []
    for u, (c0, hd) in enumerate(units):
        m = m_next
        if u + 1 < len(units):
            m_next = scores(u + 1)
        pt_ref[u % 2] = jnp.exp2(st_ref[u % 2] - m).astype(BF16)
        ot = _dot(vtg[hd // rep], pt_ref[u % 2])
        outs.append(ot[:HEAD_DIM] / ot[HEAD_DIM:HEAD_DIM + 1])
        if hd == ATT_HEADS - 1:
            o_ref[0, c0:c0 + sub, :] = jnp.concatenate(outs, axis=0).T.astype(o_ref.dtype)
            outs = []


def _attention(qt, k, vt, q_row0, n_q, k_row0, n_k, tq):
    bsz = qt.shape[0]
    kblk = k_row0 // n_k
    q_tile0 = q_row0 // tq
    n_q_tiles = n_q // tq
    assert kblk * n_k == k_row0 and q_tile0 * tq == q_row0 and n_q_tiles * tq == n_q
    assert ATT_KV_HEADS == 2
    return pl.pallas_call(
        _attn_kernel,
        grid=(bsz, n_q_tiles),
        in_specs=[
            pl.BlockSpec((1, ATT_W, tq), lambda b, j: (b, 0, q_tile0 + j)),
            pl.BlockSpec((1, n_k, KV_W), lambda b, j: (b, kblk, 0)),
            pl.BlockSpec((1, KV_W, n_k), lambda b, j: (b, 0, kblk)),
        ],
        out_specs=pl.BlockSpec((1, tq, ATT_W), lambda b, j: (b, j, 0)),
        out_shape=jax.ShapeDtypeStruct((bsz, n_q_tiles * tq, ATT_W), BF16),
        scratch_shapes=[pltpu.VMEM((2, n_k, min(tq, ATT_SUB)), F32), pltpu.VMEM((2, n_k, min(tq, ATT_SUB)), BF16)],
        compiler_params=_params("arbitrary", "arbitrary"),
        name="attention",
    )(qt, k, vt)


def _dwconv_kernel(u_ref, w_ref, b_ref, o_ref, *, n_lat, act):
    u = u_ref[0]
    t = u.shape[0]
    row = lax.broadcasted_iota(jnp.int32, u.shape, 0)
    prev = jnp.where((row == 0) | (row == n_lat), 0.0, pltpu.roll(u, 1, 0))
    nxt = jnp.where((row == n_lat - 1) | (row == t - 1), 0.0, pltpu.roll(u, t - 1, 0))
    w = w_ref[...]
    y = prev * w[0:1] + u * w[1:2] + nxt * w[2:3] + b_ref[...]
    if act:
        y = _silu(y)
    o_ref[0, 0] = y


def _dwconv(u, w, b, n_lat, act, parts):
    bsz, t, c = u.shape
    cpp = c // parts
    nb = cpp // LANES
    return pl.pallas_call(
        functools.partial(_dwconv_kernel, n_lat=n_lat, act=act),
        grid=(bsz, parts * nb),
        in_specs=[
            pl.BlockSpec((1, t, LANES), lambda b_, j: (b_, 0, j)),
            pl.BlockSpec((3, LANES), lambda b_, j: (0, j)),
            pl.BlockSpec((1, LANES), lambda b_, j: (0, j)),
        ],
        out_specs=pl.BlockSpec((1, 1, t, LANES), lambda b_, j: (j // nb, b_, 0, j % nb)),
        out_shape=jax.ShapeDtypeStruct((parts, bsz, t, cpp), F32),
        compiler_params=_params("arbitrary", "arbitrary"),
        name="dwconv",
    )(u, w, b.reshape(1, c))


def _ssd_kernel(xbc_ref, z_ref, dt_ref, dtb_ref, alog_ref, dsk_ref, nw_ref, bd_ref,
                o_ref, hf_ref, hb_ref, hbe_ref, *, n_lat_blocks):
    q = SSD_CHUNK
    nh = SSD_HEADS
    phase = pl.program_id(1)
    step = pl.program_id(2)
    block = jnp.where(phase == 0, n_lat_blocks - step, jnp.where(step < 1, n_lat_blocks, step - 1))

    lane = lax.broadcasted_iota(jnp.int32, (1, LANES), 1)
    a_row = jnp.where(lane < 2 * nh, -jnp.exp(alog_ref[...]), 0.0)
    tt = lax.broadcasted_iota(jnp.int32, (q, q), 0)
    ss = lax.broadcasted_iota(jnp.int32, (q, q), 1)
    lower = (ss <= tt)
    upper = (ss >= tt)
    lmat = jnp.where(lower, 1.0, 0.0).astype(BF16)
    umat = jnp.where(upper, 1.0, 0.0).astype(BF16)

    @pl.when(step == 0)
    def _():
        hf_ref[...] = jnp.zeros_like(hf_ref)
        hb_ref[...] = jnp.zeros_like(hb_ref)

    def load_chunk(ci):
        rows = slice(ci * q, (ci + 1) * q)
        xbc = xbc_ref[0, 0, rows, :]
        dt = _softplus(dt_ref[0, rows, :] + dtb_ref[...])
        return rows, xbc, dt, dt * a_row

    def heads_x(x):
        return [x[:, hd * SSD_HEAD_DIM:(hd + 1) * SSD_HEAD_DIM].astype(BF16) for hd in range(nh)]

    def backward_prepare(ci):
        _, xbc, dt, a = load_chunk(ci)
        bt = xbc[:, SSD_W:SSD_W + SSD_GROUPS * SSD_STATE].T
        dt_t = dt.T
        suf_t = _dot3_left(a.T, lmat)
        terms = []
        for hd in range(nh):
            g = hd // SSD_HEADS_PER_GROUP
            row_b = suf_t[nh + hd:nh + hd + 1, :]
            total = row_b[:, 0:1]
            w_t = jnp.exp(total - row_b) * dt_t[nh + hd:nh + hd + 1, :]
            terms.append((jnp.exp(total), (bt[g * SSD_STATE:(g + 1) * SSD_STATE, :] * w_t).astype(BF16)))
        return terms, heads_x(xbc[:, :SSD_W])

    def backward_state(ci, prepared):
        terms, xh = prepared
        chunk = block * SSD_BLOCK + ci
        for hd in range(nh):
            decay, bw = terms[hd]
            prev = hb_ref[hd]
            hbe_ref[chunk, hd] = prev
            hb_ref[hd] = prev * decay + _dot(bw, xh[hd])

    def forward_prepare(ci):
        rows, xbc, dt, a = load_chunk(ci)
        x = xbc[:, :SSD_W]
        bt = xbc[:, SSD_W:SSD_W + SSD_GROUPS * SSD_STATE].T
        cmat = xbc[:, SSD_W + SSD_GROUPS * SSD_STATE:]
        a_t = a.T
        pre = _dot3_right(lmat, a)
        pre_t = _dot3_left(a_t, umat)
        cgs = [cmat[:, g * SSD_STATE:(g + 1) * SSD_STATE].astype(BF16) for g in range(SSD_GROUPS)]
        btgs = [bt[g * SSD_STATE:(g + 1) * SSD_STATE, :] for g in range(SSD_GROUPS)]
        cbs = [_dot(cgs[g], btgs[g].astype(BF16)) for g in range(SSD_GROUPS)]
        return dict(rows=rows, x=x, xh=heads_x(x), a=a, a_t=a_t, dt_t=dt.T, pre=pre, pre_t=pre_t,
                    cgs=cgs, btgs=btgs, cbs=cbs)

    def forward_mix(p):
        pre, pre_t, dt_t = p["pre"], p["pre_t"], p["dt_t"]
        suf = pre[q - 1:q, :] - pre + p["a"]
        suf_t = pre_t[:, q - 1:q] - pre_t + p["a_t"]
        neg_inf = jnp.float32(-jnp.inf)
        p["y_in"], p["scale"], p["upd"] = [], [], []
        for hd in range(nh):
            g = hd // SSD_HEADS_PER_GROUP
            colf = pre[:, hd:hd + 1]
            rowf = pre_t[hd:hd + 1, :]
            colb = suf[:, nh + hd:nh + hd + 1]
            rowb = suf_t[nh + hd:nh + hd + 1, :]
            wf = jnp.exp(jnp.where(lower, colf - rowf, neg_inf)) * dt_t[hd:hd + 1, :]
            wb = jnp.exp(jnp.where(upper, colb - rowb, neg_inf)) * dt_t[nh + hd:nh + hd + 1, :]
            p["y_in"].append(_dot((p["cbs"][g] * (wf + wb)).astype(BF16), p["xh"][hd]))
            p["scale"].append((jnp.exp(colf), jnp.exp(colb)))
            total = rowf[:, q - 1:q]
            w_t = jnp.exp(total - rowf) * dt_t[hd:hd + 1, :]
            p["upd"].append((jnp.exp(total), (p["btgs"][g] * w_t).astype(BF16)))

    def forward_state(ci, p):
        chunk = block * SSD_BLOCK + ci
        rows, x = p["rows"], p["x"]
        ys = []
        for hd in range(nh):
            cg = p["cgs"][hd // SSD_HEADS_PER_GROUP]
            hf = hf_ref[hd]
            ef, eb = p["scale"][hd]
            y = p["y_in"][hd] + _dot(cg, hf.astype(BF16)) * ef + _dot(cg, hbe_ref[chunk, hd].astype(BF16)) * eb
            decay, bw = p["upd"][hd]
            hf_ref[hd] = hf * decay + _dot(bw, p["xh"][hd])
            ys.append(y)
        y = jnp.concatenate(ys, axis=-1) + x * dsk_ref[...]
        gz = y * _silu(z_ref[0, rows, :])
        ms = _dot((gz * gz).astype(BF16), bd_ref[...]) * (1.0 / (SSD_W // SSD_GROUPS))
        o_ref[0, rows, :] = (gz * lax.rsqrt(ms + EPS) * nw_ref[...]).astype(o_ref.dtype)

    @pl.when(phase == 0)
    def _():
        prepared = [backward_prepare(ci) for ci in range(SSD_BLOCK)]
        for ci in reversed(range(SSD_BLOCK)):
            backward_state(ci, prepared[ci])

    @pl.when(phase == 1)
    def _():
        prepared = [forward_prepare(ci) for ci in range(SSD_BLOCK)]
        for p in prepared:
            forward_mix(p)
        for ci in range(SSD_BLOCK):
            forward_state(ci, prepared[ci])


def _ssd(xbc, z, dt_raw, dt_bias, a_log, d_skip, norm_w, bd, n_lat):
    bsz, t, _ = z.shape
    rows = SSD_BLOCK * SSD_CHUNK
    n_lat_blocks = n_lat // rows
    n_blocks = t // rows
    assert n_blocks == n_lat_blocks + 1 and n_lat_blocks * rows == n_lat

    def block_of(p, s):
        return jnp.where(p == 0, n_lat_blocks - s, jnp.where(s < 1, n_lat_blocks, s - 1))

    def out_block(p, s):
        return jnp.where(p == 0, n_lat_blocks, block_of(p, s))

    const = lambda shape: pl.BlockSpec(shape, lambda b, p, s: tuple(0 for _ in shape))
    return pl.pallas_call(
        functools.partial(_ssd_kernel, n_lat_blocks=n_lat_blocks),
        grid=(bsz, 2, n_blocks),
        in_specs=[
            pl.BlockSpec((1, 1, rows, SSD_CONV_CH), lambda b, p, s: (0, b, block_of(p, s), 0)),
            pl.BlockSpec((1, rows, SSD_W), lambda b, p, s: (b, block_of(p, s), 0)),
            pl.BlockSpec((1, rows, LANES), lambda b, p, s: (b, block_of(p, s), 0)),
            const((1, LANES)),
            const((1, LANES)),
            const((1, SSD_W)),
            const((1, SSD_W)),
            const((SSD_W, SSD_W)),
        ],
        out_specs=pl.BlockSpec((1, rows, SSD_W), lambda b, p, s: (b, out_block(p, s), 0)),
        out_shape=jax.ShapeDtypeStruct((bsz, t, SSD_W), BF16),
        scratch_shapes=[
            pltpu.VMEM((SSD_HEADS, SSD_STATE, SSD_HEAD_DIM), F32),
            pltpu.VMEM((SSD_HEADS, SSD_STATE, SSD_HEAD_DIM), F32),
            pltpu.VMEM((n_blocks * SSD_BLOCK, SSD_HEADS, SSD_STATE, SSD_HEAD_DIM), F32),
        ],
        compiler_params=_params("arbitrary", "arbitrary", "arbitrary"),
        name="ssd",
    )(xbc, z, dt_raw, dt_bias, a_log, d_skip, norm_w, bd)


def _hyfilt_kernel(feat_ref, win_ref, w1_ref, b1_ref, fr_ref, w2_ref, b2_ref, w3_ref, o_ref):
    hp = lax.Precision.HIGHEST
    fr = fr_ref[...]
    h1 = jnp.sin(fr * (jnp.dot(feat_ref[...], w1_ref[...], precision=hp, preferred_element_type=F32) + b1_ref[...]))
    h2 = jnp.sin(fr * (jnp.dot(h1, w2_ref[...], precision=hp, preferred_element_type=F32) + b2_ref[...]))
    h = _dot_x3(h2, w3_ref[...])
    win = win_ref[...]
    first_tile = pl.program_id(0) == 0
    row = lax.broadcasted_iota(jnp.int32, win.shape, 0)
    for order in range(HY_ORDER):
        for direction in range(2):
            c0 = (direction * HY_ORDER + order) * HY_W
            f = h[:, c0:c0 + HY_W] * win
            if direction == 1:
                f = jnp.where(first_tile & (row == 0), 0.0, f)
            o_ref[order * 2 + direction] = f


def _hy_positional(length):
    n = np.arange(length, dtype=np.float64)
    t = n / max(length - 1, 1)
    bands = np.linspace(1e-4, HY_BANDS - 1, HY_BANDS)
    wpos = (2 * math.pi / length) * n
    feats = np.concatenate([t[:, None], np.cos(wpos[:, None] * bands), -np.sin(wpos[:, None] * bands)], axis=-1)
    feats = np.pad(feats, ((0, 0), (0, LANES - HY_POS_DIM)))
    deltas = np.abs(np.linspace(math.log(HY_TARGET) / HY_SLOW_DECAY, math.log(HY_TARGET) / HY_FAST_DECAY, HY_W))
    window = np.exp(-t[:, None] * deltas)
    return jnp.asarray(feats, F32), jnp.asarray(window, F32)


def _hyena_filters(length, w1, b1, freq, w2, b2, w3):
    feats, window = _hy_positional(length)
    tl = min(length, 512)
    hid = HY_FILTER_HID
    w1p = jnp.pad(w1, ((0, LANES - HY_POS_DIM), (0, 0)))
    const = lambda shape: pl.BlockSpec(shape, lambda i: tuple(0 for _ in shape))
    return pl.pallas_call(
        _hyfilt_kernel,
        grid=(length // tl,),
        in_specs=[
            pl.BlockSpec((tl, LANES), lambda i: (i, 0)),
            pl.BlockSpec((tl, HY_W), lambda i: (i, 0)),
            const((LANES, hid)), const((1, hid)), const((1, hid)),
            const((hid, hid)), const((1, hid)), const((hid, 2 * HY_ORDER * HY_W)),
        ],
        out_specs=pl.BlockSpec((2 * HY_ORDER, tl, HY_W), lambda i: (0, i, 0)),
        out_shape=jax.ShapeDtypeStruct((2 * HY_ORDER, length, HY_W), F32),
        compiler_params=_params("arbitrary"),
        name="hyena_filters",
    )(feats, window, w1p, b1.reshape(1, hid), freq.reshape(1, hid), w2, b2.reshape(1, hid), w3)


def _dft_tables(length):
    n = 2 * length
    n2 = HY_N2
    n1 = n // n2
    half = n1 // 2
    k1 = np.arange(half, dtype=np.float64) + 0.5
    idx = (n2 * np.arange(half)[None, None, :] + np.arange(n2)[:, None, None])
    ang = 2 * np.pi * k1[None, :, None] * idx / n
    fa = np.concatenate([np.cos(ang), -np.sin(ang)], axis=1)
    kk = np.arange(n2, dtype=np.float64)
    angb = 2 * np.pi * kk[:, None] * kk[None, :] / n2
    cr, sr = np.cos(angb), np.sin(angb)
    fb = np.block([[cr, sr], [-sr, cr]])
    fbi = np.block([[cr, -sr], [sr, cr]])
    idxo = (n2 * np.arange(half)[None, :, None] + np.arange(n2)[:, None, None])
    ango = 2 * np.pi * k1[None, None, :] * idxo / n
    ga = np.concatenate([np.cos(ango), -np.sin(ango)], axis=2) * (2.0 / n)
    f32 = lambda a: jnp.asarray(a, F32).astype(BF16)
    return f32(fa), f32(fb), f32(fbi), f32(ga)


def _conv3_rows(u, w, b):
    n = u.shape[0]
    row = lax.broadcasted_iota(jnp.int32, u.shape, 0)
    prev = jnp.where(row == 0, 0.0, pltpu.roll(u, 1, 0))
    nxt = jnp.where(row == n - 1, 0.0, pltpu.roll(u, n - 1, 0))
    return prev * w[0:1] + u * w[1:2] + nxt * w[2:3] + b


def _seq_pitch(n2n):
    return n2n + SUBLANES


def _to_pitched(dst_ref, val, n2n):
    pitch = _seq_pitch(n2n)
    for i in range(val.shape[0] // n2n):
        dst_ref[i * pitch:i * pitch + n2n, :] = val[i * n2n:(i + 1) * n2n, :]


def _hy_forward_a(src_ref, a_ref, fa_ref):
    n2n, two_n1, half = fa_ref.shape
    zp, ap = _seq_pitch(n2n), _seq_pitch(two_n1)

    def body(n2, carry):
        zs = src_ref[pl.ds(n2, half, stride=zp), :].astype(BF16)
        a_ref[pl.ds(pl.multiple_of(n2 * ap, SUBLANES), two_n1), :] = _dot(fa_ref[n2], zs)
        return carry

    lax.fori_loop(0, n2n, body, 0, unroll=HY_UNROLL)


def _hy_kspec_kernel(hf_ref, hb_ref, fa_ref, fb_ref, o_ref, af_ref, ab_ref, hfp_ref, hbp_ref):
    n2n, two_n1, _ = fa_ref.shape
    n1 = two_n1 // 2
    ap = _seq_pitch(two_n1)
    _to_pitched(hfp_ref, hf_ref[0], n2n)
    _to_pitched(hbp_ref, hb_ref[0], n2n)
    _hy_forward_a(hfp_ref, af_ref, fa_ref)
    _hy_forward_a(hbp_ref, ab_ref, fa_ref)
    fb = fb_ref[...]

    def body(k1, carry):
        def spectrum(a_ref):
            ar = a_ref[pl.ds(k1, n2n, stride=ap), :]
            ai = a_ref[pl.ds(n1 + k1, n2n, stride=ap), :]
            return _dot(fb, jnp.concatenate([ar, ai], axis=0).astype(BF16))

        xf, xb = spectrum(af_ref), spectrum(ab_ref)
        o_ref[0, k1] = jnp.concatenate([xf[:n2n] + xb[:n2n], xf[n2n:] - xb[n2n:]], axis=0)
        return carry

    lax.fori_loop(0, n1, body, 0, unroll=HY_UNROLL)


def _hy_kspec(filt, tables):
    fa, fb, _, _ = tables
    _, length, _ = filt.shape
    n2n, two_n1, _ = fa.shape
    n1 = two_n1 // 2
    nh = HY_W // LANES
    return pl.pallas_call(
        _hy_kspec_kernel,
        grid=(HY_ORDER, nh),
        in_specs=[
            pl.BlockSpec((1, length, LANES), lambda o, h: (2 * o, 0, h)),
            pl.BlockSpec((1, length, LANES), lambda o, h: (2 * o + 1, 0, h)),
            pl.BlockSpec(fa.shape, lambda o, h: (0, 0, 0), pipeline_mode=pl.Buffered(1)),
            pl.BlockSpec(fb.shape, lambda o, h: (0, 0)),
        ],
        out_specs=pl.BlockSpec((1, n1, 2 * n2n, LANES), lambda o, h: (o, 0, 0, h)),
        out_shape=jax.ShapeDtypeStruct((HY_ORDER, n1, 2 * n2n, HY_W), F32),
        scratch_shapes=[pltpu.VMEM((n2n * _seq_pitch(two_n1), LANES), F32),
                        pltpu.VMEM((n2n * _seq_pitch(two_n1), LANES), F32),
                        pltpu.VMEM((length // n2n * _seq_pitch(n2n), LANES), F32),
                        pltpu.VMEM((length // n2n * _seq_pitch(n2n), LANES), F32)],
        compiler_params=_params("arbitrary", "arbitrary"),
        name="hyena_kspec",
    )(filt, filt, fa, fb)


def _hy_conv_kernel(z_ref, g_ref, wz_ref, bz_ref, wg_ref, bg_ref, fa_ref, fb_ref, fbi_ref, ga_ref, k_ref, bias_ref,
                    o_ref, a_ref, c_ref, zc_ref, gc_ref, *, z_conv):
    n2n, two_n1, half = fa_ref.shape
    n1 = two_n1 // 2
    zp, ap = _seq_pitch(n2n), _seq_pitch(two_n1)
    _to_pitched(zc_ref, _conv3_rows(z_ref[0], wz_ref[...], bz_ref[...]) if z_conv else z_ref[0], n2n)
    _to_pitched(gc_ref, _conv3_rows(g_ref[0], wg_ref[...], bg_ref[...]), n2n)
    _hy_forward_a(zc_ref, a_ref, fa_ref)
    fb = fb_ref[...]
    fbi = fbi_ref[...]

    def body_b(grp, carry):
        k1s = [grp * HY_UNROLL + i for i in range(HY_UNROLL)]
        sl = [(pl.ds(k1, n2n, stride=ap), pl.ds(n1 + k1, n2n, stride=ap)) for k1 in k1s]
        xs = [_dot(fb, jnp.concatenate([a_ref[re, :], a_ref[im, :]], axis=0).astype(BF16)) for re, im in sl]
        ys = []
        for k1, x in zip(k1s, xs):
            kk = k_ref[0, k1]
            xr, xi = x[:n2n], x[n2n:]
            kr, ki = kk[:n2n], kk[n2n:]
            ys.append(jnp.concatenate([xr * kr - xi * ki, xr * ki + xi * kr], axis=0).astype(BF16))
        for (re, im), y in zip(sl, ys):
            c = _dot(fbi, y)
            c_ref[re, :] = c[:n2n]
            c_ref[im, :] = c[n2n:]
        return carry

    lax.fori_loop(0, n1 // HY_UNROLL, body_b, 0)
    bias = bias_ref[0]

    def body_c(m2, carry):
        rows = pl.ds(m2, half, stride=zp)
        c = c_ref[pl.ds(pl.multiple_of(m2 * ap, SUBLANES), two_n1), :].astype(BF16)
        y = _dot(ga_ref[m2], c)
        a_ref[rows, :] = gc_ref[rows, :] * (y + zc_ref[rows, :] * bias)
        return carry

    lax.fori_loop(0, n2n, body_c, 0, unroll=HY_UNROLL)
    for i in range(half):
        o_ref[0, i * n2n:(i + 1) * n2n, :] = a_ref[i * zp:i * zp + n2n, :]


def _hy_conv(z_arr, z_blk, g_arr, g_blk, w, b, kspec, order, bias3, tables, n_lat, z_conv):
    fa, fb, fbi, ga = tables
    bsz = z_arr.shape[0]
    n2n, two_n1, half = fa.shape
    n1 = two_n1 // 2
    nh = HY_W // LANES
    zw = z_blk if z_conv else g_blk
    resident = lambda shape: pl.BlockSpec(shape, lambda h, b_: tuple(0 for _ in shape), pipeline_mode=pl.Buffered(1))
    return pl.pallas_call(
        functools.partial(_hy_conv_kernel, z_conv=z_conv),
        grid=(nh, bsz),
        in_specs=[
            pl.BlockSpec((1, n_lat, LANES), lambda h, b_: (b_, 0, z_blk + h)),
            pl.BlockSpec((1, n_lat, LANES), lambda h, b_: (b_, 0, g_blk + h)),
            pl.BlockSpec((3, LANES), lambda h, b_: (0, zw + h)),
            pl.BlockSpec((1, LANES), lambda h, b_: (0, zw + h)),
            pl.BlockSpec((3, LANES), lambda h, b_: (0, g_blk + h)),
            pl.BlockSpec((1, LANES), lambda h, b_: (0, g_blk + h)),
            resident(fa.shape), resident(fb.shape), resident(fbi.shape), resident(ga.shape),
            pl.BlockSpec((1, n1, 2 * n2n, LANES), lambda h, b_: (order, 0, 0, h), pipeline_mode=pl.Buffered(1)),
            pl.BlockSpec((1, 1, LANES), lambda h, b_: (order, 0, h)),
        ],
        out_specs=pl.BlockSpec((1, n_lat, LANES), lambda h, b_: (b_, 0, h)),
        out_shape=jax.ShapeDtypeStruct((bsz, n_lat, HY_W), F32),
        scratch_shapes=[pltpu.VMEM((n2n * _seq_pitch(two_n1), LANES), F32),
                        pltpu.VMEM((n2n * _seq_pitch(two_n1), LANES), F32),
                        pltpu.VMEM((half * _seq_pitch(n2n), LANES), F32),
                        pltpu.VMEM((half * _seq_pitch(n2n), LANES), F32)],
        compiler_params=_params("arbitrary", "arbitrary"),
        name="hyena_conv",
    )(z_arr, g_arr, w, b, w, b, fa, fb, fbi, ga, kspec, bias3)


def _hyena_ctx_kernel(u_ref, w_ref, b_ref, filt_ref, bias_ref, fd_ref, gd_ref, o_ref, *, n_ctx):
    fd = fd_ref[...]
    gd = gd_ref[...]
    nn = 2 * n_ctx
    u = _conv3_rows(u_ref[0], w_ref[...], b_ref[...])
    v, x1, x2 = u[:, :HY_W], u[:, HY_W:2 * HY_W], u[:, 2 * HY_W:]

    def conv(zin, order):
        hf = _dot(fd, filt_ref[2 * order].astype(BF16))
        hb = _dot(fd, filt_ref[2 * order + 1].astype(BF16))
        kr, ki = hf[:nn] + hb[:nn], hf[nn:] - hb[nn:]
        zz = _dot(fd, zin.astype(BF16))
        zr, zi = zz[:nn], zz[nn:]
        y = jnp.concatenate([zr * kr - zi * ki, zr * ki + zi * kr], axis=0).astype(BF16)
        return _dot(gd, y) + zin * bias_ref[order]

    y1 = x1 * conv(v, 0)
    o_ref[0] = x2 * conv(y1, 1)


def _hyena_ctx(hy, conv_w, conv_b, filt, bias, n_lat, n_ctx):
    bsz = hy.shape[0]
    nn = 2 * n_ctx
    k = np.arange(nn, dtype=np.float64)
    t = np.arange(n_ctx, dtype=np.float64)
    ang = 2 * np.pi * k[:, None] * t[None, :] / nn
    fd = jnp.asarray(np.concatenate([np.cos(ang), -np.sin(ang)], axis=0), F32).astype(BF16)
    gd = jnp.asarray(np.concatenate([np.cos(ang.T), -np.sin(ang.T)], axis=1) / nn, F32).astype(BF16)
    blk = n_lat // n_ctx
    assert blk * n_ctx == n_lat
    return pl.pallas_call(
        functools.partial(_hyena_ctx_kernel, n_ctx=n_ctx),
        grid=(bsz,),
        in_specs=[
            pl.BlockSpec((1, n_ctx, 3 * HY_W), lambda b: (b, blk, 0)),
            pl.BlockSpec((3, 3 * HY_W), lambda b: (0, 0)),
            pl.BlockSpec((1, 3 * HY_W), lambda b: (0, 0)),
            pl.BlockSpec((2 * HY_ORDER, n_ctx, HY_W), lambda b: (0, 0, 0)),
            pl.BlockSpec((HY_ORDER, 1, HY_W), lambda b: (0, 0, 0)),
            pl.BlockSpec((2 * nn, n_ctx), lambda b: (0, 0)),
            pl.BlockSpec((n_ctx, 2 * nn), lambda b: (0, 0)),
        ],
        out_specs=pl.BlockSpec((1, n_ctx, HY_W), lambda b: (b, 0, 0)),
        out_shape=jax.ShapeDtypeStruct((bsz, n_ctx, HY_W), F32),
        compiler_params=_params("arbitrary"),
        name="hyena_ctx",
    )(hy, conv_w, conv_b.reshape(1, -1), filt, bias, fd, gd)


def _hyena_latent(hy, conv_w, conv_b, filt, bias, n_lat, tables):
    kspec = _hy_kspec(filt, tables)
    bias3 = bias.reshape(HY_ORDER, 1, HY_W)
    cb = conv_b.reshape(1, -1)
    nb = HY_W // LANES
    y1 = _hy_conv(hy, 0, hy, nb, conv_w, cb, kspec, 0, bias3, tables, n_lat, True)
    return _hy_conv(y1, 0, hy, 2 * nb, conv_w, cb, kspec, 1, bias3, tables, n_lat, False)


def _outproj_kernel(xl_ref, xc_ref, attl_ref, attc_ref, ssd_ref, hyl_ref, hyc_ref, gt_ref, wa_ref, ws_ref, wh_ref,
                    o_ref, *, n_lat_tiles):
    x = _stream_tile(xl_ref, xc_ref, n_lat_tiles)
    att = _stream_tile(attl_ref, attc_ref, n_lat_tiles)
    hy = _stream_tile(hyl_ref, hyc_ref, n_lat_tiles).astype(BF16)
    mix = _dot(att, wa_ref[...]) + _dot(ssd_ref[0], ws_ref[...]) + _dot(hy, wh_ref[...])
    o_ref[0] = x + gt_ref[0] * mix


def _outproj(x_lat, x_ctx, ctx_blk, att_l, att_c, ssd, hy_l, hy_c, mods, w_out, n_lat_tiles, n_tiles):
    bsz, _, d = x_lat.shape
    tm = TOKEN_TILE

    def mod_row(j, b):
        return jnp.where(j >= n_lat_tiles, bsz, b)

    tok = lambda w: pl.BlockSpec((1, tm, w), lambda j, b: (b, j, 0))
    return pl.pallas_call(
        functools.partial(_outproj_kernel, n_lat_tiles=n_lat_tiles),
        grid=(n_tiles, bsz),
        in_specs=[
            *_stream_specs(n_lat_tiles, ctx_blk, d),
            *_stream_specs(n_lat_tiles, 0, ATT_W),
            tok(SSD_W),
            *_stream_specs(n_lat_tiles, 0, HY_W),
            pl.BlockSpec((1, 1, d), lambda j, b: (mod_row(j, b), 0, 2)),
            pl.BlockSpec((ATT_W, d), lambda j, b: (0, 0)),
            pl.BlockSpec((SSD_W, d), lambda j, b: (0, 0)),
            pl.BlockSpec((HY_W, d), lambda j, b: (0, 0)),
        ],
        out_specs=tok(d),
        out_shape=jax.ShapeDtypeStruct((bsz, n_tiles * tm, d), F32),
        compiler_params=_params("arbitrary", "arbitrary"),
        name="outproj",
    )(x_lat, x_ctx, att_l, att_c, ssd, hy_l, hy_c, mods,
      w_out[:ATT_W], w_out[ATT_W:ATT_W + SSD_W], w_out[ATT_W + SSD_W:])


def _route(logits_t, bias_col):
    scores = jax.nn.sigmoid(logits_t)
    sel = scores + bias_col
    neg_inf = jnp.float32(-jnp.inf)
    rows = [sel[e:e + 1, :] for e in range(N_EXPERTS)]
    grp = []
    for g in range(N_GROUPS):
        r = rows[g * EXPERTS_PER_GROUP:(g + 1) * EXPERTS_PER_GROUP]
        top = functools.reduce(jnp.maximum, r)
        taken = None
        rest = []
        for ri in r:
            is_top = (ri == top) if taken is None else (ri == top) & jnp.logical_not(taken)
            rest.append(jnp.where(is_top, neg_inf, ri))
            taken = is_top if taken is None else taken | is_top
        grp.append(top + functools.reduce(jnp.maximum, rest))
    best = jnp.zeros(grp[0].shape, jnp.int32)
    cur = grp[0]
    for g in range(1, N_GROUPS):
        upd = grp[g] > cur
        best = jnp.where(upd, g, best)
        cur = jnp.where(upd, grp[g], cur)
    picked = []
    for e in range(N_EXPERTS):
        g, i = divmod(e, EXPERTS_PER_GROUP)
        rank = jnp.zeros(best.shape, jnp.int32)
        for j in range(EXPERTS_PER_GROUP):
            if j == i:
                continue
            other = rows[g * EXPERTS_PER_GROUP + j]
            ahead = (other > rows[e]) | ((other == rows[e]) & (j < i))
            rank = rank + ahead.astype(jnp.int32)
        keep = (best == g) & (rank < 2)
        picked.append(jnp.where(keep, scores[e:e + 1, :], 0.0))
    total = functools.reduce(lambda u, w: u + w, picked)
    return jnp.concatenate(picked, axis=0) / total, best


def _ffn_input(x_ref, sh_ref, sc_ref, g_ref):
    x = x_ref[...].reshape(MOE_TILE, D_MODEL)
    ms = jnp.mean(x * x, axis=-1, keepdims=True)
    t = x * lax.rsqrt(ms + EPS) * g_ref[...]
    return x, t * (1.0 + sc_ref[0]) + sh_ref[0]


def _route_kernel(x_ref, sh_ref, sc_ref, g_ref, wr_ref, rb_ref, rt_ref, srow_ref, tbl_ref):
    tm = MOE_TILE
    _, t = _ffn_input(x_ref, sh_ref, sc_ref, g_ref)
    logits = _dot_x3(t, wr_ref[...])
    comb_t, best = _route(logits.T[:N_EXPERTS, :], rb_ref[...])

    member = [jnp.where(best == g, 1.0, 0.0) for g in range(N_GROUPS)]
    comb4 = functools.reduce(
        lambda u, w: u + w,
        [member[g] * comb_t[g * EXPERTS_PER_GROUP:(g + 1) * EXPERTS_PER_GROUP, :] for g in range(N_GROUPS)])
    masks = jnp.concatenate(member + [jnp.zeros((8 - N_GROUPS, tm), F32)], axis=0)
    earlier = jnp.where(lax.broadcasted_iota(jnp.int32, (tm, tm), 0) < lax.broadcasted_iota(jnp.int32, (tm, tm), 1),
                        1.0, 0.0).astype(BF16)
    ranks = _dot(masks.astype(BF16), earlier)
    cnt = jnp.sum(masks, axis=1, keepdims=True)
    padded = jnp.ceil(cnt * (1.0 / MOE_CHUNK)) * MOE_CHUNK
    lane = lax.broadcasted_iota(jnp.int32, (1, LANES), 1).astype(F32) * MOE_CHUNK
    start = jnp.zeros((1, 1), F32)
    slot = jnp.zeros((1, tm), F32)
    gid = jnp.zeros((1, LANES), jnp.int32)
    for g in range(N_GROUPS):
        slot = slot + member[g] * (start + ranks[g:g + 1, :])
        start = start + padded[g:g + 1, :]
        gid = gid + jnp.where(lane >= start, 1, 0)
    tbl_ref[0] = gid
    srow_ref[0] = slot.astype(jnp.int32)
    rt_t = jnp.concatenate([comb4, slot, jnp.zeros((LANES - EXPERTS_PER_GROUP - 1, tm), F32)], axis=0)
    rt_ref[...] = rt_t.T


def _experts_kernel(tbl_ref, x_ref, sh_ref, sc_ref, gt_ref, g_ref, rt_ref, srow_ref, wg_ref, wu_ref, wd_ref, gf_ref,
                    *rest, final, tile_of, n_grid):
    o_ref, xp_ref, yp_ref = rest[-3:]
    tm, ch = MOE_TILE, MOE_CHUNK
    n_slots = MOE_CHUNKS * ch
    tile = tile_of(*[pl.program_id(a) for a in range(n_grid)])
    x, t = _ffn_input(x_ref, sh_ref, sc_ref, g_ref)
    rt = rt_ref[...]
    slot_col = rt[:, EXPERTS_PER_GROUP:EXPERTS_PER_GROUP + 1].astype(jnp.int32)
    gather = jnp.where(lax.broadcasted_iota(jnp.int32, (n_slots, tm), 0) == srow_ref[0], 1.0, 0.0).astype(BF16)
    scatter = jnp.where(lax.broadcasted_iota(jnp.int32, (tm, n_slots), 1) == slot_col, 1.0, 0.0).astype(BF16)
    xp_ref[...] = _dot(gather, t.astype(BF16)).astype(BF16)
    w_slot = _dot3_right(gather, rt)

    def run_chunk(rows, grp):
        xc = xp_ref[rows, :]
        e0 = grp * EXPERTS_PER_GROUP
        gu_next = _dot(xc, wg_ref[e0]), _dot(xc, wu_ref[e0])
        acc = jnp.zeros((ch, D_MODEL), F32)
        for j in range(EXPERTS_PER_GROUP):
            gate, up = gu_next
            if j + 1 < EXPERTS_PER_GROUP:
                gu_next = _dot(xc, wg_ref[e0 + j + 1]), _dot(xc, wu_ref[e0 + j + 1])
            hid = _silu(gate) * up * w_slot[rows, j:j + 1]
            acc = acc + _dot(hid.astype(BF16), wd_ref[e0 + j])
        yp_ref[rows, :] = acc.astype(BF16)

    for c in range(MOE_CHUNKS):
        rows = slice(c * ch, (c + 1) * ch)
        grp = tbl_ref[tile, c]
        pl.when(grp < N_GROUPS)(functools.partial(run_chunk, rows, grp))

        @pl.when(grp >= N_GROUPS)
        def _():
            yp_ref[rows, :] = jnp.zeros((ch, D_MODEL), BF16)

    y = x + gt_ref[0] * _dot(scatter, yp_ref[...])
    if final:
        y = y * lax.rsqrt(jnp.mean(y * y, axis=-1, keepdims=True) + EPS) * gf_ref[...]
    o_ref[...] = y.reshape(o_ref.shape)


def _moe_tiles(x, mods, mod_row, g_ffn, w_router, router_bias, wgu, wd, g_final, final, out_rows,
               grid, x_block, x_index, out_index, tile_of, n_tiles):
    bsz, _, d = x.shape
    tm = MOE_TILE
    ng = len(grid)
    const = lambda shape: pl.BlockSpec(shape, lambda *a: tuple(0 for _ in shape))
    mod = lambda col: pl.BlockSpec((1, 1, d), lambda *a: (mod_row(*a[:ng]), 0, col))
    xspec = pl.BlockSpec(x_block, lambda *a: x_index(*a[:ng]))
    rt, srow, tbl = pl.pallas_call(
        _route_kernel,
        grid=grid,
        in_specs=[xspec, mod(3), mod(4), const((1, d)), const((d, LANES)), const((N_EXPERTS, 1))],
        out_specs=[
            pl.BlockSpec((tm, LANES), lambda *a: (tile_of(*a), 0)),
            pl.BlockSpec((1, 1, tm), lambda *a: (tile_of(*a), 0, 0)),
            pl.BlockSpec((1, 1, LANES), lambda *a: (tile_of(*a), 0, 0)),
        ],
        out_shape=[
            jax.ShapeDtypeStruct((n_tiles * tm, LANES), F32),
            jax.ShapeDtypeStruct((n_tiles, 1, tm), jnp.int32),
            jax.ShapeDtypeStruct((n_tiles, 1, LANES), jnp.int32),
        ],
        compiler_params=_params(*["arbitrary"] * ng),
        name="moe_route",
    )(x, mods, mods, g_ffn, w_router, router_bias)

    resident = lambda shape: pl.BlockSpec(shape, lambda *a: tuple(0 for _ in shape), pipeline_mode=pl.Buffered(1))
    in_specs = [
        xspec, mod(3), mod(4), mod(5), const((1, d)),
        pl.BlockSpec((tm, LANES), lambda *a: (tile_of(*a[:ng]), 0)),
        pl.BlockSpec((1, 1, tm), lambda *a: (tile_of(*a[:ng]), 0, 0)),
        resident((N_EXPERTS, d, D_FF)),
        resident((N_EXPERTS, d, D_FF)),
        resident((N_EXPERTS, D_FF, d)),
        const((1, d)),
    ]
    args = [tbl.reshape(n_tiles, LANES), x, mods, mods, mods, g_ffn, rt, srow, *wgu, wd, g_final]
    return pl.pallas_call(
        functools.partial(_experts_kernel, final=final, tile_of=tile_of, n_grid=ng),
        grid_spec=pltpu.PrefetchScalarGridSpec(
            num_scalar_prefetch=1,
            grid=grid,
            in_specs=in_specs,
            out_specs=pl.BlockSpec(x_block, lambda *a: out_index(*a[:ng])),
            scratch_shapes=[pltpu.VMEM((MOE_CHUNKS * MOE_CHUNK, d), BF16), pltpu.VMEM((MOE_CHUNKS * MOE_CHUNK, d), BF16)],
        ),
        out_shape=jax.ShapeDtypeStruct((bsz, out_rows, d), F32),
        compiler_params=_params(*["arbitrary"] * ng),
        name="moe_experts",
    )(*args)


def _moe(x, mods, g_ffn, w_router, router_bias, wgu, wd, g_final, n_lat, n_ctx, with_ctx, final):
    bsz, _, d = x.shape
    tm = MOE_TILE
    per_b = n_lat // tm
    common = (g_ffn, w_router, router_bias, wgu, wd, g_final, final)
    lat_index = lambda b, j: (b, j, 0)
    out_lat = _moe_tiles(x, mods, lambda b, j: b, *common, n_lat,
                         (bsz, per_b), (1, tm, d), lat_index, lat_index, lambda b, j: b * per_b + j, bsz * per_b)
    if not with_ctx:
        return out_lat, None
    nb = tm // n_ctx
    out_ctx = _moe_tiles(x, mods, lambda i: bsz, *common, n_ctx,
                         (bsz // nb,), (nb, n_ctx, d), lambda i: (i, n_lat // n_ctx, 0), lambda i: (i, 0, 0),
                         lambda i: i, bsz // nb)
    return out_lat, out_ctx


def _block_ones(width, block):
    idx = np.arange(width) // block
    return jnp.asarray(idx[:, None] == idx[None, :], F32).astype(BF16)


def _rope_tables(n_lat, n_ctx):
    rows = n_lat // GRID_W
    row = np.repeat(np.arange(rows), GRID_W).astype(np.float64)
    col = np.tile(np.arange(GRID_W), rows).astype(np.float64)
    inv = ROPE_THETA ** (-np.arange(0, ROPE_AXIS_DIM, 2, dtype=np.float64) / ROPE_AXIS_DIM)
    ang = np.concatenate([row[:, None] * inv, col[:, None] * inv], axis=-1)
    ang = np.concatenate([ang, np.zeros((n_ctx, ang.shape[1]))], axis=0)
    cos = np.concatenate([np.cos(ang), np.cos(ang)], axis=-1)
    sin = np.concatenate([-np.sin(ang), np.sin(ang)], axis=-1)
    scale = Q_SCALE
    cs = np.tile(cos, (1, ATT_HEADS)) * scale
    sn = np.tile(sin, (1, ATT_HEADS)) * scale
    return jnp.asarray(cs, F32), jnp.asarray(sn, F32)


def _pack_w_in(w):
    cuts = np.cumsum([ATT_W, KV_W, KV_W, SSD_W, SSD_CONV_CH, 2 * SSD_HEADS])
    q, k, v, z, xbc, dt, hy = jnp.split(w, [int(c) for c in cuts], axis=-1)
    dt = jnp.pad(dt, ((0, 0), (0, LANES - 2 * SSD_HEADS)))
    return jnp.concatenate([q, k, v, z, xbc, dt, hy], axis=-1).astype(BF16)


def kernel(x, c, ctx, c_ctx, w_mod, b_mod, g_mix, g_ffn, w_in, q_norm, k_norm, ssd_conv_w, ssd_conv_b,
           ssd_dt_bias, ssd_a_log, ssd_d, ssd_norm, hy_conv_w, hy_conv_b, hy_w1, hy_b1, hy_freq, hy_w2, hy_b2,
           hy_w3, hy_bias, w_out, w_router, router_bias, w_gate, w_up, w_down, g_final):
    bsz, n_lat, d = x.shape
    n_ctx = ctx.shape[1]
    depth = w_mod.shape[0]
    t = n_lat + n_ctx
    tm = TOKEN_TILE
    n_lat_tiles = n_lat // tm
    n_tiles = t // tm
    assert n_ctx == tm and n_lat % (HY_N2 * 8) == 0 and bsz < MOD_ROWS

    cvec = jnp.concatenate([c, c_ctx[None], jnp.zeros((MOD_ROWS - bsz - 1, d), F32)], axis=0)
    mods_all = _adaln(cvec, w_mod, b_mod)

    cs, sn = _rope_tables(n_lat, n_ctx)
    bd_head = _block_ones(ATT_W, HEAD_DIM)
    bd_ssd = _block_ones(SSD_W, SSD_W // SSD_GROUPS)
    tables = _dft_tables(n_lat)
    pad_row = lambda v: jnp.pad(v.reshape(1, -1), ((0, 0), (0, LANES - v.size)))

    x_lat, x_ctx, ctx_blk = x, ctx, 0
    for i in range(depth):
        last = i == depth - 1
        mods = mods_all[i].reshape(MOD_ROWS, 1, 6 * d)
        q, kt, v, z, xbc, dt_raw, hy = _inproj(
            x_lat, x_ctx, ctx_blk, mods, g_mix[i].reshape(1, d), _pack_w_in(w_in[i]),
            jnp.tile(q_norm[i], ATT_HEADS).reshape(1, ATT_W), jnp.tile(k_norm[i], ATT_KV_HEADS).reshape(1, KV_W),
            cs, sn, bd_head, n_lat_tiles)

        att = _attention(q, kt, v, 0, n_lat, 0, t, ATT_TILE)
        att_c = _attention(q, kt, v, n_lat, n_ctx, n_lat, n_ctx, n_ctx) if not last else att

        xbc_c = _dwconv(xbc, ssd_conv_w[i], ssd_conv_b[i], n_lat, True, 1)
        ssd = _ssd(xbc_c, z, dt_raw, pad_row(ssd_dt_bias[i]), pad_row(ssd_a_log[i]),
                   jnp.repeat(ssd_d[i], SSD_HEAD_DIM).reshape(1, SSD_W), ssd_norm[i].reshape(1, SSD_W),
                   bd_ssd, n_lat)

        hyp = (hy_w1[i], hy_b1[i], hy_freq[i], hy_w2[i], hy_b2[i], hy_w3[i])
        hy_l = _hyena_latent(hy, hy_conv_w[i], hy_conv_b[i], _hyena_filters(n_lat, *hyp), hy_bias[i], n_lat, tables)
        hy_c = hy_l if last else _hyena_ctx(hy, hy_conv_w[i], hy_conv_b[i], _hyena_filters(n_ctx, *hyp),
                                            hy_bias[i].reshape(HY_ORDER, 1, HY_W), n_lat, n_ctx)

        live_tiles = n_lat_tiles if last else n_tiles
        xs = _outproj(x_lat, x_ctx, ctx_blk, att, att_c, ssd, hy_l, hy_c, mods, w_out[i].astype(BF16),
                      n_lat_tiles, live_tiles)

        wgu = (w_gate[i].astype(BF16), w_up[i].astype(BF16))
        x_lat, x_ctx = _moe(xs, mods, g_ffn[i].reshape(1, d), jnp.pad(w_router, ((0, 0), (0, LANES - N_EXPERTS))),
                            router_bias.reshape(N_EXPERTS, 1), wgu, w_down[i].astype(BF16), g_final.reshape(1, d),
                            n_lat, n_ctx, not last, last)
    return x_lat
```

```python
import functools
import math

import jax
import jax.numpy as jnp
import numpy as np
from jax import lax
from jax.experimental import pallas as pl
from jax.experimental.pallas import tpu as pltpu

F32 = jnp.float32
BF16 = jnp.bfloat16

D_MODEL = 1024
GRID_W = 64
EPS = 1e-6

ATT_HEADS = 6
ATT_KV_HEADS = 2
HEAD_DIM = 64
ATT_W = ATT_HEADS * HEAD_DIM
KV_W = ATT_KV_HEADS * HEAD_DIM
ROPE_AXIS_DIM = HEAD_DIM // 2
ROPE_THETA = 10000.0
Q_SCALE = HEAD_DIM ** -0.5 * math.log2(math.e)

SSD_HEADS = 6
SSD_HEAD_DIM = 64
SSD_W = SSD_HEADS * SSD_HEAD_DIM
SSD_GROUPS = 2
SSD_STATE = 64
SSD_CHUNK = 128
SSD_CONV_CH = SSD_W + 2 * SSD_GROUPS * SSD_STATE
SSD_HEADS_PER_GROUP = SSD_HEADS // SSD_GROUPS
SSD_BLOCK = 2

HY_W = 256
HY_ORDER = 2
HY_BANDS = 16
HY_POS_DIM = 1 + 2 * HY_BANDS
HY_FILTER_HID = 64
HY_FAST_DECAY = 0.3
HY_SLOW_DECAY = 1.5
HY_TARGET = 1e-2
HY_N2 = 64
HY_UNROLL = 8

MIX_W = ATT_W + SSD_W + HY_W
N_EXPERTS = 16
N_GROUPS = 4
EXPERTS_PER_GROUP = N_EXPERTS // N_GROUPS
D_FF = 256

LANES = 128
SUBLANES = 8
TOKEN_TILE = 256
ATT_TILE = 1024
ATT_SUB = 512
MOE_TILE = 512
MOE_CHUNK = 128
MOE_CHUNKS = MOE_TILE // MOE_CHUNK + N_GROUPS - 1
MOD_ROWS = 8
VMEM_LIMIT = 56 * 1024 * 1024

COL_Q = 0
COL_K = COL_Q + ATT_W
COL_V = COL_K + KV_W
COL_XBC = COL_V + KV_W
COL_HY = COL_XBC + SSD_CONV_CH
COL_Z = COL_HY + 3 * HY_W
COL_DT = COL_Z + SSD_W
COL_END = COL_DT + LANES


def _params(*sem):
    return pltpu.CompilerParams(dimension_semantics=sem, vmem_limit_bytes=VMEM_LIMIT)


def _silu(x):
    return x * jax.nn.sigmoid(x)


def _softplus(x):
    return jnp.maximum(x, 0.0) + jnp.log1p(jnp.exp(-jnp.abs(x)))


def _split3(x):
    hi = x.astype(BF16)
    r1 = x - hi.astype(F32)
    mid = r1.astype(BF16)
    lo = (r1 - mid.astype(F32)).astype(BF16)
    return hi, mid, lo


def _dot(a, b):
    return jnp.dot(a, b, preferred_element_type=F32)


def _dot_x3(a, b):
    a_hi, a_mid, _ = _split3(a)
    b_hi, b_mid, _ = _split3(b)
    return _dot(a_hi, b_hi) + (_dot(a_hi, b_mid) + _dot(a_mid, b_hi))


def _dot3_right(m_bf16, x):
    hi, mid, lo = _split3(x)
    return _dot(m_bf16, hi) + _dot(m_bf16, mid) + _dot(m_bf16, lo)


def _dot3_left(x, m_bf16):
    hi, mid, lo = _split3(x)
    return _dot(hi, m_bf16) + _dot(mid, m_bf16) + _dot(lo, m_bf16)


def _adaln_kernel(c_ref, w_ref, b_ref, o_ref):
    s = _silu(c_ref[...]).astype(BF16)
    o_ref[0] = _dot(s, w_ref[0].astype(BF16)) + b_ref[0]


def _adaln(cvec, w_mod, b_mod):
    depth, d, n = w_mod.shape
    bn = n // 4
    return pl.pallas_call(
        _adaln_kernel,
        grid=(depth, n // bn),
        in_specs=[
            pl.BlockSpec((MOD_ROWS, d), lambda i, j: (0, 0)),
            pl.BlockSpec((1, d, bn), lambda i, j: (i, 0, j)),
            pl.BlockSpec((1, 1, bn), lambda i, j: (i, 0, j)),
        ],
        out_specs=pl.BlockSpec((1, MOD_ROWS, bn), lambda i, j: (i, 0, j)),
        out_shape=jax.ShapeDtypeStruct((depth, MOD_ROWS, n), F32),
        compiler_params=_params("arbitrary", "arbitrary"),
        name="adaln",
    )(cvec, w_mod, b_mod.reshape(depth, 1, n))


def _head_rope(xn, cs, sn):
    width = xn.shape[-1]
    lane = lax.broadcasted_iota(jnp.int32, xn.shape, 1)
    first_half = (lane % HEAD_DIM) < (HEAD_DIM // 2)
    partner = jnp.where(first_half,
                        pltpu.roll(xn, width - HEAD_DIM // 2, 1),
                        pltpu.roll(xn, HEAD_DIM // 2, 1))
    return xn * cs + partner * sn


def _stream_specs(n_lat_tiles, ctx_blk, d):
    lat = pl.BlockSpec((1, TOKEN_TILE, d), lambda j, b: (b, jnp.minimum(j, n_lat_tiles - 1), 0))
    ctx = pl.BlockSpec((1, TOKEN_TILE, d), lambda j, b: (jnp.where(j >= n_lat_tiles, b, 0), ctx_blk, 0))
    return lat, ctx


def _stream_tile(xl_ref, xc_ref, n_lat_tiles):
    return jnp.where(pl.program_id(0) >= n_lat_tiles, xc_ref[0], xl_ref[0])


def _inproj_kernel(xl_ref, xc_ref, xp_ref, xn_ref, sh_ref, sc_ref, g_ref, w_ref, gq_ref, gk_ref, cs_ref, sn_ref,
                   bd_ref, cw_ref, cb_ref, qt_ref, k_ref, vt_ref, z_ref, xbc_ref, dt_ref, hy_ref, *, n_lat_tiles):
    j = pl.program_id(0)
    tm, halo = TOKEN_TILE, SUBLANES
    x = jnp.concatenate([xp_ref[0], _stream_tile(xl_ref, xc_ref, n_lat_tiles), xn_ref[0]], axis=0)
    ms = jnp.mean(x * x, axis=-1, keepdims=True)
    h = x * lax.rsqrt(ms + EPS) * g_ref[...]
    h = h * (1.0 + sc_ref[0]) + sh_ref[0]
    hb = h[halo:halo + tm].astype(BF16)
    qkv = _dot(hb, w_ref[:, COL_Q:COL_XBC])
    bd = bd_ref[...]
    q, k = qkv[:, COL_Q:COL_K], qkv[:, COL_K:COL_V]
    ms_q = _dot((q * q).astype(BF16), bd) * (1.0 / HEAD_DIM)
    ms_k = _dot((k * k).astype(BF16), bd[:KV_W, :KV_W]) * (1.0 / HEAD_DIM)
    u = _dot(h.astype(BF16), w_ref[:, COL_XBC:COL_Z])
    zdt = _dot(hb, w_ref[:, COL_Z:COL_END])

    cs = cs_ref[...]
    sn = sn_ref[...]
    qt_ref[0] = _head_rope(q * lax.rsqrt(ms_q + EPS) * gq_ref[...], cs, sn).T.astype(qt_ref.dtype)
    k_ref[0] = _head_rope(k * lax.rsqrt(ms_k + EPS) * gk_ref[...],
                          cs[:, :KV_W] * (1.0 / Q_SCALE), sn[:, :KV_W] * (1.0 / Q_SCALE)).astype(k_ref.dtype)
    vt_ref[0] = qkv[:, COL_V:COL_XBC].T.astype(vt_ref.dtype)

    row = lax.broadcasted_iota(jnp.int32, (u.shape[0], 1), 0)
    has_prev = (j >= 1) & (j < n_lat_tiles)
    has_next = j < n_lat_tiles - 1
    u = jnp.where(((row >= halo) | has_prev) & ((row < halo + tm) | has_next), u, 0.0)
    cw = cw_ref[...]
    own = slice(halo, halo + tm)
    y = (pltpu.roll(u, 1, 0)[own] * cw[0:1] + u[own] * cw[1:2] + pltpu.roll(u, u.shape[0] - 1, 0)[own] * cw[2:3]
         + cb_ref[...])
    xbc_ref[0] = _silu(y[:, :COL_HY - COL_XBC])
    hy_ref[0] = y[:, COL_HY - COL_XBC:]
    z_ref[0] = zdt[:, :COL_DT - COL_Z]
    dt_ref[0] = zdt[:, COL_DT - COL_Z:]


def _inproj(x_lat, x_ctx, ctx_blk, mods, g_mix, w_cat, gq, gk, cs, sn, bd, conv_w, conv_b, n_lat_tiles):
    bsz, _, d = x_lat.shape
    tm = TOKEN_TILE
    nt = n_lat_tiles + 1
    t = nt * tm
    per_tile = tm // SUBLANES
    n_conv = COL_Z - COL_XBC

    def mod_row(j, b):
        return jnp.where(j >= n_lat_tiles, bsz, b)

    lat_j = lambda j: jnp.minimum(j, n_lat_tiles - 1)
    prev_spec = pl.BlockSpec((1, SUBLANES, d), lambda j, b: (b, jnp.maximum(lat_j(j) * per_tile - 1, 0), 0))
    next_spec = pl.BlockSpec((1, SUBLANES, d),
                             lambda j, b: (b, jnp.minimum((lat_j(j) + 1) * per_tile, n_lat_tiles * per_tile - 1), 0))

    tok = lambda w: pl.BlockSpec((1, tm, w), lambda j, b: (b, j, 0))
    const = lambda shape: pl.BlockSpec(shape, lambda j, b: tuple(0 for _ in shape))
    outs = pl.pallas_call(
        functools.partial(_inproj_kernel, n_lat_tiles=n_lat_tiles),
        grid=(nt, bsz),
        in_specs=[
            *_stream_specs(n_lat_tiles, ctx_blk, d),
            prev_spec, next_spec,
            pl.BlockSpec((1, 1, d), lambda j, b: (mod_row(j, b), 0, 0)),
            pl.BlockSpec((1, 1, d), lambda j, b: (mod_row(j, b), 0, 1)),
            const((1, d)),
            const((d, COL_END)),
            const((1, ATT_W)),
            const((1, KV_W)),
            pl.BlockSpec((tm, ATT_W), lambda j, b: (j, 0)),
            pl.BlockSpec((tm, ATT_W), lambda j, b: (j, 0)),
            const((ATT_W, ATT_W)),
            const((3, n_conv)),
            const((1, n_conv)),
        ],
        out_specs=[
            pl.BlockSpec((1, ATT_W, tm), lambda j, b: (b, 0, j)),
            tok(KV_W),
            pl.BlockSpec((1, KV_W, tm), lambda j, b: (b, 0, j)),
            tok(SSD_W),
            tok(SSD_CONV_CH),
            tok(LANES),
            tok(3 * HY_W),
        ],
        out_shape=[
            jax.ShapeDtypeStruct((bsz, ATT_W, t), BF16),
            jax.ShapeDtypeStruct((bsz, t, KV_W), BF16),
            jax.ShapeDtypeStruct((bsz, KV_W, t), BF16),
            jax.ShapeDtypeStruct((bsz, t, SSD_W), F32),
            jax.ShapeDtypeStruct((bsz, t, SSD_CONV_CH), F32),
            jax.ShapeDtypeStruct((bsz, t, LANES), F32),
            jax.ShapeDtypeStruct((bsz, t, 3 * HY_W), F32),
        ],
        compiler_params=_params("arbitrary", "arbitrary"),
        name="inproj",
    )(x_lat, x_ctx, x_lat, x_lat, mods, mods, g_mix, w_cat, gq, gk, cs, sn, bd, conv_w, conv_b)
    return outs


def _attn_kernel(qt_ref, k_ref, vt_ref, o_ref, st_ref, pt_ref):
    k = k_ref[0]
    vt = vt_ref[0]
    rep = ATT_HEADS // ATT_KV_HEADS
    tq = qt_ref.shape[2]
    sub = min(tq, ATT_SUB)
    ones = jnp.ones((2 * SUBLANES, vt.shape[1]), vt.dtype)
    vtg = [jnp.concatenate([vt[g * HEAD_DIM:(g + 1) * HEAD_DIM, :], ones], axis=0) for g in range(ATT_KV_HEADS)]
    units = [(c0, hd) for c0 in range(0, tq, sub) for hd in range(ATT_HEADS)]

    def scores(u):
        c0, hd = units[u]
        qh = qt_ref[0, hd * HEAD_DIM:(hd + 1) * HEAD_DIM, c0:c0 + sub]
        zero = jnp.zeros_like(qh)
        w = jnp.concatenate([qh, zero] if hd < rep else [zero, qh], axis=0)
        st = _dot(k, w)
        st_ref[u % 2] = st
        return jnp.max(st, axis=0, keepdims=True)

    m_next = scores(0)
    outs = []
    for u, (c0, hd) in enumerate(units):
        m = m_next
        if u + 1 < len(units):
            m_next = scores(u + 1)
        pt_ref[u % 2] = jnp.exp2(st_ref[u % 2] - m).astype(BF16)
        ot = _dot(vtg[hd // rep], pt_ref[u % 2])
        outs.append(ot[:HEAD_DIM] / ot[HEAD_DIM:HEAD_DIM + 1])
        if hd == ATT_HEADS - 1:
            o_ref[0, c0:c0 + sub, :] = jnp.concatenate(outs, axis=0).T.astype(o_ref.dtype)
            outs = []


def _attention(qt, k, vt, q_row0, n_q, k_row0, n_k, tq):
    bsz = qt.shape[0]
    kblk = k_row0 // n_k
    q_tile0 = q_row0 // tq
    n_q_tiles = n_q // tq
    assert kblk * n_k == k_row0 and q_tile0 * tq == q_row0 and n_q_tiles * tq == n_q
    assert ATT_KV_HEADS == 2
    return pl.pallas_call(
        _attn_kernel,
        grid=(bsz, n_q_tiles),
        in_specs=[
            pl.BlockSpec((1, ATT_W, tq), lambda b, j: (b, 0, q_tile0 + j)),
            pl.BlockSpec((1, n_k, KV_W), lambda b, j: (b, kblk, 0)),
            pl.BlockSpec((1, KV_W, n_k), lambda b, j: (b, 0, kblk)),
        ],
        out_specs=pl.BlockSpec((1, tq, ATT_W), lambda b, j: (b, j, 0)),
        out_shape=jax.ShapeDtypeStruct((bsz, n_q_tiles * tq, ATT_W), BF16),
        scratch_shapes=[pltpu.VMEM((2, n_k, min(tq, ATT_SUB)), F32), pltpu.VMEM((2, n_k, min(tq, ATT_SUB)), BF16)],
        compiler_params=_params("arbitrary", "arbitrary"),
        name="attention",
    )(qt, k, vt)


def _ssd_kernel(xbc_ref, z_ref, dt_ref, dtb_ref, alog_ref, dsk_ref, nw_ref, bd_ref,
                o_ref, hf_ref, hb_ref, hbe_ref, *, n_lat_blocks):
    q = SSD_CHUNK
    nh = SSD_HEADS
    phase = pl.program_id(1)
    step = pl.program_id(2)
    block = jnp.where(phase == 0, n_lat_blocks - step, jnp.where(step < 1, n_lat_blocks, step - 1))

    lane = lax.broadcasted_iota(jnp.int32, (1, LANES), 1)
    a_row = jnp.where(lane < 2 * nh, -jnp.exp(alog_ref[...]), 0.0)
    tt = lax.broadcasted_iota(jnp.int32, (q, q), 0)
    ss = lax.broadcasted_iota(jnp.int32, (q, q), 1)
    lower = (ss <= tt)
    upper = (ss >= tt)
    lmat = jnp.where(lower, 1.0, 0.0).astype(BF16)
    umat = jnp.where(upper, 1.0, 0.0).astype(BF16)

    @pl.when(step == 0)
    def _():
        hf_ref[...] = jnp.zeros_like(hf_ref)
        hb_ref[...] = jnp.zeros_like(hb_ref)

    def load_chunk(ci):
        rows = slice(ci * q, (ci + 1) * q)
        xbc = xbc_ref[0, rows, :]
        dt = _softplus(dt_ref[0, rows, :] + dtb_ref[...])
        return rows, xbc, dt, dt * a_row

    def heads_x(x):
        return [x[:, hd * SSD_HEAD_DIM:(hd + 1) * SSD_HEAD_DIM].astype(BF16) for hd in range(nh)]

    def backward_prepare(ci):
        _, xbc, dt, a = load_chunk(ci)
        bt = xbc[:, SSD_W:SSD_W + SSD_GROUPS * SSD_STATE].T
        dt_t = dt.T
        suf_t = _dot3_left(a.T, lmat)
        terms = []
        for hd in range(nh):
            g = hd // SSD_HEADS_PER_GROUP
            row_b = suf_t[nh + hd:nh + hd + 1, :]
            total = row_b[:, 0:1]
            w_t = jnp.exp(total - row_b) * dt_t[nh + hd:nh + hd + 1, :]
            terms.append((jnp.exp(total), (bt[g * SSD_STATE:(g + 1) * SSD_STATE, :] * w_t).astype(BF16)))
        return terms, heads_x(xbc[:, :SSD_W])

    def backward_state(ci, prepared):
        terms, xh = prepared
        chunk = block * SSD_BLOCK + ci
        for hd in range(nh):
            decay, bw = terms[hd]
            prev = hb_ref[hd]
            hbe_ref[chunk, hd] = prev
            hb_ref[hd] = prev * decay + _dot(bw, xh[hd])

    def forward_prepare(ci):
        rows, xbc, dt, a = load_chunk(ci)
        x = xbc[:, :SSD_W]
        bt = xbc[:, SSD_W:SSD_W + SSD_GROUPS * SSD_STATE].T
        cmat = xbc[:, SSD_W + SSD_GROUPS * SSD_STATE:]
        a_t = a.T
        pre = _dot3_right(lmat, a)
        pre_t = _dot3_left(a_t, umat)
        cgs = [cmat[:, g * SSD_STATE:(g + 1) * SSD_STATE].astype(BF16) for g in range(SSD_GROUPS)]
        btgs = [bt[g * SSD_STATE:(g + 1) * SSD_STATE, :] for g in range(SSD_GROUPS)]
        cbs = [_dot(cgs[g], btgs[g].astype(BF16)) for g in range(SSD_GROUPS)]
        return dict(rows=rows, x=x, xh=heads_x(x), a=a, a_t=a_t, dt_t=dt.T, pre=pre, pre_t=pre_t,
                    cgs=cgs, btgs=btgs, cbs=cbs)

    def forward_mix(p):
        pre, pre_t, dt_t = p["pre"], p["pre_t"], p["dt_t"]
        suf = pre[q - 1:q, :] - pre + p["a"]
        suf_t = pre_t[:, q - 1:q] - pre_t + p["a_t"]
        neg_inf = jnp.float32(-jnp.inf)
        p["y_in"], p["scale"], p["upd"] = [], [], []
        for hd in range(nh):
            g = hd // SSD_HEADS_PER_GROUP
            colf = pre[:, hd:hd + 1]
            rowf = pre_t[hd:hd + 1, :]
            colb = suf[:, nh + hd:nh + hd + 1]
            rowb = suf_t[nh + hd:nh + hd + 1, :]
            wf = jnp.exp(jnp.where(lower, colf - rowf, neg_inf)) * dt_t[hd:hd + 1, :]
            wb = jnp.exp(jnp.where(upper, colb - rowb, neg_inf)) * dt_t[nh + hd:nh + hd + 1, :]
            p["y_in"].append(_dot((p["cbs"][g] * (wf + wb)).astype(BF16), p["xh"][hd]))
            p["scale"].append((jnp.exp(colf), jnp.exp(colb)))
            total = rowf[:, q - 1:q]
            w_t = jnp.exp(total - rowf) * dt_t[hd:hd + 1, :]
            p["upd"].append((jnp.exp(total), (p["btgs"][g] * w_t).astype(BF16)))

    def forward_state(ci, p):
        chunk = block * SSD_BLOCK + ci
        rows, x = p["rows"], p["x"]
        ys = []
        for hd in range(nh):
            cg = p["cgs"][hd // SSD_HEADS_PER_GROUP]
            hf = hf_ref[hd]
            ef, eb = p["scale"][hd]
            y = p["y_in"][hd] + _dot(cg, hf.astype(BF16)) * ef + _dot(cg, hbe_ref[chunk, hd].astype(BF16)) * eb
            decay, bw = p["upd"][hd]
            hf_ref[hd] = hf * decay + _dot(bw, p["xh"][hd])
            ys.append(y)
        y = jnp.concatenate(ys, axis=-1) + x * dsk_ref[...]
        gz = y * _silu(z_ref[0, rows, :])
        ms = _dot((gz * gz).astype(BF16), bd_ref[...]) * (1.0 / (SSD_W // SSD_GROUPS))
        o_ref[0, rows, :] = (gz * lax.rsqrt(ms + EPS) * nw_ref[...]).astype(o_ref.dtype)

    @pl.when(phase == 0)
    def _():
        prepared = [backward_prepare(ci) for ci in range(SSD_BLOCK)]
        for ci in reversed(range(SSD_BLOCK)):
            backward_state(ci, prepared[ci])

    @pl.when(phase == 1)
    def _():
        prepared = [forward_prepare(ci) for ci in range(SSD_BLOCK)]
        for p in prepared:
            forward_mix(p)
        for ci in range(SSD_BLOCK):
            forward_state(ci, prepared[ci])


def _ssd(xbc, z, dt_raw, dt_bias, a_log, d_skip, norm_w, bd, n_lat):
    bsz, t, _ = z.shape
    rows = SSD_BLOCK * SSD_CHUNK
    n_lat_blocks = n_lat // rows
    n_blocks = t // rows
    assert n_blocks == n_lat_blocks + 1 and n_lat_blocks * rows == n_lat

    def block_of(p, s):
        return jnp.where(p == 0, n_lat_blocks - s, jnp.where(s < 1, n_lat_blocks, s - 1))

    def out_block(p, s):
        return jnp.where(p == 0, n_lat_blocks, block_of(p, s))

    const = lambda shape: pl.BlockSpec(shape, lambda b, p, s: tuple(0 for _ in shape))
    return pl.pallas_call(
        functools.partial(_ssd_kernel, n_lat_blocks=n_lat_blocks),
        grid=(bsz, 2, n_blocks),
        in_specs=[
            pl.BlockSpec((1, rows, SSD_CONV_CH), lambda b, p, s: (b, block_of(p, s), 0)),
            pl.BlockSpec((1, rows, SSD_W), lambda b, p, s: (b, block_of(p, s), 0)),
            pl.BlockSpec((1, rows, LANES), lambda b, p, s: (b, block_of(p, s), 0)),
            const((1, LANES)),
            const((1, LANES)),
            const((1, SSD_W)),
            const((1, SSD_W)),
            const((SSD_W, SSD_W)),
        ],
        out_specs=pl.BlockSpec((1, rows, SSD_W), lambda b, p, s: (b, out_block(p, s), 0)),
        out_shape=jax.ShapeDtypeStruct((bsz, t, SSD_W), BF16),
        scratch_shapes=[
            pltpu.VMEM((SSD_HEADS, SSD_STATE, SSD_HEAD_DIM), F32),
            pltpu.VMEM((SSD_HEADS, SSD_STATE, SSD_HEAD_DIM), F32),
            pltpu.VMEM((n_blocks * SSD_BLOCK, SSD_HEADS, SSD_STATE, SSD_HEAD_DIM), F32),
        ],
        compiler_params=_params("arbitrary", "arbitrary", "arbitrary"),
        name="ssd",
    )(xbc, z, dt_raw, dt_bias, a_log, d_skip, norm_w, bd)


def _hyfilt_kernel(feat_ref, win_ref, w1_ref, b1_ref, fr_ref, w2_ref, b2_ref, w3_ref, o_ref):
    hp = lax.Precision.HIGHEST
    fr = fr_ref[...]
    h1 = jnp.sin(fr * (jnp.dot(feat_ref[...], w1_ref[...], precision=hp, preferred_element_type=F32) + b1_ref[...]))
    h2 = jnp.sin(fr * (jnp.dot(h1, w2_ref[...], precision=hp, preferred_element_type=F32) + b2_ref[...]))
    h = _dot_x3(h2, w3_ref[...])
    win = win_ref[...]
    first_tile = pl.program_id(0) == 0
    row = lax.broadcasted_iota(jnp.int32, win.shape, 0)
    for order in range(HY_ORDER):
        for direction in range(2):
            c0 = (direction * HY_ORDER + order) * HY_W
            f = h[:, c0:c0 + HY_W] * win
            if direction == 1:
                f = jnp.where(first_tile & (row == 0), 0.0, f)
            o_ref[order * 2 + direction] = f


def _hy_positional(length):
    n = np.arange(length, dtype=np.float64)
    t = n / max(length - 1, 1)
    bands = np.linspace(1e-4, HY_BANDS - 1, HY_BANDS)
    wpos = (2 * math.pi / length) * n
    feats = np.concatenate([t[:, None], np.cos(wpos[:, None] * bands), -np.sin(wpos[:, None] * bands)], axis=-1)
    feats = np.pad(feats, ((0, 0), (0, LANES - HY_POS_DIM)))
    deltas = np.abs(np.linspace(math.log(HY_TARGET) / HY_SLOW_DECAY, math.log(HY_TARGET) / HY_FAST_DECAY, HY_W))
    window = np.exp(-t[:, None] * deltas)
    return jnp.asarray(feats, F32), jnp.asarray(window, F32)


def _hyena_filters(length, w1, b1, freq, w2, b2, w3):
    feats, window = _hy_positional(length)
    tl = min(length, 512)
    hid = HY_FILTER_HID
    w1p = jnp.pad(w1, ((0, LANES - HY_POS_DIM), (0, 0)))
    const = lambda shape: pl.BlockSpec(shape, lambda i: tuple(0 for _ in shape))
    return pl.pallas_call(
        _hyfilt_kernel,
        grid=(length // tl,),
        in_specs=[
            pl.BlockSpec((tl, LANES), lambda i: (i, 0)),
            pl.BlockSpec((tl, HY_W), lambda i: (i, 0)),
            const((LANES, hid)), const((1, hid)), const((1, hid)),
            const((hid, hid)), const((1, hid)), const((hid, 2 * HY_ORDER * HY_W)),
        ],
        out_specs=pl.BlockSpec((2 * HY_ORDER, tl, HY_W), lambda i: (0, i, 0)),
        out_shape=jax.ShapeDtypeStruct((2 * HY_ORDER, length, HY_W), F32),
        compiler_params=_params("arbitrary"),
        name="hyena_filters",
    )(feats, window, w1p, b1.reshape(1, hid), freq.reshape(1, hid), w2, b2.reshape(1, hid), w3)


def _dft_tables(length):
    n = 2 * length
    n2 = HY_N2
    n1 = n // n2
    half = n1 // 2
    k1 = np.arange(half, dtype=np.float64) + 0.5
    idx = (n2 * np.arange(half)[None, None, :] + np.arange(n2)[:, None, None])
    ang = 2 * np.pi * k1[None, :, None] * idx / n
    fa = np.concatenate([np.cos(ang), -np.sin(ang)], axis=1)
    kk = np.arange(n2, dtype=np.float64)
    angb = 2 * np.pi * kk[:, None] * kk[None, :] / n2
    cr, sr = np.cos(angb), np.sin(angb)
    fb = np.block([[cr, sr], [-sr, cr]])
    fbi = np.block([[cr, -sr], [sr, cr]])
    idxo = (n2 * np.arange(half)[None, :, None] + np.arange(n2)[:, None, None])
    ango = 2 * np.pi * k1[None, None, :] * idxo / n
    ga = np.concatenate([np.cos(ango), -np.sin(ango)], axis=2) * (2.0 / n)
    f32 = lambda a: jnp.asarray(a, F32).astype(BF16)
    return f32(fa), f32(fb), f32(fbi), f32(ga)


def _seq_pitch(n2n):
    return n2n + SUBLANES


def _to_pitched(dst_ref, val, n2n):
    pitch = _seq_pitch(n2n)
    for i in range(val.shape[0] // n2n):
        dst_ref[i * pitch:i * pitch + n2n, :] = val[i * n2n:(i + 1) * n2n, :]


def _hy_forward_a(src_ref, a_ref, fa_ref):
    n2n, two_n1, half = fa_ref.shape
    zp, ap = _seq_pitch(n2n), _seq_pitch(two_n1)

    def body(n2, carry):
        zs = src_ref[pl.ds(n2, half, stride=zp), :].astype(BF16)
        a_ref[pl.ds(pl.multiple_of(n2 * ap, SUBLANES), two_n1), :] = _dot(fa_ref[n2], zs)
        return carry

    lax.fori_loop(0, n2n, body, 0, unroll=HY_UNROLL)


def _hy_kspec_kernel(hf_ref, hb_ref, fa_ref, fb_ref, o_ref, af_ref, ab_ref, hfp_ref, hbp_ref):
    n2n, two_n1, _ = fa_ref.shape
    n1 = two_n1 // 2
    ap = _seq_pitch(two_n1)
    _to_pitched(hfp_ref, hf_ref[0], n2n)
    _to_pitched(hbp_ref, hb_ref[0], n2n)
    _hy_forward_a(hfp_ref, af_ref, fa_ref)
    _hy_forward_a(hbp_ref, ab_ref, fa_ref)
    fb = fb_ref[...]

    def body(k1, carry):
        def spectrum(a_ref):
            ar = a_ref[pl.ds(k1, n2n, stride=ap), :]
            ai = a_ref[pl.ds(n1 + k1, n2n, stride=ap), :]
            return _dot(fb, jnp.concatenate([ar, ai], axis=0).astype(BF16))

        xf, xb = spectrum(af_ref), spectrum(ab_ref)
        o_ref[0, k1] = jnp.concatenate([xf[:n2n] + xb[:n2n], xf[n2n:] - xb[n2n:]], axis=0)
        return carry

    lax.fori_loop(0, n1, body, 0, unroll=HY_UNROLL)


def _hy_kspec(filt, tables):
    fa, fb, _, _ = tables
    _, length, _ = filt.shape
    n2n, two_n1, _ = fa.shape
    n1 = two_n1 // 2
    nh = HY_W // LANES
    return pl.pallas_call(
        _hy_kspec_kernel,
        grid=(HY_ORDER, nh),
        in_specs=[
            pl.BlockSpec((1, length, LANES), lambda o, h: (2 * o, 0, h)),
            pl.BlockSpec((1, length, LANES), lambda o, h: (2 * o + 1, 0, h)),
            pl.BlockSpec(fa.shape, lambda o, h: (0, 0, 0), pipeline_mode=pl.Buffered(1)),
            pl.BlockSpec(fb.shape, lambda o, h: (0, 0)),
        ],
        out_specs=pl.BlockSpec((1, n1, 2 * n2n, LANES), lambda o, h: (o, 0, 0, h)),
        out_shape=jax.ShapeDtypeStruct((HY_ORDER, n1, 2 * n2n, HY_W), F32),
        scratch_shapes=[pltpu.VMEM((n2n * _seq_pitch(two_n1), LANES), F32),
                        pltpu.VMEM((n2n * _seq_pitch(two_n1), LANES), F32),
                        pltpu.VMEM((length // n2n * _seq_pitch(n2n), LANES), F32),
                        pltpu.VMEM((length // n2n * _seq_pitch(n2n), LANES), F32)],
        compiler_params=_params("arbitrary", "arbitrary"),
        name="hyena_kspec",
    )(filt, filt, fa, fb)


def _hy_conv_kernel(z_ref, g_ref, fa_ref, fb_ref, fbi_ref, ga_ref, k_ref, bias_ref,
                    o_ref, a_ref, c_ref, zc_ref, gc_ref):
    n2n, two_n1, half = fa_ref.shape
    n1 = two_n1 // 2
    zp, ap = _seq_pitch(n2n), _seq_pitch(two_n1)
    _to_pitched(zc_ref, z_ref[0], n2n)
    _to_pitched(gc_ref, g_ref[0], n2n)
    _hy_forward_a(zc_ref, a_ref, fa_ref)
    fb = fb_ref[...]
    fbi = fbi_ref[...]

    def body_b(grp, carry):
        k1s = [grp * HY_UNROLL + i for i in range(HY_UNROLL)]
        sl = [(pl.ds(k1, n2n, stride=ap), pl.ds(n1 + k1, n2n, stride=ap)) for k1 in k1s]
        xs = [_dot(fb, jnp.concatenate([a_ref[re, :], a_ref[im, :]], axis=0).astype(BF16)) for re, im in sl]
        ys = []
        for k1, x in zip(k1s, xs):
            kk = k_ref[0, k1]
            xr, xi = x[:n2n], x[n2n:]
            kr, ki = kk[:n2n], kk[n2n:]
            ys.append(jnp.concatenate([xr * kr - xi * ki, xr * ki + xi * kr], axis=0).astype(BF16))
        for (re, im), y in zip(sl, ys):
            c = _dot(fbi, y)
            c_ref[re, :] = c[:n2n]
            c_ref[im, :] = c[n2n:]
        return carry

    lax.fori_loop(0, n1 // HY_UNROLL, body_b, 0)
    bias = bias_ref[0]

    def body_c(m2, carry):
        rows = pl.ds(m2, half, stride=zp)
        c = c_ref[pl.ds(pl.multiple_of(m2 * ap, SUBLANES), two_n1), :].astype(BF16)
        y = _dot(ga_ref[m2], c)
        a_ref[rows, :] = gc_ref[rows, :] * (y + zc_ref[rows, :] * bias)
        return carry

    lax.fori_loop(0, n2n, body_c, 0, unroll=HY_UNROLL)
    for i in range(half):
        o_ref[0, i * n2n:(i + 1) * n2n, :] = a_ref[i * zp:i * zp + n2n, :]


def _hy_conv(z_arr, z_blk, g_arr, g_blk, kspec, order, bias3, tables, n_lat):
    fa, fb, fbi, ga = tables
    bsz = z_arr.shape[0]
    n2n, two_n1, half = fa.shape
    n1 = two_n1 // 2
    nh = HY_W // LANES
    resident = lambda shape: pl.BlockSpec(shape, lambda h, b_: tuple(0 for _ in shape), pipeline_mode=pl.Buffered(1))
    return pl.pallas_call(
        _hy_conv_kernel,
        grid=(nh, bsz),
        in_specs=[
            pl.BlockSpec((1, n_lat, LANES), lambda h, b_: (b_, 0, z_blk + h)),
            pl.BlockSpec((1, n_lat, LANES), lambda h, b_: (b_, 0, g_blk + h)),
            resident(fa.shape), resident(fb.shape), resident(fbi.shape), resident(ga.shape),
            pl.BlockSpec((1, n1, 2 * n2n, LANES), lambda h, b_: (order, 0, 0, h), pipeline_mode=pl.Buffered(1)),
            pl.BlockSpec((1, 1, LANES), lambda h, b_: (order, 0, h)),
        ],
        out_specs=pl.BlockSpec((1, n_lat, LANES), lambda h, b_: (b_, 0, h)),
        out_shape=jax.ShapeDtypeStruct((bsz, n_lat, HY_W), F32),
        scratch_shapes=[pltpu.VMEM((n2n * _seq_pitch(two_n1), LANES), F32),
                        pltpu.VMEM((n2n * _seq_pitch(two_n1), LANES), F32),
                        pltpu.VMEM((half * _seq_pitch(n2n), LANES), F32),
                        pltpu.VMEM((half * _seq_pitch(n2n), LANES), F32)],
        compiler_params=_params("arbitrary", "arbitrary"),
        name="hyena_conv",
    )(z_arr, g_arr, fa, fb, fbi, ga, kspec, bias3)


def _hyena_ctx_kernel(u_ref, filt_ref, bias_ref, fd_ref, gd_ref, o_ref, *, n_ctx):
    fd = fd_ref[...]
    gd = gd_ref[...]
    nn = 2 * n_ctx
    u = u_ref[0]
    v, x1, x2 = u[:, :HY_W], u[:, HY_W:2 * HY_W], u[:, 2 * HY_W:]

    def conv(zin, order):
        hf = _dot(fd, filt_ref[2 * order].astype(BF16))
        hb = _dot(fd, filt_ref[2 * order + 1].astype(BF16))
        kr, ki = hf[:nn] + hb[:nn], hf[nn:] - hb[nn:]
        zz = _dot(fd, zin.astype(BF16))
        zr, zi = zz[:nn], zz[nn:]
        y = jnp.concatenate([zr * kr - zi * ki, zr * ki + zi * kr], axis=0).astype(BF16)
        return _dot(gd, y) + zin * bias_ref[order]

    y1 = x1 * conv(v, 0)
    o_ref[0] = x2 * conv(y1, 1)


def _hyena_ctx(hy, filt, bias, n_lat, n_ctx):
    bsz = hy.shape[0]
    nn = 2 * n_ctx
    k = np.arange(nn, dtype=np.float64)
    t = np.arange(n_ctx, dtype=np.float64)
    ang = 2 * np.pi * k[:, None] * t[None, :] / nn
    fd = jnp.asarray(np.concatenate([np.cos(ang), -np.sin(ang)], axis=0), F32).astype(BF16)
    gd = jnp.asarray(np.concatenate([np.cos(ang.T), -np.sin(ang.T)], axis=1) / nn, F32).astype(BF16)
    blk = n_lat // n_ctx
    assert blk * n_ctx == n_lat
    return pl.pallas_call(
        functools.partial(_hyena_ctx_kernel, n_ctx=n_ctx),
        grid=(bsz,),
        in_specs=[
            pl.BlockSpec((1, n_ctx, 3 * HY_W), lambda b: (b, blk, 0)),
            pl.BlockSpec((2 * HY_ORDER, n_ctx, HY_W), lambda b: (0, 0, 0)),
            pl.BlockSpec((HY_ORDER, 1, HY_W), lambda b: (0, 0, 0)),
            pl.BlockSpec((2 * nn, n_ctx), lambda b: (0, 0)),
            pl.BlockSpec((n_ctx, 2 * nn), lambda b: (0, 0)),
        ],
        out_specs=pl.BlockSpec((1, n_ctx, HY_W), lambda b: (b, 0, 0)),
        out_shape=jax.ShapeDtypeStruct((bsz, n_ctx, HY_W), F32),
        compiler_params=_params("arbitrary"),
        name="hyena_ctx",
    )(hy, filt, bias, fd, gd)


def _hyena_latent(hy, filt, bias, n_lat, tables):
    kspec = _hy_kspec(filt, tables)
    bias3 = bias.reshape(HY_ORDER, 1, HY_W)
    nb = HY_W // LANES
    y1 = _hy_conv(hy, 0, hy, nb, kspec, 0, bias3, tables, n_lat)
    return _hy_conv(y1, 0, hy, 2 * nb, kspec, 1, bias3, tables, n_lat)


def _outproj_kernel(xl_ref, xc_ref, attl_ref, attc_ref, ssd_ref, hyl_ref, hyc_ref, gt_ref, wa_ref, ws_ref, wh_ref,
                    o_ref, *, n_lat_tiles):
    x = _stream_tile(xl_ref, xc_ref, n_lat_tiles)
    att = _stream_tile(attl_ref, attc_ref, n_lat_tiles)
    hy = _stream_tile(hyl_ref, hyc_ref, n_lat_tiles).astype(BF16)
    mix = _dot(att, wa_ref[...]) + _dot(ssd_ref[0], ws_ref[...]) + _dot(hy, wh_ref[...])
    o_ref[0] = x + gt_ref[0] * mix


def _outproj(x_lat, x_ctx, ctx_blk, att_l, att_c, ssd, hy_l, hy_c, mods, w_out, n_lat_tiles, n_tiles):
    bsz, _, d = x_lat.shape
    tm = TOKEN_TILE

    def mod_row(j, b):
        return jnp.where(j >= n_lat_tiles, bsz, b)

    tok = lambda w: pl.BlockSpec((1, tm, w), lambda j, b: (b, j, 0))
    return pl.pallas_call(
        functools.partial(_outproj_kernel, n_lat_tiles=n_lat_tiles),
        grid=(n_tiles, bsz),
        in_specs=[
            *_stream_specs(n_lat_tiles, ctx_blk, d),
            *_stream_specs(n_lat_tiles, 0, ATT_W),
            tok(SSD_W),
            *_stream_specs(n_lat_tiles, 0, HY_W),
            pl.BlockSpec((1, 1, d), lambda j, b: (mod_row(j, b), 0, 2)),
            pl.BlockSpec((ATT_W, d), lambda j, b: (0, 0)),
            pl.BlockSpec((SSD_W, d), lambda j, b: (0, 0)),
            pl.BlockSpec((HY_W, d), lambda j, b: (0, 0)),
        ],
        out_specs=tok(d),
        out_shape=jax.ShapeDtypeStruct((bsz, n_tiles * tm, d), F32),
        compiler_params=_params("arbitrary", "arbitrary"),
        name="outproj",
    )(x_lat, x_ctx, att_l, att_c, ssd, hy_l, hy_c, mods,
      w_out[:ATT_W], w_out[ATT_W:ATT_W + SSD_W], w_out[ATT_W + SSD_W:])


def _route(logits_t, bias_col):
    scores = jax.nn.sigmoid(logits_t)
    sel = scores + bias_col
    neg_inf = jnp.float32(-jnp.inf)
    rows = [sel[e:e + 1, :] for e in range(N_EXPERTS)]
    grp = []
    for g in range(N_GROUPS):
        r = rows[g * EXPERTS_PER_GROUP:(g + 1) * EXPERTS_PER_GROUP]
        top = functools.reduce(jnp.maximum, r)
        taken = None
        rest = []
        for ri in r:
            is_top = (ri == top) if taken is None else (ri == top) & jnp.logical_not(taken)
            rest.append(jnp.where(is_top, neg_inf, ri))
            taken = is_top if taken is None else taken | is_top
        grp.append(top + functools.reduce(jnp.maximum, rest))
    best = jnp.zeros(grp[0].shape, jnp.int32)
    cur = grp[0]
    for g in range(1, N_GROUPS):
        upd = grp[g] > cur
        best = jnp.where(upd, g, best)
        cur = jnp.where(upd, grp[g], cur)
    picked = []
    for e in range(N_EXPERTS):
        g, i = divmod(e, EXPERTS_PER_GROUP)
        rank = jnp.zeros(best.shape, jnp.int32)
        for j in range(EXPERTS_PER_GROUP):
            if j == i:
                continue
            other = rows[g * EXPERTS_PER_GROUP + j]
            ahead = (other > rows[e]) | ((other == rows[e]) & (j < i))
            rank = rank + ahead.astype(jnp.int32)
        keep = (best == g) & (rank < 2)
        picked.append(jnp.where(keep, scores[e:e + 1, :], 0.0))
    total = functools.reduce(lambda u, w: u + w, picked)
    return jnp.concatenate(picked, axis=0) / total, best


def _ffn_input(x_ref, sh_ref, sc_ref, g_ref):
    x = x_ref[...].reshape(MOE_TILE, D_MODEL)
    ms = jnp.mean(x * x, axis=-1, keepdims=True)
    t = x * lax.rsqrt(ms + EPS) * g_ref[...]
    return x, t * (1.0 + sc_ref[0]) + sh_ref[0]


def _route_kernel(x_ref, sh_ref, sc_ref, g_ref, wr_ref, rb_ref, rt_ref, srow_ref, tbl_ref):
    tm = MOE_TILE
    _, t = _ffn_input(x_ref, sh_ref, sc_ref, g_ref)
    logits = _dot_x3(t, wr_ref[...])
    comb_t, best = _route(logits.T[:N_EXPERTS, :], rb_ref[...])

    member = [jnp.where(best == g, 1.0, 0.0) for g in range(N_GROUPS)]
    comb4 = functools.reduce(
        lambda u, w: u + w,
        [member[g] * comb_t[g * EXPERTS_PER_GROUP:(g + 1) * EXPERTS_PER_GROUP, :] for g in range(N_GROUPS)])
    masks = jnp.concatenate(member + [jnp.zeros((8 - N_GROUPS, tm), F32)], axis=0)
    earlier = jnp.where(lax.broadcasted_iota(jnp.int32, (tm, tm), 0) < lax.broadcasted_iota(jnp.int32, (tm, tm), 1),
                        1.0, 0.0).astype(BF16)
    ranks = _dot(masks.astype(BF16), earlier)
    cnt = jnp.sum(masks, axis=1, keepdims=True)
    padded = jnp.ceil(cnt * (1.0 / MOE_CHUNK)) * MOE_CHUNK
    lane = lax.broadcasted_iota(jnp.int32, (1, LANES), 1).astype(F32) * MOE_CHUNK
    start = jnp.zeros((1, 1), F32)
    slot = jnp.zeros((1, tm), F32)
    gid = jnp.zeros((1, LANES), jnp.int32)
    for g in range(N_GROUPS):
        slot = slot + member[g] * (start + ranks[g:g + 1, :])
        start = start + padded[g:g + 1, :]
        gid = gid + jnp.where(lane >= start, 1, 0)
    tbl_ref[0] = gid
    srow_ref[0] = slot.astype(jnp.int32)
    rt_t = jnp.concatenate([comb4, slot, jnp.zeros((LANES - EXPERTS_PER_GROUP - 1, tm), F32)], axis=0)
    rt_ref[...] = rt_t.T


def _experts_kernel(tbl_ref, x_ref, sh_ref, sc_ref, gt_ref, g_ref, rt_ref, srow_ref, wg_ref, wu_ref, wd_ref, gf_ref,
                    *rest, final, tile_of, n_grid):
    o_ref, xp_ref, yp_ref = rest[-3:]
    tm, ch = MOE_TILE, MOE_CHUNK
    n_slots = MOE_CHUNKS * ch
    tile = tile_of(*[pl.program_id(a) for a in range(n_grid)])
    x, t = _ffn_input(x_ref, sh_ref, sc_ref, g_ref)
    rt = rt_ref[...]
    slot_col = rt[:, EXPERTS_PER_GROUP:EXPERTS_PER_GROUP + 1].astype(jnp.int32)
    gather = jnp.where(lax.broadcasted_iota(jnp.int32, (n_slots, tm), 0) == srow_ref[0], 1.0, 0.0).astype(BF16)
    scatter = jnp.where(lax.broadcasted_iota(jnp.int32, (tm, n_slots), 1) == slot_col, 1.0, 0.0).astype(BF16)
    xp_ref[...] = _dot(gather, t.astype(BF16)).astype(BF16)
    w_slot = _dot3_right(gather, rt)

    def run_chunk(rows, grp):
        xc = xp_ref[rows, :]
        e0 = grp * EXPERTS_PER_GROUP
        gu_next = _dot(xc, wg_ref[e0]), _dot(xc, wu_ref[e0])
        acc = jnp.zeros((ch, D_MODEL), F32)
        for j in range(EXPERTS_PER_GROUP):
            gate, up = gu_next
            if j + 1 < EXPERTS_PER_GROUP:
                gu_next = _dot(xc, wg_ref[e0 + j + 1]), _dot(xc, wu_ref[e0 + j + 1])
            hid = _silu(gate) * up * w_slot[rows, j:j + 1]
            acc = acc + _dot(hid.astype(BF16), wd_ref[e0 + j])
        yp_ref[rows, :] = acc.astype(BF16)

    for c in range(MOE_CHUNKS):
        rows = slice(c * ch, (c + 1) * ch)
        grp = tbl_ref[tile, c]
        pl.when(grp < N_GROUPS)(functools.partial(run_chunk, rows, grp))

        @pl.when(grp >= N_GROUPS)
        def _():
            yp_ref[rows, :] = jnp.zeros((ch, D_MODEL), BF16)

    y = x + gt_ref[0] * _dot(scatter, yp_ref[...])
    if final:
        y = y * lax.rsqrt(jnp.mean(y * y, axis=-1, keepdims=True) + EPS) * gf_ref[...]
    o_ref[...] = y.reshape(o_ref.shape)


def _moe_tiles(x, mods, mod_row, g_ffn, w_router, router_bias, wgu, wd, g_final, final, out_rows,
               grid, x_block, x_index, out_index, tile_of, n_tiles):
    bsz, _, d = x.shape
    tm = MOE_TILE
    ng = len(grid)
    const = lambda shape: pl.BlockSpec(shape, lambda *a: tuple(0 for _ in shape))
    mod = lambda col: pl.BlockSpec((1, 1, d), lambda *a: (mod_row(*a[:ng]), 0, col))
    xspec = pl.BlockSpec(x_block, lambda *a: x_index(*a[:ng]))
    rt, srow, tbl = pl.pallas_call(
        _route_kernel,
        grid=grid,
        in_specs=[xspec, mod(3), mod(4), const((1, d)), const((d, LANES)), const((N_EXPERTS, 1))],
        out_specs=[
            pl.BlockSpec((tm, LANES), lambda *a: (tile_of(*a), 0)),
            pl.BlockSpec((1, 1, tm), lambda *a: (tile_of(*a), 0, 0)),
            pl.BlockSpec((1, 1, LANES), lambda *a: (tile_of(*a), 0, 0)),
        ],
        out_shape=[
            jax.ShapeDtypeStruct((n_tiles * tm, LANES), F32),
            jax.ShapeDtypeStruct((n_tiles, 1, tm), jnp.int32),
            jax.ShapeDtypeStruct((n_tiles, 1, LANES), jnp.int32),
        ],
        compiler_params=_params(*["arbitrary"] * ng),
        name="moe_route",
    )(x, mods, mods, g_ffn, w_router, router_bias)

    resident = lambda shape: pl.BlockSpec(shape, lambda *a: tuple(0 for _ in shape), pipeline_mode=pl.Buffered(1))
    in_specs = [
        xspec, mod(3), mod(4), mod(5), const((1, d)),
        pl.BlockSpec((tm, LANES), lambda *a: (tile_of(*a[:ng]), 0)),
        pl.BlockSpec((1, 1, tm), lambda *a: (tile_of(*a[:ng]), 0, 0)),
        resident((N_EXPERTS, d, D_FF)),
        resident((N_EXPERTS, d, D_FF)),
        resident((N_EXPERTS, D_FF, d)),
        const((1, d)),
    ]
    args = [tbl.reshape(n_tiles, LANES), x, mods, mods, mods, g_ffn, rt, srow, *wgu, wd, g_final]
    return pl.pallas_call(
        functools.partial(_experts_kernel, final=final, tile_of=tile_of, n_grid=ng),
        grid_spec=pltpu.PrefetchScalarGridSpec(
            num_scalar_prefetch=1,
            grid=grid,
            in_specs=in_specs,
            out_specs=pl.BlockSpec(x_block, lambda *a: out_index(*a[:ng])),
            scratch_shapes=[pltpu.VMEM((MOE_CHUNKS * MOE_CHUNK, d), BF16), pltpu.VMEM((MOE_CHUNKS * MOE_CHUNK, d), BF16)],
        ),
        out_shape=jax.ShapeDtypeStruct((bsz, out_rows, d), F32),
        compiler_params=_params(*["arbitrary"] * ng),
        name="moe_experts",
    )(*args)


def _moe(x, mods, g_ffn, w_router, router_bias, wgu, wd, g_final, n_lat, n_ctx, with_ctx, final):
    bsz, _, d = x.shape
    tm = MOE_TILE
    per_b = n_lat // tm
    common = (g_ffn, w_router, router_bias, wgu, wd, g_final, final)
    lat_index = lambda b, j: (b, j, 0)
    out_lat = _moe_tiles(x, mods, lambda b, j: b, *common, n_lat,
                         (bsz, per_b), (1, tm, d), lat_index, lat_index, lambda b, j: b * per_b + j, bsz * per_b)
    if not with_ctx:
        return out_lat, None
    nb = tm // n_ctx
    out_ctx = _moe_tiles(x, mods, lambda i: bsz, *common, n_ctx,
                         (bsz // nb,), (nb, n_ctx, d), lambda i: (i, n_lat // n_ctx, 0), lambda i: (i, 0, 0),
                         lambda i: i, bsz // nb)
    return out_lat, out_ctx


def _block_ones(width, block):
    idx = np.arange(width) // block
    return jnp.asarray(idx[:, None] == idx[None, :], F32).astype(BF16)


def _rope_tables(n_lat, n_ctx):
    rows = n_lat // GRID_W
    row = np.repeat(np.arange(rows), GRID_W).astype(np.float64)
    col = np.tile(np.arange(GRID_W), rows).astype(np.float64)
    inv = ROPE_THETA ** (-np.arange(0, ROPE_AXIS_DIM, 2, dtype=np.float64) / ROPE_AXIS_DIM)
    ang = np.concatenate([row[:, None] * inv, col[:, None] * inv], axis=-1)
    ang = np.concatenate([ang, np.zeros((n_ctx, ang.shape[1]))], axis=0)
    cos = np.concatenate([np.cos(ang), np.cos(ang)], axis=-1)
    sin = np.concatenate([-np.sin(ang), np.sin(ang)], axis=-1)
    scale = Q_SCALE
    cs = np.tile(cos, (1, ATT_HEADS)) * scale
    sn = np.tile(sin, (1, ATT_HEADS)) * scale
    return jnp.asarray(cs, F32), jnp.asarray(sn, F32)


def _pack_w_in(w):
    cuts = np.cumsum([ATT_W, KV_W, KV_W, SSD_W, SSD_CONV_CH, 2 * SSD_HEADS])
    q, k, v, z, xbc, dt, hy = jnp.split(w, [int(c) for c in cuts], axis=-1)
    dt = jnp.pad(dt, ((0, 0), (0, LANES - 2 * SSD_HEADS)))
    return jnp.concatenate([q, k, v, xbc, hy, z, dt], axis=-1).astype(BF16)


def kernel(x, c, ctx, c_ctx, w_mod, b_mod, g_mix, g_ffn, w_in, q_norm, k_norm, ssd_conv_w, ssd_conv_b,
           ssd_dt_bias, ssd_a_log, ssd_d, ssd_norm, hy_conv_w, hy_conv_b, hy_w1, hy_b1, hy_freq, hy_w2, hy_b2,
           hy_w3, hy_bias, w_out, w_router, router_bias, w_gate, w_up, w_down, g_final):
    bsz, n_lat, d = x.shape
    n_ctx = ctx.shape[1]
    depth = w_mod.shape[0]
    t = n_lat + n_ctx
    tm = TOKEN_TILE
    n_lat_tiles = n_lat // tm
    n_tiles = t // tm
    assert n_ctx == tm and n_lat % (HY_N2 * 8) == 0 and bsz < MOD_ROWS

    cvec = jnp.concatenate([c, c_ctx[None], jnp.zeros((MOD_ROWS - bsz - 1, d), F32)], axis=0)
    mods_all = _adaln(cvec, w_mod, b_mod)

    cs, sn = _rope_tables(n_lat, n_ctx)
    bd_head = _block_ones(ATT_W, HEAD_DIM)
    bd_ssd = _block_ones(SSD_W, SSD_W // SSD_GROUPS)
    tables = _dft_tables(n_lat)
    pad_row = lambda v: jnp.pad(v.reshape(1, -1), ((0, 0), (0, LANES - v.size)))

    x_lat, x_ctx, ctx_blk = x, ctx, 0
    for i in range(depth):
        last = i == depth - 1
        mods = mods_all[i].reshape(MOD_ROWS, 1, 6 * d)
        q, kt, v, z, xbc, dt_raw, hy = _inproj(
            x_lat, x_ctx, ctx_blk, mods, g_mix[i].reshape(1, d), _pack_w_in(w_in[i]),
            jnp.tile(q_norm[i], ATT_HEADS).reshape(1, ATT_W), jnp.tile(k_norm[i], ATT_KV_HEADS).reshape(1, KV_W),
            cs, sn, bd_head, jnp.concatenate([ssd_conv_w[i], hy_conv_w[i]], axis=-1),
            jnp.concatenate([ssd_conv_b[i], hy_conv_b[i]]).reshape(1, -1), n_lat_tiles)

        att = _attention(q, kt, v, 0, n_lat, 0, t, ATT_TILE)
        att_c = _attention(q, kt, v, n_lat, n_ctx, n_lat, n_ctx, n_ctx) if not last else att

        ssd = _ssd(xbc, z, dt_raw, pad_row(ssd_dt_bias[i]), pad_row(ssd_a_log[i]),
                   jnp.repeat(ssd_d[i], SSD_HEAD_DIM).reshape(1, SSD_W), ssd_norm[i].reshape(1, SSD_W),
                   bd_ssd, n_lat)

        hyp = (hy_w1[i], hy_b1[i], hy_freq[i], hy_w2[i], hy_b2[i], hy_w3[i])
        hy_l = _hyena_latent(hy, _hyena_filters(n_lat, *hyp), hy_bias[i], n_lat, tables)
        hy_c = hy_l if last else _hyena_ctx(hy, _hyena_filters(n_ctx, *hyp),
                                            hy_bias[i].reshape(HY_ORDER, 1, HY_W), n_lat, n_ctx)

        live_tiles = n_lat_tiles if last else n_tiles
        xs = _outproj(x_lat, x_ctx, ctx_blk, att, att_c, ssd, hy_l, hy_c, mods, w_out[i].astype(BF16),
                      n_lat_tiles, live_tiles)

        wgu = (w_gate[i].astype(BF16), w_up[i].astype(BF16))
        x_lat, x_ctx = _moe(xs, mods, g_ffn[i].reshape(1, d), jnp.pad(w_router, ((0, 0), (0, LANES - N_EXPERTS))),
                            router_bias.reshape(N_EXPERTS, 1), wgu, w_down[i].astype(BF16), g_final.reshape(1, d),
                            n_lat, n_ctx, not last, last)
    return x_lat
```

```python
import functools
import math

import jax
import jax.numpy as jnp
import numpy as np
from jax import lax
from jax.experimental import pallas as pl
from jax.experimental.pallas import tpu as pltpu

F32 = jnp.float32
BF16 = jnp.bfloat16

D_MODEL = 1024
GRID_W = 64
EPS = 1e-6

ATT_HEADS = 6
ATT_KV_HEADS = 2
HEAD_DIM = 64
ATT_W = ATT_HEADS * HEAD_DIM
KV_W = ATT_KV_HEADS * HEAD_DIM
ROPE_AXIS_DIM = HEAD_DIM // 2
ROPE_THETA = 10000.0
Q_SCALE = HEAD_DIM ** -0.5 * math.log2(math.e)

SSD_HEADS = 6
SSD_HEAD_DIM = 64
SSD_W = SSD_HEADS * SSD_HEAD_DIM
SSD_GROUPS = 2
SSD_STATE = 64
SSD_CHUNK = 128
SSD_CONV_CH = SSD_W + 2 * SSD_GROUPS * SSD_STATE
SSD_HEADS_PER_GROUP = SSD_HEADS // SSD_GROUPS
SSD_BLOCK = 2

HY_W = 256
HY_ORDER = 2
HY_BANDS = 16
HY_POS_DIM = 1 + 2 * HY_BANDS
HY_FILTER_HID = 64
HY_FAST_DECAY = 0.3
HY_SLOW_DECAY = 1.5
HY_TARGET = 1e-2
HY_N2 = 64
HY_UNROLL = 8

MIX_W = ATT_W + SSD_W + HY_W
N_EXPERTS = 16
N_GROUPS = 4
EXPERTS_PER_GROUP = N_EXPERTS // N_GROUPS
D_FF = 256

LANES = 128
SUBLANES = 8
TOKEN_TILE = 256
ATT_TILE = 1024
ATT_SUB = 512
MOE_TILE = 512
MOE_CHUNK = 128
MOE_CHUNKS = MOE_TILE // MOE_CHUNK + N_GROUPS - 1
MOD_ROWS = 8
VMEM_LIMIT = 56 * 1024 * 1024

COL_Q = 0
COL_K = COL_Q + ATT_W
COL_V = COL_K + KV_W
COL_XBC = COL_V + KV_W
COL_HY = COL_XBC + SSD_CONV_CH
COL_Z = COL_HY + 3 * HY_W
COL_DT = COL_Z + SSD_W
COL_END = COL_DT + LANES


def _params(*sem):
    return pltpu.CompilerParams(dimension_semantics=sem, vmem_limit_bytes=VMEM_LIMIT)


def _silu(x):
    return x * jax.nn.sigmoid(x)


def _softplus(x):
    return jnp.maximum(x, 0.0) + jnp.log1p(jnp.exp(-jnp.abs(x)))


def _split3(x):
    hi = x.astype(BF16)
    r1 = x - hi.astype(F32)
    mid = r1.astype(BF16)
    lo = (r1 - mid.astype(F32)).astype(BF16)
    return hi, mid, lo


def _dot(a, b):
    return jnp.dot(a, b, preferred_element_type=F32)


def _dot_x3(a, b):
    a_hi, a_mid, _ = _split3(a)
    b_hi, b_mid, _ = _split3(b)
    return _dot(a_hi, b_hi) + (_dot(a_hi, b_mid) + _dot(a_mid, b_hi))


def _dot3_right(m_bf16, x):
    hi, mid, lo = _split3(x)
    return _dot(m_bf16, hi) + _dot(m_bf16, mid) + _dot(m_bf16, lo)


def _dot3_left(x, m_bf16):
    hi, mid, lo = _split3(x)
    return _dot(hi, m_bf16) + _dot(mid, m_bf16) + _dot(lo, m_bf16)


def _adaln_kernel(c_ref, w_ref, b_ref, o_ref):
    s = _silu(c_ref[...]).astype(BF16)
    o_ref[0] = _dot(s, w_ref[0].astype(BF16)) + b_ref[0]


def _adaln(cvec, w_mod, b_mod):
    depth, d, n = w_mod.shape
    bn = n // 4
    return pl.pallas_call(
        _adaln_kernel,
        grid=(depth, n // bn),
        in_specs=[
            pl.BlockSpec((MOD_ROWS, d), lambda i, j: (0, 0)),
            pl.BlockSpec((1, d, bn), lambda i, j: (i, 0, j)),
            pl.BlockSpec((1, 1, bn), lambda i, j: (i, 0, j)),
        ],
        out_specs=pl.BlockSpec((1, MOD_ROWS, bn), lambda i, j: (i, 0, j)),
        out_shape=jax.ShapeDtypeStruct((depth, MOD_ROWS, n), F32),
        compiler_params=_params("arbitrary", "arbitrary"),
        name="adaln",
    )(cvec, w_mod, b_mod.reshape(depth, 1, n))


def _head_rope(xn, cs, sn):
    width = xn.shape[-1]
    lane = lax.broadcasted_iota(jnp.int32, xn.shape, 1)
    first_half = (lane % HEAD_DIM) < (HEAD_DIM // 2)
    partner = jnp.where(first_half,
                        pltpu.roll(xn, width - HEAD_DIM // 2, 1),
                        pltpu.roll(xn, HEAD_DIM // 2, 1))
    return xn * cs + partner * sn


def _stream_specs(n_lat_tiles, ctx_blk, d):
    lat = pl.BlockSpec((1, TOKEN_TILE, d), lambda j, b: (b, jnp.minimum(j, n_lat_tiles - 1), 0))
    ctx = pl.BlockSpec((1, TOKEN_TILE, d), lambda j, b: (jnp.where(j >= n_lat_tiles, b, 0), ctx_blk, 0))
    return lat, ctx


def _stream_tile(xl_ref, xc_ref, n_lat_tiles):
    return jnp.where(pl.program_id(0) >= n_lat_tiles, xc_ref[0], xl_ref[0])


def _pack_w_in(w_ref, wb_ref):
    src_z = ATT_W + 2 * KV_W
    src_xbc = src_z + SSD_W
    src_dt = src_xbc + SSD_CONV_CH
    n_dt = 2 * SSD_HEADS
    wb_ref[:, COL_Q:COL_XBC] = w_ref[0, :, :src_z].astype(BF16)
    wb_ref[:, COL_XBC:COL_HY] = w_ref[0, :, src_xbc:src_dt].astype(BF16)
    tail = w_ref[0, :, src_dt:]
    wb_ref[:, COL_HY:COL_Z] = tail[:, n_dt:].astype(BF16)
    wb_ref[:, COL_Z:COL_DT] = w_ref[0, :, src_z:src_xbc].astype(BF16)
    head = tail[:, :LANES]
    lane = lax.broadcasted_iota(jnp.int32, head.shape, 1)
    wb_ref[:, COL_DT:COL_END] = jnp.where(lane < n_dt, head, 0.0).astype(BF16)


def _inproj_kernel(xl_ref, xc_ref, xp_ref, xn_ref, sh_ref, sc_ref, g_ref, w_ref, gq_ref, gk_ref, cs_ref, sn_ref,
                   bd_ref, cw_ref, cb_ref, qt_ref, k_ref, vt_ref, z_ref, xbc_ref, dt_ref, hy_ref, wb_ref,
                   *, n_lat_tiles):
    j = pl.program_id(0)

    @pl.when((j == 0) & (pl.program_id(1) == 0))
    def _():
        _pack_w_in(w_ref, wb_ref)

    tm, halo = TOKEN_TILE, SUBLANES
    x = jnp.concatenate([xp_ref[0], _stream_tile(xl_ref, xc_ref, n_lat_tiles), xn_ref[0]], axis=0)
    ms = jnp.mean(x * x, axis=-1, keepdims=True)
    h = x * lax.rsqrt(ms + EPS) * g_ref[...]
    h = h * (1.0 + sc_ref[0]) + sh_ref[0]
    hb = h[halo:halo + tm].astype(BF16)
    u = _dot(h.astype(BF16), wb_ref[:, COL_XBC:COL_Z])
    qkv = _dot(hb, wb_ref[:, COL_Q:COL_XBC])
    bd = bd_ref[...]
    q, k = qkv[:, COL_Q:COL_K], qkv[:, COL_K:COL_V]
    ms_q = _dot((q * q).astype(BF16), bd) * (1.0 / HEAD_DIM)
    ms_k = _dot((k * k).astype(BF16), bd[:KV_W, :KV_W]) * (1.0 / HEAD_DIM)
    zdt = _dot(hb, wb_ref[:, COL_Z:COL_END])

    cs = cs_ref[...]
    sn = sn_ref[...]
    qt_ref[0] = _head_rope(q * lax.rsqrt(ms_q + EPS) * gq_ref[...], cs, sn).T.astype(qt_ref.dtype)
    k_ref[0] = _head_rope(k * lax.rsqrt(ms_k + EPS) * gk_ref[...],
                          cs[:, :KV_W] * (1.0 / Q_SCALE), sn[:, :KV_W] * (1.0 / Q_SCALE)).astype(k_ref.dtype)
    vt_ref[0] = qkv[:, COL_V:COL_XBC].T.astype(vt_ref.dtype)

    row = lax.broadcasted_iota(jnp.int32, (u.shape[0], 1), 0)
    has_prev = (j >= 1) & (j < n_lat_tiles)
    has_next = j < n_lat_tiles - 1
    u = jnp.where(((row >= halo) | has_prev) & ((row < halo + tm) | has_next), u, 0.0)
    cw = cw_ref[...]
    own = slice(halo, halo + tm)
    y = (pltpu.roll(u, 1, 0)[own] * cw[0:1] + u[own] * cw[1:2] + pltpu.roll(u, u.shape[0] - 1, 0)[own] * cw[2:3]
         + cb_ref[...])
    xbc_ref[0] = _silu(y[:, :COL_HY - COL_XBC])
    hy_ref[0] = y[:, COL_HY - COL_XBC:]
    z_ref[0] = zdt[:, :COL_DT - COL_Z]
    dt_ref[0] = zdt[:, COL_DT - COL_Z:]


def _inproj(x_lat, x_ctx, ctx_blk, mods, g_mix, w_in, layer, gq, gk, cs, sn, bd, conv_w, conv_b, n_lat_tiles):
    bsz, _, d = x_lat.shape
    tm = TOKEN_TILE
    nt = n_lat_tiles + 1
    t = nt * tm
    per_tile = tm // SUBLANES
    n_conv = COL_Z - COL_XBC

    def mod_row(j, b):
        return jnp.where(j >= n_lat_tiles, bsz, b)

    lat_j = lambda j: jnp.minimum(j, n_lat_tiles - 1)
    prev_spec = pl.BlockSpec((1, SUBLANES, d), lambda j, b: (b, jnp.maximum(lat_j(j) * per_tile - 1, 0), 0))
    next_spec = pl.BlockSpec((1, SUBLANES, d),
                             lambda j, b: (b, jnp.minimum((lat_j(j) + 1) * per_tile, n_lat_tiles * per_tile - 1), 0))

    tok = lambda w: pl.BlockSpec((1, tm, w), lambda j, b: (b, j, 0))
    const = lambda shape: pl.BlockSpec(shape, lambda j, b: tuple(0 for _ in shape))
    outs = pl.pallas_call(
        functools.partial(_inproj_kernel, n_lat_tiles=n_lat_tiles),
        grid=(nt, bsz),
        in_specs=[
            *_stream_specs(n_lat_tiles, ctx_blk, d),
            prev_spec, next_spec,
            pl.BlockSpec((1, 1, d), lambda j, b: (mod_row(j, b), 0, 0)),
            pl.BlockSpec((1, 1, d), lambda j, b: (mod_row(j, b), 0, 1)),
            const((1, d)),
            pl.BlockSpec((1,) + w_in.shape[1:], lambda j, b: (layer, 0, 0), pipeline_mode=pl.Buffered(1)),
            const((1, ATT_W)),
            const((1, KV_W)),
            pl.BlockSpec((tm, ATT_W), lambda j, b: (j, 0)),
            pl.BlockSpec((tm, ATT_W), lambda j, b: (j, 0)),
            const((ATT_W, ATT_W)),
            const((3, n_conv)),
            const((1, n_conv)),
        ],
        out_specs=[
            pl.BlockSpec((1, ATT_W, tm), lambda j, b: (b, 0, j)),
            tok(KV_W),
            pl.BlockSpec((1, KV_W, tm), lambda j, b: (b, 0, j)),
            tok(SSD_W),
            tok(SSD_CONV_CH),
            tok(LANES),
            tok(3 * HY_W),
        ],
        out_shape=[
            jax.ShapeDtypeStruct((bsz, ATT_W, t), BF16),
            jax.ShapeDtypeStruct((bsz, t, KV_W), BF16),
            jax.ShapeDtypeStruct((bsz, KV_W, t), BF16),
            jax.ShapeDtypeStruct((bsz, t, SSD_W), F32),
            jax.ShapeDtypeStruct((bsz, t, SSD_CONV_CH), F32),
            jax.ShapeDtypeStruct((bsz, t, LANES), F32),
            jax.ShapeDtypeStruct((bsz, t, 3 * HY_W), F32),
        ],
        scratch_shapes=[pltpu.VMEM((d, COL_END), BF16)],
        compiler_params=_params("arbitrary", "arbitrary"),
        name="inproj",
    )(x_lat, x_ctx, x_lat, x_lat, mods, mods, g_mix, w_in, gq, gk, cs, sn, bd, conv_w, conv_b)
    return outs


def _attn_kernel(qt_ref, k_ref, vt_ref, o_ref, st_ref, pt_ref):
    k = k_ref[0]
    vt = vt_ref[0]
    rep = ATT_HEADS // ATT_KV_HEADS
    tq = qt_ref.shape[2]
    sub = min(tq, ATT_SUB)
    ones = jnp.ones((2 * SUBLANES, vt.shape[1]), vt.dtype)
    vtg = [jnp.concatenate([vt[g * HEAD_DIM:(g + 1) * HEAD_DIM, :], ones], axis=0) for g in range(ATT_KV_HEADS)]
    units = [(c0, hd) for c0 in range(0, tq, sub) for hd in range(ATT_HEADS)]

    def scores(u):
        c0, hd = units[u]
        qh = qt_ref[0, hd * HEAD_DIM:(hd + 1) * HEAD_DIM, c0:c0 + sub]
        zero = jnp.zeros_like(qh)
        w = jnp.concatenate([qh, zero] if hd < rep else [zero, qh], axis=0)
        st = _dot(k, w)
        st_ref[u % 2] = st
        return jnp.max(st, axis=0, keepdims=True)

    m_next = scores(0)
    outs = []
    for u, (c0, hd) in enumerate(units):
        m = m_next
        if u + 1 < len(units):
            m_next = scores(u + 1)
        pt_ref[u % 2] = jnp.exp2(st_ref[u % 2] - m).astype(BF16)
        ot = _dot(vtg[hd // rep], pt_ref[u % 2])
        outs.append(ot[:HEAD_DIM] / ot[HEAD_DIM:HEAD_DIM + 1])
        if hd == ATT_HEADS - 1:
            o_ref[0, c0:c0 + sub, :] = jnp.concatenate(outs, axis=0).T.astype(o_ref.dtype)
            outs = []


def _attention(qt, k, vt, q_row0, n_q, k_row0, n_k, tq):
    bsz = qt.shape[0]
    kblk = k_row0 // n_k
    q_tile0 = q_row0 // tq
    n_q_tiles = n_q // tq
    assert kblk * n_k == k_row0 and q_tile0 * tq == q_row0 and n_q_tiles * tq == n_q
    assert ATT_KV_HEADS == 2
    return pl.pallas_call(
        _attn_kernel,
        grid=(bsz, n_q_tiles),
        in_specs=[
            pl.BlockSpec((1, ATT_W, tq), lambda b, j: (b, 0, q_tile0 + j)),
            pl.BlockSpec((1, n_k, KV_W), lambda b, j: (b, kblk, 0)),
            pl.BlockSpec((1, KV_W, n_k), lambda b, j: (b, 0, kblk)),
        ],
        out_specs=pl.BlockSpec((1, tq, ATT_W), lambda b, j: (b, j, 0)),
        out_shape=jax.ShapeDtypeStruct((bsz, n_q_tiles * tq, ATT_W), BF16),
        scratch_shapes=[pltpu.VMEM((2, n_k, min(tq, ATT_SUB)), F32), pltpu.VMEM((2, n_k, min(tq, ATT_SUB)), BF16)],
        compiler_params=_params("arbitrary", "arbitrary"),
        name="attention",
    )(qt, k, vt)


def _ssd_kernel(xbc_ref, z_ref, dt_ref, dtb_ref, alog_ref, dsk_ref, nw_ref, bd_ref,
                o_ref, hf_ref, hb_ref, hbe_ref, *, n_lat_blocks):
    q = SSD_CHUNK
    nh = SSD_HEADS
    phase = pl.program_id(1)
    step = pl.program_id(2)
    block = jnp.where(phase == 0, n_lat_blocks - step, jnp.where(step < 1, n_lat_blocks, step - 1))

    lane = lax.broadcasted_iota(jnp.int32, (1, LANES), 1)
    a_row = jnp.where(lane < 2 * nh, -jnp.exp(alog_ref[...]), 0.0)
    tt = lax.broadcasted_iota(jnp.int32, (q, q), 0)
    ss = lax.broadcasted_iota(jnp.int32, (q, q), 1)
    lower = (ss <= tt)
    upper = (ss >= tt)
    lmat = jnp.where(lower, 1.0, 0.0).astype(BF16)
    umat = jnp.where(upper, 1.0, 0.0).astype(BF16)

    @pl.when(step == 0)
    def _():
        hf_ref[...] = jnp.zeros_like(hf_ref)
        hb_ref[...] = jnp.zeros_like(hb_ref)

    def load_chunk(ci):
        rows = slice(ci * q, (ci + 1) * q)
        xbc = xbc_ref[0, rows, :]
        dt = _softplus(dt_ref[0, rows, :] + dtb_ref[...])
        return rows, xbc, dt, dt * a_row

    def heads_x(x):
        return [x[:, hd * SSD_HEAD_DIM:(hd + 1) * SSD_HEAD_DIM].astype(BF16) for hd in range(nh)]

    def backward_prepare(ci):
        _, xbc, dt, a = load_chunk(ci)
        bt = xbc[:, SSD_W:SSD_W + SSD_GROUPS * SSD_STATE].T
        dt_t = dt.T
        suf_t = _dot3_left(a.T, lmat)
        terms = []
        for hd in range(nh):
            g = hd // SSD_HEADS_PER_GROUP
            row_b = suf_t[nh + hd:nh + hd + 1, :]
            total = row_b[:, 0:1]
            w_t = jnp.exp(total - row_b) * dt_t[nh + hd:nh + hd + 1, :]
            terms.append((jnp.exp(total), (bt[g * SSD_STATE:(g + 1) * SSD_STATE, :] * w_t).astype(BF16)))
        return terms, heads_x(xbc[:, :SSD_W])

    def backward_state(ci, prepared):
        terms, xh = prepared
        chunk = block * SSD_BLOCK + ci
        for hd in range(nh):
            decay, bw = terms[hd]
            prev = hb_ref[hd]
            hbe_ref[chunk, hd] = prev
            hb_ref[hd] = prev * decay + _dot(bw, xh[hd])

    def forward_prepare(ci):
        rows, xbc, dt, a = load_chunk(ci)
        x = xbc[:, :SSD_W]
        bt = xbc[:, SSD_W:SSD_W + SSD_GROUPS * SSD_STATE].T
        cmat = xbc[:, SSD_W + SSD_GROUPS * SSD_STATE:]
        a_t = a.T
        pre = _dot3_right(lmat, a)
        pre_t = _dot3_left(a_t, umat)
        cgs = [cmat[:, g * SSD_STATE:(g + 1) * SSD_STATE].astype(BF16) for g in range(SSD_GROUPS)]
        btgs = [bt[g * SSD_STATE:(g + 1) * SSD_STATE, :] for g in range(SSD_GROUPS)]
        cbs = [_dot(cgs[g], btgs[g].astype(BF16)) for g in range(SSD_GROUPS)]
        return dict(rows=rows, x=x, xh=heads_x(x), a=a, a_t=a_t, dt_t=dt.T, pre=pre, pre_t=pre_t,
                    cgs=cgs, btgs=btgs, cbs=cbs)

    def forward_mix(p):
        pre, pre_t, dt_t = p["pre"], p["pre_t"], p["dt_t"]
        suf = pre[q - 1:q, :] - pre + p["a"]
        suf_t = pre_t[:, q - 1:q] - pre_t + p["a_t"]
        neg_inf = jnp.float32(-jnp.inf)
        p["y_in"], p["scale"], p["upd"] = [], [], []
        for hd in range(nh):
            g = hd // SSD_HEADS_PER_GROUP
            colf = pre[:, hd:hd + 1]
            rowf = pre_t[hd:hd + 1, :]
            colb = suf[:, nh + hd:nh + hd + 1]
            rowb = suf_t[nh + hd:nh + hd + 1, :]
            wf = jnp.exp(jnp.where(lower, colf - rowf, neg_inf)) * dt_t[hd:hd + 1, :]
            wb = jnp.exp(jnp.where(upper, colb - rowb, neg_inf)) * dt_t[nh + hd:nh + hd + 1, :]
            p["y_in"].append(_dot((p["cbs"][g] * (wf + wb)).astype(BF16), p["xh"][hd]))
            p["scale"].append((jnp.exp(colf), jnp.exp(colb)))
            total = rowf[:, q - 1:q]
            w_t = jnp.exp(total - rowf) * dt_t[hd:hd + 1, :]
            p["upd"].append((jnp.exp(total), (p["btgs"][g] * w_t).astype(BF16)))

    def forward_state(ci, p):
        chunk = block * SSD_BLOCK + ci
        rows, x = p["rows"], p["x"]
        ys = []
        for hd in range(nh):
            cg = p["cgs"][hd // SSD_HEADS_PER_GROUP]
            hf = hf_ref[hd]
            ef, eb = p["scale"][hd]
            y = p["y_in"][hd] + _dot(cg, hf.astype(BF16)) * ef + _dot(cg, hbe_ref[chunk, hd].astype(BF16)) * eb
            decay, bw = p["upd"][hd]
            hf_ref[hd] = hf * decay + _dot(bw, p["xh"][hd])
            ys.append(y)
        y = jnp.concatenate(ys, axis=-1) + x * dsk_ref[...]
        gz = y * _silu(z_ref[0, rows, :])
        ms = _dot((gz * gz).astype(BF16), bd_ref[...]) * (1.0 / (SSD_W // SSD_GROUPS))
        o_ref[0, rows, :] = (gz * lax.rsqrt(ms + EPS) * nw_ref[...]).astype(o_ref.dtype)

    @pl.when(phase == 0)
    def _():
        prepared = [backward_prepare(ci) for ci in range(SSD_BLOCK)]
        for ci in reversed(range(SSD_BLOCK)):
            backward_state(ci, prepared[ci])

    @pl.when(phase == 1)
    def _():
        prepared = [forward_prepare(ci) for ci in range(SSD_BLOCK)]
        for p in prepared:
            forward_mix(p)
        for ci in range(SSD_BLOCK):
            forward_state(ci, prepared[ci])


def _ssd(xbc, z, dt_raw, dt_bias, a_log, d_skip, norm_w, bd, n_lat):
    bsz, t, _ = z.shape
    rows = SSD_BLOCK * SSD_CHUNK
    n_lat_blocks = n_lat // rows
    n_blocks = t // rows
    assert n_blocks == n_lat_blocks + 1 and n_lat_blocks * rows == n_lat

    def block_of(p, s):
        return jnp.where(p == 0, n_lat_blocks - s, jnp.where(s < 1, n_lat_blocks, s - 1))

    def out_block(p, s):
        return jnp.where(p == 0, n_lat_blocks, block_of(p, s))

    const = lambda shape: pl.BlockSpec(shape, lambda b, p, s: tuple(0 for _ in shape))
    return pl.pallas_call(
        functools.partial(_ssd_kernel, n_lat_blocks=n_lat_blocks),
        grid=(bsz, 2, n_blocks),
        in_specs=[
            pl.BlockSpec((1, rows, SSD_CONV_CH), lambda b, p, s: (b, block_of(p, s), 0)),
            pl.BlockSpec((1, rows, SSD_W), lambda b, p, s: (b, block_of(p, s), 0)),
            pl.BlockSpec((1, rows, LANES), lambda b, p, s: (b, block_of(p, s), 0)),
            const((1, LANES)),
            const((1, LANES)),
            const((1, SSD_W)),
            const((1, SSD_W)),
            const((SSD_W, SSD_W)),
        ],
        out_specs=pl.BlockSpec((1, rows, SSD_W), lambda b, p, s: (b, out_block(p, s), 0)),
        out_shape=jax.ShapeDtypeStruct((bsz, t, SSD_W), BF16),
        scratch_shapes=[
            pltpu.VMEM((SSD_HEADS, SSD_STATE, SSD_HEAD_DIM), F32),
            pltpu.VMEM((SSD_HEADS, SSD_STATE, SSD_HEAD_DIM), F32),
            pltpu.VMEM((n_blocks * SSD_BLOCK, SSD_HEADS, SSD_STATE, SSD_HEAD_DIM), F32),
        ],
        compiler_params=_params("arbitrary", "arbitrary", "arbitrary"),
        name="ssd",
    )(xbc, z, dt_raw, dt_bias, a_log, d_skip, norm_w, bd)


def _hyfilt_kernel(feat_ref, win_ref, w1_ref, b1_ref, fr_ref, w2_ref, b2_ref, w3_ref, o_ref):
    hp = lax.Precision.HIGHEST
    fr = fr_ref[...]
    h1 = jnp.sin(fr * (jnp.dot(feat_ref[...], w1_ref[...], precision=hp, preferred_element_type=F32) + b1_ref[...]))
    h2 = jnp.sin(fr * (jnp.dot(h1, w2_ref[...], precision=hp, preferred_element_type=F32) + b2_ref[...]))
    h = _dot_x3(h2, w3_ref[...])
    win = win_ref[...]
    first_tile = pl.program_id(0) == 0
    row = lax.broadcasted_iota(jnp.int32, win.shape, 0)
    for order in range(HY_ORDER):
        for direction in range(2):
            c0 = (direction * HY_ORDER + order) * HY_W
            f = h[:, c0:c0 + HY_W] * win
            if direction == 1:
                f = jnp.where(first_tile & (row == 0), 0.0, f)
            o_ref[order * 2 + direction] = f


def _hy_positional(length):
    n = np.arange(length, dtype=np.float64)
    t = n / max(length - 1, 1)
    bands = np.linspace(1e-4, HY_BANDS - 1, HY_BANDS)
    wpos = (2 * math.pi / length) * n
    feats = np.concatenate([t[:, None], np.cos(wpos[:, None] * bands), -np.sin(wpos[:, None] * bands)], axis=-1)
    feats = np.pad(feats, ((0, 0), (0, LANES - HY_POS_DIM)))
    deltas = np.abs(np.linspace(math.log(HY_TARGET) / HY_SLOW_DECAY, math.log(HY_TARGET) / HY_FAST_DECAY, HY_W))
    window = np.exp(-t[:, None] * deltas)
    return jnp.asarray(feats, F32), jnp.asarray(window, F32)


def _hyena_filters(length, w1, b1, freq, w2, b2, w3):
    feats, window = _hy_positional(length)
    tl = min(length, 512)
    hid = HY_FILTER_HID
    w1p = jnp.pad(w1, ((0, LANES - HY_POS_DIM), (0, 0)))
    const = lambda shape: pl.BlockSpec(shape, lambda i: tuple(0 for _ in shape))
    return pl.pallas_call(
        _hyfilt_kernel,
        grid=(length // tl,),
        in_specs=[
            pl.BlockSpec((tl, LANES), lambda i: (i, 0)),
            pl.BlockSpec((tl, HY_W), lambda i: (i, 0)),
            const((LANES, hid)), const((1, hid)), const((1, hid)),
            const((hid, hid)), const((1, hid)), const((hid, 2 * HY_ORDER * HY_W)),
        ],
        out_specs=pl.BlockSpec((2 * HY_ORDER, tl, HY_W), lambda i: (0, i, 0)),
        out_shape=jax.ShapeDtypeStruct((2 * HY_ORDER, length, HY_W), F32),
        compiler_params=_params("arbitrary"),
        name="hyena_filters",
    )(feats, window, w1p, b1.reshape(1, hid), freq.reshape(1, hid), w2, b2.reshape(1, hid), w3)


def _dft_tables(length):
    n = 2 * length
    n2 = HY_N2
    n1 = n // n2
    half = n1 // 2
    k1 = np.arange(half, dtype=np.float64) + 0.5
    idx = (n2 * np.arange(half)[None, None, :] + np.arange(n2)[:, None, None])
    ang = 2 * np.pi * k1[None, :, None] * idx / n
    fa = np.concatenate([np.cos(ang), -np.sin(ang)], axis=1)
    kk = np.arange(n2, dtype=np.float64)
    angb = 2 * np.pi * kk[:, None] * kk[None, :] / n2
    cr, sr = np.cos(angb), np.sin(angb)
    fb = np.block([[cr, sr], [-sr, cr]])
    fbi = np.block([[cr, -sr], [sr, cr]])
    idxo = (n2 * np.arange(half)[None, :, None] + np.arange(n2)[:, None, None])
    ango = 2 * np.pi * k1[None, None, :] * idxo / n
    ga = np.concatenate([np.cos(ango), -np.sin(ango)], axis=2) * (2.0 / n)
    f32 = lambda a: jnp.asarray(a, F32).astype(BF16)
    return f32(fa), f32(fb), f32(fbi), f32(ga)


def _seq_pitch(n2n):
    return n2n + SUBLANES


def _to_pitched(dst_ref, val, n2n):
    pitch = _seq_pitch(n2n)
    for i in range(val.shape[0] // n2n):
        dst_ref[i * pitch:i * pitch + n2n, :] = val[i * n2n:(i + 1) * n2n, :]


def _hy_forward_a(src_ref, a_ref, fa_ref):
    n2n, two_n1, half = fa_ref.shape
    zp, ap = _seq_pitch(n2n), _seq_pitch(two_n1)

    def body(n2, carry):
        zs = src_ref[pl.ds(n2, half, stride=zp), :].astype(BF16)
        a_ref[pl.ds(pl.multiple_of(n2 * ap, SUBLANES), two_n1), :] = _dot(fa_ref[n2], zs)
        return carry

    lax.fori_loop(0, n2n, body, 0, unroll=HY_UNROLL)


def _hy_kspec_kernel(hf_ref, hb_ref, fa_ref, fb_ref, o_ref, af_ref, ab_ref, hfp_ref, hbp_ref):
    n2n, two_n1, _ = fa_ref.shape
    n1 = two_n1 // 2
    ap = _seq_pitch(two_n1)
    _to_pitched(hfp_ref, hf_ref[0], n2n)
    _to_pitched(hbp_ref, hb_ref[0], n2n)
    _hy_forward_a(hfp_ref, af_ref, fa_ref)
    _hy_forward_a(hbp_ref, ab_ref, fa_ref)
    fb = fb_ref[...]

    def body(k1, carry):
        def spectrum(a_ref):
            ar = a_ref[pl.ds(k1, n2n, stride=ap), :]
            ai = a_ref[pl.ds(n1 + k1, n2n, stride=ap), :]
            return _dot(fb, jnp.concatenate([ar, ai], axis=0).astype(BF16))

        xf, xb = spectrum(af_ref), spectrum(ab_ref)
        o_ref[0, k1] = jnp.concatenate([xf[:n2n] + xb[:n2n], xf[n2n:] - xb[n2n:]], axis=0)
        return carry

    lax.fori_loop(0, n1, body, 0, unroll=HY_UNROLL)


def _hy_kspec(filt, tables):
    fa, fb, _, _ = tables
    _, length, _ = filt.shape
    n2n, two_n1, _ = fa.shape
    n1 = two_n1 // 2
    nh = HY_W // LANES
    return pl.pallas_call(
        _hy_kspec_kernel,
        grid=(HY_ORDER, nh),
        in_specs=[
            pl.BlockSpec((1, length, LANES), lambda o, h: (2 * o, 0, h)),
            pl.BlockSpec((1, length, LANES), lambda o, h: (2 * o + 1, 0, h)),
            pl.BlockSpec(fa.shape, lambda o, h: (0, 0, 0), pipeline_mode=pl.Buffered(1)),
            pl.BlockSpec(fb.shape, lambda o, h: (0, 0)),
        ],
        out_specs=pl.BlockSpec((1, n1, 2 * n2n, LANES), lambda o, h: (o, 0, 0, h)),
        out_shape=jax.ShapeDtypeStruct((HY_ORDER, n1, 2 * n2n, HY_W), F32),
        scratch_shapes=[pltpu.VMEM((n2n * _seq_pitch(two_n1), LANES), F32),
                        pltpu.VMEM((n2n * _seq_pitch(two_n1), LANES), F32),
                        pltpu.VMEM((length // n2n * _seq_pitch(n2n), LANES), F32),
                        pltpu.VMEM((length // n2n * _seq_pitch(n2n), LANES), F32)],
        compiler_params=_params("arbitrary", "arbitrary"),
        name="hyena_kspec",
    )(filt, filt, fa, fb)


def _hy_conv_kernel(z_ref, g_ref, fa_ref, fb_ref, fbi_ref, ga_ref, k_ref, bias_ref,
                    o_ref, a_ref, c_ref, zc_ref, gc_ref):
    n2n, two_n1, half = fa_ref.shape
    n1 = two_n1 // 2
    zp, ap = _seq_pitch(n2n), _seq_pitch(two_n1)
    _to_pitched(zc_ref, z_ref[0], n2n)
    _to_pitched(gc_ref, g_ref[0], n2n)
    _hy_forward_a(zc_ref, a_ref, fa_ref)
    fb = fb_ref[...]
    fbi = fbi_ref[...]

    def body_b(grp, carry):
        k1s = [grp * HY_UNROLL + i for i in range(HY_UNROLL)]
        sl = [(pl.ds(k1, n2n, stride=ap), pl.ds(n1 + k1, n2n, stride=ap)) for k1 in k1s]
        xs = [_dot(fb, jnp.concatenate([a_ref[re, :], a_ref[im, :]], axis=0).astype(BF16)) for re, im in sl]
        ys = []
        for k1, x in zip(k1s, xs):
            kk = k_ref[0, k1]
            xr, xi = x[:n2n], x[n2n:]
            kr, ki = kk[:n2n], kk[n2n:]
            ys.append(jnp.concatenate([xr * kr - xi * ki, xr * ki + xi * kr], axis=0).astype(BF16))
        for (re, im), y in zip(sl, ys):
            c = _dot(fbi, y)
            c_ref[re, :] = c[:n2n]
            c_ref[im, :] = c[n2n:]
        return carry

    lax.fori_loop(0, n1 // HY_UNROLL, body_b, 0)
    bias = bias_ref[0]

    def body_c(m2, carry):
        rows = pl.ds(m2, half, stride=zp)
        c = c_ref[pl.ds(pl.multiple_of(m2 * ap, SUBLANES), two_n1), :].astype(BF16)
        y = _dot(ga_ref[m2], c)
        a_ref[rows, :] = gc_ref[rows, :] * (y + zc_ref[rows, :] * bias)
        return carry

    lax.fori_loop(0, n2n, body_c, 0, unroll=HY_UNROLL)
    for i in range(half):
        o_ref[0, i * n2n:(i + 1) * n2n, :] = a_ref[i * zp:i * zp + n2n, :]


def _hy_conv(z_arr, z_blk, g_arr, g_blk, kspec, order, bias3, tables, n_lat):
    fa, fb, fbi, ga = tables
    bsz = z_arr.shape[0]
    n2n, two_n1, half = fa.shape
    n1 = two_n1 // 2
    nh = HY_W // LANES
    resident = lambda shape: pl.BlockSpec(shape, lambda h, b_: tuple(0 for _ in shape), pipeline_mode=pl.Buffered(1))
    return pl.pallas_call(
        _hy_conv_kernel,
        grid=(nh, bsz),
        in_specs=[
            pl.BlockSpec((1, n_lat, LANES), lambda h, b_: (b_, 0, z_blk + h)),
            pl.BlockSpec((1, n_lat, LANES), lambda h, b_: (b_, 0, g_blk + h)),
            resident(fa.shape), resident(fb.shape), resident(fbi.shape), resident(ga.shape),
            pl.BlockSpec((1, n1, 2 * n2n, LANES), lambda h, b_: (order, 0, 0, h), pipeline_mode=pl.Buffered(1)),
            pl.BlockSpec((1, 1, LANES), lambda h, b_: (order, 0, h)),
        ],
        out_specs=pl.BlockSpec((1, n_lat, LANES), lambda h, b_: (b_, 0, h)),
        out_shape=jax.ShapeDtypeStruct((bsz, n_lat, HY_W), F32),
        scratch_shapes=[pltpu.VMEM((n2n * _seq_pitch(two_n1), LANES), F32),
                        pltpu.VMEM((n2n * _seq_pitch(two_n1), LANES), F32),
                        pltpu.VMEM((half * _seq_pitch(n2n), LANES), F32),
                        pltpu.VMEM((half * _seq_pitch(n2n), LANES), F32)],
        compiler_params=_params("arbitrary", "arbitrary"),
        name="hyena_conv",
    )(z_arr, g_arr, fa, fb, fbi, ga, kspec, bias3)


def _hyena_ctx_kernel(u_ref, filt_ref, bias_ref, fd_ref, gd_ref, o_ref, *, n_ctx):
    fd = fd_ref[...]
    gd = gd_ref[...]
    nn = 2 * n_ctx
    u = u_ref[0]
    v, x1, x2 = u[:, :HY_W], u[:, HY_W:2 * HY_W], u[:, 2 * HY_W:]

    def conv(zin, order):
        hf = _dot(fd, filt_ref[2 * order].astype(BF16))
        hb = _dot(fd, filt_ref[2 * order + 1].astype(BF16))
        kr, ki = hf[:nn] + hb[:nn], hf[nn:] - hb[nn:]
        zz = _dot(fd, zin.astype(BF16))
        zr, zi = zz[:nn], zz[nn:]
        y = jnp.concatenate([zr * kr - zi * ki, zr * ki + zi * kr], axis=0).astype(BF16)
        return _dot(gd, y) + zin * bias_ref[order]

    y1 = x1 * conv(v, 0)
    o_ref[0] = x2 * conv(y1, 1)


def _hyena_ctx(hy, filt, bias, n_lat, n_ctx):
    bsz = hy.shape[0]
    nn = 2 * n_ctx
    k = np.arange(nn, dtype=np.float64)
    t = np.arange(n_ctx, dtype=np.float64)
    ang = 2 * np.pi * k[:, None] * t[None, :] / nn
    fd = jnp.asarray(np.concatenate([np.cos(ang), -np.sin(ang)], axis=0), F32).astype(BF16)
    gd = jnp.asarray(np.concatenate([np.cos(ang.T), -np.sin(ang.T)], axis=1) / nn, F32).astype(BF16)
    blk = n_lat // n_ctx
    assert blk * n_ctx == n_lat
    return pl.pallas_call(
        functools.partial(_hyena_ctx_kernel, n_ctx=n_ctx),
        grid=(bsz,),
        in_specs=[
            pl.BlockSpec((1, n_ctx, 3 * HY_W), lambda b: (b, blk, 0)),
            pl.BlockSpec((2 * HY_ORDER, n_ctx, HY_W), lambda b: (0, 0, 0)),
            pl.BlockSpec((HY_ORDER, 1, HY_W), lambda b: (0, 0, 0)),
            pl.BlockSpec((2 * nn, n_ctx), lambda b: (0, 0)),
            pl.BlockSpec((n_ctx, 2 * nn), lambda b: (0, 0)),
        ],
        out_specs=pl.BlockSpec((1, n_ctx, HY_W), lambda b: (b, 0, 0)),
        out_shape=jax.ShapeDtypeStruct((bsz, n_ctx, HY_W), F32),
        compiler_params=_params("arbitrary"),
        name="hyena_ctx",
    )(hy, filt, bias, fd, gd)


def _hyena_latent(hy, filt, bias, n_lat, tables):
    kspec = _hy_kspec(filt, tables)
    bias3 = bias.reshape(HY_ORDER, 1, HY_W)
    nb = HY_W // LANES
    y1 = _hy_conv(hy, 0, hy, nb, kspec, 0, bias3, tables, n_lat)
    return _hy_conv(y1, 0, hy, 2 * nb, kspec, 1, bias3, tables, n_lat)


def _outproj_kernel(xl_ref, xc_ref, attl_ref, attc_ref, ssd_ref, hyl_ref, hyc_ref, gt_ref, wa_ref, ws_ref, wh_ref,
                    o_ref, *, n_lat_tiles):
    x = _stream_tile(xl_ref, xc_ref, n_lat_tiles)
    att = _stream_tile(attl_ref, attc_ref, n_lat_tiles)
    hy = _stream_tile(hyl_ref, hyc_ref, n_lat_tiles).astype(BF16)
    mix = _dot(att, wa_ref[...]) + _dot(ssd_ref[0], ws_ref[...]) + _dot(hy, wh_ref[...])
    o_ref[0] = x + gt_ref[0] * mix


def _outproj(x_lat, x_ctx, ctx_blk, att_l, att_c, ssd, hy_l, hy_c, mods, w_out, n_lat_tiles, n_tiles):
    bsz, _, d = x_lat.shape
    tm = TOKEN_TILE

    def mod_row(j, b):
        return jnp.where(j >= n_lat_tiles, bsz, b)

    tok = lambda w: pl.BlockSpec((1, tm, w), lambda j, b: (b, j, 0))
    return pl.pallas_call(
        functools.partial(_outproj_kernel, n_lat_tiles=n_lat_tiles),
        grid=(n_tiles, bsz),
        in_specs=[
            *_stream_specs(n_lat_tiles, ctx_blk, d),
            *_stream_specs(n_lat_tiles, 0, ATT_W),
            tok(SSD_W),
            *_stream_specs(n_lat_tiles, 0, HY_W),
            pl.BlockSpec((1, 1, d), lambda j, b: (mod_row(j, b), 0, 2)),
            pl.BlockSpec((ATT_W, d), lambda j, b: (0, 0)),
            pl.BlockSpec((SSD_W, d), lambda j, b: (0, 0)),
            pl.BlockSpec((HY_W, d), lambda j, b: (0, 0)),
        ],
        out_specs=tok(d),
        out_shape=jax.ShapeDtypeStruct((bsz, n_tiles * tm, d), F32),
        compiler_params=_params("arbitrary", "arbitrary"),
        name="outproj",
    )(x_lat, x_ctx, att_l, att_c, ssd, hy_l, hy_c, mods,
      w_out[:ATT_W], w_out[ATT_W:ATT_W + SSD_W], w_out[ATT_W + SSD_W:])


def _route(logits_t, bias_col):
    scores = jax.nn.sigmoid(logits_t)
    sel = scores + bias_col
    neg_inf = jnp.float32(-jnp.inf)
    rows = [sel[e:e + 1, :] for e in range(N_EXPERTS)]
    grp = []
    for g in range(N_GROUPS):
        r = rows[g * EXPERTS_PER_GROUP:(g + 1) * EXPERTS_PER_GROUP]
        top = functools.reduce(jnp.maximum, r)
        taken = None
        rest = []
        for ri in r:
            is_top = (ri == top) if taken is None else (ri == top) & jnp.logical_not(taken)
            rest.append(jnp.where(is_top, neg_inf, ri))
            taken = is_top if taken is None else taken | is_top
        grp.append(top + functools.reduce(jnp.maximum, rest))
    best = jnp.zeros(grp[0].shape, jnp.int32)
    cur = grp[0]
    for g in range(1, N_GROUPS):
        upd = grp[g] > cur
        best = jnp.where(upd, g, best)
        cur = jnp.where(upd, grp[g], cur)
    picked = []
    for e in range(N_EXPERTS):
        g, i = divmod(e, EXPERTS_PER_GROUP)
        rank = jnp.zeros(best.shape, jnp.int32)
        for j in range(EXPERTS_PER_GROUP):
            if j == i:
                continue
            other = rows[g * EXPERTS_PER_GROUP + j]
            ahead = (other > rows[e]) | ((other == rows[e]) & (j < i))
            rank = rank + ahead.astype(jnp.int32)
        keep = (best == g) & (rank < 2)
        picked.append(jnp.where(keep, scores[e:e + 1, :], 0.0))
    total = functools.reduce(lambda u, w: u + w, picked)
    return jnp.concatenate(picked, axis=0) / total, best


def _ffn_input(x_ref, sh_ref, sc_ref, g_ref):
    x = x_ref[...].reshape(MOE_TILE, D_MODEL)
    ms = jnp.mean(x * x, axis=-1, keepdims=True)
    t = x * lax.rsqrt(ms + EPS) * g_ref[...]
    return x, t * (1.0 + sc_ref[0]) + sh_ref[0]


def _route_kernel(x_ref, sh_ref, sc_ref, g_ref, wr_ref, rb_ref, rt_ref, srow_ref, tbl_ref):
    tm = MOE_TILE
    _, t = _ffn_input(x_ref, sh_ref, sc_ref, g_ref)
    logits = _dot_x3(t, wr_ref[...])
    comb_t, best = _route(logits.T[:N_EXPERTS, :], rb_ref[...])

    member = [jnp.where(best == g, 1.0, 0.0) for g in range(N_GROUPS)]
    comb4 = functools.reduce(
        lambda u, w: u + w,
        [member[g] * comb_t[g * EXPERTS_PER_GROUP:(g + 1) * EXPERTS_PER_GROUP, :] for g in range(N_GROUPS)])
    masks = jnp.concatenate(member + [jnp.zeros((8 - N_GROUPS, tm), F32)], axis=0)
    earlier = jnp.where(lax.broadcasted_iota(jnp.int32, (tm, tm), 0) < lax.broadcasted_iota(jnp.int32, (tm, tm), 1),
                        1.0, 0.0).astype(BF16)
    ranks = _dot(masks.astype(BF16), earlier)
    cnt = jnp.sum(masks, axis=1, keepdims=True)
    padded = jnp.ceil(cnt * (1.0 / MOE_CHUNK)) * MOE_CHUNK
    lane = lax.broadcasted_iota(jnp.int32, (1, LANES), 1).astype(F32) * MOE_CHUNK
    start = jnp.zeros((1, 1), F32)
    slot = jnp.zeros((1, tm), F32)
    gid = jnp.zeros((1, LANES), jnp.int32)
    for g in range(N_GROUPS):
        slot = slot + member[g] * (start + ranks[g:g + 1, :])
        start = start + padded[g:g + 1, :]
        gid = gid + jnp.where(lane >= start, 1, 0)
    tbl_ref[0] = gid
    srow_ref[0] = slot.astype(jnp.int32)
    rt_t = jnp.concatenate([comb4, slot, jnp.zeros((LANES - EXPERTS_PER_GROUP - 1, tm), F32)], axis=0)
    rt_ref[...] = rt_t.T


def _experts_kernel(tbl_ref, x_ref, sh_ref, sc_ref, gt_ref, g_ref, rt_ref, srow_ref, wg_ref, wu_ref, wd_ref, gf_ref,
                    *rest, final, tile_of, n_grid):
    o_ref, xp_ref, yp_ref = rest[-3:]
    tm, ch = MOE_TILE, MOE_CHUNK
    n_slots = MOE_CHUNKS * ch
    tile = tile_of(*[pl.program_id(a) for a in range(n_grid)])
    x, t = _ffn_input(x_ref, sh_ref, sc_ref, g_ref)
    rt = rt_ref[...]
    slot_col = rt[:, EXPERTS_PER_GROUP:EXPERTS_PER_GROUP + 1].astype(jnp.int32)
    gather = jnp.where(lax.broadcasted_iota(jnp.int32, (n_slots, tm), 0) == srow_ref[0], 1.0, 0.0).astype(BF16)
    scatter = jnp.where(lax.broadcasted_iota(jnp.int32, (tm, n_slots), 1) == slot_col, 1.0, 0.0).astype(BF16)
    xp_ref[...] = _dot(gather, t.astype(BF16)).astype(BF16)
    w_slot = _dot3_right(gather, rt)

    def run_chunk(rows, grp):
        xc = xp_ref[rows, :]
        e0 = grp * EXPERTS_PER_GROUP
        gu_next = _dot(xc, wg_ref[e0]), _dot(xc, wu_ref[e0])
        acc = jnp.zeros((ch, D_MODEL), F32)
        for j in range(EXPERTS_PER_GROUP):
            gate, up = gu_next
            if j + 1 < EXPERTS_PER_GROUP:
                gu_next = _dot(xc, wg_ref[e0 + j + 1]), _dot(xc, wu_ref[e0 + j + 1])
            hid = _silu(gate) * up * w_slot[rows, j:j + 1]
            acc = acc + _dot(hid.astype(BF16), wd_ref[e0 + j])
        yp_ref[rows, :] = acc.astype(BF16)

    for c in range(MOE_CHUNKS):
        rows = slice(c * ch, (c + 1) * ch)
        grp = tbl_ref[tile, c]
        pl.when(grp < N_GROUPS)(functools.partial(run_chunk, rows, grp))

        @pl.when(grp >= N_GROUPS)
        def _():
            yp_ref[rows, :] = jnp.zeros((ch, D_MODEL), BF16)

    y = x + gt_ref[0] * _dot(scatter, yp_ref[...])
    if final:
        y = y * lax.rsqrt(jnp.mean(y * y, axis=-1, keepdims=True) + EPS) * gf_ref[...]
    o_ref[...] = y.reshape(o_ref.shape)


def _moe_tiles(x, mods, mod_row, g_ffn, w_router, router_bias, wgu, wd, g_final, final, out_rows,
               grid, x_block, x_index, out_index, tile_of, n_tiles):
    bsz, _, d = x.shape
    tm = MOE_TILE
    ng = len(grid)
    const = lambda shape: pl.BlockSpec(shape, lambda *a: tuple(0 for _ in shape))
    mod = lambda col: pl.BlockSpec((1, 1, d), lambda *a: (mod_row(*a[:ng]), 0, col))
    xspec = pl.BlockSpec(x_block, lambda *a: x_index(*a[:ng]))
    rt, srow, tbl = pl.pallas_call(
        _route_kernel,
        grid=grid,
        in_specs=[xspec, mod(3), mod(4), const((1, d)), const((d, LANES)), const((N_EXPERTS, 1))],
        out_specs=[
            pl.BlockSpec((tm, LANES), lambda *a: (tile_of(*a), 0)),
            pl.BlockSpec((1, 1, tm), lambda *a: (tile_of(*a), 0, 0)),
            pl.BlockSpec((1, 1, LANES), lambda *a: (tile_of(*a), 0, 0)),
        ],
        out_shape=[
            jax.ShapeDtypeStruct((n_tiles * tm, LANES), F32),
            jax.ShapeDtypeStruct((n_tiles, 1, tm), jnp.int32),
            jax.ShapeDtypeStruct((n_tiles, 1, LANES), jnp.int32),
        ],
        compiler_params=_params(*["arbitrary"] * ng),
        name="moe_route",
    )(x, mods, mods, g_ffn, w_router, router_bias)

    resident = lambda shape: pl.BlockSpec(shape, lambda *a: tuple(0 for _ in shape), pipeline_mode=pl.Buffered(1))
    in_specs = [
        xspec, mod(3), mod(4), mod(5), const((1, d)),
        pl.BlockSpec((tm, LANES), lambda *a: (tile_of(*a[:ng]), 0)),
        pl.BlockSpec((1, 1, tm), lambda *a: (tile_of(*a[:ng]), 0, 0)),
        resident((N_EXPERTS, d, D_FF)),
        resident((N_EXPERTS, d, D_FF)),
        resident((N_EXPERTS, D_FF, d)),
        const((1, d)),
    ]
    args = [tbl.reshape(n_tiles, LANES), x, mods, mods, mods, g_ffn, rt, srow, *wgu, wd, g_final]
    return pl.pallas_call(
        functools.partial(_experts_kernel, final=final, tile_of=tile_of, n_grid=ng),
        grid_spec=pltpu.PrefetchScalarGridSpec(
            num_scalar_prefetch=1,
            grid=grid,
            in_specs=in_specs,
            out_specs=pl.BlockSpec(x_block, lambda *a: out_index(*a[:ng])),
            scratch_shapes=[pltpu.VMEM((MOE_CHUNKS * MOE_CHUNK, d), BF16), pltpu.VMEM((MOE_CHUNKS * MOE_CHUNK, d), BF16)],
        ),
        out_shape=jax.ShapeDtypeStruct((bsz, out_rows, d), F32),
        compiler_params=_params(*["arbitrary"] * ng),
        name="moe_experts",
    )(*args)


def _moe(x, mods, g_ffn, w_router, router_bias, wgu, wd, g_final, n_lat, n_ctx, with_ctx, final):
    bsz, _, d = x.shape
    tm = MOE_TILE
    per_b = n_lat // tm
    common = (g_ffn, w_router, router_bias, wgu, wd, g_final, final)
    lat_index = lambda b, j: (b, j, 0)
    out_lat = _moe_tiles(x, mods, lambda b, j: b, *common, n_lat,
                         (bsz, per_b), (1, tm, d), lat_index, lat_index, lambda b, j: b * per_b + j, bsz * per_b)
    if not with_ctx:
        return out_lat, None
    nb = tm // n_ctx
    out_ctx = _moe_tiles(x, mods, lambda i: bsz, *common, n_ctx,
                         (bsz // nb,), (nb, n_ctx, d), lambda i: (i, n_lat // n_ctx, 0), lambda i: (i, 0, 0),
                         lambda i: i, bsz // nb)
    return out_lat, out_ctx


def _block_ones(width, block):
    idx = np.arange(width) // block
    return jnp.asarray(idx[:, None] == idx[None, :], F32).astype(BF16)


def _rope_tables(n_lat, n_ctx):
    rows = n_lat // GRID_W
    row = np.repeat(np.arange(rows), GRID_W).astype(np.float64)
    col = np.tile(np.arange(GRID_W), rows).astype(np.float64)
    inv = ROPE_THETA ** (-np.arange(0, ROPE_AXIS_DIM, 2, dtype=np.float64) / ROPE_AXIS_DIM)
    ang = np.concatenate([row[:, None] * inv, col[:, None] * inv], axis=-1)
    ang = np.concatenate([ang, np.zeros((n_ctx, ang.shape[1]))], axis=0)
    cos = np.concatenate([np.cos(ang), np.cos(ang)], axis=-1)
    sin = np.concatenate([-np.sin(ang), np.sin(ang)], axis=-1)
    scale = Q_SCALE
    cs = np.tile(cos, (1, ATT_HEADS)) * scale
    sn = np.tile(sin, (1, ATT_HEADS)) * scale
    return jnp.asarray(cs, F32), jnp.asarray(sn, F32)


def kernel(x, c, ctx, c_ctx, w_mod, b_mod, g_mix, g_ffn, w_in, q_norm, k_norm, ssd_conv_w, ssd_conv_b,
           ssd_dt_bias, ssd_a_log, ssd_d, ssd_norm, hy_conv_w, hy_conv_b, hy_w1, hy_b1, hy_freq, hy_w2, hy_b2,
           hy_w3, hy_bias, w_out, w_router, router_bias, w_gate, w_up, w_down, g_final):
    bsz, n_lat, d = x.shape
    n_ctx = ctx.shape[1]
    depth = w_mod.shape[0]
    t = n_lat + n_ctx
    tm = TOKEN_TILE
    n_lat_tiles = n_lat // tm
    n_tiles = t // tm
    assert n_ctx == tm and n_lat % (HY_N2 * 8) == 0 and bsz < MOD_ROWS

    cvec = jnp.concatenate([c, c_ctx[None], jnp.zeros((MOD_ROWS - bsz - 1, d), F32)], axis=0)
    mods_all = _adaln(cvec, w_mod, b_mod)

    cs, sn = _rope_tables(n_lat, n_ctx)
    bd_head = _block_ones(ATT_W, HEAD_DIM)
    bd_ssd = _block_ones(SSD_W, SSD_W // SSD_GROUPS)
    tables = _dft_tables(n_lat)
    pad_row = lambda v: jnp.pad(v.reshape(1, -1), ((0, 0), (0, LANES - v.size)))

    x_lat, x_ctx, ctx_blk = x, ctx, 0
    for i in range(depth):
        last = i == depth - 1
        mods = mods_all[i].reshape(MOD_ROWS, 1, 6 * d)
        q, kt, v, z, xbc, dt_raw, hy = _inproj(
            x_lat, x_ctx, ctx_blk, mods, g_mix[i].reshape(1, d), w_in, i,
            jnp.tile(q_norm[i], ATT_HEADS).reshape(1, ATT_W), jnp.tile(k_norm[i], ATT_KV_HEADS).reshape(1, KV_W),
            cs, sn, bd_head, jnp.concatenate([ssd_conv_w[i], hy_conv_w[i]], axis=-1),
            jnp.concatenate([ssd_conv_b[i], hy_conv_b[i]]).reshape(1, -1), n_lat_tiles)

        att = _attention(q, kt, v, 0, n_lat, 0, t, ATT_TILE)
        att_c = _attention(q, kt, v, n_lat, n_ctx, n_lat, n_ctx, n_ctx) if not last else att

        ssd = _ssd(xbc, z, dt_raw, pad_row(ssd_dt_bias[i]), pad_row(ssd_a_log[i]),
                   jnp.repeat(ssd_d[i], SSD_HEAD_DIM).reshape(1, SSD_W), ssd_norm[i].reshape(1, SSD_W),
                   bd_ssd, n_lat)

        hyp = (hy_w1[i], hy_b1[i], hy_freq[i], hy_w2[i], hy_b2[i], hy_w3[i])
        hy_l = _hyena_latent(hy, _hyena_filters(n_lat, *hyp), hy_bias[i], n_lat, tables)
        hy_c = hy_l if last else _hyena_ctx(hy, _hyena_filters(n_ctx, *hyp),
                                            hy_bias[i].reshape(HY_ORDER, 1, HY_W), n_lat, n_ctx)

        live_tiles = n_lat_tiles if last else n_tiles
        xs = _outproj(x_lat, x_ctx, ctx_blk, att, att_c, ssd, hy_l, hy_c, mods, w_out[i].astype(BF16),
                      n_lat_tiles, live_tiles)

        wgu = (w_gate[i].astype(BF16), w_up[i].astype(BF16))
        x_lat, x_ctx = _moe(xs, mods, g_ffn[i].reshape(1, d), jnp.pad(w_router, ((0, 0), (0, LANES - N_EXPERTS))),
                            router_bias.reshape(N_EXPERTS, 1), wgu, w_down[i].astype(BF16), g_final.reshape(1, d),
                            n_lat, n_ctx, not last, last)
    return x_lat
```

```python
import functools
import math

import jax
import jax.numpy as jnp
import numpy as np
from jax import lax
from jax.experimental import pallas as pl
from jax.experimental.pallas import tpu as pltpu

F32 = jnp.float32
BF16 = jnp.bfloat16

D_MODEL = 1024
GRID_W = 64
EPS = 1e-6

ATT_HEADS = 6
ATT_KV_HEADS = 2
HEAD_DIM = 64
ATT_W = ATT_HEADS * HEAD_DIM
KV_W = ATT_KV_HEADS * HEAD_DIM
ROPE_AXIS_DIM = HEAD_DIM // 2
ROPE_THETA = 10000.0
Q_SCALE = HEAD_DIM ** -0.5 * math.log2(math.e)

SSD_HEADS = 6
SSD_HEAD_DIM = 64
SSD_W = SSD_HEADS * SSD_HEAD_DIM
SSD_GROUPS = 2
SSD_STATE = 64
SSD_CHUNK = 128
SSD_CONV_CH = SSD_W + 2 * SSD_GROUPS * SSD_STATE
SSD_HEADS_PER_GROUP = SSD_HEADS // SSD_GROUPS
SSD_BLOCK = 2

HY_W = 256
HY_ORDER = 2
HY_BANDS = 16
HY_POS_DIM = 1 + 2 * HY_BANDS
HY_FILTER_HID = 64
HY_FAST_DECAY = 0.3
HY_SLOW_DECAY = 1.5
HY_TARGET = 1e-2
HY_N2 = 64
HY_UNROLL = 8

MIX_W = ATT_W + SSD_W + HY_W
N_EXPERTS = 16
N_GROUPS = 4
EXPERTS_PER_GROUP = N_EXPERTS // N_GROUPS
D_FF = 256

LANES = 128
SUBLANES = 8
TOKEN_TILE = 256
ATT_TILE = 1024
ATT_SUB = 512
MOE_TILE = 512
MOE_CHUNK = 128
MOE_CHUNKS = MOE_TILE // MOE_CHUNK + N_GROUPS - 1
MOD_ROWS = 8
VMEM_LIMIT = 56 * 1024 * 1024

COL_Q = 0
COL_K = COL_Q + ATT_W
COL_V = COL_K + KV_W
COL_XBC = COL_V + KV_W
COL_HY = COL_XBC + SSD_CONV_CH
COL_Z = COL_HY + 3 * HY_W
COL_DT = COL_Z + SSD_W
COL_END = COL_DT + LANES


def _params(*sem):
    return pltpu.CompilerParams(dimension_semantics=sem, vmem_limit_bytes=VMEM_LIMIT)


def _silu(x):
    return x * jax.nn.sigmoid(x)


def _softplus(x):
    return jnp.maximum(x, 0.0) + jnp.log1p(jnp.exp(-jnp.abs(x)))


def _split3(x):
    hi = x.astype(BF16)
    r1 = x - hi.astype(F32)
    mid = r1.astype(BF16)
    lo = (r1 - mid.astype(F32)).astype(BF16)
    return hi, mid, lo


def _dot(a, b):
    return jnp.dot(a, b, preferred_element_type=F32)


def _dot_x3(a, b):
    a_hi, a_mid, _ = _split3(a)
    b_hi, b_mid, _ = _split3(b)
    return _dot(a_hi, b_hi) + (_dot(a_hi, b_mid) + _dot(a_mid, b_hi))


def _dot3_right(m_bf16, x):
    hi, mid, lo = _split3(x)
    return _dot(m_bf16, hi) + _dot(m_bf16, mid) + _dot(m_bf16, lo)


def _dot3_left(x, m_bf16):
    hi, mid, lo = _split3(x)
    return _dot(hi, m_bf16) + _dot(mid, m_bf16) + _dot(lo, m_bf16)


def _adaln_kernel(c_ref, w_ref, b_ref, o_ref):
    s = _silu(c_ref[...]).astype(BF16)
    o_ref[0] = _dot(s, w_ref[0].astype(BF16)) + b_ref[0]


def _adaln(cvec, w_mod, b_mod):
    depth, d, n = w_mod.shape
    bn = n // 4
    return pl.pallas_call(
        _adaln_kernel,
        grid=(depth, n // bn),
        in_specs=[
            pl.BlockSpec((MOD_ROWS, d), lambda i, j: (0, 0)),
            pl.BlockSpec((1, d, bn), lambda i, j: (i, 0, j)),
            pl.BlockSpec((1, 1, bn), lambda i, j: (i, 0, j)),
        ],
        out_specs=pl.BlockSpec((1, MOD_ROWS, bn), lambda i, j: (i, 0, j)),
        out_shape=jax.ShapeDtypeStruct((depth, MOD_ROWS, n), F32),
        compiler_params=_params("arbitrary", "arbitrary"),
        name="adaln",
    )(cvec, w_mod, b_mod.reshape(depth, 1, n))


def _head_rope(xn, cs, sn):
    width = xn.shape[-1]
    lane = lax.broadcasted_iota(jnp.int32, xn.shape, 1)
    first_half = (lane % HEAD_DIM) < (HEAD_DIM // 2)
    partner = jnp.where(first_half,
                        pltpu.roll(xn, width - HEAD_DIM // 2, 1),
                        pltpu.roll(xn, HEAD_DIM // 2, 1))
    return xn * cs + partner * sn


def _stream_specs(n_lat_tiles, ctx_blk, d):
    lat = pl.BlockSpec((1, TOKEN_TILE, d), lambda j, b: (b, jnp.minimum(j, n_lat_tiles - 1), 0))
    ctx = pl.BlockSpec((1, TOKEN_TILE, d), lambda j, b: (jnp.where(j >= n_lat_tiles, b, 0), ctx_blk, 0))
    return lat, ctx


def _stream_tile(xl_ref, xc_ref, n_lat_tiles):
    return jnp.where(pl.program_id(0) >= n_lat_tiles, xc_ref[0], xl_ref[0])


def _pack_w_in(w_ref, wb_ref):
    src_z = ATT_W + 2 * KV_W
    src_xbc = src_z + SSD_W
    src_dt = src_xbc + SSD_CONV_CH
    n_dt = 2 * SSD_HEADS
    wb_ref[:, COL_Q:COL_XBC] = w_ref[0, :, :src_z].astype(BF16)
    wb_ref[:, COL_XBC:COL_HY] = w_ref[0, :, src_xbc:src_dt].astype(BF16)
    tail = w_ref[0, :, src_dt:]
    wb_ref[:, COL_HY:COL_Z] = tail[:, n_dt:].astype(BF16)
    wb_ref[:, COL_Z:COL_DT] = w_ref[0, :, src_z:src_xbc].astype(BF16)
    head = tail[:, :LANES]
    lane = lax.broadcasted_iota(jnp.int32, head.shape, 1)
    wb_ref[:, COL_DT:COL_END] = jnp.where(lane < n_dt, head, 0.0).astype(BF16)


def _inproj_kernel(xl_ref, xc_ref, xp_ref, xn_ref, sh_ref, sc_ref, g_ref, w_ref, gq_ref, gk_ref, cs_ref, sn_ref,
                   bd_ref, cw_ref, cb_ref, qt_ref, k_ref, vt_ref, z_ref, xbc_ref, dt_ref, hy_ref, wb_ref,
                   *, n_lat_tiles):
    j = pl.program_id(0)

    @pl.when((j == 0) & (pl.program_id(1) == 0))
    def _():
        _pack_w_in(w_ref, wb_ref)

    tm, halo = TOKEN_TILE, SUBLANES
    x = jnp.concatenate([xp_ref[0], _stream_tile(xl_ref, xc_ref, n_lat_tiles), xn_ref[0]], axis=0)
    ms = jnp.mean(x * x, axis=-1, keepdims=True)
    h = x * lax.rsqrt(ms + EPS) * g_ref[...]
    h = h * (1.0 + sc_ref[0]) + sh_ref[0]
    hb = h[halo:halo + tm].astype(BF16)
    u = _dot(h.astype(BF16), wb_ref[:, COL_XBC:COL_Z])
    qkv = _dot(hb, wb_ref[:, COL_Q:COL_XBC])
    bd = bd_ref[...]
    q, k = qkv[:, COL_Q:COL_K], qkv[:, COL_K:COL_V]
    ms_q = _dot((q * q).astype(BF16), bd) * (1.0 / HEAD_DIM)
    ms_k = _dot((k * k).astype(BF16), bd[:KV_W, :KV_W]) * (1.0 / HEAD_DIM)
    zdt = _dot(hb, wb_ref[:, COL_Z:COL_END])

    cs = cs_ref[...]
    sn = sn_ref[...]
    qt_ref[0] = _head_rope(q * lax.rsqrt(ms_q + EPS) * gq_ref[...], cs, sn).T.astype(qt_ref.dtype)
    k_ref[0] = _head_rope(k * lax.rsqrt(ms_k + EPS) * gk_ref[...],
                          cs[:, :KV_W] * (1.0 / Q_SCALE), sn[:, :KV_W] * (1.0 / Q_SCALE)).astype(k_ref.dtype)
    vt_ref[0] = qkv[:, COL_V:COL_XBC].T.astype(vt_ref.dtype)

    row = lax.broadcasted_iota(jnp.int32, (u.shape[0], 1), 0)
    has_prev = (j >= 1) & (j < n_lat_tiles)
    has_next = j < n_lat_tiles - 1
    u = jnp.where(((row >= halo) | has_prev) & ((row < halo + tm) | has_next), u, 0.0)
    cw = cw_ref[...]
    own = slice(halo, halo + tm)
    y = (pltpu.roll(u, 1, 0)[own] * cw[0:1] + u[own] * cw[1:2] + pltpu.roll(u, u.shape[0] - 1, 0)[own] * cw[2:3]
         + cb_ref[...])
    xbc_ref[0] = _silu(y[:, :COL_HY - COL_XBC])
    hy_ref[0] = y[:, COL_HY - COL_XBC:]
    z_ref[0] = zdt[:, :COL_DT - COL_Z]
    dt_ref[0] = zdt[:, COL_DT - COL_Z:]


def _inproj(x_lat, x_ctx, ctx_blk, mods, g_mix, w_in, layer, gq, gk, cs, sn, bd, conv_w, conv_b, n_lat_tiles):
    bsz, _, d = x_lat.shape
    tm = TOKEN_TILE
    nt = n_lat_tiles + 1
    t = nt * tm
    per_tile = tm // SUBLANES
    n_conv = COL_Z - COL_XBC

    def mod_row(j, b):
        return jnp.where(j >= n_lat_tiles, bsz, b)

    lat_j = lambda j: jnp.minimum(j, n_lat_tiles - 1)
    prev_spec = pl.BlockSpec((1, SUBLANES, d), lambda j, b: (b, jnp.maximum(lat_j(j) * per_tile - 1, 0), 0))
    next_spec = pl.BlockSpec((1, SUBLANES, d),
                             lambda j, b: (b, jnp.minimum((lat_j(j) + 1) * per_tile, n_lat_tiles * per_tile - 1), 0))

    tok = lambda w: pl.BlockSpec((1, tm, w), lambda j, b: (b, j, 0))
    const = lambda shape: pl.BlockSpec(shape, lambda j, b: tuple(0 for _ in shape))
    outs = pl.pallas_call(
        functools.partial(_inproj_kernel, n_lat_tiles=n_lat_tiles),
        grid=(nt, bsz),
        in_specs=[
            *_stream_specs(n_lat_tiles, ctx_blk, d),
            prev_spec, next_spec,
            pl.BlockSpec((1, 1, d), lambda j, b: (mod_row(j, b), 0, 0)),
            pl.BlockSpec((1, 1, d), lambda j, b: (mod_row(j, b), 0, 1)),
            const((1, d)),
            pl.BlockSpec((1,) + w_in.shape[1:], lambda j, b: (layer, 0, 0), pipeline_mode=pl.Buffered(1)),
            const((1, ATT_W)),
            const((1, KV_W)),
            pl.BlockSpec((tm, ATT_W), lambda j, b: (j, 0)),
            pl.BlockSpec((tm, ATT_W), lambda j, b: (j, 0)),
            const((ATT_W, ATT_W)),
            const((3, n_conv)),
            const((1, n_conv)),
        ],
        out_specs=[
            pl.BlockSpec((1, ATT_W, tm), lambda j, b: (b, 0, j)),
            tok(KV_W),
            pl.BlockSpec((1, KV_W, tm), lambda j, b: (b, 0, j)),
            tok(SSD_W),
            tok(SSD_CONV_CH),
            tok(LANES),
            tok(3 * HY_W),
        ],
        out_shape=[
            jax.ShapeDtypeStruct((bsz, ATT_W, t), BF16),
            jax.ShapeDtypeStruct((bsz, t, KV_W), BF16),
            jax.ShapeDtypeStruct((bsz, KV_W, t), BF16),
            jax.ShapeDtypeStruct((bsz, t, SSD_W), F32),
            jax.ShapeDtypeStruct((bsz, t, SSD_CONV_CH), F32),
            jax.ShapeDtypeStruct((bsz, t, LANES), F32),
            jax.ShapeDtypeStruct((bsz, t, 3 * HY_W), F32),
        ],
        scratch_shapes=[pltpu.VMEM((d, COL_END), BF16)],
        compiler_params=_params("arbitrary", "arbitrary"),
        name="inproj",
    )(x_lat, x_ctx, x_lat, x_lat, mods, mods, g_mix, w_in, gq, gk, cs, sn, bd, conv_w, conv_b)
    return outs


def _attn_kernel(qt_ref, k_ref, vt_ref, o_ref, st_ref, pt_ref):
    k = k_ref[0]
    vt = vt_ref[0]
    rep = ATT_HEADS // ATT_KV_HEADS
    tq = qt_ref.shape[2]
    sub = min(tq, ATT_SUB)
    ones = jnp.ones((2 * SUBLANES, vt.shape[1]), vt.dtype)
    vtg = [jnp.concatenate([vt[g * HEAD_DIM:(g + 1) * HEAD_DIM, :], ones], axis=0) for g in range(ATT_KV_HEADS)]
    units = [(c0, hd) for c0 in range(0, tq, sub) for hd in range(ATT_HEADS)]

    def scores(u):
        c0, hd = units[u]
        qh = qt_ref[0, hd * HEAD_DIM:(hd + 1) * HEAD_DIM, c0:c0 + sub]
        zero = jnp.zeros_like(qh)
        w = jnp.concatenate([qh, zero] if hd < rep else [zero, qh], axis=0)
        st = _dot(k, w)
        st_ref[u % 2] = st
        return jnp.max(st, axis=0, keepdims=True)

    m_next = scores(0)
    outs = []
    for u, (c0, hd) in enumerate(units):
        m = m_next
        if u + 1 < len(units):
            m_next = scores(u + 1)
        pt_ref[u % 2] = jnp.exp2(st_ref[u % 2] - m).astype(BF16)
        ot = _dot(vtg[hd // rep], pt_ref[u % 2])
        outs.append(ot[:HEAD_DIM] / ot[HEAD_DIM:HEAD_DIM + 1])
        if hd == ATT_HEADS - 1:
            o_ref[0, c0:c0 + sub, :] = jnp.concatenate(outs, axis=0).T.astype(o_ref.dtype)
            outs = []


def _attention(qt, k, vt, q_row0, n_q, k_row0, n_k, tq):
    bsz = qt.shape[0]
    kblk = k_row0 // n_k
    q_tile0 = q_row0 // tq
    n_q_tiles = n_q // tq
    assert kblk * n_k == k_row0 and q_tile0 * tq == q_row0 and n_q_tiles * tq == n_q
    assert ATT_KV_HEADS == 2
    return pl.pallas_call(
        _attn_kernel,
        grid=(bsz, n_q_tiles),
        in_specs=[
            pl.BlockSpec((1, ATT_W, tq), lambda b, j: (b, 0, q_tile0 + j)),
            pl.BlockSpec((1, n_k, KV_W), lambda b, j: (b, kblk, 0)),
            pl.BlockSpec((1, KV_W, n_k), lambda b, j: (b, 0, kblk)),
        ],
        out_specs=pl.BlockSpec((1, tq, ATT_W), lambda b, j: (b, j, 0)),
        out_shape=jax.ShapeDtypeStruct((bsz, n_q_tiles * tq, ATT_W), BF16),
        scratch_shapes=[pltpu.VMEM((2, n_k, min(tq, ATT_SUB)), F32), pltpu.VMEM((2, n_k, min(tq, ATT_SUB)), BF16)],
        compiler_params=_params("arbitrary", "arbitrary"),
        name="attention",
    )(qt, k, vt)


def _ssd_kernel(xbc_ref, z_ref, dt_ref, dtb_ref, alog_ref, dsk_ref, nw_ref, bd_ref,
                o_ref, hf_ref, hb_ref, hbe_ref, *, n_lat_blocks):
    q = SSD_CHUNK
    nh = SSD_HEADS
    phase = pl.program_id(1)
    step = pl.program_id(2)
    block = jnp.where(phase == 0, n_lat_blocks - step, jnp.where(step < 1, n_lat_blocks, step - 1))

    lane = lax.broadcasted_iota(jnp.int32, (1, LANES), 1)
    a_row = jnp.where(lane < 2 * nh, -jnp.exp(alog_ref[...]), 0.0)
    tt = lax.broadcasted_iota(jnp.int32, (q, q), 0)
    ss = lax.broadcasted_iota(jnp.int32, (q, q), 1)
    lower = (ss <= tt)
    upper = (ss >= tt)
    lmat = jnp.where(lower, 1.0, 0.0).astype(BF16)
    umat = jnp.where(upper, 1.0, 0.0).astype(BF16)

    @pl.when(step == 0)
    def _():
        hf_ref[...] = jnp.zeros_like(hf_ref)
        hb_ref[...] = jnp.zeros_like(hb_ref)

    def load_chunk(ci):
        rows = slice(ci * q, (ci + 1) * q)
        xbc = xbc_ref[0, rows, :]
        dt = _softplus(dt_ref[0, rows, :] + dtb_ref[...])
        return rows, xbc, dt, dt * a_row

    def heads_x(x):
        return [x[:, hd * SSD_HEAD_DIM:(hd + 1) * SSD_HEAD_DIM].astype(BF16) for hd in range(nh)]

    def backward_prepare(ci):
        _, xbc, dt, a = load_chunk(ci)
        bt = xbc[:, SSD_W:SSD_W + SSD_GROUPS * SSD_STATE].T
        dt_t = dt.T
        suf_t = _dot3_left(a.T, lmat)
        terms = []
        for hd in range(nh):
            g = hd // SSD_HEADS_PER_GROUP
            row_b = suf_t[nh + hd:nh + hd + 1, :]
            total = row_b[:, 0:1]
            w_t = jnp.exp(total - row_b) * dt_t[nh + hd:nh + hd + 1, :]
            terms.append((jnp.exp(total), (bt[g * SSD_STATE:(g + 1) * SSD_STATE, :] * w_t).astype(BF16)))
        return terms, heads_x(xbc[:, :SSD_W])

    def backward_state(ci, prepared):
        terms, xh = prepared
        chunk = block * SSD_BLOCK + ci
        for hd in range(nh):
            decay, bw = terms[hd]
            prev = hb_ref[hd]
            hbe_ref[chunk, hd] = prev
            hb_ref[hd] = prev * decay + _dot(bw, xh[hd])

    def forward_prepare(ci):
        rows, xbc, dt, a = load_chunk(ci)
        x = xbc[:, :SSD_W]
        bt = xbc[:, SSD_W:SSD_W + SSD_GROUPS * SSD_STATE].T
        cmat = xbc[:, SSD_W + SSD_GROUPS * SSD_STATE:]
        a_t = a.T
        pre = _dot3_right(lmat, a)
        pre_t = _dot3_left(a_t, umat)
        cgs = [cmat[:, g * SSD_STATE:(g + 1) * SSD_STATE].astype(BF16) for g in range(SSD_GROUPS)]
        btgs = [bt[g * SSD_STATE:(g + 1) * SSD_STATE, :] for g in range(SSD_GROUPS)]
        cbs = [_dot(cgs[g], btgs[g].astype(BF16)) for g in range(SSD_GROUPS)]
        return dict(rows=rows, x=x, xh=heads_x(x), a=a, a_t=a_t, dt_t=dt.T, pre=pre, pre_t=pre_t,
                    cgs=cgs, btgs=btgs, cbs=cbs)

    def forward_mix(p):
        pre, pre_t, dt_t = p["pre"], p["pre_t"], p["dt_t"]
        suf = pre[q - 1:q, :] - pre + p["a"]
        suf_t = pre_t[:, q - 1:q] - pre_t + p["a_t"]
        neg_inf = jnp.float32(-jnp.inf)
        p["y_in"], p["scale"], p["upd"] = [], [], []
        for hd in range(nh):
            g = hd // SSD_HEADS_PER_GROUP
            colf = pre[:, hd:hd + 1]
            rowf = pre_t[hd:hd + 1, :]
            colb = suf[:, nh + hd:nh + hd + 1]
            rowb = suf_t[nh + hd:nh + hd + 1, :]
            wf = jnp.exp(jnp.where(lower, colf - rowf, neg_inf)) * dt_t[hd:hd + 1, :]
            wb = jnp.exp(jnp.where(upper, colb - rowb, neg_inf)) * dt_t[nh + hd:nh + hd + 1, :]
            p["y_in"].append(_dot((p["cbs"][g] * (wf + wb)).astype(BF16), p["xh"][hd]))
            p["scale"].append((jnp.exp(colf), jnp.exp(colb)))
            total = rowf[:, q - 1:q]
            w_t = jnp.exp(total - rowf) * dt_t[hd:hd + 1, :]
            p["upd"].append((jnp.exp(total), (p["btgs"][g] * w_t).astype(BF16)))

    def forward_state(ci, p):
        chunk = block * SSD_BLOCK + ci
        rows, x = p["rows"], p["x"]
        ys = []
        for hd in range(nh):
            cg = p["cgs"][hd // SSD_HEADS_PER_GROUP]
            hf = hf_ref[hd]
            ef, eb = p["scale"][hd]
            y = p["y_in"][hd] + _dot(cg, hf.astype(BF16)) * ef + _dot(cg, hbe_ref[chunk, hd].astype(BF16)) * eb
            decay, bw = p["upd"][hd]
            hf_ref[hd] = hf * decay + _dot(bw, p["xh"][hd])
            ys.append(y)
        y = jnp.concatenate(ys, axis=-1) + x * dsk_ref[...]
        gz = y * _silu(z_ref[0, rows, :])
        ms = _dot((gz * gz).astype(BF16), bd_ref[...]) * (1.0 / (SSD_W // SSD_GROUPS))
        o_ref[0, rows, :] = (gz * lax.rsqrt(ms + EPS) * nw_ref[...]).astype(o_ref.dtype)

    @pl.when(phase == 0)
    def _():
        prepared = [backward_prepare(ci) for ci in range(SSD_BLOCK)]
        for ci in reversed(range(SSD_BLOCK)):
            backward_state(ci, prepared[ci])

    @pl.when(phase == 1)
    def _():
        prepared = [forward_prepare(ci) for ci in range(SSD_BLOCK)]
        for p in prepared:
            forward_mix(p)
        for ci in range(SSD_BLOCK):
            forward_state(ci, prepared[ci])


def _ssd(xbc, z, dt_raw, dt_bias, a_log, d_skip, norm_w, bd, n_lat):
    bsz, t, _ = z.shape
    rows = SSD_BLOCK * SSD_CHUNK
    n_lat_blocks = n_lat // rows
    n_blocks = t // rows
    assert n_blocks == n_lat_blocks + 1 and n_lat_blocks * rows == n_lat

    def block_of(p, s):
        return jnp.where(p == 0, n_lat_blocks - s, jnp.where(s < 1, n_lat_blocks, s - 1))

    def out_block(p, s):
        return jnp.where(p == 0, n_lat_blocks, block_of(p, s))

    const = lambda shape: pl.BlockSpec(shape, lambda b, p, s: tuple(0 for _ in shape))
    return pl.pallas_call(
        functools.partial(_ssd_kernel, n_lat_blocks=n_lat_blocks),
        grid=(bsz, 2, n_blocks),
        in_specs=[
            pl.BlockSpec((1, rows, SSD_CONV_CH), lambda b, p, s: (b, block_of(p, s), 0)),
            pl.BlockSpec((1, rows, SSD_W), lambda b, p, s: (b, block_of(p, s), 0)),
            pl.BlockSpec((1, rows, LANES), lambda b, p, s: (b, block_of(p, s), 0)),
            const((1, LANES)),
            const((1, LANES)),
            const((1, SSD_W)),
            const((1, SSD_W)),
            const((SSD_W, SSD_W)),
        ],
        out_specs=pl.BlockSpec((1, rows, SSD_W), lambda b, p, s: (b, out_block(p, s), 0)),
        out_shape=jax.ShapeDtypeStruct((bsz, t, SSD_W), BF16),
        scratch_shapes=[
            pltpu.VMEM((SSD_HEADS, SSD_STATE, SSD_HEAD_DIM), F32),
            pltpu.VMEM((SSD_HEADS, SSD_STATE, SSD_HEAD_DIM), F32),
            pltpu.VMEM((n_blocks * SSD_BLOCK, SSD_HEADS, SSD_STATE, SSD_HEAD_DIM), F32),
        ],
        compiler_params=_params("arbitrary", "arbitrary", "arbitrary"),
        name="ssd",
    )(xbc, z, dt_raw, dt_bias, a_log, d_skip, norm_w, bd)


def _hyfilt_kernel(feat_ref, win_ref, w1_ref, b1_ref, fr_ref, w2_ref, b2_ref, w3_ref, o_ref):
    hp = lax.Precision.HIGHEST
    fr = fr_ref[...]
    h1 = jnp.sin(fr * (jnp.dot(feat_ref[...], w1_ref[...], precision=hp, preferred_element_type=F32) + b1_ref[...]))
    h2 = jnp.sin(fr * (jnp.dot(h1, w2_ref[...], precision=hp, preferred_element_type=F32) + b2_ref[...]))
    h = _dot_x3(h2, w3_ref[...])
    win = win_ref[...]
    first_tile = pl.program_id(0) == 0
    row = lax.broadcasted_iota(jnp.int32, win.shape, 0)
    for order in range(HY_ORDER):
        for direction in range(2):
            c0 = (direction * HY_ORDER + order) * HY_W
            f = h[:, c0:c0 + HY_W] * win
            if direction == 1:
                f = jnp.where(first_tile & (row == 0), 0.0, f)
            o_ref[order * 2 + direction] = f


def _hy_positional(length):
    n = np.arange(length, dtype=np.float64)
    t = n / max(length - 1, 1)
    bands = np.linspace(1e-4, HY_BANDS - 1, HY_BANDS)
    wpos = (2 * math.pi / length) * n
    feats = np.concatenate([t[:, None], np.cos(wpos[:, None] * bands), -np.sin(wpos[:, None] * bands)], axis=-1)
    feats = np.pad(feats, ((0, 0), (0, LANES - HY_POS_DIM)))
    deltas = np.abs(np.linspace(math.log(HY_TARGET) / HY_SLOW_DECAY, math.log(HY_TARGET) / HY_FAST_DECAY, HY_W))
    window = np.exp(-t[:, None] * deltas)
    return jnp.asarray(feats, F32), jnp.asarray(window, F32)


def _hyena_filters(length, w1, b1, freq, w2, b2, w3):
    feats, window = _hy_positional(length)
    tl = min(length, 512)
    hid = HY_FILTER_HID
    w1p = jnp.pad(w1, ((0, LANES - HY_POS_DIM), (0, 0)))
    const = lambda shape: pl.BlockSpec(shape, lambda i: tuple(0 for _ in shape))
    return pl.pallas_call(
        _hyfilt_kernel,
        grid=(length // tl,),
        in_specs=[
            pl.BlockSpec((tl, LANES), lambda i: (i, 0)),
            pl.BlockSpec((tl, HY_W), lambda i: (i, 0)),
            const((LANES, hid)), const((1, hid)), const((1, hid)),
            const((hid, hid)), const((1, hid)), const((hid, 2 * HY_ORDER * HY_W)),
        ],
        out_specs=pl.BlockSpec((2 * HY_ORDER, tl, HY_W), lambda i: (0, i, 0)),
        out_shape=jax.ShapeDtypeStruct((2 * HY_ORDER, length, HY_W), F32),
        compiler_params=_params("arbitrary"),
        name="hyena_filters",
    )(feats, window, w1p, b1.reshape(1, hid), freq.reshape(1, hid), w2, b2.reshape(1, hid), w3)


def _dft_tables(length):
    n = 2 * length
    n2 = HY_N2
    n1 = n // n2
    half = n1 // 2
    k1 = np.arange(half, dtype=np.float64) + 0.5
    idx = (n2 * np.arange(half)[None, None, :] + np.arange(n2)[:, None, None])
    ang = 2 * np.pi * k1[None, :, None] * idx / n
    fa = np.concatenate([np.cos(ang), -np.sin(ang)], axis=1)
    kk = np.arange(n2, dtype=np.float64)
    angb = 2 * np.pi * kk[:, None] * kk[None, :] / n2
    cr, sr = np.cos(angb), np.sin(angb)
    fb = np.block([[cr, sr], [-sr, cr]])
    fbi = np.block([[cr, -sr], [sr, cr]])
    idxo = (n2 * np.arange(half)[None, :, None] + np.arange(n2)[:, None, None])
    ango = 2 * np.pi * k1[None, None, :] * idxo / n
    ga = np.concatenate([np.cos(ango), -np.sin(ango)], axis=2) * (2.0 / n)
    f32 = lambda a: jnp.asarray(a, F32).astype(BF16)
    return f32(fa), f32(fb), f32(fbi), f32(ga)


def _seq_pitch(n2n):
    return n2n + SUBLANES


def _to_pitched(dst_ref, val, n2n):
    pitch = _seq_pitch(n2n)
    for i in range(val.shape[0] // n2n):
        dst_ref[i * pitch:i * pitch + n2n, :] = val[i * n2n:(i + 1) * n2n, :]


def _hy_forward_a(src_ref, a_ref, fa_ref):
    n2n, two_n1, half = fa_ref.shape
    zp, ap = _seq_pitch(n2n), _seq_pitch(two_n1)

    def body(n2, carry):
        zs = src_ref[pl.ds(n2, half, stride=zp), :].astype(BF16)
        a_ref[pl.ds(pl.multiple_of(n2 * ap, SUBLANES), two_n1), :] = _dot(fa_ref[n2], zs)
        return carry

    lax.fori_loop(0, n2n, body, 0, unroll=HY_UNROLL)


def _hy_kspec_kernel(hf_ref, hb_ref, fa_ref, fb_ref, o_ref, af_ref, ab_ref, hfp_ref, hbp_ref):
    n2n, two_n1, _ = fa_ref.shape
    n1 = two_n1 // 2
    ap = _seq_pitch(two_n1)
    _to_pitched(hfp_ref, hf_ref[0], n2n)
    _to_pitched(hbp_ref, hb_ref[0], n2n)
    _hy_forward_a(hfp_ref, af_ref, fa_ref)
    _hy_forward_a(hbp_ref, ab_ref, fa_ref)
    fb = fb_ref[...]

    def body(k1, carry):
        def spectrum(a_ref):
            ar = a_ref[pl.ds(k1, n2n, stride=ap), :]
            ai = a_ref[pl.ds(n1 + k1, n2n, stride=ap), :]
            return _dot(fb, jnp.concatenate([ar, ai], axis=0).astype(BF16))

        xf, xb = spectrum(af_ref), spectrum(ab_ref)
        o_ref[0, k1] = jnp.concatenate([xf[:n2n] + xb[:n2n], xf[n2n:] - xb[n2n:]], axis=0)
        return carry

    lax.fori_loop(0, n1, body, 0, unroll=HY_UNROLL)


def _hy_kspec(filt, tables):
    fa, fb, _, _ = tables
    _, length, _ = filt.shape
    n2n, two_n1, _ = fa.shape
    n1 = two_n1 // 2
    nh = HY_W // LANES
    return pl.pallas_call(
        _hy_kspec_kernel,
        grid=(HY_ORDER, nh),
        in_specs=[
            pl.BlockSpec((1, length, LANES), lambda o, h: (2 * o, 0, h)),
            pl.BlockSpec((1, length, LANES), lambda o, h: (2 * o + 1, 0, h)),
            pl.BlockSpec(fa.shape, lambda o, h: (0, 0, 0), pipeline_mode=pl.Buffered(1)),
            pl.BlockSpec(fb.shape, lambda o, h: (0, 0)),
        ],
        out_specs=pl.BlockSpec((1, n1, 2 * n2n, LANES), lambda o, h: (o, 0, 0, h)),
        out_shape=jax.ShapeDtypeStruct((HY_ORDER, n1, 2 * n2n, HY_W), F32),
        scratch_shapes=[pltpu.VMEM((n2n * _seq_pitch(two_n1), LANES), F32),
                        pltpu.VMEM((n2n * _seq_pitch(two_n1), LANES), F32),
                        pltpu.VMEM((length // n2n * _seq_pitch(n2n), LANES), F32),
                        pltpu.VMEM((length // n2n * _seq_pitch(n2n), LANES), F32)],
        compiler_params=_params("arbitrary", "arbitrary"),
        name="hyena_kspec",
    )(filt, filt, fa, fb)


def _hy_conv_kernel(z_ref, g_ref, fa_ref, fb_ref, fbi_ref, ga_ref, k_ref, bias_ref,
                    o_ref, a_ref, c_ref, zc_ref, gc_ref):
    n2n, two_n1, half = fa_ref.shape
    n1 = two_n1 // 2
    zp, ap = _seq_pitch(n2n), _seq_pitch(two_n1)
    _to_pitched(zc_ref, z_ref[0], n2n)
    _to_pitched(gc_ref, g_ref[0], n2n)
    _hy_forward_a(zc_ref, a_ref, fa_ref)
    fb = fb_ref[...]
    fbi = fbi_ref[...]

    def body_b(grp, carry):
        k1s = [grp * HY_UNROLL + i for i in range(HY_UNROLL)]
        sl = [(pl.ds(k1, n2n, stride=ap), pl.ds(n1 + k1, n2n, stride=ap)) for k1 in k1s]
        xs = [_dot(fb, jnp.concatenate([a_ref[re, :], a_ref[im, :]], axis=0).astype(BF16)) for re, im in sl]
        ys = []
        for k1, x in zip(k1s, xs):
            kk = k_ref[0, k1]
            xr, xi = x[:n2n], x[n2n:]
            kr, ki = kk[:n2n], kk[n2n:]
            ys.append(jnp.concatenate([xr * kr - xi * ki, xr * ki + xi * kr], axis=0).astype(BF16))
        for (re, im), y in zip(sl, ys):
            c = _dot(fbi, y)
            c_ref[re, :] = c[:n2n]
            c_ref[im, :] = c[n2n:]
        return carry

    lax.fori_loop(0, n1 // HY_UNROLL, body_b, 0)
    bias = bias_ref[0]

    def body_c(m2, carry):
        rows = pl.ds(m2, half, stride=zp)
        c = c_ref[pl.ds(pl.multiple_of(m2 * ap, SUBLANES), two_n1), :].astype(BF16)
        y = _dot(ga_ref[m2], c)
        a_ref[rows, :] = gc_ref[rows, :] * (y + zc_ref[rows, :] * bias)
        return carry

    lax.fori_loop(0, n2n, body_c, 0, unroll=HY_UNROLL)
    for i in range(half):
        o_ref[0, i * n2n:(i + 1) * n2n, :] = a_ref[i * zp:i * zp + n2n, :]


def _hy_conv(z_arr, z_blk, g_arr, g_blk, kspec, order, bias3, tables, n_lat):
    fa, fb, fbi, ga = tables
    bsz = z_arr.shape[0]
    n2n, two_n1, half = fa.shape
    n1 = two_n1 // 2
    nh = HY_W // LANES
    resident = lambda shape: pl.BlockSpec(shape, lambda h, b_: tuple(0 for _ in shape), pipeline_mode=pl.Buffered(1))
    return pl.pallas_call(
        _hy_conv_kernel,
        grid=(nh, bsz),
        in_specs=[
            pl.BlockSpec((1, n_lat, LANES), lambda h, b_: (b_, 0, z_blk + h)),
            pl.BlockSpec((1, n_lat, LANES), lambda h, b_: (b_, 0, g_blk + h)),
            resident(fa.shape), resident(fb.shape), resident(fbi.shape), resident(ga.shape),
            pl.BlockSpec((1, n1, 2 * n2n, LANES), lambda h, b_: (order, 0, 0, h), pipeline_mode=pl.Buffered(1)),
            pl.BlockSpec((1, 1, LANES), lambda h, b_: (order, 0, h)),
        ],
        out_specs=pl.BlockSpec((1, n_lat, LANES), lambda h, b_: (b_, 0, h)),
        out_shape=jax.ShapeDtypeStruct((bsz, n_lat, HY_W), F32),
        scratch_shapes=[pltpu.VMEM((n2n * _seq_pitch(two_n1), LANES), F32),
                        pltpu.VMEM((n2n * _seq_pitch(two_n1), LANES), F32),
                        pltpu.VMEM((half * _seq_pitch(n2n), LANES), F32),
                        pltpu.VMEM((half * _seq_pitch(n2n), LANES), F32)],
        compiler_params=_params("arbitrary", "arbitrary"),
        name="hyena_conv",
    )(z_arr, g_arr, fa, fb, fbi, ga, kspec, bias3)


def _hyena_ctx_kernel(u_ref, filt_ref, bias_ref, fd_ref, gd_ref, o_ref, *, n_ctx):
    fd = fd_ref[...]
    gd = gd_ref[...]
    nn = 2 * n_ctx
    u = u_ref[0]
    v, x1, x2 = u[:, :HY_W], u[:, HY_W:2 * HY_W], u[:, 2 * HY_W:]

    def conv(zin, order):
        hf = _dot(fd, filt_ref[2 * order].astype(BF16))
        hb = _dot(fd, filt_ref[2 * order + 1].astype(BF16))
        kr, ki = hf[:nn] + hb[:nn], hf[nn:] - hb[nn:]
        zz = _dot(fd, zin.astype(BF16))
        zr, zi = zz[:nn], zz[nn:]
        y = jnp.concatenate([zr * kr - zi * ki, zr * ki + zi * kr], axis=0).astype(BF16)
        return _dot(gd, y) + zin * bias_ref[order]

    y1 = x1 * conv(v, 0)
    o_ref[0] = x2 * conv(y1, 1)


def _hyena_ctx(hy, filt, bias, n_lat, n_ctx):
    bsz = hy.shape[0]
    nn = 2 * n_ctx
    k = np.arange(nn, dtype=np.float64)
    t = np.arange(n_ctx, dtype=np.float64)
    ang = 2 * np.pi * k[:, None] * t[None, :] / nn
    fd = jnp.asarray(np.concatenate([np.cos(ang), -np.sin(ang)], axis=0), F32).astype(BF16)
    gd = jnp.asarray(np.concatenate([np.cos(ang.T), -np.sin(ang.T)], axis=1) / nn, F32).astype(BF16)
    blk = n_lat // n_ctx
    assert blk * n_ctx == n_lat
    return pl.pallas_call(
        functools.partial(_hyena_ctx_kernel, n_ctx=n_ctx),
        grid=(bsz,),
        in_specs=[
            pl.BlockSpec((1, n_ctx, 3 * HY_W), lambda b: (b, blk, 0)),
            pl.BlockSpec((2 * HY_ORDER, n_ctx, HY_W), lambda b: (0, 0, 0)),
            pl.BlockSpec((HY_ORDER, 1, HY_W), lambda b: (0, 0, 0)),
            pl.BlockSpec((2 * nn, n_ctx), lambda b: (0, 0)),
            pl.BlockSpec((n_ctx, 2 * nn), lambda b: (0, 0)),
        ],
        out_specs=pl.BlockSpec((1, n_ctx, HY_W), lambda b: (b, 0, 0)),
        out_shape=jax.ShapeDtypeStruct((bsz, n_ctx, HY_W), F32),
        compiler_params=_params("arbitrary"),
        name="hyena_ctx",
    )(hy, filt, bias, fd, gd)


def _hyena_latent(hy, filt, bias, n_lat, tables):
    kspec = _hy_kspec(filt, tables)
    bias3 = bias.reshape(HY_ORDER, 1, HY_W)
    nb = HY_W // LANES
    y1 = _hy_conv(hy, 0, hy, nb, kspec, 0, bias3, tables, n_lat)
    return _hy_conv(y1, 0, hy, 2 * nb, kspec, 1, bias3, tables, n_lat)


def _outproj_kernel(xl_ref, xc_ref, attl_ref, attc_ref, ssd_ref, hyl_ref, hyc_ref, gt_ref, wa_ref, ws_ref, wh_ref,
                    o_ref, *, n_lat_tiles):
    x = _stream_tile(xl_ref, xc_ref, n_lat_tiles)
    att = _stream_tile(attl_ref, attc_ref, n_lat_tiles)
    hy = _stream_tile(hyl_ref, hyc_ref, n_lat_tiles).astype(BF16)
    mix = _dot(att, wa_ref[0]) + _dot(ssd_ref[0], ws_ref[0]) + _dot(hy, wh_ref[0])
    o_ref[0] = x + gt_ref[0] * mix


def _outproj(x_lat, x_ctx, ctx_blk, att_l, att_c, ssd, hy_l, hy_c, mods, w_out, layer, n_lat_tiles, n_tiles):
    bsz, _, d = x_lat.shape
    tm = TOKEN_TILE

    def mod_row(j, b):
        return jnp.where(j >= n_lat_tiles, bsz, b)

    tok = lambda w: pl.BlockSpec((1, tm, w), lambda j, b: (b, j, 0))
    return pl.pallas_call(
        functools.partial(_outproj_kernel, n_lat_tiles=n_lat_tiles),
        grid=(n_tiles, bsz),
        in_specs=[
            *_stream_specs(n_lat_tiles, ctx_blk, d),
            *_stream_specs(n_lat_tiles, 0, ATT_W),
            tok(SSD_W),
            *_stream_specs(n_lat_tiles, 0, HY_W),
            pl.BlockSpec((1, 1, d), lambda j, b: (mod_row(j, b), 0, 2)),
            pl.BlockSpec((1, ATT_W, d), lambda j, b: (layer, 0, 0)),
            pl.BlockSpec((1, SSD_W, d), lambda j, b: (layer, ATT_W // SSD_W, 0)),
            pl.BlockSpec((1, HY_W, d), lambda j, b: (layer, (ATT_W + SSD_W) // HY_W, 0)),
        ],
        out_specs=tok(d),
        out_shape=jax.ShapeDtypeStruct((bsz, n_tiles * tm, d), F32),
        compiler_params=_params("arbitrary", "arbitrary"),
        name="outproj",
    )(x_lat, x_ctx, att_l, att_c, ssd, hy_l, hy_c, mods,
      w_out, w_out, w_out)


def _route(logits_t, bias_col):
    scores = jax.nn.sigmoid(logits_t)
    sel = scores + bias_col
    neg_inf = jnp.float32(-jnp.inf)
    rows = [sel[e:e + 1, :] for e in range(N_EXPERTS)]
    grp = []
    for g in range(N_GROUPS):
        r = rows[g * EXPERTS_PER_GROUP:(g + 1) * EXPERTS_PER_GROUP]
        top = functools.reduce(jnp.maximum, r)
        taken = None
        rest = []
        for ri in r:
            is_top = (ri == top) if taken is None else (ri == top) & jnp.logical_not(taken)
            rest.append(jnp.where(is_top, neg_inf, ri))
            taken = is_top if taken is None else taken | is_top
        grp.append(top + functools.reduce(jnp.maximum, rest))
    best = jnp.zeros(grp[0].shape, jnp.int32)
    cur = grp[0]
    for g in range(1, N_GROUPS):
        upd = grp[g] > cur
        best = jnp.where(upd, g, best)
        cur = jnp.where(upd, grp[g], cur)
    picked = []
    for e in range(N_EXPERTS):
        g, i = divmod(e, EXPERTS_PER_GROUP)
        rank = jnp.zeros(best.shape, jnp.int32)
        for j in range(EXPERTS_PER_GROUP):
            if j == i:
                continue
            other = rows[g * EXPERTS_PER_GROUP + j]
            ahead = (other > rows[e]) | ((other == rows[e]) & (j < i))
            rank = rank + ahead.astype(jnp.int32)
        keep = (best == g) & (rank < 2)
        picked.append(jnp.where(keep, scores[e:e + 1, :], 0.0))
    total = functools.reduce(lambda u, w: u + w, picked)
    return jnp.concatenate(picked, axis=0) / total, best


def _ffn_input(x_ref, sh_ref, sc_ref, g_ref):
    x = x_ref[...].reshape(MOE_TILE, D_MODEL)
    ms = jnp.mean(x * x, axis=-1, keepdims=True)
    t = x * lax.rsqrt(ms + EPS) * g_ref[...]
    return x, t * (1.0 + sc_ref[0]) + sh_ref[0]


def _route_kernel(x_ref, sh_ref, sc_ref, g_ref, wr_ref, rb_ref, rt_ref, srow_ref, tbl_ref):
    tm = MOE_TILE
    _, t = _ffn_input(x_ref, sh_ref, sc_ref, g_ref)
    logits = _dot_x3(t, wr_ref[...])
    comb_t, best = _route(logits.T[:N_EXPERTS, :], rb_ref[...])

    member = [jnp.where(best == g, 1.0, 0.0) for g in range(N_GROUPS)]
    comb4 = functools.reduce(
        lambda u, w: u + w,
        [member[g] * comb_t[g * EXPERTS_PER_GROUP:(g + 1) * EXPERTS_PER_GROUP, :] for g in range(N_GROUPS)])
    masks = jnp.concatenate(member + [jnp.zeros((8 - N_GROUPS, tm), F32)], axis=0)
    earlier = jnp.where(lax.broadcasted_iota(jnp.int32, (tm, tm), 0) < lax.broadcasted_iota(jnp.int32, (tm, tm), 1),
                        1.0, 0.0).astype(BF16)
    ranks = _dot(masks.astype(BF16), earlier)
    cnt = jnp.sum(masks, axis=1, keepdims=True)
    padded = jnp.ceil(cnt * (1.0 / MOE_CHUNK)) * MOE_CHUNK
    lane = lax.broadcasted_iota(jnp.int32, (1, LANES), 1).astype(F32) * MOE_CHUNK
    start = jnp.zeros((1, 1), F32)
    slot = jnp.zeros((1, tm), F32)
    gid = jnp.zeros((1, LANES), jnp.int32)
    for g in range(N_GROUPS):
        slot = slot + member[g] * (start + ranks[g:g + 1, :])
        start = start + padded[g:g + 1, :]
        gid = gid + jnp.where(lane >= start, 1, 0)
    tbl_ref[0] = gid
    srow_ref[0] = slot.astype(jnp.int32)
    rt_t = jnp.concatenate([comb4, slot, jnp.zeros((LANES - EXPERTS_PER_GROUP - 1, tm), F32)], axis=0)
    rt_ref[...] = rt_t.T


def _experts_kernel(tbl_ref, x_ref, sh_ref, sc_ref, gt_ref, g_ref, rt_ref, srow_ref, wg_ref, wu_ref, wd_ref, gf_ref,
                    *rest, final, tile_of, n_grid):
    o_ref, xp_ref, yp_ref = rest[-3:]
    tm, ch = MOE_TILE, MOE_CHUNK
    n_slots = MOE_CHUNKS * ch
    tile = tile_of(*[pl.program_id(a) for a in range(n_grid)])
    x, t = _ffn_input(x_ref, sh_ref, sc_ref, g_ref)
    rt = rt_ref[...]
    slot_col = rt[:, EXPERTS_PER_GROUP:EXPERTS_PER_GROUP + 1].astype(jnp.int32)
    gather = jnp.where(lax.broadcasted_iota(jnp.int32, (n_slots, tm), 0) == srow_ref[0], 1.0, 0.0).astype(BF16)
    scatter = jnp.where(lax.broadcasted_iota(jnp.int32, (tm, n_slots), 1) == slot_col, 1.0, 0.0).astype(BF16)
    xp_ref[...] = _dot(gather, t.astype(BF16)).astype(BF16)
    w_slot = _dot3_right(gather, rt)

    def run_chunk(rows, grp):
        xc = xp_ref[rows, :]
        e0 = grp * EXPERTS_PER_GROUP
        gu_next = _dot(xc, wg_ref[0, e0]), _dot(xc, wu_ref[0, e0])
        acc = jnp.zeros((ch, D_MODEL), F32)
        for j in range(EXPERTS_PER_GROUP):
            gate, up = gu_next
            if j + 1 < EXPERTS_PER_GROUP:
                gu_next = _dot(xc, wg_ref[0, e0 + j + 1]), _dot(xc, wu_ref[0, e0 + j + 1])
            hid = _silu(gate) * up * w_slot[rows, j:j + 1]
            acc = acc + _dot(hid.astype(BF16), wd_ref[0, e0 + j])
        yp_ref[rows, :] = acc.astype(BF16)

    for c in range(MOE_CHUNKS):
        rows = slice(c * ch, (c + 1) * ch)
        grp = tbl_ref[tile, c]
        pl.when(grp < N_GROUPS)(functools.partial(run_chunk, rows, grp))

        @pl.when(grp >= N_GROUPS)
        def _():
            yp_ref[rows, :] = jnp.zeros((ch, D_MODEL), BF16)

    y = x + gt_ref[0] * _dot(scatter, yp_ref[...])
    if final:
        y = y * lax.rsqrt(jnp.mean(y * y, axis=-1, keepdims=True) + EPS) * gf_ref[...]
    o_ref[...] = y.reshape(o_ref.shape)


def _moe_tiles(x, mods, mod_row, g_ffn, w_router, router_bias, wgu, wd, layer, g_final, final, out_rows,
               grid, x_block, x_index, out_index, tile_of, n_tiles):
    bsz, _, d = x.shape
    tm = MOE_TILE
    ng = len(grid)
    const = lambda shape: pl.BlockSpec(shape, lambda *a: tuple(0 for _ in shape))
    mod = lambda col: pl.BlockSpec((1, 1, d), lambda *a: (mod_row(*a[:ng]), 0, col))
    xspec = pl.BlockSpec(x_block, lambda *a: x_index(*a[:ng]))
    rt, srow, tbl = pl.pallas_call(
        _route_kernel,
        grid=grid,
        in_specs=[xspec, mod(3), mod(4), const((1, d)), const((d, LANES)), const((N_EXPERTS, 1))],
        out_specs=[
            pl.BlockSpec((tm, LANES), lambda *a: (tile_of(*a), 0)),
            pl.BlockSpec((1, 1, tm), lambda *a: (tile_of(*a), 0, 0)),
            pl.BlockSpec((1, 1, LANES), lambda *a: (tile_of(*a), 0, 0)),
        ],
        out_shape=[
            jax.ShapeDtypeStruct((n_tiles * tm, LANES), F32),
            jax.ShapeDtypeStruct((n_tiles, 1, tm), jnp.int32),
            jax.ShapeDtypeStruct((n_tiles, 1, LANES), jnp.int32),
        ],
        compiler_params=_params(*["arbitrary"] * ng),
        name="moe_route",
    )(x, mods, mods, g_ffn, w_router, router_bias)

    resident = lambda shape: pl.BlockSpec(shape, lambda *a: (layer,) + tuple(0 for _ in shape[1:]),
                                          pipeline_mode=pl.Buffered(1))
    in_specs = [
        xspec, mod(3), mod(4), mod(5), const((1, d)),
        pl.BlockSpec((tm, LANES), lambda *a: (tile_of(*a[:ng]), 0)),
        pl.BlockSpec((1, 1, tm), lambda *a: (tile_of(*a[:ng]), 0, 0)),
        resident((1, N_EXPERTS, d, D_FF)),
        resident((1, N_EXPERTS, d, D_FF)),
        resident((1, N_EXPERTS, D_FF, d)),
        const((1, d)),
    ]
    args = [tbl.reshape(n_tiles, LANES), x, mods, mods, mods, g_ffn, rt, srow, *wgu, wd, g_final]
    return pl.pallas_call(
        functools.partial(_experts_kernel, final=final, tile_of=tile_of, n_grid=ng),
        grid_spec=pltpu.PrefetchScalarGridSpec(
            num_scalar_prefetch=1,
            grid=grid,
            in_specs=in_specs,
            out_specs=pl.BlockSpec(x_block, lambda *a: out_index(*a[:ng])),
            scratch_shapes=[pltpu.VMEM((MOE_CHUNKS * MOE_CHUNK, d), BF16), pltpu.VMEM((MOE_CHUNKS * MOE_CHUNK, d), BF16)],
        ),
        out_shape=jax.ShapeDtypeStruct((bsz, out_rows, d), F32),
        compiler_params=_params(*["arbitrary"] * ng),
        name="moe_experts",
    )(*args)


def _moe(x, mods, g_ffn, w_router, router_bias, wgu, wd, layer, g_final, n_lat, n_ctx, with_ctx, final):
    bsz, _, d = x.shape
    tm = MOE_TILE
    per_b = n_lat // tm
    common = (g_ffn, w_router, router_bias, wgu, wd, layer, g_final, final)
    lat_index = lambda b, j: (b, j, 0)
    out_lat = _moe_tiles(x, mods, lambda b, j: b, *common, n_lat,
                         (bsz, per_b), (1, tm, d), lat_index, lat_index, lambda b, j: b * per_b + j, bsz * per_b)
    if not with_ctx:
        return out_lat, None
    nb = tm // n_ctx
    out_ctx = _moe_tiles(x, mods, lambda i: bsz, *common, n_ctx,
                         (bsz // nb,), (nb, n_ctx, d), lambda i: (i, n_lat // n_ctx, 0), lambda i: (i, 0, 0),
                         lambda i: i, bsz // nb)
    return out_lat, out_ctx


def _block_ones(width, block):
    idx = np.arange(width) // block
    return jnp.asarray(idx[:, None] == idx[None, :], F32).astype(BF16)


def _rope_tables(n_lat, n_ctx):
    rows = n_lat // GRID_W
    row = np.repeat(np.arange(rows), GRID_W).astype(np.float64)
    col = np.tile(np.arange(GRID_W), rows).astype(np.float64)
    inv = ROPE_THETA ** (-np.arange(0, ROPE_AXIS_DIM, 2, dtype=np.float64) / ROPE_AXIS_DIM)
    ang = np.concatenate([row[:, None] * inv, col[:, None] * inv], axis=-1)
    ang = np.concatenate([ang, np.zeros((n_ctx, ang.shape[1]))], axis=0)
    cos = np.concatenate([np.cos(ang), np.cos(ang)], axis=-1)
    sin = np.concatenate([-np.sin(ang), np.sin(ang)], axis=-1)
    scale = Q_SCALE
    cs = np.tile(cos, (1, ATT_HEADS)) * scale
    sn = np.tile(sin, (1, ATT_HEADS)) * scale
    return jnp.asarray(cs, F32), jnp.asarray(sn, F32)


def kernel(x, c, ctx, c_ctx, w_mod, b_mod, g_mix, g_ffn, w_in, q_norm, k_norm, ssd_conv_w, ssd_conv_b,
           ssd_dt_bias, ssd_a_log, ssd_d, ssd_norm, hy_conv_w, hy_conv_b, hy_w1, hy_b1, hy_freq, hy_w2, hy_b2,
           hy_w3, hy_bias, w_out, w_router, router_bias, w_gate, w_up, w_down, g_final):
    bsz, n_lat, d = x.shape
    n_ctx = ctx.shape[1]
    depth = w_mod.shape[0]
    t = n_lat + n_ctx
    tm = TOKEN_TILE
    n_lat_tiles = n_lat // tm
    n_tiles = t // tm
    assert n_ctx == tm and n_lat % (HY_N2 * 8) == 0 and bsz < MOD_ROWS

    cvec = jnp.concatenate([c, c_ctx[None], jnp.zeros((MOD_ROWS - bsz - 1, d), F32)], axis=0)
    mods_all = _adaln(cvec, w_mod, b_mod)

    cs, sn = _rope_tables(n_lat, n_ctx)
    bd_head = _block_ones(ATT_W, HEAD_DIM)
    bd_ssd = _block_ones(SSD_W, SSD_W // SSD_GROUPS)
    tables = _dft_tables(n_lat)
    pad_row = lambda v: jnp.pad(v.reshape(1, -1), ((0, 0), (0, LANES - v.size)))
    w_out_b = w_out.astype(BF16)
    wgu_b = (w_gate.astype(BF16), w_up.astype(BF16))
    wd_b = w_down.astype(BF16)

    x_lat, x_ctx, ctx_blk = x, ctx, 0
    for i in range(depth):
        last = i == depth - 1
        mods = mods_all[i].reshape(MOD_ROWS, 1, 6 * d)
        q, kt, v, z, xbc, dt_raw, hy = _inproj(
            x_lat, x_ctx, ctx_blk, mods, g_mix[i].reshape(1, d), w_in, i,
            jnp.tile(q_norm[i], ATT_HEADS).reshape(1, ATT_W), jnp.tile(k_norm[i], ATT_KV_HEADS).reshape(1, KV_W),
            cs, sn, bd_head, jnp.concatenate([ssd_conv_w[i], hy_conv_w[i]], axis=-1),
            jnp.concatenate([ssd_conv_b[i], hy_conv_b[i]]).reshape(1, -1), n_lat_tiles)

        att = _attention(q, kt, v, 0, n_lat, 0, t, ATT_TILE)
        att_c = _attention(q, kt, v, n_lat, n_ctx, n_lat, n_ctx, n_ctx) if not last else att

        ssd = _ssd(xbc, z, dt_raw, pad_row(ssd_dt_bias[i]), pad_row(ssd_a_log[i]),
                   jnp.repeat(ssd_d[i], SSD_HEAD_DIM).reshape(1, SSD_W), ssd_norm[i].reshape(1, SSD_W),
                   bd_ssd, n_lat)

        hyp = (hy_w1[i], hy_b1[i], hy_freq[i], hy_w2[i], hy_b2[i], hy_w3[i])
        hy_l = _hyena_latent(hy, _hyena_filters(n_lat, *hyp), hy_bias[i], n_lat, tables)
        hy_c = hy_l if last else _hyena_ctx(hy, _hyena_filters(n_ctx, *hyp),
                                            hy_bias[i].reshape(HY_ORDER, 1, HY_W), n_lat, n_ctx)

        live_tiles = n_lat_tiles if last else n_tiles
        xs = _outproj(x_lat, x_ctx, ctx_blk, att, att_c, ssd, hy_l, hy_c, mods, w_out_b, i,
                      n_lat_tiles, live_tiles)

        x_lat, x_ctx = _moe(xs, mods, g_ffn[i].reshape(1, d), jnp.pad(w_router, ((0, 0), (0, LANES - N_EXPERTS))),
                            router_bias.reshape(N_EXPERTS, 1), wgu_b, wd_b, i, g_final.reshape(1, d),
                            n_lat, n_ctx, not last, last)
    return x_lat
```

```python
import functools
import math

import jax
import jax.numpy as jnp
import numpy as np
from jax import lax
from jax.experimental import pallas as pl
from jax.experimental.pallas import tpu as pltpu

F32 = jnp.float32
BF16 = jnp.bfloat16

D_MODEL = 1024
GRID_W = 64
EPS = 1e-6

ATT_HEADS = 6
ATT_KV_HEADS = 2
HEAD_DIM = 64
ATT_W = ATT_HEADS * HEAD_DIM
KV_W = ATT_KV_HEADS * HEAD_DIM
ROPE_AXIS_DIM = HEAD_DIM // 2
ROPE_THETA = 10000.0
Q_SCALE = HEAD_DIM ** -0.5 * math.log2(math.e)

SSD_HEADS = 6
SSD_HEAD_DIM = 64
SSD_W = SSD_HEADS * SSD_HEAD_DIM
SSD_GROUPS = 2
SSD_STATE = 64
SSD_CHUNK = 128
SSD_CONV_CH = SSD_W + 2 * SSD_GROUPS * SSD_STATE
SSD_HEADS_PER_GROUP = SSD_HEADS // SSD_GROUPS
SSD_BLOCK = 2

HY_W = 256
HY_ORDER = 2
HY_BANDS = 16
HY_POS_DIM = 1 + 2 * HY_BANDS
HY_FILTER_HID = 64
HY_FAST_DECAY = 0.3
HY_SLOW_DECAY = 1.5
HY_TARGET = 1e-2
HY_N2 = 64
HY_UNROLL = 8

MIX_W = ATT_W + SSD_W + HY_W
N_EXPERTS = 16
N_GROUPS = 4
EXPERTS_PER_GROUP = N_EXPERTS // N_GROUPS
D_FF = 256

LANES = 128
SUBLANES = 8
TOKEN_TILE = 256
ATT_TILE = 1024
ATT_SUB = 512
OUT_TILE = 512
MOE_TILE = 512
MOE_CHUNK = 128
MOE_CHUNKS = MOE_TILE // MOE_CHUNK + N_GROUPS - 1
MOD_ROWS = 8
VMEM_LIMIT = 56 * 1024 * 1024

COL_Q = 0
COL_K = COL_Q + ATT_W
COL_V = COL_K + KV_W
COL_XBC = COL_V + KV_W
COL_HY = COL_XBC + SSD_CONV_CH
COL_Z = COL_HY + 3 * HY_W
COL_DT = COL_Z + SSD_W
COL_END = COL_DT + LANES


def _params(*sem):
    return pltpu.CompilerParams(dimension_semantics=sem, vmem_limit_bytes=VMEM_LIMIT)


def _silu(x):
    return x * jax.nn.sigmoid(x)


def _softplus(x):
    return jnp.maximum(x, 0.0) + jnp.log1p(jnp.exp(-jnp.abs(x)))


def _split3(x):
    hi = x.astype(BF16)
    r1 = x - hi.astype(F32)
    mid = r1.astype(BF16)
    lo = (r1 - mid.astype(F32)).astype(BF16)
    return hi, mid, lo


def _dot(a, b):
    return jnp.dot(a, b, preferred_element_type=F32)


def _dot_x3(a, b):
    a_hi, a_mid, _ = _split3(a)
    b_hi, b_mid, _ = _split3(b)
    return _dot(a_hi, b_hi) + (_dot(a_hi, b_mid) + _dot(a_mid, b_hi))


def _dot3_right(m_bf16, x):
    hi, mid, lo = _split3(x)
    return _dot(m_bf16, hi) + _dot(m_bf16, mid) + _dot(m_bf16, lo)


def _dot3_left(x, m_bf16):
    hi, mid, lo = _split3(x)
    return _dot(hi, m_bf16) + _dot(mid, m_bf16) + _dot(lo, m_bf16)


def _adaln_kernel(c_ref, w_ref, b_ref, o_ref):
    s = _silu(c_ref[...]).astype(BF16)
    o_ref[0] = _dot(s, w_ref[0].astype(BF16)) + b_ref[0]


def _adaln(cvec, w_mod, b_mod):
    depth, d, n = w_mod.shape
    bn = n // 4
    return pl.pallas_call(
        _adaln_kernel,
        grid=(depth, n // bn),
        in_specs=[
            pl.BlockSpec((MOD_ROWS, d), lambda i, j: (0, 0)),
            pl.BlockSpec((1, d, bn), lambda i, j: (i, 0, j)),
            pl.BlockSpec((1, 1, bn), lambda i, j: (i, 0, j)),
        ],
        out_specs=pl.BlockSpec((1, MOD_ROWS, bn), lambda i, j: (i, 0, j)),
        out_shape=jax.ShapeDtypeStruct((depth, MOD_ROWS, n), F32),
        compiler_params=_params("arbitrary", "arbitrary"),
        name="adaln",
    )(cvec, w_mod, b_mod.reshape(depth, 1, n))


def _head_rope(xn, cs, sn):
    width = xn.shape[-1]
    lane = lax.broadcasted_iota(jnp.int32, xn.shape, 1)
    first_half = (lane % HEAD_DIM) < (HEAD_DIM // 2)
    partner = jnp.where(first_half,
                        pltpu.roll(xn, width - HEAD_DIM // 2, 1),
                        pltpu.roll(xn, HEAD_DIM // 2, 1))
    return xn * cs + partner * sn


def _stream_specs(n_lat_tiles, ctx_blk, d):
    lat = pl.BlockSpec((1, TOKEN_TILE, d), lambda j, b: (b, jnp.minimum(j, n_lat_tiles - 1), 0))
    ctx = pl.BlockSpec((1, TOKEN_TILE, d), lambda j, b: (jnp.where(j >= n_lat_tiles, b, 0), ctx_blk, 0))
    return lat, ctx


def _stream_tile(xl_ref, xc_ref, n_lat_tiles):
    return jnp.where(pl.program_id(0) >= n_lat_tiles, xc_ref[0], xl_ref[0])


def _pack_w_in(w_ref, wb_ref):
    src_z = ATT_W + 2 * KV_W
    src_xbc = src_z + SSD_W
    src_dt = src_xbc + SSD_CONV_CH
    n_dt = 2 * SSD_HEADS
    wb_ref[:, COL_Q:COL_XBC] = w_ref[0, :, :src_z].astype(BF16)
    wb_ref[:, COL_XBC:COL_HY] = w_ref[0, :, src_xbc:src_dt].astype(BF16)
    tail = w_ref[0, :, src_dt:]
    wb_ref[:, COL_HY:COL_Z] = tail[:, n_dt:].astype(BF16)
    wb_ref[:, COL_Z:COL_DT] = w_ref[0, :, src_z:src_xbc].astype(BF16)
    head = tail[:, :LANES]
    lane = lax.broadcasted_iota(jnp.int32, head.shape, 1)
    wb_ref[:, COL_DT:COL_END] = jnp.where(lane < n_dt, head, 0.0).astype(BF16)


def _inproj_kernel(xl_ref, xc_ref, xp_ref, xn_ref, sh_ref, sc_ref, g_ref, w_ref, gq_ref, gk_ref, cs_ref, sn_ref,
                   bd_ref, cw_ref, cb_ref, qt_ref, k_ref, vt_ref, z_ref, xbc_ref, dt_ref, hy_ref, wb_ref, u_ref,
                   *, n_lat_tiles):
    j = pl.program_id(0)

    @pl.when((j == 0) & (pl.program_id(1) == 0))
    def _():
        _pack_w_in(w_ref, wb_ref)

    tm, halo = TOKEN_TILE, SUBLANES
    x = jnp.concatenate([xp_ref[0], _stream_tile(xl_ref, xc_ref, n_lat_tiles), xn_ref[0]], axis=0)
    ms = jnp.mean(x * x, axis=-1, keepdims=True)
    h = x * lax.rsqrt(ms + EPS) * g_ref[...]
    h = h * (1.0 + sc_ref[0]) + sh_ref[0]
    hb = h[halo:halo + tm].astype(BF16)
    u = _dot(h.astype(BF16), wb_ref[:, COL_XBC:COL_Z])
    qkv = _dot(hb, wb_ref[:, COL_Q:COL_XBC])
    bd = bd_ref[...]
    q, k = qkv[:, COL_Q:COL_K], qkv[:, COL_K:COL_V]
    ms_q = _dot((q * q).astype(BF16), bd) * (1.0 / HEAD_DIM)
    ms_k = _dot((k * k).astype(BF16), bd[:KV_W, :KV_W]) * (1.0 / HEAD_DIM)
    zdt = _dot(hb, wb_ref[:, COL_Z:COL_END])

    cs = cs_ref[...]
    sn = sn_ref[...]
    qt_ref[0] = _head_rope(q * lax.rsqrt(ms_q + EPS) * gq_ref[...], cs, sn).T.astype(qt_ref.dtype)
    k_ref[0] = _head_rope(k * lax.rsqrt(ms_k + EPS) * gk_ref[...],
                          cs[:, :KV_W] * (1.0 / Q_SCALE), sn[:, :KV_W] * (1.0 / Q_SCALE)).astype(k_ref.dtype)
    vt_ref[0] = qkv[:, COL_V:COL_XBC].T.astype(vt_ref.dtype)

    has_prev = (j >= 1) & (j < n_lat_tiles)
    has_next = j < n_lat_tiles - 1
    u_ref[halo:halo + tm, :] = u[halo:halo + tm]
    u_ref[:halo, :] = jnp.where(has_prev, u[:halo], 0.0)
    u_ref[halo + tm:, :] = jnp.where(has_next, u[halo + tm:], 0.0)
    cw = cw_ref[...]
    y = (u_ref[halo - 1:halo - 1 + tm, :] * cw[0:1] + u_ref[halo:halo + tm, :] * cw[1:2]
         + u_ref[halo + 1:halo + 1 + tm, :] * cw[2:3] + cb_ref[...])
    xbc_ref[0] = _silu(y[:, :COL_HY - COL_XBC])
    hy_ref[0] = y[:, COL_HY - COL_XBC:]
    z_ref[0] = zdt[:, :COL_DT - COL_Z]
    dt_ref[0] = zdt[:, COL_DT - COL_Z:]


def _inproj(x_lat, x_ctx, ctx_blk, mods, g_mix, w_in, layer, gq, gk, cs, sn, bd, conv_w, conv_b, n_lat_tiles):
    bsz, _, d = x_lat.shape
    tm = TOKEN_TILE
    nt = n_lat_tiles + 1
    t = nt * tm
    per_tile = tm // SUBLANES
    n_conv = COL_Z - COL_XBC

    def mod_row(j, b):
        return jnp.where(j >= n_lat_tiles, bsz, b)

    lat_j = lambda j: jnp.minimum(j, n_lat_tiles - 1)
    prev_spec = pl.BlockSpec((1, SUBLANES, d), lambda j, b: (b, jnp.maximum(lat_j(j) * per_tile - 1, 0), 0))
    next_spec = pl.BlockSpec((1, SUBLANES, d),
                             lambda j, b: (b, jnp.minimum((lat_j(j) + 1) * per_tile, n_lat_tiles * per_tile - 1), 0))

    tok = lambda w: pl.BlockSpec((1, tm, w), lambda j, b: (b, j, 0))
    const = lambda shape: pl.BlockSpec(shape, lambda j, b: tuple(0 for _ in shape))
    outs = pl.pallas_call(
        functools.partial(_inproj_kernel, n_lat_tiles=n_lat_tiles),
        grid=(nt, bsz),
        in_specs=[
            *_stream_specs(n_lat_tiles, ctx_blk, d),
            prev_spec, next_spec,
            pl.BlockSpec((1, 1, d), lambda j, b: (mod_row(j, b), 0, 0)),
            pl.BlockSpec((1, 1, d), lambda j, b: (mod_row(j, b), 0, 1)),
            const((1, d)),
            pl.BlockSpec((1,) + w_in.shape[1:], lambda j, b: (layer, 0, 0), pipeline_mode=pl.Buffered(1)),
            const((1, ATT_W)),
            const((1, KV_W)),
            pl.BlockSpec((tm, ATT_W), lambda j, b: (j, 0)),
            pl.BlockSpec((tm, ATT_W), lambda j, b: (j, 0)),
            const((ATT_W, ATT_W)),
            const((3, n_conv)),
            const((1, n_conv)),
        ],
        out_specs=[
            pl.BlockSpec((1, ATT_W, tm), lambda j, b: (b, 0, j)),
            tok(KV_W),
            pl.BlockSpec((1, KV_W, tm), lambda j, b: (b, 0, j)),
            tok(SSD_W),
            tok(SSD_CONV_CH),
            tok(LANES),
            tok(3 * HY_W),
        ],
        out_shape=[
            jax.ShapeDtypeStruct((bsz, ATT_W, t), BF16),
            jax.ShapeDtypeStruct((bsz, t, KV_W), BF16),
            jax.ShapeDtypeStruct((bsz, KV_W, t), BF16),
            jax.ShapeDtypeStruct((bsz, t, SSD_W), F32),
            jax.ShapeDtypeStruct((bsz, t, SSD_CONV_CH), F32),
            jax.ShapeDtypeStruct((bsz, t, LANES), F32),
            jax.ShapeDtypeStruct((bsz, t, 3 * HY_W), F32),
        ],
        scratch_shapes=[pltpu.VMEM((d, COL_END), BF16), pltpu.VMEM((tm + 2 * SUBLANES, n_conv), F32)],
        compiler_params=_params("arbitrary", "arbitrary"),
        name="inproj",
    )(x_lat, x_ctx, x_lat, x_lat, mods, mods, g_mix, w_in, gq, gk, cs, sn, bd, conv_w, conv_b)
    return outs


def _attn_kernel(qt_ref, k_ref, vt_ref, o_ref, st_ref, pt_ref):
    k = k_ref[0]
    vt = vt_ref[0]
    rep = ATT_HEADS // ATT_KV_HEADS
    tq = qt_ref.shape[2]
    sub = min(tq, ATT_SUB)
    ones = jnp.ones((2 * SUBLANES, vt.shape[1]), vt.dtype)
    vtg = [jnp.concatenate([vt[g * HEAD_DIM:(g + 1) * HEAD_DIM, :], ones], axis=0) for g in range(ATT_KV_HEADS)]
    units = [(c0, hd) for c0 in range(0, tq, sub) for hd in range(ATT_HEADS)]

    def scores(u):
        c0, hd = units[u]
        qh = qt_ref[0, hd * HEAD_DIM:(hd + 1) * HEAD_DIM, c0:c0 + sub]
        zero = jnp.zeros_like(qh)
        w = jnp.concatenate([qh, zero] if hd < rep else [zero, qh], axis=0)
        st = _dot(k, w)
        st_ref[u % 2] = st
        return jnp.max(st, axis=0, keepdims=True)

    m_next = scores(0)
    outs = []
    for u, (c0, hd) in enumerate(units):
        m = m_next
        if u + 1 < len(units):
            m_next = scores(u + 1)
        pt_ref[u % 2] = jnp.exp2(st_ref[u % 2] - m).astype(BF16)
        ot = _dot(vtg[hd // rep], pt_ref[u % 2])
        outs.append(ot[:HEAD_DIM] / ot[HEAD_DIM:HEAD_DIM + 1])
        if hd == ATT_HEADS - 1:
            o_ref[0, c0:c0 + sub, :] = jnp.concatenate(outs, axis=0).T.astype(o_ref.dtype)
            outs = []


def _attention(qt, k, vt, q_row0, n_q, k_row0, n_k, tq):
    bsz = qt.shape[0]
    kblk = k_row0 // n_k
    q_tile0 = q_row0 // tq
    n_q_tiles = n_q // tq
    assert kblk * n_k == k_row0 and q_tile0 * tq == q_row0 and n_q_tiles * tq == n_q
    assert ATT_KV_HEADS == 2
    return pl.pallas_call(
        _attn_kernel,
        grid=(bsz, n_q_tiles),
        in_specs=[
            pl.BlockSpec((1, ATT_W, tq), lambda b, j: (b, 0, q_tile0 + j)),
            pl.BlockSpec((1, n_k, KV_W), lambda b, j: (b, kblk, 0)),
            pl.BlockSpec((1, KV_W, n_k), lambda b, j: (b, 0, kblk)),
        ],
        out_specs=pl.BlockSpec((1, tq, ATT_W), lambda b, j: (b, j, 0)),
        out_shape=jax.ShapeDtypeStruct((bsz, n_q_tiles * tq, ATT_W), BF16),
        scratch_shapes=[pltpu.VMEM((2, n_k, min(tq, ATT_SUB)), F32), pltpu.VMEM((2, n_k, min(tq, ATT_SUB)), BF16)],
        compiler_params=_params("arbitrary", "arbitrary"),
        name="attention",
    )(qt, k, vt)


def _ssd_kernel(xbc_ref, z_ref, dt_ref, dtb_ref, alog_ref, dsk_ref, nw_ref, bd_ref,
                o_ref, hf_ref, hb_ref, hbe_ref, *, n_lat_blocks):
    q = SSD_CHUNK
    nh = SSD_HEADS
    phase = pl.program_id(1)
    step = pl.program_id(2)
    block = jnp.where(phase == 0, n_lat_blocks - step, jnp.where(step < 1, n_lat_blocks, step - 1))

    lane = lax.broadcasted_iota(jnp.int32, (1, LANES), 1)
    a_row = jnp.where(lane < 2 * nh, -jnp.exp(alog_ref[...]), 0.0)
    tt = lax.broadcasted_iota(jnp.int32, (q, q), 0)
    ss = lax.broadcasted_iota(jnp.int32, (q, q), 1)
    lower = (ss <= tt)
    upper = (ss >= tt)
    lmat = jnp.where(lower, 1.0, 0.0).astype(BF16)
    umat = jnp.where(upper, 1.0, 0.0).astype(BF16)

    @pl.when(step == 0)
    def _():
        hf_ref[...] = jnp.zeros_like(hf_ref)
        hb_ref[...] = jnp.zeros_like(hb_ref)

    def load_chunk(ci):
        rows = slice(ci * q, (ci + 1) * q)
        xbc = xbc_ref[0, rows, :]
        dt = _softplus(dt_ref[0, rows, :] + dtb_ref[...])
        return rows, xbc, dt, dt * a_row

    def heads_x(x):
        return [x[:, hd * SSD_HEAD_DIM:(hd + 1) * SSD_HEAD_DIM].astype(BF16) for hd in range(nh)]

    def backward_prepare(ci):
        _, xbc, dt, a = load_chunk(ci)
        bt = xbc[:, SSD_W:SSD_W + SSD_GROUPS * SSD_STATE].T
        dt_t = dt.T
        suf_t = _dot3_left(a.T, lmat)
        terms = []
        for hd in range(nh):
            g = hd // SSD_HEADS_PER_GROUP
            row_b = suf_t[nh + hd:nh + hd + 1, :]
            total = row_b[:, 0:1]
            w_t = jnp.exp(total - row_b) * dt_t[nh + hd:nh + hd + 1, :]
            terms.append((jnp.exp(total), (bt[g * SSD_STATE:(g + 1) * SSD_STATE, :] * w_t).astype(BF16)))
        return terms, heads_x(xbc[:, :SSD_W])

    def backward_state(ci, prepared):
        terms, xh = prepared
        chunk = block * SSD_BLOCK + ci
        for hd in range(nh):
            decay, bw = terms[hd]
            prev = hb_ref[hd]
            hbe_ref[chunk, hd] = prev
            hb_ref[hd] = prev * decay + _dot(bw, xh[hd])

    def forward_prepare(ci):
        rows, xbc, dt, a = load_chunk(ci)
        x = xbc[:, :SSD_W]
        bt = xbc[:, SSD_W:SSD_W + SSD_GROUPS * SSD_STATE].T
        cmat = xbc[:, SSD_W + SSD_GROUPS * SSD_STATE:]
        a_t = a.T
        pre = _dot3_right(lmat, a)
        pre_t = _dot3_left(a_t, umat)
        cgs = [cmat[:, g * SSD_STATE:(g + 1) * SSD_STATE].astype(BF16) for g in range(SSD_GROUPS)]
        btgs = [bt[g * SSD_STATE:(g + 1) * SSD_STATE, :] for g in range(SSD_GROUPS)]
        cbs = [_dot(cgs[g], btgs[g].astype(BF16)) for g in range(SSD_GROUPS)]
        return dict(rows=rows, x=x, xh=heads_x(x), a=a, a_t=a_t, dt_t=dt.T, pre=pre, pre_t=pre_t,
                    cgs=cgs, btgs=btgs, cbs=cbs)

    def forward_mix(p):
        pre, pre_t, dt_t = p["pre"], p["pre_t"], p["dt_t"]
        suf = pre[q - 1:q, :] - pre + p["a"]
        suf_t = pre_t[:, q - 1:q] - pre_t + p["a_t"]
        neg_inf = jnp.float32(-jnp.inf)
        p["y_in"], p["scale"], p["upd"] = [], [], []
        for hd in range(nh):
            g = hd // SSD_HEADS_PER_GROUP
            colf = pre[:, hd:hd + 1]
            rowf = pre_t[hd:hd + 1, :]
            colb = suf[:, nh + hd:nh + hd + 1]
            rowb = suf_t[nh + hd:nh + hd + 1, :]
            wf = jnp.exp(jnp.where(lower, colf - rowf, neg_inf)) * dt_t[hd:hd + 1, :]
            wb = jnp.exp(jnp.where(upper, colb - rowb, neg_inf)) * dt_t[nh + hd:nh + hd + 1, :]
            p["y_in"].append(_dot((p["cbs"][g] * (wf + wb)).astype(BF16), p["xh"][hd]))
            p["scale"].append((jnp.exp(colf), jnp.exp(colb)))
            total = rowf[:, q - 1:q]
            w_t = jnp.exp(total - rowf) * dt_t[hd:hd + 1, :]
            p["upd"].append((jnp.exp(total), (p["btgs"][g] * w_t).astype(BF16)))

    def forward_state(ci, p):
        chunk = block * SSD_BLOCK + ci
        rows, x = p["rows"], p["x"]
        ys = []
        for hd in range(nh):
            cg = p["cgs"][hd // SSD_HEADS_PER_GROUP]
            hf = hf_ref[hd]
            ef, eb = p["scale"][hd]
            y = p["y_in"][hd] + _dot(cg, hf.astype(BF16)) * ef + _dot(cg, hbe_ref[chunk, hd].astype(BF16)) * eb
            decay, bw = p["upd"][hd]
            hf_ref[hd] = hf * decay + _dot(bw, p["xh"][hd])
            ys.append(y)
        y = jnp.concatenate(ys, axis=-1) + x * dsk_ref[...]
        gz = y * _silu(z_ref[0, rows, :])
        ms = _dot((gz * gz).astype(BF16), bd_ref[...]) * (1.0 / (SSD_W // SSD_GROUPS))
        o_ref[0, rows, :] = (gz * lax.rsqrt(ms + EPS) * nw_ref[...]).astype(o_ref.dtype)

    @pl.when(phase == 0)
    def _():
        prepared = [backward_prepare(ci) for ci in range(SSD_BLOCK)]
        for ci in reversed(range(SSD_BLOCK)):
            backward_state(ci, prepared[ci])

    @pl.when(phase == 1)
    def _():
        prepared = [forward_prepare(ci) for ci in range(SSD_BLOCK)]
        for p in prepared:
            forward_mix(p)
        for ci in range(SSD_BLOCK):
            forward_state(ci, prepared[ci])


def _ssd(xbc, z, dt_raw, dt_bias, a_log, d_skip, norm_w, bd, n_lat):
    bsz, t, _ = z.shape
    rows = SSD_BLOCK * SSD_CHUNK
    n_lat_blocks = n_lat // rows
    n_blocks = t // rows
    assert n_blocks == n_lat_blocks + 1 and n_lat_blocks * rows == n_lat

    def block_of(p, s):
        return jnp.where(p == 0, n_lat_blocks - s, jnp.where(s < 1, n_lat_blocks, s - 1))

    def out_block(p, s):
        return jnp.where(p == 0, n_lat_blocks, block_of(p, s))

    const = lambda shape: pl.BlockSpec(shape, lambda b, p, s: tuple(0 for _ in shape))
    return pl.pallas_call(
        functools.partial(_ssd_kernel, n_lat_blocks=n_lat_blocks),
        grid=(bsz, 2, n_blocks),
        in_specs=[
            pl.BlockSpec((1, rows, SSD_CONV_CH), lambda b, p, s: (b, block_of(p, s), 0)),
            pl.BlockSpec((1, rows, SSD_W), lambda b, p, s: (b, block_of(p, s), 0)),
            pl.BlockSpec((1, rows, LANES), lambda b, p, s: (b, block_of(p, s), 0)),
            const((1, LANES)),
            const((1, LANES)),
            const((1, SSD_W)),
            const((1, SSD_W)),
            const((SSD_W, SSD_W)),
        ],
        out_specs=pl.BlockSpec((1, rows, SSD_W), lambda b, p, s: (b, out_block(p, s), 0)),
        out_shape=jax.ShapeDtypeStruct((bsz, t, SSD_W), BF16),
        scratch_shapes=[
            pltpu.VMEM((SSD_HEADS, SSD_STATE, SSD_HEAD_DIM), F32),
            pltpu.VMEM((SSD_HEADS, SSD_STATE, SSD_HEAD_DIM), F32),
            pltpu.VMEM((n_blocks * SSD_BLOCK, SSD_HEADS, SSD_STATE, SSD_HEAD_DIM), F32),
        ],
        compiler_params=_params("arbitrary", "arbitrary", "arbitrary"),
        name="ssd",
    )(xbc, z, dt_raw, dt_bias, a_log, d_skip, norm_w, bd)


def _hyfilt_kernel(feat_ref, win_ref, w1_ref, b1_ref, fr_ref, w2_ref, b2_ref, w3_ref, o_ref):
    hp = lax.Precision.HIGHEST
    fr = fr_ref[...]
    h1 = jnp.sin(fr * (jnp.dot(feat_ref[...], w1_ref[...], precision=hp, preferred_element_type=F32) + b1_ref[...]))
    h2 = jnp.sin(fr * (jnp.dot(h1, w2_ref[...], precision=hp, preferred_element_type=F32) + b2_ref[...]))
    h = _dot_x3(h2, w3_ref[...])
    win = win_ref[...]
    first_tile = pl.program_id(0) == 0
    row = lax.broadcasted_iota(jnp.int32, win.shape, 0)
    for order in range(HY_ORDER):
        for direction in range(2):
            c0 = (direction * HY_ORDER + order) * HY_W
            f = h[:, c0:c0 + HY_W] * win
            if direction == 1:
                f = jnp.where(first_tile & (row == 0), 0.0, f)
            o_ref[order * 2 + direction] = f


def _hy_positional(length):
    n = np.arange(length, dtype=np.float64)
    t = n / max(length - 1, 1)
    bands = np.linspace(1e-4, HY_BANDS - 1, HY_BANDS)
    wpos = (2 * math.pi / length) * n
    feats = np.concatenate([t[:, None], np.cos(wpos[:, None] * bands), -np.sin(wpos[:, None] * bands)], axis=-1)
    feats = np.pad(feats, ((0, 0), (0, LANES - HY_POS_DIM)))
    deltas = np.abs(np.linspace(math.log(HY_TARGET) / HY_SLOW_DECAY, math.log(HY_TARGET) / HY_FAST_DECAY, HY_W))
    window = np.exp(-t[:, None] * deltas)
    return jnp.asarray(feats, F32), jnp.asarray(window, F32)


def _hyena_filters(length, w1, b1, freq, w2, b2, w3):
    feats, window = _hy_positional(length)
    tl = min(length, 512)
    hid = HY_FILTER_HID
    w1p = jnp.pad(w1, ((0, LANES - HY_POS_DIM), (0, 0)))
    const = lambda shape: pl.BlockSpec(shape, lambda i: tuple(0 for _ in shape))
    return pl.pallas_call(
        _hyfilt_kernel,
        grid=(length // tl,),
        in_specs=[
            pl.BlockSpec((tl, LANES), lambda i: (i, 0)),
            pl.BlockSpec((tl, HY_W), lambda i: (i, 0)),
            const((LANES, hid)), const((1, hid)), const((1, hid)),
            const((hid, hid)), const((1, hid)), const((hid, 2 * HY_ORDER * HY_W)),
        ],
        out_specs=pl.BlockSpec((2 * HY_ORDER, tl, HY_W), lambda i: (0, i, 0)),
        out_shape=jax.ShapeDtypeStruct((2 * HY_ORDER, length, HY_W), F32),
        compiler_params=_params("arbitrary"),
        name="hyena_filters",
    )(feats, window, w1p, b1.reshape(1, hid), freq.reshape(1, hid), w2, b2.reshape(1, hid), w3)


def _dft_tables(length):
    n = 2 * length
    n2 = HY_N2
    n1 = n // n2
    half = n1 // 2
    k1 = np.arange(half, dtype=np.float64) + 0.5
    idx = (n2 * np.arange(half)[None, None, :] + np.arange(n2)[:, None, None])
    ang = 2 * np.pi * k1[None, :, None] * idx / n
    fa = np.concatenate([np.cos(ang), -np.sin(ang)], axis=1)
    kk = np.arange(n2, dtype=np.float64)
    angb = 2 * np.pi * kk[:, None] * kk[None, :] / n2
    cr, sr = np.cos(angb), np.sin(angb)
    fb = np.block([[cr, sr], [-sr, cr]])
    fbi = np.block([[cr, -sr], [sr, cr]])
    idxo = (n2 * np.arange(half)[None, :, None] + np.arange(n2)[:, None, None])
    ango = 2 * np.pi * k1[None, None, :] * idxo / n
    ga = np.concatenate([np.cos(ango), -np.sin(ango)], axis=2) * (2.0 / n)
    f32 = lambda a: jnp.asarray(a, F32).astype(BF16)
    return f32(fa), f32(fb), f32(fbi), f32(ga)


def _seq_pitch(n2n):
    return n2n + SUBLANES


def _to_pitched(dst_ref, val, n2n):
    pitch = _seq_pitch(n2n)
    for i in range(val.shape[0] // n2n):
        dst_ref[i * pitch:i * pitch + n2n, :] = val[i * n2n:(i + 1) * n2n, :]


def _hy_forward_a(src_ref, a_ref, fa_ref):
    n2n, two_n1, half = fa_ref.shape
    zp, ap = _seq_pitch(n2n), _seq_pitch(two_n1)

    def body(n2, carry):
        zs = src_ref[pl.ds(n2, half, stride=zp), :].astype(BF16)
        a_ref[pl.ds(pl.multiple_of(n2 * ap, SUBLANES), two_n1), :] = _dot(fa_ref[n2], zs)
        return carry

    lax.fori_loop(0, n2n, body, 0, unroll=HY_UNROLL)


def _hy_kspec_kernel(hf_ref, hb_ref, fa_ref, fb_ref, o_ref, af_ref, ab_ref, hfp_ref, hbp_ref):
    n2n, two_n1, _ = fa_ref.shape
    n1 = two_n1 // 2
    ap = _seq_pitch(two_n1)
    _to_pitched(hfp_ref, hf_ref[0], n2n)
    _to_pitched(hbp_ref, hb_ref[0], n2n)
    _hy_forward_a(hfp_ref, af_ref, fa_ref)
    _hy_forward_a(hbp_ref, ab_ref, fa_ref)
    fb = fb_ref[...]

    def body(k1, carry):
        def spectrum(a_ref):
            ar = a_ref[pl.ds(k1, n2n, stride=ap), :]
            ai = a_ref[pl.ds(n1 + k1, n2n, stride=ap), :]
            return _dot(fb, jnp.concatenate([ar, ai], axis=0).astype(BF16))

        xf, xb = spectrum(af_ref), spectrum(ab_ref)
        o_ref[0, k1] = jnp.concatenate([xf[:n2n] + xb[:n2n], xf[n2n:] - xb[n2n:]], axis=0)
        return carry

    lax.fori_loop(0, n1, body, 0, unroll=HY_UNROLL)


def _hy_kspec(filt, tables):
    fa, fb, _, _ = tables
    _, length, _ = filt.shape
    n2n, two_n1, _ = fa.shape
    n1 = two_n1 // 2
    nh = HY_W // LANES
    return pl.pallas_call(
        _hy_kspec_kernel,
        grid=(HY_ORDER, nh),
        in_specs=[
            pl.BlockSpec((1, length, LANES), lambda o, h: (2 * o, 0, h)),
            pl.BlockSpec((1, length, LANES), lambda o, h: (2 * o + 1, 0, h)),
            pl.BlockSpec(fa.shape, lambda o, h: (0, 0, 0), pipeline_mode=pl.Buffered(1)),
            pl.BlockSpec(fb.shape, lambda o, h: (0, 0)),
        ],
        out_specs=pl.BlockSpec((1, n1, 2 * n2n, LANES), lambda o, h: (o, 0, 0, h)),
        out_shape=jax.ShapeDtypeStruct((HY_ORDER, n1, 2 * n2n, HY_W), F32),
        scratch_shapes=[pltpu.VMEM((n2n * _seq_pitch(two_n1), LANES), F32),
                        pltpu.VMEM((n2n * _seq_pitch(two_n1), LANES), F32),
                        pltpu.VMEM((length // n2n * _seq_pitch(n2n), LANES), F32),
                        pltpu.VMEM((length // n2n * _seq_pitch(n2n), LANES), F32)],
        compiler_params=_params("arbitrary", "arbitrary"),
        name="hyena_kspec",
    )(filt, filt, fa, fb)


def _hy_conv_kernel(z_ref, g_ref, fa_ref, fb_ref, fbi_ref, ga_ref, k_ref, bias_ref,
                    o_ref, a_ref, c_ref, zc_ref, gc_ref):
    n2n, two_n1, half = fa_ref.shape
    n1 = two_n1 // 2
    zp, ap = _seq_pitch(n2n), _seq_pitch(two_n1)
    _to_pitched(zc_ref, z_ref[0], n2n)
    _to_pitched(gc_ref, g_ref[0], n2n)
    _hy_forward_a(zc_ref, a_ref, fa_ref)
    fb = fb_ref[...]
    fbi = fbi_ref[...]

    def body_b(grp, carry):
        k1s = [grp * HY_UNROLL + i for i in range(HY_UNROLL)]
        sl = [(pl.ds(k1, n2n, stride=ap), pl.ds(n1 + k1, n2n, stride=ap)) for k1 in k1s]
        xs = [_dot(fb, jnp.concatenate([a_ref[re, :], a_ref[im, :]], axis=0).astype(BF16)) for re, im in sl]
        ys = []
        for k1, x in zip(k1s, xs):
            kk = k_ref[0, k1]
            xr, xi = x[:n2n], x[n2n:]
            kr, ki = kk[:n2n], kk[n2n:]
            ys.append(jnp.concatenate([xr * kr - xi * ki, xr * ki + xi * kr], axis=0).astype(BF16))
        for (re, im), y in zip(sl, ys):
            c = _dot(fbi, y)
            c_ref[re, :] = c[:n2n]
            c_ref[im, :] = c[n2n:]
        return carry

    lax.fori_loop(0, n1 // HY_UNROLL, body_b, 0)
    bias = bias_ref[0]

    def body_c(m2, carry):
        rows = pl.ds(m2, half, stride=zp)
        c = c_ref[pl.ds(pl.multiple_of(m2 * ap, SUBLANES), two_n1), :].astype(BF16)
        y = _dot(ga_ref[m2], c)
        a_ref[rows, :] = gc_ref[rows, :] * (y + zc_ref[rows, :] * bias)
        return carry

    lax.fori_loop(0, n2n, body_c, 0, unroll=HY_UNROLL)
    for i in range(half):
        o_ref[0, i * n2n:(i + 1) * n2n, :] = a_ref[i * zp:i * zp + n2n, :]


def _hy_conv(z_arr, z_blk, g_arr, g_blk, kspec, order, bias3, tables, n_lat):
    fa, fb, fbi, ga = tables
    bsz = z_arr.shape[0]
    n2n, two_n1, half = fa.shape
    n1 = two_n1 // 2
    nh = HY_W // LANES
    resident = lambda shape: pl.BlockSpec(shape, lambda h, b_: tuple(0 for _ in shape), pipeline_mode=pl.Buffered(1))
    return pl.pallas_call(
        _hy_conv_kernel,
        grid=(nh, bsz),
        in_specs=[
            pl.BlockSpec((1, n_lat, LANES), lambda h, b_: (b_, 0, z_blk + h)),
            pl.BlockSpec((1, n_lat, LANES), lambda h, b_: (b_, 0, g_blk + h)),
            resident(fa.shape), resident(fb.shape), resident(fbi.shape), resident(ga.shape),
            pl.BlockSpec((1, n1, 2 * n2n, LANES), lambda h, b_: (order, 0, 0, h), pipeline_mode=pl.Buffered(1)),
            pl.BlockSpec((1, 1, LANES), lambda h, b_: (order, 0, h)),
        ],
        out_specs=pl.BlockSpec((1, n_lat, LANES), lambda h, b_: (b_, 0, h)),
        out_shape=jax.ShapeDtypeStruct((bsz, n_lat, HY_W), F32),
        scratch_shapes=[pltpu.VMEM((n2n * _seq_pitch(two_n1), LANES), F32),
                        pltpu.VMEM((n2n * _seq_pitch(two_n1), LANES), F32),
                        pltpu.VMEM((half * _seq_pitch(n2n), LANES), F32),
                        pltpu.VMEM((half * _seq_pitch(n2n), LANES), F32)],
        compiler_params=_params("arbitrary", "arbitrary"),
        name="hyena_conv",
    )(z_arr, g_arr, fa, fb, fbi, ga, kspec, bias3)


def _hyena_ctx_kernel(u_ref, filt_ref, bias_ref, fd_ref, gd_ref, o_ref, *, n_ctx):
    fd = fd_ref[...]
    gd = gd_ref[...]
    nn = 2 * n_ctx
    u = u_ref[0]
    v, x1, x2 = u[:, :HY_W], u[:, HY_W:2 * HY_W], u[:, 2 * HY_W:]

    def conv(zin, order):
        hf = _dot(fd, filt_ref[2 * order].astype(BF16))
        hb = _dot(fd, filt_ref[2 * order + 1].astype(BF16))
        kr, ki = hf[:nn] + hb[:nn], hf[nn:] - hb[nn:]
        zz = _dot(fd, zin.astype(BF16))
        zr, zi = zz[:nn], zz[nn:]
        y = jnp.concatenate([zr * kr - zi * ki, zr * ki + zi * kr], axis=0).astype(BF16)
        return _dot(gd, y) + zin * bias_ref[order]

    y1 = x1 * conv(v, 0)
    o_ref[0] = x2 * conv(y1, 1)


def _hyena_ctx(hy, filt, bias, n_lat, n_ctx):
    bsz = hy.shape[0]
    nn = 2 * n_ctx
    k = np.arange(nn, dtype=np.float64)
    t = np.arange(n_ctx, dtype=np.float64)
    ang = 2 * np.pi * k[:, None] * t[None, :] / nn
    fd = jnp.asarray(np.concatenate([np.cos(ang), -np.sin(ang)], axis=0), F32).astype(BF16)
    gd = jnp.asarray(np.concatenate([np.cos(ang.T), -np.sin(ang.T)], axis=1) / nn, F32).astype(BF16)
    blk = n_lat // n_ctx
    assert blk * n_ctx == n_lat
    return pl.pallas_call(
        functools.partial(_hyena_ctx_kernel, n_ctx=n_ctx),
        grid=(bsz,),
        in_specs=[
            pl.BlockSpec((1, n_ctx, 3 * HY_W), lambda b: (b, blk, 0)),
            pl.BlockSpec((2 * HY_ORDER, n_ctx, HY_W), lambda b: (0, 0, 0)),
            pl.BlockSpec((HY_ORDER, 1, HY_W), lambda b: (0, 0, 0)),
            pl.BlockSpec((2 * nn, n_ctx), lambda b: (0, 0)),
            pl.BlockSpec((n_ctx, 2 * nn), lambda b: (0, 0)),
        ],
        out_specs=pl.BlockSpec((1, n_ctx, HY_W), lambda b: (b, 0, 0)),
        out_shape=jax.ShapeDtypeStruct((bsz, n_ctx, HY_W), F32),
        compiler_params=_params("arbitrary"),
        name="hyena_ctx",
    )(hy, filt, bias, fd, gd)


def _hyena_latent(hy, filt, bias, n_lat, tables):
    kspec = _hy_kspec(filt, tables)
    bias3 = bias.reshape(HY_ORDER, 1, HY_W)
    nb = HY_W // LANES
    y1 = _hy_conv(hy, 0, hy, nb, kspec, 0, bias3, tables, n_lat)
    return _hy_conv(y1, 0, hy, 2 * nb, kspec, 1, bias3, tables, n_lat)


def _outproj_kernel(x_ref, att_ref, ssd_ref, hy_ref, gt_ref, wa_ref, ws_ref, wh_ref, o_ref):
    mix = _dot(att_ref[0], wa_ref[0]) + _dot(ssd_ref[0], ws_ref[0]) + _dot(hy_ref[0].astype(BF16), wh_ref[0])
    o_ref[0] = x_ref[0] + gt_ref[0] * mix


def _outproj(x, att, ssd, ssd_row0, hy, mods, mod_row, w_out, layer, tm):
    bsz, n, _ = att.shape
    d = x.shape[-1]
    ssd_blk0 = ssd_row0 // tm
    assert ssd_blk0 * tm == ssd_row0 and n % tm == 0
    tok = lambda w: pl.BlockSpec((1, tm, w), lambda j, b: (b, j, 0))
    return pl.pallas_call(
        _outproj_kernel,
        grid=(n // tm, bsz),
        in_specs=[
            tok(d), tok(ATT_W),
            pl.BlockSpec((1, tm, SSD_W), lambda j, b: (b, ssd_blk0 + j, 0)),
            tok(HY_W),
            pl.BlockSpec((1, 1, d), lambda j, b: (mod_row(b), 0, 2)),
            pl.BlockSpec((1, ATT_W, d), lambda j, b: (layer, 0, 0)),
            pl.BlockSpec((1, SSD_W, d), lambda j, b: (layer, ATT_W // SSD_W, 0)),
            pl.BlockSpec((1, HY_W, d), lambda j, b: (layer, (ATT_W + SSD_W) // HY_W, 0)),
        ],
        out_specs=tok(d),
        out_shape=jax.ShapeDtypeStruct((bsz, n, d), F32),
        compiler_params=_params("arbitrary", "arbitrary"),
        name="outproj",
    )(x, att, ssd, hy, mods, w_out, w_out, w_out)


def _route(logits_t, bias_col):
    scores = jax.nn.sigmoid(logits_t)
    sel = scores + bias_col
    neg_inf = jnp.float32(-jnp.inf)
    rows = [sel[e:e + 1, :] for e in range(N_EXPERTS)]
    grp = []
    for g in range(N_GROUPS):
        r = rows[g * EXPERTS_PER_GROUP:(g + 1) * EXPERTS_PER_GROUP]
        top = functools.reduce(jnp.maximum, r)
        taken = None
        rest = []
        for ri in r:
            is_top = (ri == top) if taken is None else (ri == top) & jnp.logical_not(taken)
            rest.append(jnp.where(is_top, neg_inf, ri))
            taken = is_top if taken is None else taken | is_top
        grp.append(top + functools.reduce(jnp.maximum, rest))
    best = jnp.zeros(grp[0].shape, jnp.int32)
    cur = grp[0]
    for g in range(1, N_GROUPS):
        upd = grp[g] > cur
        best = jnp.where(upd, g, best)
        cur = jnp.where(upd, grp[g], cur)
    picked = []
    for e in range(N_EXPERTS):
        g, i = divmod(e, EXPERTS_PER_GROUP)
        rank = jnp.zeros(best.shape, jnp.int32)
        for j in range(EXPERTS_PER_GROUP):
            if j == i:
                continue
            other = rows[g * EXPERTS_PER_GROUP + j]
            ahead = (other > rows[e]) | ((other == rows[e]) & (j < i))
            rank = rank + ahead.astype(jnp.int32)
        keep = (best == g) & (rank < 2)
        picked.append(jnp.where(keep, scores[e:e + 1, :], 0.0))
    total = functools.reduce(lambda u, w: u + w, picked)
    return jnp.concatenate(picked, axis=0) / total, best


def _ffn_input(x_ref, sh_ref, sc_ref, g_ref):
    x = x_ref[...].reshape(MOE_TILE, D_MODEL)
    ms = jnp.mean(x * x, axis=-1, keepdims=True)
    t = x * lax.rsqrt(ms + EPS) * g_ref[...]
    return x, t * (1.0 + sc_ref[0]) + sh_ref[0]


def _route_kernel(x_ref, sh_ref, sc_ref, g_ref, wr_ref, rb_ref, rt_ref, srow_ref, tbl_ref):
    tm = MOE_TILE
    _, t = _ffn_input(x_ref, sh_ref, sc_ref, g_ref)
    logits = _dot_x3(t, wr_ref[...])
    comb_t, best = _route(logits.T[:N_EXPERTS, :], rb_ref[...])

    member = [jnp.where(best == g, 1.0, 0.0) for g in range(N_GROUPS)]
    comb4 = functools.reduce(
        lambda u, w: u + w,
        [member[g] * comb_t[g * EXPERTS_PER_GROUP:(g + 1) * EXPERTS_PER_GROUP, :] for g in range(N_GROUPS)])
    masks = jnp.concatenate(member + [jnp.zeros((8 - N_GROUPS, tm), F32)], axis=0)
    earlier = jnp.where(lax.broadcasted_iota(jnp.int32, (tm, tm), 0) < lax.broadcasted_iota(jnp.int32, (tm, tm), 1),
                        1.0, 0.0).astype(BF16)
    ranks = _dot(masks.astype(BF16), earlier)
    cnt = jnp.sum(masks, axis=1, keepdims=True)
    padded = jnp.ceil(cnt * (1.0 / MOE_CHUNK)) * MOE_CHUNK
    lane = lax.broadcasted_iota(jnp.int32, (1, LANES), 1).astype(F32) * MOE_CHUNK
    start = jnp.zeros((1, 1), F32)
    slot = jnp.zeros((1, tm), F32)
    gid = jnp.zeros((1, LANES), jnp.int32)
    for g in range(N_GROUPS):
        slot = slot + member[g] * (start + ranks[g:g + 1, :])
        start = start + padded[g:g + 1, :]
        gid = gid + jnp.where(lane >= start, 1, 0)
    tbl_ref[0] = gid
    srow_ref[0] = slot.astype(jnp.int32)
    rt_t = jnp.concatenate([comb4, slot, jnp.zeros((LANES - EXPERTS_PER_GROUP - 1, tm), F32)], axis=0)
    rt_ref[...] = rt_t.T


def _experts_kernel(tbl_ref, x_ref, sh_ref, sc_ref, gt_ref, g_ref, rt_ref, srow_ref, wg_ref, wu_ref, wd_ref, gf_ref,
                    *rest, final, tile_of, n_grid):
    o_ref, xp_ref, yp_ref = rest[-3:]
    tm, ch = MOE_TILE, MOE_CHUNK
    n_slots = MOE_CHUNKS * ch
    tile = tile_of(*[pl.program_id(a) for a in range(n_grid)])
    x, t = _ffn_input(x_ref, sh_ref, sc_ref, g_ref)
    rt = rt_ref[...]
    slot_col = rt[:, EXPERTS_PER_GROUP:EXPERTS_PER_GROUP + 1].astype(jnp.int32)
    gather = jnp.where(lax.broadcasted_iota(jnp.int32, (n_slots, tm), 0) == srow_ref[0], 1.0, 0.0).astype(BF16)
    scatter = jnp.where(lax.broadcasted_iota(jnp.int32, (tm, n_slots), 1) == slot_col, 1.0, 0.0).astype(BF16)
    xp_ref[...] = _dot(gather, t.astype(BF16)).astype(BF16)
    w_slot = _dot3_right(gather, rt)

    def run_chunk(rows, grp):
        xc = xp_ref[rows, :]
        e0 = grp * EXPERTS_PER_GROUP
        gu_next = _dot(xc, wg_ref[0, e0]), _dot(xc, wu_ref[0, e0])
        acc = jnp.zeros((ch, D_MODEL), F32)
        for j in range(EXPERTS_PER_GROUP):
            gate, up = gu_next
            if j + 1 < EXPERTS_PER_GROUP:
                gu_next = _dot(xc, wg_ref[0, e0 + j + 1]), _dot(xc, wu_ref[0, e0 + j + 1])
            hid = _silu(gate) * up * w_slot[rows, j:j + 1]
            acc = acc + _dot(hid.astype(BF16), wd_ref[0, e0 + j])
        yp_ref[rows, :] = acc.astype(BF16)

    for c in range(MOE_CHUNKS):
        rows = slice(c * ch, (c + 1) * ch)
        grp = tbl_ref[tile, c]
        pl.when(grp < N_GROUPS)(functools.partial(run_chunk, rows, grp))

        @pl.when(grp >= N_GROUPS)
        def _():
            yp_ref[rows, :] = jnp.zeros((ch, D_MODEL), BF16)

    y = x + gt_ref[0] * _dot(scatter, yp_ref[...])
    if final:
        y = y * lax.rsqrt(jnp.mean(y * y, axis=-1, keepdims=True) + EPS) * gf_ref[...]
    o_ref[...] = y.reshape(o_ref.shape)


def _moe_tiles(x, mods, mod_row, g_ffn, w_router, router_bias, wgu, wd, layer, g_final, final, out_rows,
               grid, x_block, x_index, out_index, tile_of, n_tiles):
    bsz, _, d = x.shape
    tm = MOE_TILE
    ng = len(grid)
    const = lambda shape: pl.BlockSpec(shape, lambda *a: tuple(0 for _ in shape))
    mod = lambda col: pl.BlockSpec((1, 1, d), lambda *a: (mod_row(*a[:ng]), 0, col))
    xspec = pl.BlockSpec(x_block, lambda *a: x_index(*a[:ng]))
    rt, srow, tbl = pl.pallas_call(
        _route_kernel,
        grid=grid,
        in_specs=[xspec, mod(3), mod(4), const((1, d)), const((d, LANES)), const((N_EXPERTS, 1))],
        out_specs=[
            pl.BlockSpec((tm, LANES), lambda *a: (tile_of(*a), 0)),
            pl.BlockSpec((1, 1, tm), lambda *a: (tile_of(*a), 0, 0)),
            pl.BlockSpec((1, 1, LANES), lambda *a: (tile_of(*a), 0, 0)),
        ],
        out_shape=[
            jax.ShapeDtypeStruct((n_tiles * tm, LANES), F32),
            jax.ShapeDtypeStruct((n_tiles, 1, tm), jnp.int32),
            jax.ShapeDtypeStruct((n_tiles, 1, LANES), jnp.int32),
        ],
        compiler_params=_params(*["arbitrary"] * ng),
        name="moe_route",
    )(x, mods, mods, g_ffn, w_router, router_bias)

    resident = lambda shape: pl.BlockSpec(shape, lambda *a: (layer,) + tuple(0 for _ in shape[1:]),
                                          pipeline_mode=pl.Buffered(1))
    in_specs = [
        xspec, mod(3), mod(4), mod(5), const((1, d)),
        pl.BlockSpec((tm, LANES), lambda *a: (tile_of(*a[:ng]), 0)),
        pl.BlockSpec((1, 1, tm), lambda *a: (tile_of(*a[:ng]), 0, 0)),
        resident((1, N_EXPERTS, d, D_FF)),
        resident((1, N_EXPERTS, d, D_FF)),
        resident((1, N_EXPERTS, D_FF, d)),
        const((1, d)),
    ]
    args = [tbl.reshape(n_tiles, LANES), x, mods, mods, mods, g_ffn, rt, srow, *wgu, wd, g_final]
    return pl.pallas_call(
        functools.partial(_experts_kernel, final=final, tile_of=tile_of, n_grid=ng),
        grid_spec=pltpu.PrefetchScalarGridSpec(
            num_scalar_prefetch=1,
            grid=grid,
            in_specs=in_specs,
            out_specs=pl.BlockSpec(x_block, lambda *a: out_index(*a[:ng])),
            scratch_shapes=[pltpu.VMEM((MOE_CHUNKS * MOE_CHUNK, d), BF16), pltpu.VMEM((MOE_CHUNKS * MOE_CHUNK, d), BF16)],
        ),
        out_shape=jax.ShapeDtypeStruct((bsz, out_rows, d), F32),
        compiler_params=_params(*["arbitrary"] * ng),
        name="moe_experts",
    )(*args)


def _moe(x_lat, x_ctx, mods, g_ffn, w_router, router_bias, wgu, wd, layer, g_final, final):
    bsz, n_lat, d = x_lat.shape
    tm = MOE_TILE
    per_b = n_lat // tm
    common = (g_ffn, w_router, router_bias, wgu, wd, layer, g_final, final)
    lat_index = lambda b, j: (b, j, 0)
    out_lat = _moe_tiles(x_lat, mods, lambda b, j: b, *common, n_lat,
                         (bsz, per_b), (1, tm, d), lat_index, lat_index, lambda b, j: b * per_b + j, bsz * per_b)
    if x_ctx is None:
        return out_lat, None
    n_ctx = x_ctx.shape[1]
    nb = tm // n_ctx
    ctx_index = lambda i: (i, 0, 0)
    out_ctx = _moe_tiles(x_ctx, mods, lambda i: bsz, *common, n_ctx,
                         (bsz // nb,), (nb, n_ctx, d), ctx_index, ctx_index, lambda i: i, bsz // nb)
    return out_lat, out_ctx


def _block_ones(width, block):
    idx = np.arange(width) // block
    return jnp.asarray(idx[:, None] == idx[None, :], F32).astype(BF16)


def _rope_tables(n_lat, n_ctx):
    rows = n_lat // GRID_W
    row = np.repeat(np.arange(rows), GRID_W).astype(np.float64)
    col = np.tile(np.arange(GRID_W), rows).astype(np.float64)
    inv = ROPE_THETA ** (-np.arange(0, ROPE_AXIS_DIM, 2, dtype=np.float64) / ROPE_AXIS_DIM)
    ang = np.concatenate([row[:, None] * inv, col[:, None] * inv], axis=-1)
    ang = np.concatenate([ang, np.zeros((n_ctx, ang.shape[1]))], axis=0)
    cos = np.concatenate([np.cos(ang), np.cos(ang)], axis=-1)
    sin = np.concatenate([-np.sin(ang), np.sin(ang)], axis=-1)
    scale = Q_SCALE
    cs = np.tile(cos, (1, ATT_HEADS)) * scale
    sn = np.tile(sin, (1, ATT_HEADS)) * scale
    return jnp.asarray(cs, F32), jnp.asarray(sn, F32)


def kernel(x, c, ctx, c_ctx, w_mod, b_mod, g_mix, g_ffn, w_in, q_norm, k_norm, ssd_conv_w, ssd_conv_b,
           ssd_dt_bias, ssd_a_log, ssd_d, ssd_norm, hy_conv_w, hy_conv_b, hy_w1, hy_b1, hy_freq, hy_w2, hy_b2,
           hy_w3, hy_bias, w_out, w_router, router_bias, w_gate, w_up, w_down, g_final):
    bsz, n_lat, d = x.shape
    n_ctx = ctx.shape[1]
    depth = w_mod.shape[0]
    t = n_lat + n_ctx
    tm = TOKEN_TILE
    n_lat_tiles = n_lat // tm
    n_tiles = t // tm
    assert n_ctx == tm and n_lat % (HY_N2 * 8) == 0 and bsz < MOD_ROWS

    cvec = jnp.concatenate([c, c_ctx[None], jnp.zeros((MOD_ROWS - bsz - 1, d), F32)], axis=0)
    mods_all = _adaln(cvec, w_mod, b_mod)

    cs, sn = _rope_tables(n_lat, n_ctx)
    bd_head = _block_ones(ATT_W, HEAD_DIM)
    bd_ssd = _block_ones(SSD_W, SSD_W // SSD_GROUPS)
    tables = _dft_tables(n_lat)
    pad_row = lambda v: jnp.pad(v.reshape(1, -1), ((0, 0), (0, LANES - v.size)))
    w_out_b = w_out.astype(BF16)
    wgu_b = (w_gate.astype(BF16), w_up.astype(BF16))
    wd_b = w_down.astype(BF16)

    x_lat, x_ctx, ctx_blk = x, ctx, 0
    for i in range(depth):
        last = i == depth - 1
        mods = mods_all[i].reshape(MOD_ROWS, 1, 6 * d)
        q, kt, v, z, xbc, dt_raw, hy = _inproj(
            x_lat, x_ctx, ctx_blk, mods, g_mix[i].reshape(1, d), w_in, i,
            jnp.tile(q_norm[i], ATT_HEADS).reshape(1, ATT_W), jnp.tile(k_norm[i], ATT_KV_HEADS).reshape(1, KV_W),
            cs, sn, bd_head, jnp.concatenate([ssd_conv_w[i], hy_conv_w[i]], axis=-1),
            jnp.concatenate([ssd_conv_b[i], hy_conv_b[i]]).reshape(1, -1), n_lat_tiles)

        att = _attention(q, kt, v, 0, n_lat, 0, t, ATT_TILE)

        ssd = _ssd(xbc, z, dt_raw, pad_row(ssd_dt_bias[i]), pad_row(ssd_a_log[i]),
                   jnp.repeat(ssd_d[i], SSD_HEAD_DIM).reshape(1, SSD_W), ssd_norm[i].reshape(1, SSD_W),
                   bd_ssd, n_lat)

        hyp = (hy_w1[i], hy_b1[i], hy_freq[i], hy_w2[i], hy_b2[i], hy_w3[i])
        hy_l = _hyena_latent(hy, _hyena_filters(n_lat, *hyp), hy_bias[i], n_lat, tables)

        mid_lat = _outproj(x_lat, att, ssd, 0, hy_l, mods, lambda b: b, w_out_b, i, OUT_TILE)
        mid_ctx = None
        if not last:
            att_c = _attention(q, kt, v, n_lat, n_ctx, n_lat, n_ctx, n_ctx)
            hy_c = _hyena_ctx(hy, _hyena_filters(n_ctx, *hyp), hy_bias[i].reshape(HY_ORDER, 1, HY_W), n_lat, n_ctx)
            mid_ctx = _outproj(x_ctx, att_c, ssd, n_lat, hy_c, mods, lambda b: bsz, w_out_b, i, n_ctx)

        x_lat, x_ctx = _moe(mid_lat, mid_ctx, mods, g_ffn[i].reshape(1, d),
                            jnp.pad(w_router, ((0, 0), (0, LANES - N_EXPERTS))), router_bias.reshape(N_EXPERTS, 1),
                            wgu_b, wd_b, i, g_final.reshape(1, d), last)
    return x_lat
```

```python
import functools
import math

import jax
import jax.numpy as jnp
import numpy as np
from jax import lax
from jax.experimental import pallas as pl
from jax.experimental.pallas import tpu as pltpu

F32 = jnp.float32
BF16 = jnp.bfloat16

D_MODEL = 1024
GRID_W = 64
EPS = 1e-6

ATT_HEADS = 6
ATT_KV_HEADS = 2
HEAD_DIM = 64
ATT_W = ATT_HEADS * HEAD_DIM
KV_W = ATT_KV_HEADS * HEAD_DIM
ROPE_AXIS_DIM = HEAD_DIM // 2
ROPE_THETA = 10000.0
Q_SCALE = HEAD_DIM ** -0.5 * math.log2(math.e)

SSD_HEADS = 6
SSD_HEAD_DIM = 64
SSD_W = SSD_HEADS * SSD_HEAD_DIM
SSD_GROUPS = 2
SSD_STATE = 64
SSD_CHUNK = 128
SSD_CONV_CH = SSD_W + 2 * SSD_GROUPS * SSD_STATE
SSD_HEADS_PER_GROUP = SSD_HEADS // SSD_GROUPS
SSD_BLOCK = 2

HY_W = 256
HY_ORDER = 2
HY_BANDS = 16
HY_POS_DIM = 1 + 2 * HY_BANDS
HY_FILTER_HID = 64
HY_FAST_DECAY = 0.3
HY_SLOW_DECAY = 1.5
HY_TARGET = 1e-2
HY_N2 = 64
HY_UNROLL = 8

MIX_W = ATT_W + SSD_W + HY_W
N_EXPERTS = 16
N_GROUPS = 4
EXPERTS_PER_GROUP = N_EXPERTS // N_GROUPS
D_FF = 256

LANES = 128
SUBLANES = 8
TOKEN_TILE = 256
ATT_TILE = 1024
ATT_SUB = 512
OUT_TILE = 512
MOE_TILE = 512
MOE_CHUNK = 128
MOE_CHUNKS = MOE_TILE // MOE_CHUNK + N_GROUPS - 1
MOD_ROWS = 8
VMEM_LIMIT = 56 * 1024 * 1024

COL_Q = 0
COL_K = COL_Q + ATT_W
COL_V = COL_K + KV_W
COL_XBC = COL_V + KV_W
COL_HY = COL_XBC + SSD_CONV_CH
COL_Z = COL_HY + 3 * HY_W
COL_DT = COL_Z + SSD_W
COL_END = COL_DT + LANES


def _params(*sem):
    return pltpu.CompilerParams(dimension_semantics=sem, vmem_limit_bytes=VMEM_LIMIT)


def _silu(x):
    return x * jax.nn.sigmoid(x)


def _softplus(x):
    return jnp.maximum(x, 0.0) + jnp.log1p(jnp.exp(-jnp.abs(x)))


def _split3(x):
    hi = x.astype(BF16)
    r1 = x - hi.astype(F32)
    mid = r1.astype(BF16)
    lo = (r1 - mid.astype(F32)).astype(BF16)
    return hi, mid, lo


def _dot(a, b):
    return jnp.dot(a, b, preferred_element_type=F32)


def _dot_x3(a, b):
    a_hi, a_mid, _ = _split3(a)
    b_hi, b_mid, _ = _split3(b)
    return _dot(a_hi, b_hi) + (_dot(a_hi, b_mid) + _dot(a_mid, b_hi))


def _dot3_right(m_bf16, x):
    hi, mid, lo = _split3(x)
    return _dot(m_bf16, hi) + _dot(m_bf16, mid) + _dot(m_bf16, lo)


def _dot3_left(x, m_bf16):
    hi, mid, lo = _split3(x)
    return _dot(hi, m_bf16) + _dot(mid, m_bf16) + _dot(lo, m_bf16)


def _adaln_kernel(c_ref, w_ref, b_ref, o_ref):
    s = _silu(c_ref[...]).astype(BF16)
    o_ref[0] = _dot(s, w_ref[0].astype(BF16)) + b_ref[0]


def _adaln(cvec, w_mod, b_mod):
    depth, d, n = w_mod.shape
    bn = n // 4
    return pl.pallas_call(
        _adaln_kernel,
        grid=(depth, n // bn),
        in_specs=[
            pl.BlockSpec((MOD_ROWS, d), lambda i, j: (0, 0)),
            pl.BlockSpec((1, d, bn), lambda i, j: (i, 0, j)),
            pl.BlockSpec((1, 1, bn), lambda i, j: (i, 0, j)),
        ],
        out_specs=pl.BlockSpec((1, MOD_ROWS, bn), lambda i, j: (i, 0, j)),
        out_shape=jax.ShapeDtypeStruct((depth, MOD_ROWS, n), F32),
        compiler_params=_params("arbitrary", "arbitrary"),
        name="adaln",
    )(cvec, w_mod, b_mod.reshape(depth, 1, n))


def _head_rope(xn, cs, sn):
    width = xn.shape[-1]
    lane = lax.broadcasted_iota(jnp.int32, xn.shape, 1)
    first_half = (lane % HEAD_DIM) < (HEAD_DIM // 2)
    partner = jnp.where(first_half,
                        pltpu.roll(xn, width - HEAD_DIM // 2, 1),
                        pltpu.roll(xn, HEAD_DIM // 2, 1))
    return xn * cs + partner * sn


def _stream_specs(n_lat_tiles, ctx_blk, d):
    lat = pl.BlockSpec((1, TOKEN_TILE, d), lambda j, b: (b, jnp.minimum(j, n_lat_tiles - 1), 0))
    ctx = pl.BlockSpec((1, TOKEN_TILE, d), lambda j, b: (jnp.where(j >= n_lat_tiles, b, 0), ctx_blk, 0))
    return lat, ctx


def _stream_tile(xl_ref, xc_ref, n_lat_tiles):
    return jnp.where(pl.program_id(0) >= n_lat_tiles, xc_ref[0], xl_ref[0])


def _pack_w_in(w_ref, wb_ref):
    src_z = ATT_W + 2 * KV_W
    src_xbc = src_z + SSD_W
    src_dt = src_xbc + SSD_CONV_CH
    n_dt = 2 * SSD_HEADS
    wb_ref[:, COL_Q:COL_XBC] = w_ref[0, :, :src_z].astype(BF16)
    wb_ref[:, COL_XBC:COL_HY] = w_ref[0, :, src_xbc:src_dt].astype(BF16)
    tail = w_ref[0, :, src_dt:]
    wb_ref[:, COL_HY:COL_Z] = tail[:, n_dt:].astype(BF16)
    wb_ref[:, COL_Z:COL_DT] = w_ref[0, :, src_z:src_xbc].astype(BF16)
    head = tail[:, :LANES]
    lane = lax.broadcasted_iota(jnp.int32, head.shape, 1)
    wb_ref[:, COL_DT:COL_END] = jnp.where(lane < n_dt, head, 0.0).astype(BF16)


def _inproj_kernel(xl_ref, xc_ref, xp_ref, xn_ref, sh_ref, sc_ref, g_ref, w_ref, gq_ref, gk_ref, cs_ref, sn_ref,
                   bd_ref, cw_ref, cb_ref, qt_ref, k_ref, vt_ref, z_ref, xbc_ref, dt_ref, hy_ref, wb_ref,
                   *, n_lat_tiles):
    j = pl.program_id(0)

    @pl.when((j == 0) & (pl.program_id(1) == 0))
    def _():
        _pack_w_in(w_ref, wb_ref)

    tm, halo = TOKEN_TILE, SUBLANES
    x = jnp.concatenate([xp_ref[0], _stream_tile(xl_ref, xc_ref, n_lat_tiles), xn_ref[0]], axis=0)
    ms = jnp.mean(x * x, axis=-1, keepdims=True)
    h = x * lax.rsqrt(ms + EPS) * g_ref[...]
    h = h * (1.0 + sc_ref[0]) + sh_ref[0]
    hb = h[halo:halo + tm].astype(BF16)
    u = _dot(h.astype(BF16), wb_ref[:, COL_XBC:COL_Z])
    qkv = _dot(hb, wb_ref[:, COL_Q:COL_XBC])
    bd = bd_ref[...]
    q, k = qkv[:, COL_Q:COL_K], qkv[:, COL_K:COL_V]
    ms_q = _dot((q * q).astype(BF16), bd) * (1.0 / HEAD_DIM)
    ms_k = _dot((k * k).astype(BF16), bd[:KV_W, :KV_W]) * (1.0 / HEAD_DIM)
    zdt = _dot(hb, wb_ref[:, COL_Z:COL_END])

    cs = cs_ref[...]
    sn = sn_ref[...]
    qt_ref[0] = _head_rope(q * lax.rsqrt(ms_q + EPS) * gq_ref[...], cs, sn).T.astype(qt_ref.dtype)
    k_ref[0] = _head_rope(k * lax.rsqrt(ms_k + EPS) * gk_ref[...],
                          cs[:, :KV_W] * (1.0 / Q_SCALE), sn[:, :KV_W] * (1.0 / Q_SCALE)).astype(k_ref.dtype)
    vt_ref[0] = qkv[:, COL_V:COL_XBC].T.astype(vt_ref.dtype)

    has_prev = (j >= 1) & (j < n_lat_tiles)
    has_next = j < n_lat_tiles - 1
    own = slice(halo, halo + tm)
    u = jnp.concatenate([jnp.where(has_prev, u[:halo], 0.0), u[own], jnp.where(has_next, u[halo + tm:], 0.0)], axis=0)
    cw = cw_ref[...]
    y = (pltpu.roll(u, 1, 0)[own] * cw[0:1] + u[own] * cw[1:2] + pltpu.roll(u, u.shape[0] - 1, 0)[own] * cw[2:3]
         + cb_ref[...])
    xbc_ref[0] = _silu(y[:, :COL_HY - COL_XBC])
    hy_ref[0] = y[:, COL_HY - COL_XBC:]
    z_ref[0] = zdt[:, :COL_DT - COL_Z]
    dt_ref[0] = zdt[:, COL_DT - COL_Z:]


def _inproj(x_lat, x_ctx, ctx_blk, mods, g_mix, w_in, layer, gq, gk, cs, sn, bd, conv_w, conv_b, n_lat_tiles):
    bsz, _, d = x_lat.shape
    tm = TOKEN_TILE
    nt = n_lat_tiles + 1
    t = nt * tm
    per_tile = tm // SUBLANES
    n_conv = COL_Z - COL_XBC

    def mod_row(j, b):
        return jnp.where(j >= n_lat_tiles, bsz, b)

    lat_j = lambda j: jnp.minimum(j, n_lat_tiles - 1)
    prev_spec = pl.BlockSpec((1, SUBLANES, d), lambda j, b: (b, jnp.maximum(lat_j(j) * per_tile - 1, 0), 0))
    next_spec = pl.BlockSpec((1, SUBLANES, d),
                             lambda j, b: (b, jnp.minimum((lat_j(j) + 1) * per_tile, n_lat_tiles * per_tile - 1), 0))

    tok = lambda w: pl.BlockSpec((1, tm, w), lambda j, b: (b, j, 0))
    const = lambda shape: pl.BlockSpec(shape, lambda j, b: tuple(0 for _ in shape))
    outs = pl.pallas_call(
        functools.partial(_inproj_kernel, n_lat_tiles=n_lat_tiles),
        grid=(nt, bsz),
        in_specs=[
            *_stream_specs(n_lat_tiles, ctx_blk, d),
            prev_spec, next_spec,
            pl.BlockSpec((1, 1, d), lambda j, b: (mod_row(j, b), 0, 0)),
            pl.BlockSpec((1, 1, d), lambda j, b: (mod_row(j, b), 0, 1)),
            const((1, d)),
            pl.BlockSpec((1,) + w_in.shape[1:], lambda j, b: (layer, 0, 0), pipeline_mode=pl.Buffered(1)),
            const((1, ATT_W)),
            const((1, KV_W)),
            pl.BlockSpec((tm, ATT_W), lambda j, b: (j, 0)),
            pl.BlockSpec((tm, ATT_W), lambda j, b: (j, 0)),
            const((ATT_W, ATT_W)),
            const((3, n_conv)),
            const((1, n_conv)),
        ],
        out_specs=[
            pl.BlockSpec((1, ATT_W, tm), lambda j, b: (b, 0, j)),
            tok(KV_W),
            pl.BlockSpec((1, KV_W, tm), lambda j, b: (b, 0, j)),
            tok(SSD_W),
            tok(SSD_CONV_CH),
            tok(LANES),
            tok(3 * HY_W),
        ],
        out_shape=[
            jax.ShapeDtypeStruct((bsz, ATT_W, t), BF16),
            jax.ShapeDtypeStruct((bsz, t, KV_W), BF16),
            jax.ShapeDtypeStruct((bsz, KV_W, t), BF16),
            jax.ShapeDtypeStruct((bsz, t, SSD_W), F32),
            jax.ShapeDtypeStruct((bsz, t, SSD_CONV_CH), F32),
            jax.ShapeDtypeStruct((bsz, t, LANES), F32),
            jax.ShapeDtypeStruct((bsz, t, 3 * HY_W), F32),
        ],
        scratch_shapes=[pltpu.VMEM((d, COL_END), BF16)],
        compiler_params=_params("arbitrary", "arbitrary"),
        name="inproj",
    )(x_lat, x_ctx, x_lat, x_lat, mods, mods, g_mix, w_in, gq, gk, cs, sn, bd, conv_w, conv_b)
    return outs


def _attn_kernel(qt_ref, k_ref, vt_ref, o_ref, st_ref, pt_ref):
    k = k_ref[0]
    vt = vt_ref[0]
    rep = ATT_HEADS // ATT_KV_HEADS
    tq = qt_ref.shape[2]
    sub = min(tq, ATT_SUB)
    ones = jnp.ones((2 * SUBLANES, vt.shape[1]), vt.dtype)
    vtg = [jnp.concatenate([vt[g * HEAD_DIM:(g + 1) * HEAD_DIM, :], ones], axis=0) for g in range(ATT_KV_HEADS)]
    units = [(c0, hd) for c0 in range(0, tq, sub) for hd in range(ATT_HEADS)]

    def scores(u):
        c0, hd = units[u]
        qh = qt_ref[0, hd * HEAD_DIM:(hd + 1) * HEAD_DIM, c0:c0 + sub]
        zero = jnp.zeros_like(qh)
        w = jnp.concatenate([qh, zero] if hd < rep else [zero, qh], axis=0)
        st = _dot(k, w)
        st_ref[u % 2] = st
        return jnp.max(st, axis=0, keepdims=True)

    m_next = scores(0)
    outs = []
    for u, (c0, hd) in enumerate(units):
        m = m_next
        if u + 1 < len(units):
            m_next = scores(u + 1)
        pt_ref[u % 2] = jnp.exp2(st_ref[u % 2] - m).astype(BF16)
        ot = _dot(vtg[hd // rep], pt_ref[u % 2])
        outs.append(ot[:HEAD_DIM] / ot[HEAD_DIM:HEAD_DIM + 1])
        if hd == ATT_HEADS - 1:
            o_ref[0, c0:c0 + sub, :] = jnp.concatenate(outs, axis=0).T.astype(o_ref.dtype)
            outs = []


def _attention(qt, k, vt, q_row0, n_q, k_row0, n_k, tq):
    bsz = qt.shape[0]
    kblk = k_row0 // n_k
    q_tile0 = q_row0 // tq
    n_q_tiles = n_q // tq
    assert kblk * n_k == k_row0 and q_tile0 * tq == q_row0 and n_q_tiles * tq == n_q
    assert ATT_KV_HEADS == 2
    return pl.pallas_call(
        _attn_kernel,
        grid=(bsz, n_q_tiles),
        in_specs=[
            pl.BlockSpec((1, ATT_W, tq), lambda b, j: (b, 0, q_tile0 + j)),
            pl.BlockSpec((1, n_k, KV_W), lambda b, j: (b, kblk, 0)),
            pl.BlockSpec((1, KV_W, n_k), lambda b, j: (b, 0, kblk)),
        ],
        out_specs=pl.BlockSpec((1, tq, ATT_W), lambda b, j: (b, j, 0)),
        out_shape=jax.ShapeDtypeStruct((bsz, n_q_tiles * tq, ATT_W), BF16),
        scratch_shapes=[pltpu.VMEM((2, n_k, min(tq, ATT_SUB)), F32), pltpu.VMEM((2, n_k, min(tq, ATT_SUB)), BF16)],
        compiler_params=_params("arbitrary", "arbitrary"),
        name="attention",
    )(qt, k, vt)


def _ssd_kernel(xbc_ref, z_ref, dt_ref, dtb_ref, alog_ref, dsk_ref, nw_ref, bd_ref,
                o_ref, hf_ref, hb_ref, hbe_ref, *, n_lat_blocks):
    q = SSD_CHUNK
    nh = SSD_HEADS
    phase = pl.program_id(1)
    step = pl.program_id(2)
    block = jnp.where(phase == 0, n_lat_blocks - step, jnp.where(step < 1, n_lat_blocks, step - 1))

    lane = lax.broadcasted_iota(jnp.int32, (1, LANES), 1)
    a_row = jnp.where(lane < 2 * nh, -jnp.exp(alog_ref[...]), 0.0)
    tt = lax.broadcasted_iota(jnp.int32, (q, q), 0)
    ss = lax.broadcasted_iota(jnp.int32, (q, q), 1)
    lower = (ss <= tt)
    upper = (ss >= tt)
    lmat = jnp.where(lower, 1.0, 0.0).astype(BF16)
    umat = jnp.where(upper, 1.0, 0.0).astype(BF16)

    @pl.when(step == 0)
    def _():
        hf_ref[...] = jnp.zeros_like(hf_ref)
        hb_ref[...] = jnp.zeros_like(hb_ref)

    def load_chunk(ci):
        rows = slice(ci * q, (ci + 1) * q)
        xbc = xbc_ref[0, rows, :]
        dt = _softplus(dt_ref[0, rows, :] + dtb_ref[...])
        return rows, xbc, dt, dt * a_row

    def heads_x(x):
        return [x[:, hd * SSD_HEAD_DIM:(hd + 1) * SSD_HEAD_DIM].astype(BF16) for hd in range(nh)]

    def backward_prepare(ci):
        _, xbc, dt, a = load_chunk(ci)
        bt = xbc[:, SSD_W:SSD_W + SSD_GROUPS * SSD_STATE].T
        dt_t = dt.T
        suf_t = _dot3_left(a.T, lmat)
        terms = []
        for hd in range(nh):
            g = hd // SSD_HEADS_PER_GROUP
            row_b = suf_t[nh + hd:nh + hd + 1, :]
            total = row_b[:, 0:1]
            w_t = jnp.exp(total - row_b) * dt_t[nh + hd:nh + hd + 1, :]
            terms.append((jnp.exp(total), (bt[g * SSD_STATE:(g + 1) * SSD_STATE, :] * w_t).astype(BF16)))
        return terms, heads_x(xbc[:, :SSD_W])

    def backward_state(ci, prepared):
        terms, xh = prepared
        chunk = block * SSD_BLOCK + ci
        for hd in range(nh):
            decay, bw = terms[hd]
            prev = hb_ref[hd]
            hbe_ref[chunk, hd] = prev
            hb_ref[hd] = prev * decay + _dot(bw, xh[hd])

    def forward_prepare(ci):
        rows, xbc, dt, a = load_chunk(ci)
        x = xbc[:, :SSD_W]
        bt = xbc[:, SSD_W:SSD_W + SSD_GROUPS * SSD_STATE].T
        cmat = xbc[:, SSD_W + SSD_GROUPS * SSD_STATE:]
        a_t = a.T
        pre = _dot3_right(lmat, a)
        pre_t = _dot3_left(a_t, umat)
        cgs = [cmat[:, g * SSD_STATE:(g + 1) * SSD_STATE].astype(BF16) for g in range(SSD_GROUPS)]
        btgs = [bt[g * SSD_STATE:(g + 1) * SSD_STATE, :] for g in range(SSD_GROUPS)]
        cbs = [_dot(cgs[g], btgs[g].astype(BF16)) for g in range(SSD_GROUPS)]
        return dict(rows=rows, x=x, xh=heads_x(x), a=a, a_t=a_t, dt_t=dt.T, pre=pre, pre_t=pre_t,
                    cgs=cgs, btgs=btgs, cbs=cbs)

    def forward_mix(p):
        pre, pre_t, dt_t = p["pre"], p["pre_t"], p["dt_t"]
        suf = pre[q - 1:q, :] - pre + p["a"]
        suf_t = pre_t[:, q - 1:q] - pre_t + p["a_t"]
        neg_inf = jnp.float32(-jnp.inf)
        p["y_in"], p["scale"], p["upd"] = [], [], []
        for hd in range(nh):
            g = hd // SSD_HEADS_PER_GROUP
            colf = pre[:, hd:hd + 1]
            rowf = pre_t[hd:hd + 1, :]
            colb = suf[:, nh + hd:nh + hd + 1]
            rowb = suf_t[nh + hd:nh + hd + 1, :]
            wf = jnp.exp(jnp.where(lower, colf - rowf, neg_inf)) * dt_t[hd:hd + 1, :]
            wb = jnp.exp(jnp.where(upper, colb - rowb, neg_inf)) * dt_t[nh + hd:nh + hd + 1, :]
            p["y_in"].append(_dot((p["cbs"][g] * (wf + wb)).astype(BF16), p["xh"][hd]))
            p["scale"].append((jnp.exp(colf), jnp.exp(colb)))
            total = rowf[:, q - 1:q]
            w_t = jnp.exp(total - rowf) * dt_t[hd:hd + 1, :]
            p["upd"].append((jnp.exp(total), (p["btgs"][g] * w_t).astype(BF16)))

    def forward_state(ci, p):
        chunk = block * SSD_BLOCK + ci
        rows, x = p["rows"], p["x"]
        ys = []
        for hd in range(nh):
            cg = p["cgs"][hd // SSD_HEADS_PER_GROUP]
            hf = hf_ref[hd]
            ef, eb = p["scale"][hd]
            y = p["y_in"][hd] + _dot(cg, hf.astype(BF16)) * ef + _dot(cg, hbe_ref[chunk, hd].astype(BF16)) * eb
            decay, bw = p["upd"][hd]
            hf_ref[hd] = hf * decay + _dot(bw, p["xh"][hd])
            ys.append(y)
        y = jnp.concatenate(ys, axis=-1) + x * dsk_ref[...]
        gz = y * _silu(z_ref[0, rows, :])
        ms = _dot((gz * gz).astype(BF16), bd_ref[...]) * (1.0 / (SSD_W // SSD_GROUPS))
        o_ref[0, rows, :] = (gz * lax.rsqrt(ms + EPS) * nw_ref[...]).astype(o_ref.dtype)

    @pl.when(phase == 0)
    def _():
        prepared = [backward_prepare(ci) for ci in range(SSD_BLOCK)]
        for ci in reversed(range(SSD_BLOCK)):
            backward_state(ci, prepared[ci])

    @pl.when(phase == 1)
    def _():
        prepared = [forward_prepare(ci) for ci in range(SSD_BLOCK)]
        for p in prepared:
            forward_mix(p)
        for ci in range(SSD_BLOCK):
            forward_state(ci, prepared[ci])


def _ssd(xbc, z, dt_raw, dt_bias, a_log, d_skip, norm_w, bd, n_lat):
    bsz, t, _ = z.shape
    rows = SSD_BLOCK * SSD_CHUNK
    n_lat_blocks = n_lat // rows
    n_blocks = t // rows
    assert n_blocks == n_lat_blocks + 1 and n_lat_blocks * rows == n_lat

    def block_of(p, s):
        return jnp.where(p == 0, n_lat_blocks - s, jnp.where(s < 1, n_lat_blocks, s - 1))

    def out_block(p, s):
        return jnp.where(p == 0, n_lat_blocks, block_of(p, s))

    const = lambda shape: pl.BlockSpec(shape, lambda b, p, s: tuple(0 for _ in shape))
    return pl.pallas_call(
        functools.partial(_ssd_kernel, n_lat_blocks=n_lat_blocks),
        grid=(bsz, 2, n_blocks),
        in_specs=[
            pl.BlockSpec((1, rows, SSD_CONV_CH), lambda b, p, s: (b, block_of(p, s), 0)),
            pl.BlockSpec((1, rows, SSD_W), lambda b, p, s: (b, block_of(p, s), 0)),
            pl.BlockSpec((1, rows, LANES), lambda b, p, s: (b, block_of(p, s), 0)),
            const((1, LANES)),
            const((1, LANES)),
            const((1, SSD_W)),
            const((1, SSD_W)),
            const((SSD_W, SSD_W)),
        ],
        out_specs=pl.BlockSpec((1, rows, SSD_W), lambda b, p, s: (b, out_block(p, s), 0)),
        out_shape=jax.ShapeDtypeStruct((bsz, t, SSD_W), BF16),
        scratch_shapes=[
            pltpu.VMEM((SSD_HEADS, SSD_STATE, SSD_HEAD_DIM), F32),
            pltpu.VMEM((SSD_HEADS, SSD_STATE, SSD_HEAD_DIM), F32),
            pltpu.VMEM((n_blocks * SSD_BLOCK, SSD_HEADS, SSD_STATE, SSD_HEAD_DIM), F32),
        ],
        compiler_params=_params("arbitrary", "arbitrary", "arbitrary"),
        name="ssd",
    )(xbc, z, dt_raw, dt_bias, a_log, d_skip, norm_w, bd)


def _hyfilt_kernel(feat_ref, win_ref, w1_ref, b1_ref, fr_ref, w2_ref, b2_ref, w3_ref, o_ref):
    hp = lax.Precision.HIGHEST
    fr = fr_ref[...]
    h1 = jnp.sin(fr * (jnp.dot(feat_ref[...], w1_ref[...], precision=hp, preferred_element_type=F32) + b1_ref[...]))
    h2 = jnp.sin(fr * (jnp.dot(h1, w2_ref[...], precision=hp, preferred_element_type=F32) + b2_ref[...]))
    h = _dot_x3(h2, w3_ref[...])
    win = win_ref[...]
    first_tile = pl.program_id(0) == 0
    row = lax.broadcasted_iota(jnp.int32, win.shape, 0)
    for order in range(HY_ORDER):
        for direction in range(2):
            c0 = (direction * HY_ORDER + order) * HY_W
            f = h[:, c0:c0 + HY_W] * win
            if direction == 1:
                f = jnp.where(first_tile & (row == 0), 0.0, f)
            o_ref[order * 2 + direction] = f


def _hy_positional(length):
    n = np.arange(length, dtype=np.float64)
    t = n / max(length - 1, 1)
    bands = np.linspace(1e-4, HY_BANDS - 1, HY_BANDS)
    wpos = (2 * math.pi / length) * n
    feats = np.concatenate([t[:, None], np.cos(wpos[:, None] * bands), -np.sin(wpos[:, None] * bands)], axis=-1)
    feats = np.pad(feats, ((0, 0), (0, LANES - HY_POS_DIM)))
    deltas = np.abs(np.linspace(math.log(HY_TARGET) / HY_SLOW_DECAY, math.log(HY_TARGET) / HY_FAST_DECAY, HY_W))
    window = np.exp(-t[:, None] * deltas)
    return jnp.asarray(feats, F32), jnp.asarray(window, F32)


def _hyena_filters(length, w1, b1, freq, w2, b2, w3):
    feats, window = _hy_positional(length)
    tl = min(length, 512)
    hid = HY_FILTER_HID
    w1p = jnp.pad(w1, ((0, LANES - HY_POS_DIM), (0, 0)))
    const = lambda shape: pl.BlockSpec(shape, lambda i: tuple(0 for _ in shape))
    return pl.pallas_call(
        _hyfilt_kernel,
        grid=(length // tl,),
        in_specs=[
            pl.BlockSpec((tl, LANES), lambda i: (i, 0)),
            pl.BlockSpec((tl, HY_W), lambda i: (i, 0)),
            const((LANES, hid)), const((1, hid)), const((1, hid)),
            const((hid, hid)), const((1, hid)), const((hid, 2 * HY_ORDER * HY_W)),
        ],
        out_specs=pl.BlockSpec((2 * HY_ORDER, tl, HY_W), lambda i: (0, i, 0)),
        out_shape=jax.ShapeDtypeStruct((2 * HY_ORDER, length, HY_W), F32),
        compiler_params=_params("arbitrary"),
        name="hyena_filters",
    )(feats, window, w1p, b1.reshape(1, hid), freq.reshape(1, hid), w2, b2.reshape(1, hid), w3)


def _dft_tables(length):
    n = 2 * length
    n2 = HY_N2
    n1 = n // n2
    half = n1 // 2
    k1 = np.arange(half, dtype=np.float64) + 0.5
    idx = (n2 * np.arange(half)[None, None, :] + np.arange(n2)[:, None, None])
    ang = 2 * np.pi * k1[None, :, None] * idx / n
    fa = np.concatenate([np.cos(ang), -np.sin(ang)], axis=1)
    kk = np.arange(n2, dtype=np.float64)
    angb = 2 * np.pi * kk[:, None] * kk[None, :] / n2
    cr, sr = np.cos(angb), np.sin(angb)
    fb = np.block([[cr, sr], [-sr, cr]])
    fbi = np.block([[cr, -sr], [sr, cr]])
    idxo = (n2 * np.arange(half)[None, :, None] + np.arange(n2)[:, None, None])
    ango = 2 * np.pi * k1[None, None, :] * idxo / n
    ga = np.concatenate([np.cos(ango), -np.sin(ango)], axis=2) * (2.0 / n)
    f32 = lambda a: jnp.asarray(a, F32).astype(BF16)
    return f32(fa), f32(fb), f32(fbi), f32(ga)


def _seq_pitch(n2n):
    return n2n + SUBLANES


def _to_pitched(dst_ref, val, n2n):
    pitch = _seq_pitch(n2n)
    for i in range(val.shape[0] // n2n):
        dst_ref[i * pitch:i * pitch + n2n, :] = val[i * n2n:(i + 1) * n2n, :]


def _hy_forward_a(src_ref, a_ref, fa_ref):
    n2n, two_n1, half = fa_ref.shape
    zp, ap = _seq_pitch(n2n), _seq_pitch(two_n1)

    def body(n2, carry):
        zs = src_ref[pl.ds(n2, half, stride=zp), :].astype(BF16)
        a_ref[pl.ds(pl.multiple_of(n2 * ap, SUBLANES), two_n1), :] = _dot(fa_ref[n2], zs)
        return carry

    lax.fori_loop(0, n2n, body, 0, unroll=HY_UNROLL)


def _hy_kspec_kernel(hf_ref, hb_ref, fa_ref, fb_ref, o_ref, af_ref, ab_ref, hfp_ref, hbp_ref):
    n2n, two_n1, _ = fa_ref.shape
    n1 = two_n1 // 2
    ap = _seq_pitch(two_n1)
    _to_pitched(hfp_ref, hf_ref[0], n2n)
    _to_pitched(hbp_ref, hb_ref[0], n2n)
    _hy_forward_a(hfp_ref, af_ref, fa_ref)
    _hy_forward_a(hbp_ref, ab_ref, fa_ref)
    fb = fb_ref[...]

    def body(k1, carry):
        def spectrum(a_ref):
            ar = a_ref[pl.ds(k1, n2n, stride=ap), :]
            ai = a_ref[pl.ds(n1 + k1, n2n, stride=ap), :]
            return _dot(fb, jnp.concatenate([ar, ai], axis=0).astype(BF16))

        xf, xb = spectrum(af_ref), spectrum(ab_ref)
        o_ref[0, k1] = jnp.concatenate([xf[:n2n] + xb[:n2n], xf[n2n:] - xb[n2n:]], axis=0)
        return carry

    lax.fori_loop(0, n1, body, 0, unroll=HY_UNROLL)


def _hy_kspec(filt, tables):
    fa, fb, _, _ = tables
    _, length, _ = filt.shape
    n2n, two_n1, _ = fa.shape
    n1 = two_n1 // 2
    nh = HY_W // LANES
    return pl.pallas_call(
        _hy_kspec_kernel,
        grid=(HY_ORDER, nh),
        in_specs=[
            pl.BlockSpec((1, length, LANES), lambda o, h: (2 * o, 0, h)),
            pl.BlockSpec((1, length, LANES), lambda o, h: (2 * o + 1, 0, h)),
            pl.BlockSpec(fa.shape, lambda o, h: (0, 0, 0), pipeline_mode=pl.Buffered(1)),
            pl.BlockSpec(fb.shape, lambda o, h: (0, 0)),
        ],
        out_specs=pl.BlockSpec((1, n1, 2 * n2n, LANES), lambda o, h: (o, 0, 0, h)),
        out_shape=jax.ShapeDtypeStruct((HY_ORDER, n1, 2 * n2n, HY_W), F32),
        scratch_shapes=[pltpu.VMEM((n2n * _seq_pitch(two_n1), LANES), F32),
                        pltpu.VMEM((n2n * _seq_pitch(two_n1), LANES), F32),
                        pltpu.VMEM((length // n2n * _seq_pitch(n2n), LANES), F32),
                        pltpu.VMEM((length // n2n * _seq_pitch(n2n), LANES), F32)],
        compiler_params=_params("arbitrary", "arbitrary"),
        name="hyena_kspec",
    )(filt, filt, fa, fb)


def _hy_conv_kernel(z_ref, g_ref, fa_ref, fb_ref, fbi_ref, ga_ref, k_ref, bias_ref,
                    o_ref, a_ref, c_ref, zc_ref, gc_ref):
    n2n, two_n1, half = fa_ref.shape
    n1 = two_n1 // 2
    zp, ap = _seq_pitch(n2n), _seq_pitch(two_n1)
    _to_pitched(zc_ref, z_ref[0], n2n)
    _to_pitched(gc_ref, g_ref[0], n2n)
    _hy_forward_a(zc_ref, a_ref, fa_ref)
    fb = fb_ref[...]
    fbi = fbi_ref[...]

    def body_b(grp, carry):
        k1s = [grp * HY_UNROLL + i for i in range(HY_UNROLL)]
        sl = [(pl.ds(k1, n2n, stride=ap), pl.ds(n1 + k1, n2n, stride=ap)) for k1 in k1s]
        xs = [_dot(fb, jnp.concatenate([a_ref[re, :], a_ref[im, :]], axis=0).astype(BF16)) for re, im in sl]
        ys = []
        for k1, x in zip(k1s, xs):
            kk = k_ref[0, k1]
            xr, xi = x[:n2n], x[n2n:]
            kr, ki = kk[:n2n], kk[n2n:]
            ys.append(jnp.concatenate([xr * kr - xi * ki, xr * ki + xi * kr], axis=0).astype(BF16))
        for (re, im), y in zip(sl, ys):
            c = _dot(fbi, y)
            c_ref[re, :] = c[:n2n]
            c_ref[im, :] = c[n2n:]
        return carry

    lax.fori_loop(0, n1 // HY_UNROLL, body_b, 0)
    bias = bias_ref[0]

    def body_c(m2, carry):
        rows = pl.ds(m2, half, stride=zp)
        c = c_ref[pl.ds(pl.multiple_of(m2 * ap, SUBLANES), two_n1), :].astype(BF16)
        y = _dot(ga_ref[m2], c)
        a_ref[rows, :] = gc_ref[rows, :] * (y + zc_ref[rows, :] * bias)
        return carry

    lax.fori_loop(0, n2n, body_c, 0, unroll=HY_UNROLL)
    for i in range(half):
        o_ref[0, i * n2n:(i + 1) * n2n, :] = a_ref[i * zp:i * zp + n2n, :]


def _hy_conv(z_arr, z_blk, g_arr, g_blk, kspec, order, bias3, tables, n_lat):
    fa, fb, fbi, ga = tables
    bsz = z_arr.shape[0]
    n2n, two_n1, half = fa.shape
    n1 = two_n1 // 2
    nh = HY_W // LANES
    resident = lambda shape: pl.BlockSpec(shape, lambda h, b_: tuple(0 for _ in shape), pipeline_mode=pl.Buffered(1))
    return pl.pallas_call(
        _hy_conv_kernel,
        grid=(nh, bsz),
        in_specs=[
            pl.BlockSpec((1, n_lat, LANES), lambda h, b_: (b_, 0, z_blk + h)),
            pl.BlockSpec((1, n_lat, LANES), lambda h, b_: (b_, 0, g_blk + h)),
            resident(fa.shape), resident(fb.shape), resident(fbi.shape), resident(ga.shape),
            pl.BlockSpec((1, n1, 2 * n2n, LANES), lambda h, b_: (order, 0, 0, h), pipeline_mode=pl.Buffered(1)),
            pl.BlockSpec((1, 1, LANES), lambda h, b_: (order, 0, h)),
        ],
        out_specs=pl.BlockSpec((1, n_lat, LANES), lambda h, b_: (b_, 0, h)),
        out_shape=jax.ShapeDtypeStruct((bsz, n_lat, HY_W), F32),
        scratch_shapes=[pltpu.VMEM((n2n * _seq_pitch(two_n1), LANES), F32),
                        pltpu.VMEM((n2n * _seq_pitch(two_n1), LANES), F32),
                        pltpu.VMEM((half * _seq_pitch(n2n), LANES), F32),
                        pltpu.VMEM((half * _seq_pitch(n2n), LANES), F32)],
        compiler_params=_params("arbitrary", "arbitrary"),
        name="hyena_conv",
    )(z_arr, g_arr, fa, fb, fbi, ga, kspec, bias3)


def _hyena_ctx_kernel(u_ref, filt_ref, bias_ref, fd_ref, gd_ref, o_ref, *, n_ctx):
    fd = fd_ref[...]
    gd = gd_ref[...]
    nn = 2 * n_ctx
    u = u_ref[0]
    v, x1, x2 = u[:, :HY_W], u[:, HY_W:2 * HY_W], u[:, 2 * HY_W:]

    def conv(zin, order):
        hf = _dot(fd, filt_ref[2 * order].astype(BF16))
        hb = _dot(fd, filt_ref[2 * order + 1].astype(BF16))
        kr, ki = hf[:nn] + hb[:nn], hf[nn:] - hb[nn:]
        zz = _dot(fd, zin.astype(BF16))
        zr, zi = zz[:nn], zz[nn:]
        y = jnp.concatenate([zr * kr - zi * ki, zr * ki + zi * kr], axis=0).astype(BF16)
        return _dot(gd, y) + zin * bias_ref[order]

    y1 = x1 * conv(v, 0)
    o_ref[0] = x2 * conv(y1, 1)


def _hyena_ctx(hy, filt, bias, n_lat, n_ctx):
    bsz = hy.shape[0]
    nn = 2 * n_ctx
    k = np.arange(nn, dtype=np.float64)
    t = np.arange(n_ctx, dtype=np.float64)
    ang = 2 * np.pi * k[:, None] * t[None, :] / nn
    fd = jnp.asarray(np.concatenate([np.cos(ang), -np.sin(ang)], axis=0), F32).astype(BF16)
    gd = jnp.asarray(np.concatenate([np.cos(ang.T), -np.sin(ang.T)], axis=1) / nn, F32).astype(BF16)
    blk = n_lat // n_ctx
    assert blk * n_ctx == n_lat
    return pl.pallas_call(
        functools.partial(_hyena_ctx_kernel, n_ctx=n_ctx),
        grid=(bsz,),
        in_specs=[
            pl.BlockSpec((1, n_ctx, 3 * HY_W), lambda b: (b, blk, 0)),
            pl.BlockSpec((2 * HY_ORDER, n_ctx, HY_W), lambda b: (0, 0, 0)),
            pl.BlockSpec((HY_ORDER, 1, HY_W), lambda b: (0, 0, 0)),
            pl.BlockSpec((2 * nn, n_ctx), lambda b: (0, 0)),
            pl.BlockSpec((n_ctx, 2 * nn), lambda b: (0, 0)),
        ],
        out_specs=pl.BlockSpec((1, n_ctx, HY_W), lambda b: (b, 0, 0)),
        out_shape=jax.ShapeDtypeStruct((bsz, n_ctx, HY_W), F32),
        compiler_params=_params("arbitrary"),
        name="hyena_ctx",
    )(hy, filt, bias, fd, gd)


def _hyena_latent(hy, filt, bias, n_lat, tables):
    kspec = _hy_kspec(filt, tables)
    bias3 = bias.reshape(HY_ORDER, 1, HY_W)
    nb = HY_W // LANES
    y1 = _hy_conv(hy, 0, hy, nb, kspec, 0, bias3, tables, n_lat)
    return _hy_conv(y1, 0, hy, 2 * nb, kspec, 1, bias3, tables, n_lat)


def _outproj_kernel(x_ref, att_ref, ssd_ref, hy_ref, gt_ref, wa_ref, ws_ref, wh_ref, o_ref):
    mix = _dot(att_ref[0], wa_ref[0]) + _dot(ssd_ref[0], ws_ref[0]) + _dot(hy_ref[0].astype(BF16), wh_ref[0])
    o_ref[0] = x_ref[0] + gt_ref[0] * mix


def _outproj(x, att, ssd, ssd_row0, hy, mods, mod_row, w_out, layer, tm):
    bsz, n, _ = att.shape
    d = x.shape[-1]
    ssd_blk0 = ssd_row0 // tm
    assert ssd_blk0 * tm == ssd_row0 and n % tm == 0
    tok = lambda w: pl.BlockSpec((1, tm, w), lambda j, b: (b, j, 0))
    return pl.pallas_call(
        _outproj_kernel,
        grid=(n // tm, bsz),
        in_specs=[
            tok(d), tok(ATT_W),
            pl.BlockSpec((1, tm, SSD_W), lambda j, b: (b, ssd_blk0 + j, 0)),
            tok(HY_W),
            pl.BlockSpec((1, 1, d), lambda j, b: (mod_row(b), 0, 2)),
            pl.BlockSpec((1, ATT_W, d), lambda j, b: (layer, 0, 0)),
            pl.BlockSpec((1, SSD_W, d), lambda j, b: (layer, ATT_W // SSD_W, 0)),
            pl.BlockSpec((1, HY_W, d), lambda j, b: (layer, (ATT_W + SSD_W) // HY_W, 0)),
        ],
        out_specs=tok(d),
        out_shape=jax.ShapeDtypeStruct((bsz, n, d), F32),
        compiler_params=_params("arbitrary", "arbitrary"),
        name="outproj",
    )(x, att, ssd, hy, mods, w_out, w_out, w_out)


def _route(logits_t, bias_col):
    scores = jax.nn.sigmoid(logits_t)
    sel = scores + bias_col
    neg_inf = jnp.float32(-jnp.inf)
    rows = [sel[e:e + 1, :] for e in range(N_EXPERTS)]
    grp = []
    for g in range(N_GROUPS):
        r = rows[g * EXPERTS_PER_GROUP:(g + 1) * EXPERTS_PER_GROUP]
        top = functools.reduce(jnp.maximum, r)
        taken = None
        rest = []
        for ri in r:
            is_top = (ri == top) if taken is None else (ri == top) & jnp.logical_not(taken)
            rest.append(jnp.where(is_top, neg_inf, ri))
            taken = is_top if taken is None else taken | is_top
        grp.append(top + functools.reduce(jnp.maximum, rest))
    best = jnp.zeros(grp[0].shape, jnp.int32)
    cur = grp[0]
    for g in range(1, N_GROUPS):
        upd = grp[g] > cur
        best = jnp.where(upd, g, best)
        cur = jnp.where(upd, grp[g], cur)
    picked = []
    for e in range(N_EXPERTS):
        g, i = divmod(e, EXPERTS_PER_GROUP)
        rank = jnp.zeros(best.shape, jnp.int32)
        for j in range(EXPERTS_PER_GROUP):
            if j == i:
                continue
            other = rows[g * EXPERTS_PER_GROUP + j]
            ahead = (other > rows[e]) | ((other == rows[e]) & (j < i))
            rank = rank + ahead.astype(jnp.int32)
        keep = (best == g) & (rank < 2)
        picked.append(jnp.where(keep, scores[e:e + 1, :], 0.0))
    total = functools.reduce(lambda u, w: u + w, picked)
    return jnp.concatenate(picked, axis=0) / total, best


def _ffn_input(x_ref, sh_ref, sc_ref, g_ref):
    x = x_ref[...].reshape(MOE_TILE, D_MODEL)
    ms = jnp.mean(x * x, axis=-1, keepdims=True)
    t = x * lax.rsqrt(ms + EPS) * g_ref[...]
    return x, t * (1.0 + sc_ref[0]) + sh_ref[0]


def _route_kernel(x_ref, sh_ref, sc_ref, g_ref, wr_ref, rb_ref, rt_ref, srow_ref, tbl_ref):
    tm = MOE_TILE
    _, t = _ffn_input(x_ref, sh_ref, sc_ref, g_ref)
    logits = _dot_x3(t, wr_ref[...])
    comb_t, best = _route(logits.T[:N_EXPERTS, :], rb_ref[...])

    member = [jnp.where(best == g, 1.0, 0.0) for g in range(N_GROUPS)]
    comb4 = functools.reduce(
        lambda u, w: u + w,
        [member[g] * comb_t[g * EXPERTS_PER_GROUP:(g + 1) * EXPERTS_PER_GROUP, :] for g in range(N_GROUPS)])
    masks = jnp.concatenate(member + [jnp.zeros((8 - N_GROUPS, tm), F32)], axis=0)
    earlier = jnp.where(lax.broadcasted_iota(jnp.int32, (tm, tm), 0) < lax.broadcasted_iota(jnp.int32, (tm, tm), 1),
                        1.0, 0.0).astype(BF16)
    ranks = _dot(masks.astype(BF16), earlier)
    cnt = jnp.sum(masks, axis=1, keepdims=True)
    padded = jnp.ceil(cnt * (1.0 / MOE_CHUNK)) * MOE_CHUNK
    lane = lax.broadcasted_iota(jnp.int32, (1, LANES), 1).astype(F32) * MOE_CHUNK
    start = jnp.zeros((1, 1), F32)
    slot = jnp.zeros((1, tm), F32)
    gid = jnp.zeros((1, LANES), jnp.int32)
    for g in range(N_GROUPS):
        slot = slot + member[g] * (start + ranks[g:g + 1, :])
        start = start + padded[g:g + 1, :]
        gid = gid + jnp.where(lane >= start, 1, 0)
    tbl_ref[0] = gid
    srow_ref[0] = slot.astype(jnp.int32)
    rt_t = jnp.concatenate([comb4, slot, jnp.zeros((LANES - EXPERTS_PER_GROUP - 1, tm), F32)], axis=0)
    rt_ref[...] = rt_t.T


def _experts_kernel(tbl_ref, x_ref, sh_ref, sc_ref, gt_ref, g_ref, rt_ref, srow_ref, wg_ref, wu_ref, wd_ref, gf_ref,
                    *rest, final, tile_of, n_grid):
    o_ref, xp_ref, yp_ref = rest[-3:]
    tm, ch = MOE_TILE, MOE_CHUNK
    n_slots = MOE_CHUNKS * ch
    tile = tile_of(*[pl.program_id(a) for a in range(n_grid)])
    x, t = _ffn_input(x_ref, sh_ref, sc_ref, g_ref)
    rt = rt_ref[...]
    slot_col = rt[:, EXPERTS_PER_GROUP:EXPERTS_PER_GROUP + 1].astype(jnp.int32)
    gather = jnp.where(lax.broadcasted_iota(jnp.int32, (n_slots, tm), 0) == srow_ref[0], 1.0, 0.0).astype(BF16)
    scatter = jnp.where(lax.broadcasted_iota(jnp.int32, (tm, n_slots), 1) == slot_col, 1.0, 0.0).astype(BF16)
    xp_ref[...] = _dot(gather, t.astype(BF16)).astype(BF16)
    w_slot = _dot3_right(gather, rt)

    def run_chunk(rows, grp):
        xc = xp_ref[rows, :]
        e0 = grp * EXPERTS_PER_GROUP
        gu_next = _dot(xc, wg_ref[0, e0]), _dot(xc, wu_ref[0, e0])
        acc = jnp.zeros((ch, D_MODEL), F32)
        for j in range(EXPERTS_PER_GROUP):
            gate, up = gu_next
            if j + 1 < EXPERTS_PER_GROUP:
                gu_next = _dot(xc, wg_ref[0, e0 + j + 1]), _dot(xc, wu_ref[0, e0 + j + 1])
            hid = _silu(gate) * up * w_slot[rows, j:j + 1]
            acc = acc + _dot(hid.astype(BF16), wd_ref[0, e0 + j])
        yp_ref[rows, :] = acc.astype(BF16)

    for c in range(MOE_CHUNKS):
        rows = slice(c * ch, (c + 1) * ch)
        grp = tbl_ref[tile, c]
        pl.when(grp < N_GROUPS)(functools.partial(run_chunk, rows, grp))

        @pl.when(grp >= N_GROUPS)
        def _():
            yp_ref[rows, :] = jnp.zeros((ch, D_MODEL), BF16)

    y = x + gt_ref[0] * _dot(scatter, yp_ref[...])
    if final:
        y = y * lax.rsqrt(jnp.mean(y * y, axis=-1, keepdims=True) + EPS) * gf_ref[...]
    o_ref[...] = y.reshape(o_ref.shape)


def _moe_tiles(x, mods, mod_row, g_ffn, w_router, router_bias, wgu, wd, layer, g_final, final, out_rows,
               grid, x_block, x_index, out_index, tile_of, n_tiles):
    bsz, _, d = x.shape
    tm = MOE_TILE
    ng = len(grid)
    const = lambda shape: pl.BlockSpec(shape, lambda *a: tuple(0 for _ in shape))
    mod = lambda col: pl.BlockSpec((1, 1, d), lambda *a: (mod_row(*a[:ng]), 0, col))
    xspec = pl.BlockSpec(x_block, lambda *a: x_index(*a[:ng]))
    rt, srow, tbl = pl.pallas_call(
        _route_kernel,
        grid=grid,
        in_specs=[xspec, mod(3), mod(4), const((1, d)), const((d, LANES)), const((N_EXPERTS, 1))],
        out_specs=[
            pl.BlockSpec((tm, LANES), lambda *a: (tile_of(*a), 0)),
            pl.BlockSpec((1, 1, tm), lambda *a: (tile_of(*a), 0, 0)),
            pl.BlockSpec((1, 1, LANES), lambda *a: (tile_of(*a), 0, 0)),
        ],
        out_shape=[
            jax.ShapeDtypeStruct((n_tiles * tm, LANES), F32),
            jax.ShapeDtypeStruct((n_tiles, 1, tm), jnp.int32),
            jax.ShapeDtypeStruct((n_tiles, 1, LANES), jnp.int32),
        ],
        compiler_params=_params(*["arbitrary"] * ng),
        name="moe_route",
    )(x, mods, mods, g_ffn, w_router, router_bias)

    resident = lambda shape: pl.BlockSpec(shape, lambda *a: (layer,) + tuple(0 for _ in shape[1:]),
                                          pipeline_mode=pl.Buffered(1))
    in_specs = [
        xspec, mod(3), mod(4), mod(5), const((1, d)),
        pl.BlockSpec((tm, LANES), lambda *a: (tile_of(*a[:ng]), 0)),
        pl.BlockSpec((1, 1, tm), lambda *a: (tile_of(*a[:ng]), 0, 0)),
        resident((1, N_EXPERTS, d, D_FF)),
        resident((1, N_EXPERTS, d, D_FF)),
        resident((1, N_EXPERTS, D_FF, d)),
        const((1, d)),
    ]
    args = [tbl.reshape(n_tiles, LANES), x, mods, mods, mods, g_ffn, rt, srow, *wgu, wd, g_final]
    return pl.pallas_call(
        functools.partial(_experts_kernel, final=final, tile_of=tile_of, n_grid=ng),
        grid_spec=pltpu.PrefetchScalarGridSpec(
            num_scalar_prefetch=1,
            grid=grid,
            in_specs=in_specs,
            out_specs=pl.BlockSpec(x_block, lambda *a: out_index(*a[:ng])),
            scratch_shapes=[pltpu.VMEM((MOE_CHUNKS * MOE_CHUNK, d), BF16), pltpu.VMEM((MOE_CHUNKS * MOE_CHUNK, d), BF16)],
        ),
        out_shape=jax.ShapeDtypeStruct((bsz, out_rows, d), F32),
        compiler_params=_params(*["arbitrary"] * ng),
        name="moe_experts",
    )(*args)


def _moe(x_lat, x_ctx, mods, g_ffn, w_router, router_bias, wgu, wd, layer, g_final, final):
    bsz, n_lat, d = x_lat.shape
    tm = MOE_TILE
    per_b = n_lat // tm
    common = (g_ffn, w_router, router_bias, wgu, wd, layer, g_final, final)
    lat_index = lambda b, j: (b, j, 0)
    out_lat = _moe_tiles(x_lat, mods, lambda b, j: b, *common, n_lat,
                         (bsz, per_b), (1, tm, d), lat_index, lat_index, lambda b, j: b * per_b + j, bsz * per_b)
    if x_ctx is None:
        return out_lat, None
    n_ctx = x_ctx.shape[1]
    nb = tm // n_ctx
    ctx_index = lambda i: (i, 0, 0)
    out_ctx = _moe_tiles(x_ctx, mods, lambda i: bsz, *common, n_ctx,
                         (bsz // nb,), (nb, n_ctx, d), ctx_index, ctx_index, lambda i: i, bsz // nb)
    return out_lat, out_ctx


def _block_ones(width, block):
    idx = np.arange(width) // block
    return jnp.asarray(idx[:, None] == idx[None, :], F32).astype(BF16)


def _rope_tables(n_lat, n_ctx):
    rows = n_lat // GRID_W
    row = np.repeat(np.arange(rows), GRID_W).astype(np.float64)
    col = np.tile(np.arange(GRID_W), rows).astype(np.float64)
    inv = ROPE_THETA ** (-np.arange(0, ROPE_AXIS_DIM, 2, dtype=np.float64) / ROPE_AXIS_DIM)
    ang = np.concatenate([row[:, None] * inv, col[:, None] * inv], axis=-1)
    ang = np.concatenate([ang, np.zeros((n_ctx, ang.shape[1]))], axis=0)
    cos = np.concatenate([np.cos(ang), np.cos(ang)], axis=-1)
    sin = np.concatenate([-np.sin(ang), np.sin(ang)], axis=-1)
    scale = Q_SCALE
    cs = np.tile(cos, (1, ATT_HEADS)) * scale
    sn = np.tile(sin, (1, ATT_HEADS)) * scale
    return jnp.asarray(cs, F32), jnp.asarray(sn, F32)


def kernel(x, c, ctx, c_ctx, w_mod, b_mod, g_mix, g_ffn, w_in, q_norm, k_norm, ssd_conv_w, ssd_conv_b,
           ssd_dt_bias, ssd_a_log, ssd_d, ssd_norm, hy_conv_w, hy_conv_b, hy_w1, hy_b1, hy_freq, hy_w2, hy_b2,
           hy_w3, hy_bias, w_out, w_router, router_bias, w_gate, w_up, w_down, g_final):
    bsz, n_lat, d = x.shape
    n_ctx = ctx.shape[1]
    depth = w_mod.shape[0]
    t = n_lat + n_ctx
    tm = TOKEN_TILE
    n_lat_tiles = n_lat // tm
    n_tiles = t // tm
    assert n_ctx == tm and n_lat % (HY_N2 * 8) == 0 and bsz < MOD_ROWS

    cvec = jnp.concatenate([c, c_ctx[None], jnp.zeros((MOD_ROWS - bsz - 1, d), F32)], axis=0)
    mods_all = _adaln(cvec, w_mod, b_mod)

    cs, sn = _rope_tables(n_lat, n_ctx)
    bd_head = _block_ones(ATT_W, HEAD_DIM)
    bd_ssd = _block_ones(SSD_W, SSD_W // SSD_GROUPS)
    tables = _dft_tables(n_lat)
    pad_row = lambda v: jnp.pad(v.reshape(1, -1), ((0, 0), (0, LANES - v.size)))
    w_out_b = w_out.astype(BF16)
    wgu_b = (w_gate.astype(BF16), w_up.astype(BF16))
    wd_b = w_down.astype(BF16)

    x_lat, x_ctx, ctx_blk = x, ctx, 0
    for i in range(depth):
        last = i == depth - 1
        mods = mods_all[i].reshape(MOD_ROWS, 1, 6 * d)
        q, kt, v, z, xbc, dt_raw, hy = _inproj(
            x_lat, x_ctx, ctx_blk, mods, g_mix[i].reshape(1, d), w_in, i,
            jnp.tile(q_norm[i], ATT_HEADS).reshape(1, ATT_W), jnp.tile(k_norm[i], ATT_KV_HEADS).reshape(1, KV_W),
            cs, sn, bd_head, jnp.concatenate([ssd_conv_w[i], hy_conv_w[i]], axis=-1),
            jnp.concatenate([ssd_conv_b[i], hy_conv_b[i]]).reshape(1, -1), n_lat_tiles)

        att = _attention(q, kt, v, 0, n_lat, 0, t, ATT_TILE)

        ssd = _ssd(xbc, z, dt_raw, pad_row(ssd_dt_bias[i]), pad_row(ssd_a_log[i]),
                   jnp.repeat(ssd_d[i], SSD_HEAD_DIM).reshape(1, SSD_W), ssd_norm[i].reshape(1, SSD_W),
                   bd_ssd, n_lat)

        hyp = (hy_w1[i], hy_b1[i], hy_freq[i], hy_w2[i], hy_b2[i], hy_w3[i])
        hy_l = _hyena_latent(hy, _hyena_filters(n_lat, *hyp), hy_bias[i], n_lat, tables)

        mid_lat = _outproj(x_lat, att, ssd, 0, hy_l, mods, lambda b: b, w_out_b, i, OUT_TILE)
        mid_ctx = None
        if not last:
            att_c = _attention(q, kt, v, n_lat, n_ctx, n_lat, n_ctx, n_ctx)
            hy_c = _hyena_ctx(hy, _hyena_filters(n_ctx, *hyp), hy_bias[i].reshape(HY_ORDER, 1, HY_W), n_lat, n_ctx)
            mid_ctx = _outproj(x_ctx, att_c, ssd, n_lat, hy_c, mods, lambda b: bsz, w_out_b, i, n_ctx)

        x_lat, x_ctx = _moe(mid_lat, mid_ctx, mods, g_ffn[i].reshape(1, d),
                            jnp.pad(w_router, ((0, 0), (0, LANES - N_EXPERTS))), router_bias.reshape(N_EXPERTS, 1),
                            wgu_b, wd_b, i, g_final.reshape(1, d), last)
    return x_lat
```

```python
import functools
import math

import jax
import jax.numpy as jnp
import numpy as np
from jax import lax
from jax.experimental import pallas as pl
from jax.experimental.pallas import tpu as pltpu

F32 = jnp.float32
BF16 = jnp.bfloat16

D_MODEL = 1024
GRID_W = 64
EPS = 1e-6

ATT_HEADS = 6
ATT_KV_HEADS = 2
HEAD_DIM = 64
ATT_W = ATT_HEADS * HEAD_DIM
KV_W = ATT_KV_HEADS * HEAD_DIM
ROPE_AXIS_DIM = HEAD_DIM // 2
ROPE_THETA = 10000.0
Q_SCALE = HEAD_DIM ** -0.5 * math.log2(math.e)

SSD_HEADS = 6
SSD_HEAD_DIM = 64
SSD_W = SSD_HEADS * SSD_HEAD_DIM
SSD_GROUPS = 2
SSD_STATE = 64
SSD_CHUNK = 128
SSD_CONV_CH = SSD_W + 2 * SSD_GROUPS * SSD_STATE
SSD_HEADS_PER_GROUP = SSD_HEADS // SSD_GROUPS
SSD_BLOCK = 2

HY_W = 256
HY_ORDER = 2
HY_BANDS = 16
HY_POS_DIM = 1 + 2 * HY_BANDS
HY_FILTER_HID = 64
HY_FAST_DECAY = 0.3
HY_SLOW_DECAY = 1.5
HY_TARGET = 1e-2
HY_N2 = 64
HY_UNROLL = 16

MIX_W = ATT_W + SSD_W + HY_W
N_EXPERTS = 16
N_GROUPS = 4
EXPERTS_PER_GROUP = N_EXPERTS // N_GROUPS
D_FF = 256

LANES = 128
SUBLANES = 8
TOKEN_TILE = 256
ATT_TILE = 1024
ATT_SUB = 512
OUT_TILE = 512
MOE_TILE = 512
MOE_CHUNK = 128
MOE_CHUNKS = MOE_TILE // MOE_CHUNK + N_GROUPS - 1
MOD_ROWS = 8
VMEM_LIMIT = 56 * 1024 * 1024

COL_Q = 0
COL_K = COL_Q + ATT_W
COL_V = COL_K + KV_W
COL_XBC = COL_V + KV_W
COL_HY = COL_XBC + SSD_CONV_CH
COL_Z = COL_HY + 3 * HY_W
COL_DT = COL_Z + SSD_W
COL_END = COL_DT + LANES


def _params(*sem):
    return pltpu.CompilerParams(dimension_semantics=sem, vmem_limit_bytes=VMEM_LIMIT)


def _silu(x):
    return x * jax.nn.sigmoid(x)


def _softplus(x):
    return jnp.maximum(x, 0.0) + jnp.log1p(jnp.exp(-jnp.abs(x)))


def _split3(x):
    hi = x.astype(BF16)
    r1 = x - hi.astype(F32)
    mid = r1.astype(BF16)
    lo = (r1 - mid.astype(F32)).astype(BF16)
    return hi, mid, lo


def _dot(a, b):
    return jnp.dot(a, b, preferred_element_type=F32)


def _dot_x3(a, b):
    a_hi, a_mid, _ = _split3(a)
    b_hi, b_mid, _ = _split3(b)
    return _dot(a_hi, b_hi) + (_dot(a_hi, b_mid) + _dot(a_mid, b_hi))


def _dot3_right(m_bf16, x):
    hi, mid, lo = _split3(x)
    return _dot(m_bf16, hi) + _dot(m_bf16, mid) + _dot(m_bf16, lo)


def _dot3_left(x, m_bf16):
    hi, mid, lo = _split3(x)
    return _dot(hi, m_bf16) + _dot(mid, m_bf16) + _dot(lo, m_bf16)


def _adaln_kernel(c_ref, w_ref, b_ref, o_ref):
    s = _silu(c_ref[...]).astype(BF16)
    o_ref[0] = _dot(s, w_ref[0].astype(BF16)) + b_ref[0]


def _adaln(cvec, w_mod, b_mod):
    depth, d, n = w_mod.shape
    bn = n // 4
    return pl.pallas_call(
        _adaln_kernel,
        grid=(depth, n // bn),
        in_specs=[
            pl.BlockSpec((MOD_ROWS, d), lambda i, j: (0, 0)),
            pl.BlockSpec((1, d, bn), lambda i, j: (i, 0, j)),
            pl.BlockSpec((1, 1, bn), lambda i, j: (i, 0, j)),
        ],
        out_specs=pl.BlockSpec((1, MOD_ROWS, bn), lambda i, j: (i, 0, j)),
        out_shape=jax.ShapeDtypeStruct((depth, MOD_ROWS, n), F32),
        compiler_params=_params("arbitrary", "arbitrary"),
        name="adaln",
    )(cvec, w_mod, b_mod.reshape(depth, 1, n))


def _head_rope(xn, cs, sn):
    width = xn.shape[-1]
    lane = lax.broadcasted_iota(jnp.int32, xn.shape, 1)
    first_half = (lane % HEAD_DIM) < (HEAD_DIM // 2)
    partner = jnp.where(first_half,
                        pltpu.roll(xn, width - HEAD_DIM // 2, 1),
                        pltpu.roll(xn, HEAD_DIM // 2, 1))
    return xn * cs + partner * sn


def _stream_specs(n_lat_tiles, ctx_blk, d):
    lat = pl.BlockSpec((1, TOKEN_TILE, d), lambda j, b: (b, jnp.minimum(j, n_lat_tiles - 1), 0))
    ctx = pl.BlockSpec((1, TOKEN_TILE, d), lambda j, b: (jnp.where(j >= n_lat_tiles, b, 0), ctx_blk, 0))
    return lat, ctx


def _stream_tile(xl_ref, xc_ref, n_lat_tiles):
    return jnp.where(pl.program_id(0) >= n_lat_tiles, xc_ref[0], xl_ref[0])


def _pack_w_in(w_ref, wb_ref):
    src_z = ATT_W + 2 * KV_W
    src_xbc = src_z + SSD_W
    src_dt = src_xbc + SSD_CONV_CH
    n_dt = 2 * SSD_HEADS
    wb_ref[:, COL_Q:COL_XBC] = w_ref[0, :, :src_z].astype(BF16)
    wb_ref[:, COL_XBC:COL_HY] = w_ref[0, :, src_xbc:src_dt].astype(BF16)
    tail = w_ref[0, :, src_dt:]
    wb_ref[:, COL_HY:COL_Z] = tail[:, n_dt:].astype(BF16)
    wb_ref[:, COL_Z:COL_DT] = w_ref[0, :, src_z:src_xbc].astype(BF16)
    head = tail[:, :LANES]
    lane = lax.broadcasted_iota(jnp.int32, head.shape, 1)
    wb_ref[:, COL_DT:COL_END] = jnp.where(lane < n_dt, head, 0.0).astype(BF16)


def _inproj_kernel(xl_ref, xc_ref, xp_ref, xn_ref, sh_ref, sc_ref, g_ref, w_ref, gq_ref, gk_ref, cs_ref, sn_ref,
                   bd_ref, cw_ref, cb_ref, qt_ref, k_ref, vt_ref, z_ref, xbc_ref, dt_ref, hy_ref, wb_ref,
                   *, n_lat_tiles):
    j = pl.program_id(0)

    @pl.when((j == 0) & (pl.program_id(1) == 0))
    def _():
        _pack_w_in(w_ref, wb_ref)

    tm, halo = TOKEN_TILE, SUBLANES
    x = jnp.concatenate([xp_ref[0], _stream_tile(xl_ref, xc_ref, n_lat_tiles), xn_ref[0]], axis=0)
    ms = jnp.mean(x * x, axis=-1, keepdims=True)
    h = x * lax.rsqrt(ms + EPS) * g_ref[...]
    h = h * (1.0 + sc_ref[0]) + sh_ref[0]
    hb = h[halo:halo + tm].astype(BF16)
    he = h.astype(BF16)
    us = [_dot(he, wb_ref[:, COL_XBC:COL_HY]), _dot(he, wb_ref[:, COL_HY:COL_Z])]
    qkv = _dot(hb, wb_ref[:, COL_Q:COL_XBC])
    bd = bd_ref[...]
    q, k = qkv[:, COL_Q:COL_K], qkv[:, COL_K:COL_V]
    ms_q = _dot((q * q).astype(BF16), bd) * (1.0 / HEAD_DIM)
    ms_k = _dot((k * k).astype(BF16), bd[:KV_W, :KV_W]) * (1.0 / HEAD_DIM)
    zdt = _dot(hb, wb_ref[:, COL_Z:COL_END])

    cs = cs_ref[...]
    sn = sn_ref[...]
    qt_ref[0] = _head_rope(q * lax.rsqrt(ms_q + EPS) * gq_ref[...], cs, sn).T.astype(qt_ref.dtype)
    k_ref[0] = _head_rope(k * lax.rsqrt(ms_k + EPS) * gk_ref[...],
                          cs[:, :KV_W] * (1.0 / Q_SCALE), sn[:, :KV_W] * (1.0 / Q_SCALE)).astype(k_ref.dtype)
    vt_ref[0] = qkv[:, COL_V:COL_XBC].T.astype(vt_ref.dtype)

    has_prev = (j >= 1) & (j < n_lat_tiles)
    has_next = j < n_lat_tiles - 1
    own = slice(halo, halo + tm)

    def conv3(u, c0, c1):
        u = jnp.concatenate([jnp.where(has_prev, u[:halo], 0.0), u[own], jnp.where(has_next, u[halo + tm:], 0.0)],
                            axis=0)
        cw = cw_ref[:, c0:c1]
        return (pltpu.roll(u, 1, 0)[own] * cw[0:1] + u[own] * cw[1:2]
                + pltpu.roll(u, u.shape[0] - 1, 0)[own] * cw[2:3] + cb_ref[:, c0:c1])

    xbc_ref[0] = _silu(conv3(us[0], 0, COL_HY - COL_XBC))
    hy_ref[0] = conv3(us[1], COL_HY - COL_XBC, COL_Z - COL_XBC)
    z_ref[0] = zdt[:, :COL_DT - COL_Z]
    dt_ref[0] = zdt[:, COL_DT - COL_Z:]


def _inproj(x_lat, x_ctx, ctx_blk, mods, g_mix, w_in, layer, gq, gk, cs, sn, bd, conv_w, conv_b, n_lat_tiles):
    bsz, _, d = x_lat.shape
    tm = TOKEN_TILE
    nt = n_lat_tiles + 1
    t = nt * tm
    per_tile = tm // SUBLANES
    n_conv = COL_Z - COL_XBC

    def mod_row(j, b):
        return jnp.where(j >= n_lat_tiles, bsz, b)

    lat_j = lambda j: jnp.minimum(j, n_lat_tiles - 1)
    prev_spec = pl.BlockSpec((1, SUBLANES, d), lambda j, b: (b, jnp.maximum(lat_j(j) * per_tile - 1, 0), 0))
    next_spec = pl.BlockSpec((1, SUBLANES, d),
                             lambda j, b: (b, jnp.minimum((lat_j(j) + 1) * per_tile, n_lat_tiles * per_tile - 1), 0))

    tok = lambda w: pl.BlockSpec((1, tm, w), lambda j, b: (b, j, 0))
    const = lambda shape: pl.BlockSpec(shape, lambda j, b: tuple(0 for _ in shape))
    outs = pl.pallas_call(
        functools.partial(_inproj_kernel, n_lat_tiles=n_lat_tiles),
        grid=(nt, bsz),
        in_specs=[
            *_stream_specs(n_lat_tiles, ctx_blk, d),
            prev_spec, next_spec,
            pl.BlockSpec((1, 1, d), lambda j, b: (mod_row(j, b), 0, 0)),
            pl.BlockSpec((1, 1, d), lambda j, b: (mod_row(j, b), 0, 1)),
            const((1, d)),
            pl.BlockSpec((1,) + w_in.shape[1:], lambda j, b: (layer, 0, 0), pipeline_mode=pl.Buffered(1)),
            const((1, ATT_W)),
            const((1, KV_W)),
            pl.BlockSpec((tm, ATT_W), lambda j, b: (j, 0)),
            pl.BlockSpec((tm, ATT_W), lambda j, b: (j, 0)),
            const((ATT_W, ATT_W)),
            const((3, n_conv)),
            const((1, n_conv)),
        ],
        out_specs=[
            pl.BlockSpec((1, ATT_W, tm), lambda j, b: (b, 0, j)),
            tok(KV_W),
            pl.BlockSpec((1, KV_W, tm), lambda j, b: (b, 0, j)),
            tok(SSD_W),
            tok(SSD_CONV_CH),
            tok(LANES),
            tok(3 * HY_W),
        ],
        out_shape=[
            jax.ShapeDtypeStruct((bsz, ATT_W, t), BF16),
            jax.ShapeDtypeStruct((bsz, t, KV_W), BF16),
            jax.ShapeDtypeStruct((bsz, KV_W, t), BF16),
            jax.ShapeDtypeStruct((bsz, t, SSD_W), F32),
            jax.ShapeDtypeStruct((bsz, t, SSD_CONV_CH), F32),
            jax.ShapeDtypeStruct((bsz, t, LANES), F32),
            jax.ShapeDtypeStruct((bsz, t, 3 * HY_W), F32),
        ],
        scratch_shapes=[pltpu.VMEM((d, COL_END), BF16)],
        compiler_params=_params("arbitrary", "arbitrary"),
        name="inproj",
    )(x_lat, x_ctx, x_lat, x_lat, mods, mods, g_mix, w_in, gq, gk, cs, sn, bd, conv_w, conv_b)
    return outs


def _attn_kernel(qt_ref, k_ref, vt_ref, o_ref, st_ref, pt_ref):
    k = k_ref[0]
    vt = vt_ref[0]
    rep = ATT_HEADS // ATT_KV_HEADS
    tq = qt_ref.shape[2]
    sub = min(tq, ATT_SUB)
    ones = jnp.ones((2 * SUBLANES, vt.shape[1]), vt.dtype)
    vtg = [jnp.concatenate([vt[g * HEAD_DIM:(g + 1) * HEAD_DIM, :], ones], axis=0) for g in range(ATT_KV_HEADS)]
    units = [(c0, hd) for c0 in range(0, tq, sub) for hd in range(ATT_HEADS)]

    def scores(u):
        c0, hd = units[u]
        qh = qt_ref[0, hd * HEAD_DIM:(hd + 1) * HEAD_DIM, c0:c0 + sub]
        zero = jnp.zeros_like(qh)
        w = jnp.concatenate([qh, zero] if hd < rep else [zero, qh], axis=0)
        st = _dot(k, w)
        st_ref[u % 2] = st
        return jnp.max(st, axis=0, keepdims=True)

    m_next = scores(0)
    outs = []
    for u, (c0, hd) in enumerate(units):
        m = m_next
        if u + 1 < len(units):
            m_next = scores(u + 1)
        pt_ref[u % 2] = jnp.exp2(st_ref[u % 2] - m).astype(BF16)
        ot = _dot(vtg[hd // rep], pt_ref[u % 2])
        outs.append(ot[:HEAD_DIM] / ot[HEAD_DIM:HEAD_DIM + 1])
        if hd == ATT_HEADS - 1:
            o_ref[0, c0:c0 + sub, :] = jnp.concatenate(outs, axis=0).T.astype(o_ref.dtype)
            outs = []


def _attention(qt, k, vt, q_row0, n_q, k_row0, n_k, tq):
    bsz = qt.shape[0]
    kblk = k_row0 // n_k
    q_tile0 = q_row0 // tq
    n_q_tiles = n_q // tq
    assert kblk * n_k == k_row0 and q_tile0 * tq == q_row0 and n_q_tiles * tq == n_q
    assert ATT_KV_HEADS == 2
    return pl.pallas_call(
        _attn_kernel,
        grid=(bsz, n_q_tiles),
        in_specs=[
            pl.BlockSpec((1, ATT_W, tq), lambda b, j: (b, 0, q_tile0 + j)),
            pl.BlockSpec((1, n_k, KV_W), lambda b, j: (b, kblk, 0)),
            pl.BlockSpec((1, KV_W, n_k), lambda b, j: (b, 0, kblk)),
        ],
        out_specs=pl.BlockSpec((1, tq, ATT_W), lambda b, j: (b, j, 0)),
        out_shape=jax.ShapeDtypeStruct((bsz, n_q_tiles * tq, ATT_W), BF16),
        scratch_shapes=[pltpu.VMEM((2, n_k, min(tq, ATT_SUB)), F32), pltpu.VMEM((2, n_k, min(tq, ATT_SUB)), BF16)],
        compiler_params=_params("arbitrary", "arbitrary"),
        name="attention",
    )(qt, k, vt)


def _ssd_kernel(xbc_ref, z_ref, dt_ref, dtb_ref, alog_ref, dsk_ref, nw_ref, bd_ref,
                o_ref, hf_ref, hb_ref, hbe_ref, *, n_lat_blocks):
    q = SSD_CHUNK
    nh = SSD_HEADS
    phase = pl.program_id(1)
    step = pl.program_id(2)
    block = jnp.where(phase == 0, n_lat_blocks - step, jnp.where(step < 1, n_lat_blocks, step - 1))

    lane = lax.broadcasted_iota(jnp.int32, (1, LANES), 1)
    a_row = jnp.where(lane < 2 * nh, -jnp.exp(alog_ref[...]), 0.0)
    tt = lax.broadcasted_iota(jnp.int32, (q, q), 0)
    ss = lax.broadcasted_iota(jnp.int32, (q, q), 1)
    lower = (ss <= tt)
    upper = (ss >= tt)
    lmat = jnp.where(lower, 1.0, 0.0).astype(BF16)
    umat = jnp.where(upper, 1.0, 0.0).astype(BF16)

    @pl.when(step == 0)
    def _():
        hf_ref[...] = jnp.zeros_like(hf_ref)
        hb_ref[...] = jnp.zeros_like(hb_ref)

    def load_chunk(ci):
        rows = slice(ci * q, (ci + 1) * q)
        xbc = xbc_ref[0, rows, :]
        dt = _softplus(dt_ref[0, rows, :] + dtb_ref[...])
        return rows, xbc, dt, dt * a_row

    def heads_x(x):
        return [x[:, hd * SSD_HEAD_DIM:(hd + 1) * SSD_HEAD_DIM].astype(BF16) for hd in range(nh)]

    def backward_prepare(ci):
        _, xbc, dt, a = load_chunk(ci)
        bt = xbc[:, SSD_W:SSD_W + SSD_GROUPS * SSD_STATE].T
        dt_t = dt.T
        suf_t = _dot3_left(a.T, lmat)
        terms = []
        for hd in range(nh):
            g = hd // SSD_HEADS_PER_GROUP
            row_b = suf_t[nh + hd:nh + hd + 1, :]
            total = row_b[:, 0:1]
            w_t = jnp.exp(total - row_b) * dt_t[nh + hd:nh + hd + 1, :]
            terms.append((jnp.exp(total), (bt[g * SSD_STATE:(g + 1) * SSD_STATE, :] * w_t).astype(BF16)))
        return terms, heads_x(xbc[:, :SSD_W])

    def backward_state(ci, prepared):
        terms, xh = prepared
        chunk = block * SSD_BLOCK + ci
        for hd in range(nh):
            decay, bw = terms[hd]
            prev = hb_ref[hd]
            hbe_ref[chunk, hd] = prev
            hb_ref[hd] = prev * decay + _dot(bw, xh[hd])

    def forward_prepare(ci):
        rows, xbc, dt, a = load_chunk(ci)
        x = xbc[:, :SSD_W]
        bt = xbc[:, SSD_W:SSD_W + SSD_GROUPS * SSD_STATE].T
        cmat = xbc[:, SSD_W + SSD_GROUPS * SSD_STATE:]
        a_t = a.T
        pre = _dot3_right(lmat, a)
        pre_t = _dot3_left(a_t, umat)
        cgs = [cmat[:, g * SSD_STATE:(g + 1) * SSD_STATE].astype(BF16) for g in range(SSD_GROUPS)]
        btgs = [bt[g * SSD_STATE:(g + 1) * SSD_STATE, :] for g in range(SSD_GROUPS)]
        cbs = [_dot(cgs[g], btgs[g].astype(BF16)) for g in range(SSD_GROUPS)]
        return dict(rows=rows, x=x, xh=heads_x(x), a=a, a_t=a_t, dt_t=dt.T, pre=pre, pre_t=pre_t,
                    cgs=cgs, btgs=btgs, cbs=cbs)

    def forward_mix(p):
        pre, pre_t, dt_t = p["pre"], p["pre_t"], p["dt_t"]
        suf = pre[q - 1:q, :] - pre + p["a"]
        suf_t = pre_t[:, q - 1:q] - pre_t + p["a_t"]
        neg_inf = jnp.float32(-jnp.inf)
        p["y_in"], p["scale"], p["upd"] = [], [], []
        for hd in range(nh):
            g = hd // SSD_HEADS_PER_GROUP
            colf = pre[:, hd:hd + 1]
            rowf = pre_t[hd:hd + 1, :]
            colb = suf[:, nh + hd:nh + hd + 1]
            rowb = suf_t[nh + hd:nh + hd + 1, :]
            wf = jnp.exp(jnp.where(lower, colf - rowf, neg_inf)) * dt_t[hd:hd + 1, :]
            wb = jnp.exp(jnp.where(upper, colb - rowb, neg_inf)) * dt_t[nh + hd:nh + hd + 1, :]
            p["y_in"].append(_dot((p["cbs"][g] * (wf + wb)).astype(BF16), p["xh"][hd]))
            p["scale"].append((jnp.exp(colf), jnp.exp(colb)))
            total = rowf[:, q - 1:q]
            w_t = jnp.exp(total - rowf) * dt_t[hd:hd + 1, :]
            p["upd"].append((jnp.exp(total), (p["btgs"][g] * w_t).astype(BF16)))

    def forward_state(ci, p):
        chunk = block * SSD_BLOCK + ci
        rows, x = p["rows"], p["x"]
        ys = []
        for hd in range(nh):
            cg = p["cgs"][hd // SSD_HEADS_PER_GROUP]
            hf = hf_ref[hd]
            ef, eb = p["scale"][hd]
            y = p["y_in"][hd] + _dot(cg, hf.astype(BF16)) * ef + _dot(cg, hbe_ref[chunk, hd].astype(BF16)) * eb
            decay, bw = p["upd"][hd]
            hf_ref[hd] = hf * decay + _dot(bw, p["xh"][hd])
            ys.append(y)
        y = jnp.concatenate(ys, axis=-1) + x * dsk_ref[...]
        gz = y * _silu(z_ref[0, rows, :])
        ms = _dot((gz * gz).astype(BF16), bd_ref[...]) * (1.0 / (SSD_W // SSD_GROUPS))
        o_ref[0, rows, :] = (gz * lax.rsqrt(ms + EPS) * nw_ref[...]).astype(o_ref.dtype)

    @pl.when(phase == 0)
    def _():
        prepared = [backward_prepare(ci) for ci in range(SSD_BLOCK)]
        for ci in reversed(range(SSD_BLOCK)):
            backward_state(ci, prepared[ci])

    @pl.when(phase == 1)
    def _():
        prepared = [forward_prepare(ci) for ci in range(SSD_BLOCK)]
        for p in prepared:
            forward_mix(p)
        for ci in range(SSD_BLOCK):
            forward_state(ci, prepared[ci])


def _ssd(xbc, z, dt_raw, dt_bias, a_log, d_skip, norm_w, bd, n_lat):
    bsz, t, _ = z.shape
    rows = SSD_BLOCK * SSD_CHUNK
    n_lat_blocks = n_lat // rows
    n_blocks = t // rows
    assert n_blocks == n_lat_blocks + 1 and n_lat_blocks * rows == n_lat

    def block_of(p, s):
        return jnp.where(p == 0, n_lat_blocks - s, jnp.where(s < 1, n_lat_blocks, s - 1))

    def out_block(p, s):
        return jnp.where(p == 0, n_lat_blocks, block_of(p, s))

    const = lambda shape: pl.BlockSpec(shape, lambda b, p, s: tuple(0 for _ in shape))
    return pl.pallas_call(
        functools.partial(_ssd_kernel, n_lat_blocks=n_lat_blocks),
        grid=(bsz, 2, n_blocks),
        in_specs=[
            pl.BlockSpec((1, rows, SSD_CONV_CH), lambda b, p, s: (b, block_of(p, s), 0)),
            pl.BlockSpec((1, rows, SSD_W), lambda b, p, s: (b, block_of(p, s), 0)),
            pl.BlockSpec((1, rows, LANES), lambda b, p, s: (b, block_of(p, s), 0)),
            const((1, LANES)),
            const((1, LANES)),
            const((1, SSD_W)),
            const((1, SSD_W)),
            const((SSD_W, SSD_W)),
        ],
        out_specs=pl.BlockSpec((1, rows, SSD_W), lambda b, p, s: (b, out_block(p, s), 0)),
        out_shape=jax.ShapeDtypeStruct((bsz, t, SSD_W), BF16),
        scratch_shapes=[
            pltpu.VMEM((SSD_HEADS, SSD_STATE, SSD_HEAD_DIM), F32),
            pltpu.VMEM((SSD_HEADS, SSD_STATE, SSD_HEAD_DIM), F32),
            pltpu.VMEM((n_blocks * SSD_BLOCK, SSD_HEADS, SSD_STATE, SSD_HEAD_DIM), F32),
        ],
        compiler_params=_params("arbitrary", "arbitrary", "arbitrary"),
        name="ssd",
    )(xbc, z, dt_raw, dt_bias, a_log, d_skip, norm_w, bd)


def _hyfilt_kernel(feat_ref, win_ref, w1_ref, b1_ref, fr_ref, w2_ref, b2_ref, w3_ref, o_ref):
    hp = lax.Precision.HIGHEST
    fr = fr_ref[...]
    h1 = jnp.sin(fr * (jnp.dot(feat_ref[...], w1_ref[...], precision=hp, preferred_element_type=F32) + b1_ref[...]))
    h2 = jnp.sin(fr * (jnp.dot(h1, w2_ref[...], precision=hp, preferred_element_type=F32) + b2_ref[...]))
    h = _dot_x3(h2, w3_ref[...])
    win = win_ref[...]
    first_tile = pl.program_id(0) == 0
    row = lax.broadcasted_iota(jnp.int32, win.shape, 0)
    for order in range(HY_ORDER):
        for direction in range(2):
            c0 = (direction * HY_ORDER + order) * HY_W
            f = h[:, c0:c0 + HY_W] * win
            if direction == 1:
                f = jnp.where(first_tile & (row == 0), 0.0, f)
            o_ref[order * 2 + direction] = f


def _hy_positional(length):
    n = np.arange(length, dtype=np.float64)
    t = n / max(length - 1, 1)
    bands = np.linspace(1e-4, HY_BANDS - 1, HY_BANDS)
    wpos = (2 * math.pi / length) * n
    feats = np.concatenate([t[:, None], np.cos(wpos[:, None] * bands), -np.sin(wpos[:, None] * bands)], axis=-1)
    feats = np.pad(feats, ((0, 0), (0, LANES - HY_POS_DIM)))
    deltas = np.abs(np.linspace(math.log(HY_TARGET) / HY_SLOW_DECAY, math.log(HY_TARGET) / HY_FAST_DECAY, HY_W))
    window = np.exp(-t[:, None] * deltas)
    return jnp.asarray(feats, F32), jnp.asarray(window, F32)


def _hyena_filters(length, w1, b1, freq, w2, b2, w3):
    feats, window = _hy_positional(length)
    tl = min(length, 512)
    hid = HY_FILTER_HID
    w1p = jnp.pad(w1, ((0, LANES - HY_POS_DIM), (0, 0)))
    const = lambda shape: pl.BlockSpec(shape, lambda i: tuple(0 for _ in shape))
    return pl.pallas_call(
        _hyfilt_kernel,
        grid=(length // tl,),
        in_specs=[
            pl.BlockSpec((tl, LANES), lambda i: (i, 0)),
            pl.BlockSpec((tl, HY_W), lambda i: (i, 0)),
            const((LANES, hid)), const((1, hid)), const((1, hid)),
            const((hid, hid)), const((1, hid)), const((hid, 2 * HY_ORDER * HY_W)),
        ],
        out_specs=pl.BlockSpec((2 * HY_ORDER, tl, HY_W), lambda i: (0, i, 0)),
        out_shape=jax.ShapeDtypeStruct((2 * HY_ORDER, length, HY_W), F32),
        compiler_params=_params("arbitrary"),
        name="hyena_filters",
    )(feats, window, w1p, b1.reshape(1, hid), freq.reshape(1, hid), w2, b2.reshape(1, hid), w3)


def _dft_tables(length):
    n = 2 * length
    n2 = HY_N2
    n1 = n // n2
    half = n1 // 2
    k1 = np.arange(half, dtype=np.float64) + 0.5
    idx = (n2 * np.arange(half)[None, None, :] + np.arange(n2)[:, None, None])
    ang = 2 * np.pi * k1[None, :, None] * idx / n
    fa = np.concatenate([np.cos(ang), -np.sin(ang)], axis=1)
    kk = np.arange(n2, dtype=np.float64)
    angb = 2 * np.pi * kk[:, None] * kk[None, :] / n2
    cr, sr = np.cos(angb), np.sin(angb)
    fb = np.block([[cr, sr], [-sr, cr]])
    fbi = np.block([[cr, -sr], [sr, cr]])
    idxo = (n2 * np.arange(half)[None, :, None] + np.arange(n2)[:, None, None])
    ango = 2 * np.pi * k1[None, None, :] * idxo / n
    ga = np.concatenate([np.cos(ango), -np.sin(ango)], axis=2) * (2.0 / n)
    f32 = lambda a: jnp.asarray(a, F32).astype(BF16)
    return f32(fa), f32(fb), f32(fbi), f32(ga)


def _seq_pitch(n2n):
    return n2n + SUBLANES


def _to_pitched(dst_ref, val, n2n):
    pitch = _seq_pitch(n2n)
    for i in range(val.shape[0] // n2n):
        dst_ref[i * pitch:i * pitch + n2n, :] = val[i * n2n:(i + 1) * n2n, :]


def _hy_forward_a(src_ref, a_ref, fa_ref):
    n2n, two_n1, half = fa_ref.shape
    zp, ap = _seq_pitch(n2n), _seq_pitch(two_n1)

    def body(n2, carry):
        zs = src_ref[pl.ds(n2, half, stride=zp), :].astype(BF16)
        a_ref[pl.ds(pl.multiple_of(n2 * ap, SUBLANES), two_n1), :] = _dot(fa_ref[n2], zs)
        return carry

    lax.fori_loop(0, n2n, body, 0, unroll=HY_UNROLL)


def _hy_kspec_kernel(hf_ref, hb_ref, fa_ref, fb_ref, o_ref, af_ref, ab_ref, hfp_ref, hbp_ref):
    n2n, two_n1, _ = fa_ref.shape
    n1 = two_n1 // 2
    ap = _seq_pitch(two_n1)
    _to_pitched(hfp_ref, hf_ref[0], n2n)
    _to_pitched(hbp_ref, hb_ref[0], n2n)
    _hy_forward_a(hfp_ref, af_ref, fa_ref)
    _hy_forward_a(hbp_ref, ab_ref, fa_ref)
    fb = fb_ref[...]

    def body(k1, carry):
        def spectrum(a_ref):
            ar = a_ref[pl.ds(k1, n2n, stride=ap), :]
            ai = a_ref[pl.ds(n1 + k1, n2n, stride=ap), :]
            return _dot(fb, jnp.concatenate([ar, ai], axis=0).astype(BF16))

        xf, xb = spectrum(af_ref), spectrum(ab_ref)
        o_ref[0, k1] = jnp.concatenate([xf[:n2n] + xb[:n2n], xf[n2n:] - xb[n2n:]], axis=0)
        return carry

    lax.fori_loop(0, n1, body, 0, unroll=HY_UNROLL)


def _hy_kspec(filt, tables):
    fa, fb, _, _ = tables
    _, length, _ = filt.shape
    n2n, two_n1, _ = fa.shape
    n1 = two_n1 // 2
    nh = HY_W // LANES
    return pl.pallas_call(
        _hy_kspec_kernel,
        grid=(HY_ORDER, nh),
        in_specs=[
            pl.BlockSpec((1, length, LANES), lambda o, h: (2 * o, 0, h)),
            pl.BlockSpec((1, length, LANES), lambda o, h: (2 * o + 1, 0, h)),
            pl.BlockSpec(fa.shape, lambda o, h: (0, 0, 0), pipeline_mode=pl.Buffered(1)),
            pl.BlockSpec(fb.shape, lambda o, h: (0, 0)),
        ],
        out_specs=pl.BlockSpec((1, n1, 2 * n2n, LANES), lambda o, h: (o, 0, 0, h)),
        out_shape=jax.ShapeDtypeStruct((HY_ORDER, n1, 2 * n2n, HY_W), F32),
        scratch_shapes=[pltpu.VMEM((n2n * _seq_pitch(two_n1), LANES), F32),
                        pltpu.VMEM((n2n * _seq_pitch(two_n1), LANES), F32),
                        pltpu.VMEM((length // n2n * _seq_pitch(n2n), LANES), F32),
                        pltpu.VMEM((length // n2n * _seq_pitch(n2n), LANES), F32)],
        compiler_params=_params("arbitrary", "arbitrary"),
        name="hyena_kspec",
    )(filt, filt, fa, fb)


def _hy_conv_kernel(z_ref, g_ref, fa_ref, fb_ref, fbi_ref, ga_ref, k_ref, bias_ref,
                    o_ref, a_ref, c_ref, zc_ref, gc_ref):
    n2n, two_n1, half = fa_ref.shape
    n1 = two_n1 // 2
    zp, ap = _seq_pitch(n2n), _seq_pitch(two_n1)
    _to_pitched(zc_ref, z_ref[0], n2n)
    _to_pitched(gc_ref, g_ref[0], n2n)
    _hy_forward_a(zc_ref, a_ref, fa_ref)
    fb = fb_ref[...]
    fbi = fbi_ref[...]

    def body_b(grp, carry):
        k1s = [grp * HY_UNROLL + i for i in range(HY_UNROLL)]
        sl = [(pl.ds(k1, n2n, stride=ap), pl.ds(n1 + k1, n2n, stride=ap)) for k1 in k1s]
        xs = [_dot(fb, jnp.concatenate([a_ref[re, :], a_ref[im, :]], axis=0).astype(BF16)) for re, im in sl]
        ys = []
        for k1, x in zip(k1s, xs):
            kk = k_ref[0, k1]
            xr, xi = x[:n2n], x[n2n:]
            kr, ki = kk[:n2n], kk[n2n:]
            ys.append(jnp.concatenate([xr * kr - xi * ki, xr * ki + xi * kr], axis=0).astype(BF16))
        for (re, im), y in zip(sl, ys):
            c = _dot(fbi, y)
            c_ref[re, :] = c[:n2n]
            c_ref[im, :] = c[n2n:]
        return carry

    lax.fori_loop(0, n1 // HY_UNROLL, body_b, 0)
    bias = bias_ref[0]

    def body_c(m2, carry):
        rows = pl.ds(m2, half, stride=zp)
        c = c_ref[pl.ds(pl.multiple_of(m2 * ap, SUBLANES), two_n1), :].astype(BF16)
        y = _dot(ga_ref[m2], c)
        a_ref[rows, :] = gc_ref[rows, :] * (y + zc_ref[rows, :] * bias)
        return carry

    lax.fori_loop(0, n2n, body_c, 0, unroll=HY_UNROLL)
    for i in range(half):
        o_ref[0, i * n2n:(i + 1) * n2n, :] = a_ref[i * zp:i * zp + n2n, :]


def _hy_conv(z_arr, z_blk, g_arr, g_blk, kspec, order, bias3, tables, n_lat):
    fa, fb, fbi, ga = tables
    bsz = z_arr.shape[0]
    n2n, two_n1, half = fa.shape
    n1 = two_n1 // 2
    nh = HY_W // LANES
    resident = lambda shape: pl.BlockSpec(shape, lambda h, b_: tuple(0 for _ in shape), pipeline_mode=pl.Buffered(1))
    return pl.pallas_call(
        _hy_conv_kernel,
        grid=(nh, bsz),
        in_specs=[
            pl.BlockSpec((1, n_lat, LANES), lambda h, b_: (b_, 0, z_blk + h)),
            pl.BlockSpec((1, n_lat, LANES), lambda h, b_: (b_, 0, g_blk + h)),
            resident(fa.shape), resident(fb.shape), resident(fbi.shape), resident(ga.shape),
            pl.BlockSpec((1, n1, 2 * n2n, LANES), lambda h, b_: (order, 0, 0, h), pipeline_mode=pl.Buffered(1)),
            pl.BlockSpec((1, 1, LANES), lambda h, b_: (order, 0, h)),
        ],
        out_specs=pl.BlockSpec((1, n_lat, LANES), lambda h, b_: (b_, 0, h)),
        out_shape=jax.ShapeDtypeStruct((bsz, n_lat, HY_W), F32),
        scratch_shapes=[pltpu.VMEM((n2n * _seq_pitch(two_n1), LANES), F32),
                        pltpu.VMEM((n2n * _seq_pitch(two_n1), LANES), F32),
                        pltpu.VMEM((half * _seq_pitch(n2n), LANES), F32),
                        pltpu.VMEM((half * _seq_pitch(n2n), LANES), F32)],
        compiler_params=_params("arbitrary", "arbitrary"),
        name="hyena_conv",
    )(z_arr, g_arr, fa, fb, fbi, ga, kspec, bias3)


def _hyena_ctx_kernel(u_ref, filt_ref, bias_ref, fd_ref, gd_ref, o_ref, *, n_ctx):
    fd = fd_ref[...]
    gd = gd_ref[...]
    nn = 2 * n_ctx
    u = u_ref[0]
    v, x1, x2 = u[:, :HY_W], u[:, HY_W:2 * HY_W], u[:, 2 * HY_W:]

    def conv(zin, order):
        hf = _dot(fd, filt_ref[2 * order].astype(BF16))
        hb = _dot(fd, filt_ref[2 * order + 1].astype(BF16))
        kr, ki = hf[:nn] + hb[:nn], hf[nn:] - hb[nn:]
        zz = _dot(fd, zin.astype(BF16))
        zr, zi = zz[:nn], zz[nn:]
        y = jnp.concatenate([zr * kr - zi * ki, zr * ki + zi * kr], axis=0).astype(BF16)
        return _dot(gd, y) + zin * bias_ref[order]

    y1 = x1 * conv(v, 0)
    o_ref[0] = x2 * conv(y1, 1)


def _hyena_ctx(hy, filt, bias, n_lat, n_ctx):
    bsz = hy.shape[0]
    nn = 2 * n_ctx
    k = np.arange(nn, dtype=np.float64)
    t = np.arange(n_ctx, dtype=np.float64)
    ang = 2 * np.pi * k[:, None] * t[None, :] / nn
    fd = jnp.asarray(np.concatenate([np.cos(ang), -np.sin(ang)], axis=0), F32).astype(BF16)
    gd = jnp.asarray(np.concatenate([np.cos(ang.T), -np.sin(ang.T)], axis=1) / nn, F32).astype(BF16)
    blk = n_lat // n_ctx
    assert blk * n_ctx == n_lat
    return pl.pallas_call(
        functools.partial(_hyena_ctx_kernel, n_ctx=n_ctx),
        grid=(bsz,),
        in_specs=[
            pl.BlockSpec((1, n_ctx, 3 * HY_W), lambda b: (b, blk, 0)),
            pl.BlockSpec((2 * HY_ORDER, n_ctx, HY_W), lambda b: (0, 0, 0)),
            pl.BlockSpec((HY_ORDER, 1, HY_W), lambda b: (0, 0, 0)),
            pl.BlockSpec((2 * nn, n_ctx), lambda b: (0, 0)),
            pl.BlockSpec((n_ctx, 2 * nn), lambda b: (0, 0)),
        ],
        out_specs=pl.BlockSpec((1, n_ctx, HY_W), lambda b: (b, 0, 0)),
        out_shape=jax.ShapeDtypeStruct((bsz, n_ctx, HY_W), F32),
        compiler_params=_params("arbitrary"),
        name="hyena_ctx",
    )(hy, filt, bias, fd, gd)


def _hyena_latent(hy, filt, bias, n_lat, tables):
    kspec = _hy_kspec(filt, tables)
    bias3 = bias.reshape(HY_ORDER, 1, HY_W)
    nb = HY_W // LANES
    y1 = _hy_conv(hy, 0, hy, nb, kspec, 0, bias3, tables, n_lat)
    return _hy_conv(y1, 0, hy, 2 * nb, kspec, 1, bias3, tables, n_lat)


def _outproj_kernel(x_ref, att_ref, ssd_ref, hy_ref, gt_ref, wa_ref, ws_ref, wh_ref, o_ref):
    mix = _dot(att_ref[0], wa_ref[0]) + _dot(ssd_ref[0], ws_ref[0]) + _dot(hy_ref[0].astype(BF16), wh_ref[0])
    o_ref[0] = x_ref[0] + gt_ref[0] * mix


def _outproj(x, att, ssd, ssd_row0, hy, mods, mod_row, w_out, layer, tm):
    bsz, n, _ = att.shape
    d = x.shape[-1]
    ssd_blk0 = ssd_row0 // tm
    assert ssd_blk0 * tm == ssd_row0 and n % tm == 0
    tok = lambda w: pl.BlockSpec((1, tm, w), lambda j, b: (b, j, 0))
    return pl.pallas_call(
        _outproj_kernel,
        grid=(n // tm, bsz),
        in_specs=[
            tok(d), tok(ATT_W),
            pl.BlockSpec((1, tm, SSD_W), lambda j, b: (b, ssd_blk0 + j, 0)),
            tok(HY_W),
            pl.BlockSpec((1, 1, d), lambda j, b: (mod_row(b), 0, 2)),
            pl.BlockSpec((1, ATT_W, d), lambda j, b: (layer, 0, 0)),
            pl.BlockSpec((1, SSD_W, d), lambda j, b: (layer, ATT_W // SSD_W, 0)),
            pl.BlockSpec((1, HY_W, d), lambda j, b: (layer, (ATT_W + SSD_W) // HY_W, 0)),
        ],
        out_specs=tok(d),
        out_shape=jax.ShapeDtypeStruct((bsz, n, d), F32),
        compiler_params=_params("arbitrary", "arbitrary"),
        name="outproj",
    )(x, att, ssd, hy, mods, w_out, w_out, w_out)


def _route(logits_t, bias_col):
    scores = jax.nn.sigmoid(logits_t)
    sel = scores + bias_col
    neg_inf = jnp.float32(-jnp.inf)
    rows = [sel[e:e + 1, :] for e in range(N_EXPERTS)]
    grp = []
    for g in range(N_GROUPS):
        r = rows[g * EXPERTS_PER_GROUP:(g + 1) * EXPERTS_PER_GROUP]
        top = functools.reduce(jnp.maximum, r)
        taken = None
        rest = []
        for ri in r:
            is_top = (ri == top) if taken is None else (ri == top) & jnp.logical_not(taken)
            rest.append(jnp.where(is_top, neg_inf, ri))
            taken = is_top if taken is None else taken | is_top
        grp.append(top + functools.reduce(jnp.maximum, rest))
    best = jnp.zeros(grp[0].shape, jnp.int32)
    cur = grp[0]
    for g in range(1, N_GROUPS):
        upd = grp[g] > cur
        best = jnp.where(upd, g, best)
        cur = jnp.where(upd, grp[g], cur)
    picked = []
    for e in range(N_EXPERTS):
        g, i = divmod(e, EXPERTS_PER_GROUP)
        rank = jnp.zeros(best.shape, jnp.int32)
        for j in range(EXPERTS_PER_GROUP):
            if j == i:
                continue
            other = rows[g * EXPERTS_PER_GROUP + j]
            ahead = (other > rows[e]) | ((other == rows[e]) & (j < i))
            rank = rank + ahead.astype(jnp.int32)
        keep = (best == g) & (rank < 2)
        picked.append(jnp.where(keep, scores[e:e + 1, :], 0.0))
    total = functools.reduce(lambda u, w: u + w, picked)
    return jnp.concatenate(picked, axis=0) / total, best


def _ffn_input(x_ref, sh_ref, sc_ref, g_ref):
    x = x_ref[...].reshape(MOE_TILE, D_MODEL)
    ms = jnp.mean(x * x, axis=-1, keepdims=True)
    t = x * lax.rsqrt(ms + EPS) * g_ref[...]
    return x, t * (1.0 + sc_ref[0]) + sh_ref[0]


def _route_kernel(x_ref, sh_ref, sc_ref, g_ref, wr_ref, rb_ref, rt_ref, srow_ref, tbl_ref):
    tm = MOE_TILE
    _, t = _ffn_input(x_ref, sh_ref, sc_ref, g_ref)
    logits = _dot_x3(t, wr_ref[...])
    comb_t, best = _route(logits.T[:N_EXPERTS, :], rb_ref[...])

    member = [jnp.where(best == g, 1.0, 0.0) for g in range(N_GROUPS)]
    comb4 = functools.reduce(
        lambda u, w: u + w,
        [member[g] * comb_t[g * EXPERTS_PER_GROUP:(g + 1) * EXPERTS_PER_GROUP, :] for g in range(N_GROUPS)])
    masks = jnp.concatenate(member + [jnp.zeros((8 - N_GROUPS, tm), F32)], axis=0)
    earlier = jnp.where(lax.broadcasted_iota(jnp.int32, (tm, tm), 0) < lax.broadcasted_iota(jnp.int32, (tm, tm), 1),
                        1.0, 0.0).astype(BF16)
    ranks = _dot(masks.astype(BF16), earlier)
    cnt = jnp.sum(masks, axis=1, keepdims=True)
    padded = jnp.ceil(cnt * (1.0 / MOE_CHUNK)) * MOE_CHUNK
    lane = lax.broadcasted_iota(jnp.int32, (1, LANES), 1).astype(F32) * MOE_CHUNK
    start = jnp.zeros((1, 1), F32)
    slot = jnp.zeros((1, tm), F32)
    gid = jnp.zeros((1, LANES), jnp.int32)
    for g in range(N_GROUPS):
        slot = slot + member[g] * (start + ranks[g:g + 1, :])
        start = start + padded[g:g + 1, :]
        gid = gid + jnp.where(lane >= start, 1, 0)
    tbl_ref[0] = gid
    srow_ref[0] = slot.astype(jnp.int32)
    rt_t = jnp.concatenate([comb4, slot, jnp.zeros((LANES - EXPERTS_PER_GROUP - 1, tm), F32)], axis=0)
    rt_ref[...] = rt_t.T


def _experts_kernel(tbl_ref, x_ref, sh_ref, sc_ref, gt_ref, g_ref, rt_ref, srow_ref, wg_ref, wu_ref, wd_ref, gf_ref,
                    *rest, final, tile_of, n_grid):
    o_ref, xp_ref, yp_ref = rest[-3:]
    tm, ch = MOE_TILE, MOE_CHUNK
    n_slots = MOE_CHUNKS * ch
    tile = tile_of(*[pl.program_id(a) for a in range(n_grid)])
    x, t = _ffn_input(x_ref, sh_ref, sc_ref, g_ref)
    rt = rt_ref[...]
    slot_col = rt[:, EXPERTS_PER_GROUP:EXPERTS_PER_GROUP + 1].astype(jnp.int32)
    gather = jnp.where(lax.broadcasted_iota(jnp.int32, (n_slots, tm), 0) == srow_ref[0], 1.0, 0.0).astype(BF16)
    scatter = jnp.where(lax.broadcasted_iota(jnp.int32, (tm, n_slots), 1) == slot_col, 1.0, 0.0).astype(BF16)
    xp_ref[...] = _dot(gather, t.astype(BF16)).astype(BF16)
    w_slot = _dot3_right(gather, rt)

    def run_chunk(rows, grp):
        xc = xp_ref[rows, :]
        e0 = grp * EXPERTS_PER_GROUP
        gu_next = _dot(xc, wg_ref[0, e0]), _dot(xc, wu_ref[0, e0])
        acc = jnp.zeros((ch, D_MODEL), F32)
        for j in range(EXPERTS_PER_GROUP):
            gate, up = gu_next
            if j + 1 < EXPERTS_PER_GROUP:
                gu_next = _dot(xc, wg_ref[0, e0 + j + 1]), _dot(xc, wu_ref[0, e0 + j + 1])
            hid = _silu(gate) * up * w_slot[rows, j:j + 1]
            acc = acc + _dot(hid.astype(BF16), wd_ref[0, e0 + j])
        yp_ref[rows, :] = acc.astype(BF16)

    for c in range(MOE_CHUNKS):
        rows = slice(c * ch, (c + 1) * ch)
        grp = tbl_ref[tile, c]
        pl.when(grp < N_GROUPS)(functools.partial(run_chunk, rows, grp))

        @pl.when(grp >= N_GROUPS)
        def _():
            yp_ref[rows, :] = jnp.zeros((ch, D_MODEL), BF16)

    y = x + gt_ref[0] * _dot(scatter, yp_ref[...])
    if final:
        y = y * lax.rsqrt(jnp.mean(y * y, axis=-1, keepdims=True) + EPS) * gf_ref[...]
    o_ref[...] = y.reshape(o_ref.shape)


def _moe_tiles(x, mods, mod_row, g_ffn, w_router, router_bias, wgu, wd, layer, g_final, final, out_rows,
               grid, x_block, x_index, out_index, tile_of, n_tiles):
    bsz, _, d = x.shape
    tm = MOE_TILE
    ng = len(grid)
    const = lambda shape: pl.BlockSpec(shape, lambda *a: tuple(0 for _ in shape))
    mod = lambda col: pl.BlockSpec((1, 1, d), lambda *a: (mod_row(*a[:ng]), 0, col))
    xspec = pl.BlockSpec(x_block, lambda *a: x_index(*a[:ng]))
    rt, srow, tbl = pl.pallas_call(
        _route_kernel,
        grid=grid,
        in_specs=[xspec, mod(3), mod(4), const((1, d)), const((d, LANES)), const((N_EXPERTS, 1))],
        out_specs=[
            pl.BlockSpec((tm, LANES), lambda *a: (tile_of(*a), 0)),
            pl.BlockSpec((1, 1, tm), lambda *a: (tile_of(*a), 0, 0)),
            pl.BlockSpec((1, 1, LANES), lambda *a: (tile_of(*a), 0, 0)),
        ],
        out_shape=[
            jax.ShapeDtypeStruct((n_tiles * tm, LANES), F32),
            jax.ShapeDtypeStruct((n_tiles, 1, tm), jnp.int32),
            jax.ShapeDtypeStruct((n_tiles, 1, LANES), jnp.int32),
        ],
        compiler_params=_params(*["arbitrary"] * ng),
        name="moe_route",
    )(x, mods, mods, g_ffn, w_router, router_bias)

    resident = lambda shape: pl.BlockSpec(shape, lambda *a: (layer,) + tuple(0 for _ in shape[1:]),
                                          pipeline_mode=pl.Buffered(1))
    in_specs = [
        xspec, mod(3), mod(4), mod(5), const((1, d)),
        pl.BlockSpec((tm, LANES), lambda *a: (tile_of(*a[:ng]), 0)),
        pl.BlockSpec((1, 1, tm), lambda *a: (tile_of(*a[:ng]), 0, 0)),
        resident((1, N_EXPERTS, d, D_FF)),
        resident((1, N_EXPERTS, d, D_FF)),
        resident((1, N_EXPERTS, D_FF, d)),
        const((1, d)),
    ]
    args = [tbl.reshape(n_tiles, LANES), x, mods, mods, mods, g_ffn, rt, srow, *wgu, wd, g_final]
    return pl.pallas_call(
        functools.partial(_experts_kernel, final=final, tile_of=tile_of, n_grid=ng),
        grid_spec=pltpu.PrefetchScalarGridSpec(
            num_scalar_prefetch=1,
            grid=grid,
            in_specs=in_specs,
            out_specs=pl.BlockSpec(x_block, lambda *a: out_index(*a[:ng])),
            scratch_shapes=[pltpu.VMEM((MOE_CHUNKS * MOE_CHUNK, d), BF16), pltpu.VMEM((MOE_CHUNKS * MOE_CHUNK, d), BF16)],
        ),
        out_shape=jax.ShapeDtypeStruct((bsz, out_rows, d), F32),
        compiler_params=_params(*["arbitrary"] * ng),
        name="moe_experts",
    )(*args)


def _moe(x_lat, x_ctx, mods, g_ffn, w_router, router_bias, wgu, wd, layer, g_final, final):
    bsz, n_lat, d = x_lat.shape
    tm = MOE_TILE
    per_b = n_lat // tm
    common = (g_ffn, w_router, router_bias, wgu, wd, layer, g_final, final)
    lat_index = lambda b, j: (b, j, 0)
    out_lat = _moe_tiles(x_lat, mods, lambda b, j: b, *common, n_lat,
                         (bsz, per_b), (1, tm, d), lat_index, lat_index, lambda b, j: b * per_b + j, bsz * per_b)
    if x_ctx is None:
        return out_lat, None
    n_ctx = x_ctx.shape[1]
    nb = tm // n_ctx
    ctx_index = lambda i: (i, 0, 0)
    out_ctx = _moe_tiles(x_ctx, mods, lambda i: bsz, *common, n_ctx,
                         (bsz // nb,), (nb, n_ctx, d), ctx_index, ctx_index, lambda i: i, bsz // nb)
    return out_lat, out_ctx


def _block_ones(width, block):
    idx = np.arange(width) // block
    return jnp.asarray(idx[:, None] == idx[None, :], F32).astype(BF16)


def _rope_tables(n_lat, n_ctx):
    rows = n_lat // GRID_W
    row = np.repeat(np.arange(rows), GRID_W).astype(np.float64)
    col = np.tile(np.arange(GRID_W), rows).astype(np.float64)
    inv = ROPE_THETA ** (-np.arange(0, ROPE_AXIS_DIM, 2, dtype=np.float64) / ROPE_AXIS_DIM)
    ang = np.concatenate([row[:, None] * inv, col[:, None] * inv], axis=-1)
    ang = np.concatenate([ang, np.zeros((n_ctx, ang.shape[1]))], axis=0)
    cos = np.concatenate([np.cos(ang), np.cos(ang)], axis=-1)
    sin = np.concatenate([-np.sin(ang), np.sin(ang)], axis=-1)
    scale = Q_SCALE
    cs = np.tile(cos, (1, ATT_HEADS)) * scale
    sn = np.tile(sin, (1, ATT_HEADS)) * scale
    return jnp.asarray(cs, F32), jnp.asarray(sn, F32)


def kernel(x, c, ctx, c_ctx, w_mod, b_mod, g_mix, g_ffn, w_in, q_norm, k_norm, ssd_conv_w, ssd_conv_b,
           ssd_dt_bias, ssd_a_log, ssd_d, ssd_norm, hy_conv_w, hy_conv_b, hy_w1, hy_b1, hy_freq, hy_w2, hy_b2,
           hy_w3, hy_bias, w_out, w_router, router_bias, w_gate, w_up, w_down, g_final):
    bsz, n_lat, d = x.shape
    n_ctx = ctx.shape[1]
    depth = w_mod.shape[0]
    t = n_lat + n_ctx
    tm = TOKEN_TILE
    n_lat_tiles = n_lat // tm
    n_tiles = t // tm
    assert n_ctx == tm and n_lat % (HY_N2 * 8) == 0 and bsz < MOD_ROWS

    cvec = jnp.concatenate([c, c_ctx[None], jnp.zeros((MOD_ROWS - bsz - 1, d), F32)], axis=0)
    mods_all = _adaln(cvec, w_mod, b_mod)

    cs, sn = _rope_tables(n_lat, n_ctx)
    bd_head = _block_ones(ATT_W, HEAD_DIM)
    bd_ssd = _block_ones(SSD_W, SSD_W // SSD_GROUPS)
    tables = _dft_tables(n_lat)
    pad_row = lambda v: jnp.pad(v.reshape(1, -1), ((0, 0), (0, LANES - v.size)))
    w_out_b = w_out.astype(BF16)
    wgu_b = (w_gate.astype(BF16), w_up.astype(BF16))
    wd_b = w_down.astype(BF16)

    x_lat, x_ctx, ctx_blk = x, ctx, 0
    for i in range(depth):
        last = i == depth - 1
        mods = mods_all[i].reshape(MOD_ROWS, 1, 6 * d)
        q, kt, v, z, xbc, dt_raw, hy = _inproj(
            x_lat, x_ctx, ctx_blk, mods, g_mix[i].reshape(1, d), w_in, i,
            jnp.tile(q_norm[i], ATT_HEADS).reshape(1, ATT_W), jnp.tile(k_norm[i], ATT_KV_HEADS).reshape(1, KV_W),
            cs, sn, bd_head, jnp.concatenate([ssd_conv_w[i], hy_conv_w[i]], axis=-1),
            jnp.concatenate([ssd_conv_b[i], hy_conv_b[i]]).reshape(1, -1), n_lat_tiles)

        att = _attention(q, kt, v, 0, n_lat, 0, t, ATT_TILE)

        ssd = _ssd(xbc, z, dt_raw, pad_row(ssd_dt_bias[i]), pad_row(ssd_a_log[i]),
                   jnp.repeat(ssd_d[i], SSD_HEAD_DIM).reshape(1, SSD_W), ssd_norm[i].reshape(1, SSD_W),
                   bd_ssd, n_lat)

        hyp = (hy_w1[i], hy_b1[i], hy_freq[i], hy_w2[i], hy_b2[i], hy_w3[i])
        hy_l = _hyena_latent(hy, _hyena_filters(n_lat, *hyp), hy_bias[i], n_lat, tables)

        mid_lat = _outproj(x_lat, att, ssd, 0, hy_l, mods, lambda b: b, w_out_b, i, OUT_TILE)
        mid_ctx = None
        if not last:
            att_c = _attention(q, kt, v, n_lat, n_ctx, n_lat, n_ctx, n_ctx)
            hy_c = _hyena_ctx(hy, _hyena_filters(n_ctx, *hyp), hy_bias[i].reshape(HY_ORDER, 1, HY_W), n_lat, n_ctx)
            mid_ctx = _outproj(x_ctx, att_c, ssd, n_lat, hy_c, mods, lambda b: bsz, w_out_b, i, n_ctx)

        x_lat, x_ctx = _moe(mid_lat, mid_ctx, mods, g_ffn[i].reshape(1, d),
                            jnp.pad(w_router, ((0, 0), (0, LANES - N_EXPERTS))), router_bias.reshape(N_EXPERTS, 1),
                            wgu_b, wd_b, i, g_final.reshape(1, d), last)
    return x_lat
```

```python
import functools
import math

import jax
import jax.numpy as jnp
import numpy as np
from jax import lax
from jax.experimental import pallas as pl
from jax.experimental.pallas import tpu as pltpu

F32 = jnp.float32
BF16 = jnp.bfloat16

D_MODEL = 1024
GRID_W = 64
EPS = 1e-6

ATT_HEADS = 6
ATT_KV_HEADS = 2
HEAD_DIM = 64
ATT_W = ATT_HEADS * HEAD_DIM
KV_W = ATT_KV_HEADS * HEAD_DIM
ROPE_AXIS_DIM = HEAD_DIM // 2
ROPE_THETA = 10000.0
Q_SCALE = HEAD_DIM ** -0.5 * math.log2(math.e)

SSD_HEADS = 6
SSD_HEAD_DIM = 64
SSD_W = SSD_HEADS * SSD_HEAD_DIM
SSD_GROUPS = 2
SSD_STATE = 64
SSD_CHUNK = 128
SSD_CONV_CH = SSD_W + 2 * SSD_GROUPS * SSD_STATE
SSD_HEADS_PER_GROUP = SSD_HEADS // SSD_GROUPS
SSD_BLOCK = 2

HY_W = 256
HY_ORDER = 2
HY_BANDS = 16
HY_POS_DIM = 1 + 2 * HY_BANDS
HY_FILTER_HID = 64
HY_FAST_DECAY = 0.3
HY_SLOW_DECAY = 1.5
HY_TARGET = 1e-2
HY_N2 = 64
HY_UNROLL = 32

MIX_W = ATT_W + SSD_W + HY_W
N_EXPERTS = 16
N_GROUPS = 4
EXPERTS_PER_GROUP = N_EXPERTS // N_GROUPS
D_FF = 256

LANES = 128
SUBLANES = 8
TOKEN_TILE = 256
ATT_TILE = 1024
ATT_SUB = 512
OUT_TILE = 512
MOE_TILE = 512
MOE_CHUNK = 128
MOE_CHUNKS = MOE_TILE // MOE_CHUNK + N_GROUPS - 1
MOD_ROWS = 8
VMEM_LIMIT = 56 * 1024 * 1024

COL_Q = 0
COL_K = COL_Q + ATT_W
COL_V = COL_K + KV_W
COL_XBC = COL_V + KV_W
COL_HY = COL_XBC + SSD_CONV_CH
COL_Z = COL_HY + 3 * HY_W
COL_DT = COL_Z + SSD_W
COL_END = COL_DT + LANES


def _params(*sem):
    return pltpu.CompilerParams(dimension_semantics=sem, vmem_limit_bytes=VMEM_LIMIT)


def _silu(x):
    return x * jax.nn.sigmoid(x)


def _softplus(x):
    return jnp.maximum(x, 0.0) + jnp.log1p(jnp.exp(-jnp.abs(x)))


def _split3(x):
    hi = x.astype(BF16)
    r1 = x - hi.astype(F32)
    mid = r1.astype(BF16)
    lo = (r1 - mid.astype(F32)).astype(BF16)
    return hi, mid, lo


def _dot(a, b):
    return jnp.dot(a, b, preferred_element_type=F32)


def _dot_x3(a, b):
    a_hi, a_mid, _ = _split3(a)
    b_hi, b_mid, _ = _split3(b)
    return _dot(a_hi, b_hi) + (_dot(a_hi, b_mid) + _dot(a_mid, b_hi))


def _dot3_right(m_bf16, x):
    hi, mid, lo = _split3(x)
    return _dot(m_bf16, hi) + _dot(m_bf16, mid) + _dot(m_bf16, lo)


def _dot3_left(x, m_bf16):
    hi, mid, lo = _split3(x)
    return _dot(hi, m_bf16) + _dot(mid, m_bf16) + _dot(lo, m_bf16)


def _adaln_kernel(c_ref, w_ref, b_ref, o_ref):
    s = _silu(c_ref[...]).astype(BF16)
    o_ref[0] = _dot(s, w_ref[0].astype(BF16)) + b_ref[0]


def _adaln(cvec, w_mod, b_mod):
    depth, d, n = w_mod.shape
    bn = n // 4
    return pl.pallas_call(
        _adaln_kernel,
        grid=(depth, n // bn),
        in_specs=[
            pl.BlockSpec((MOD_ROWS, d), lambda i, j: (0, 0)),
            pl.BlockSpec((1, d, bn), lambda i, j: (i, 0, j)),
            pl.BlockSpec((1, 1, bn), lambda i, j: (i, 0, j)),
        ],
        out_specs=pl.BlockSpec((1, MOD_ROWS, bn), lambda i, j: (i, 0, j)),
        out_shape=jax.ShapeDtypeStruct((depth, MOD_ROWS, n), F32),
        compiler_params=_params("arbitrary", "arbitrary"),
        name="adaln",
    )(cvec, w_mod, b_mod.reshape(depth, 1, n))


def _head_rope(xn, cs, sn):
    width = xn.shape[-1]
    lane = lax.broadcasted_iota(jnp.int32, xn.shape, 1)
    first_half = (lane % HEAD_DIM) < (HEAD_DIM // 2)
    partner = jnp.where(first_half,
                        pltpu.roll(xn, width - HEAD_DIM // 2, 1),
                        pltpu.roll(xn, HEAD_DIM // 2, 1))
    return xn * cs + partner * sn


def _stream_specs(n_lat_tiles, ctx_blk, d):
    lat = pl.BlockSpec((1, TOKEN_TILE, d), lambda j, b: (b, jnp.minimum(j, n_lat_tiles - 1), 0))
    ctx = pl.BlockSpec((1, TOKEN_TILE, d), lambda j, b: (jnp.where(j >= n_lat_tiles, b, 0), ctx_blk, 0))
    return lat, ctx


def _stream_tile(xl_ref, xc_ref, n_lat_tiles):
    return jnp.where(pl.program_id(0) >= n_lat_tiles, xc_ref[0], xl_ref[0])


def _pack_w_in(w_ref, wb_ref):
    src_z = ATT_W + 2 * KV_W
    src_xbc = src_z + SSD_W
    src_dt = src_xbc + SSD_CONV_CH
    n_dt = 2 * SSD_HEADS
    wb_ref[:, COL_Q:COL_XBC] = w_ref[0, :, :src_z].astype(BF16)
    wb_ref[:, COL_XBC:COL_HY] = w_ref[0, :, src_xbc:src_dt].astype(BF16)
    tail = w_ref[0, :, src_dt:]
    wb_ref[:, COL_HY:COL_Z] = tail[:, n_dt:].astype(BF16)
    wb_ref[:, COL_Z:COL_DT] = w_ref[0, :, src_z:src_xbc].astype(BF16)
    head = tail[:, :LANES]
    lane = lax.broadcasted_iota(jnp.int32, head.shape, 1)
    wb_ref[:, COL_DT:COL_END] = jnp.where(lane < n_dt, head, 0.0).astype(BF16)


def _inproj_kernel(xl_ref, xc_ref, xp_ref, xn_ref, sh_ref, sc_ref, g_ref, w_ref, gq_ref, gk_ref, cs_ref, sn_ref,
                   bd_ref, cw_ref, cb_ref, qt_ref, k_ref, vt_ref, z_ref, xbc_ref, dt_ref, hy_ref, wb_ref,
                   *, n_lat_tiles):
    j = pl.program_id(0)

    @pl.when((j == 0) & (pl.program_id(1) == 0))
    def _():
        _pack_w_in(w_ref, wb_ref)

    tm, halo = TOKEN_TILE, SUBLANES
    x = jnp.concatenate([xp_ref[0], _stream_tile(xl_ref, xc_ref, n_lat_tiles), xn_ref[0]], axis=0)
    ms = jnp.mean(x * x, axis=-1, keepdims=True)
    h = x * lax.rsqrt(ms + EPS) * g_ref[...]
    h = h * (1.0 + sc_ref[0]) + sh_ref[0]
    hb = h[halo:halo + tm].astype(BF16)
    he = h.astype(BF16)
    us = [_dot(he, wb_ref[:, COL_XBC:COL_HY]), _dot(he, wb_ref[:, COL_HY:COL_Z])]
    qkv = _dot(hb, wb_ref[:, COL_Q:COL_XBC])
    bd = bd_ref[...]
    q, k = qkv[:, COL_Q:COL_K], qkv[:, COL_K:COL_V]
    ms_q = _dot((q * q).astype(BF16), bd) * (1.0 / HEAD_DIM)
    ms_k = _dot((k * k).astype(BF16), bd[:KV_W, :KV_W]) * (1.0 / HEAD_DIM)
    zdt = _dot(hb, wb_ref[:, COL_Z:COL_END])

    cs = cs_ref[...]
    sn = sn_ref[...]
    qt_ref[0] = _head_rope(q * lax.rsqrt(ms_q + EPS) * gq_ref[...], cs, sn).T.astype(qt_ref.dtype)
    k_ref[0] = _head_rope(k * lax.rsqrt(ms_k + EPS) * gk_ref[...],
                          cs[:, :KV_W] * (1.0 / Q_SCALE), sn[:, :KV_W] * (1.0 / Q_SCALE)).astype(k_ref.dtype)
    vt_ref[0] = qkv[:, COL_V:COL_XBC].T.astype(vt_ref.dtype)

    has_prev = (j >= 1) & (j < n_lat_tiles)
    has_next = j < n_lat_tiles - 1
    own = slice(halo, halo + tm)

    def conv3(u, c0, c1):
        u = jnp.concatenate([jnp.where(has_prev, u[:halo], 0.0), u[own], jnp.where(has_next, u[halo + tm:], 0.0)],
                            axis=0)
        cw = cw_ref[:, c0:c1]
        return (pltpu.roll(u, 1, 0)[own] * cw[0:1] + u[own] * cw[1:2]
                + pltpu.roll(u, u.shape[0] - 1, 0)[own] * cw[2:3] + cb_ref[:, c0:c1])

    xbc_ref[0] = _silu(conv3(us[0], 0, COL_HY - COL_XBC))
    hy_ref[0] = conv3(us[1], COL_HY - COL_XBC, COL_Z - COL_XBC)
    z_ref[0] = zdt[:, :COL_DT - COL_Z]
    dt_ref[0] = zdt[:, COL_DT - COL_Z:]


def _inproj(x_lat, x_ctx, ctx_blk, mods, g_mix, w_in, layer, gq, gk, cs, sn, bd, conv_w, conv_b, n_lat_tiles):
    bsz, _, d = x_lat.shape
    tm = TOKEN_TILE
    nt = n_lat_tiles + 1
    t = nt * tm
    per_tile = tm // SUBLANES
    n_conv = COL_Z - COL_XBC

    def mod_row(j, b):
        return jnp.where(j >= n_lat_tiles, bsz, b)

    lat_j = lambda j: jnp.minimum(j, n_lat_tiles - 1)
    prev_spec = pl.BlockSpec((1, SUBLANES, d), lambda j, b: (b, jnp.maximum(lat_j(j) * per_tile - 1, 0), 0))
    next_spec = pl.BlockSpec((1, SUBLANES, d),
                             lambda j, b: (b, jnp.minimum((lat_j(j) + 1) * per_tile, n_lat_tiles * per_tile - 1), 0))

    tok = lambda w: pl.BlockSpec((1, tm, w), lambda j, b: (b, j, 0))
    const = lambda shape: pl.BlockSpec(shape, lambda j, b: tuple(0 for _ in shape))
    outs = pl.pallas_call(
        functools.partial(_inproj_kernel, n_lat_tiles=n_lat_tiles),
        grid=(nt, bsz),
        in_specs=[
            *_stream_specs(n_lat_tiles, ctx_blk, d),
            prev_spec, next_spec,
            pl.BlockSpec((1, 1, d), lambda j, b: (mod_row(j, b), 0, 0)),
            pl.BlockSpec((1, 1, d), lambda j, b: (mod_row(j, b), 0, 1)),
            const((1, d)),
            pl.BlockSpec((1,) + w_in.shape[1:], lambda j, b: (layer, 0, 0), pipeline_mode=pl.Buffered(1)),
            const((1, ATT_W)),
            const((1, KV_W)),
            pl.BlockSpec((tm, ATT_W), lambda j, b: (j, 0)),
            pl.BlockSpec((tm, ATT_W), lambda j, b: (j, 0)),
            const((ATT_W, ATT_W)),
            const((3, n_conv)),
            const((1, n_conv)),
        ],
        out_specs=[
            pl.BlockSpec((1, ATT_W, tm), lambda j, b: (b, 0, j)),
            tok(KV_W),
            pl.BlockSpec((1, KV_W, tm), lambda j, b: (b, 0, j)),
            tok(SSD_W),
            tok(SSD_CONV_CH),
            tok(LANES),
            tok(3 * HY_W),
        ],
        out_shape=[
            jax.ShapeDtypeStruct((bsz, ATT_W, t), BF16),
            jax.ShapeDtypeStruct((bsz, t, KV_W), BF16),
            jax.ShapeDtypeStruct((bsz, KV_W, t), BF16),
            jax.ShapeDtypeStruct((bsz, t, SSD_W), F32),
            jax.ShapeDtypeStruct((bsz, t, SSD_CONV_CH), F32),
            jax.ShapeDtypeStruct((bsz, t, LANES), F32),
            jax.ShapeDtypeStruct((bsz, t, 3 * HY_W), F32),
        ],
        scratch_shapes=[pltpu.VMEM((d, COL_END), BF16)],
        compiler_params=_params("arbitrary", "arbitrary"),
        name="inproj",
    )(x_lat, x_ctx, x_lat, x_lat, mods, mods, g_mix, w_in, gq, gk, cs, sn, bd, conv_w, conv_b)
    return outs


def _attn_kernel(qt_ref, k_ref, vt_ref, o_ref, st_ref, pt_ref):
    k = k_ref[0]
    vt = vt_ref[0]
    rep = ATT_HEADS // ATT_KV_HEADS
    tq = qt_ref.shape[2]
    sub = min(tq, ATT_SUB)
    ones = jnp.ones((2 * SUBLANES, vt.shape[1]), vt.dtype)
    vtg = [jnp.concatenate([vt[g * HEAD_DIM:(g + 1) * HEAD_DIM, :], ones], axis=0) for g in range(ATT_KV_HEADS)]
    units = [(c0, hd) for c0 in range(0, tq, sub) for hd in range(ATT_HEADS)]

    def scores(u):
        c0, hd = units[u]
        qh = qt_ref[0, hd * HEAD_DIM:(hd + 1) * HEAD_DIM, c0:c0 + sub]
        zero = jnp.zeros_like(qh)
        w = jnp.concatenate([qh, zero] if hd < rep else [zero, qh], axis=0)
        st = _dot(k, w)
        st_ref[u % 2] = st
        return jnp.max(st, axis=0, keepdims=True)

    m_next = scores(0)
    outs = []
    for u, (c0, hd) in enumerate(units):
        m = m_next
        if u + 1 < len(units):
            m_next = scores(u + 1)
        pt_ref[u % 2] = jnp.exp2(st_ref[u % 2] - m).astype(BF16)
        ot = _dot(vtg[hd // rep], pt_ref[u % 2])
        outs.append(ot[:HEAD_DIM] / ot[HEAD_DIM:HEAD_DIM + 1])
        if hd == ATT_HEADS - 1:
            o_ref[0, c0:c0 + sub, :] = jnp.concatenate(outs, axis=0).T.astype(o_ref.dtype)
            outs = []


def _attention(qt, k, vt, q_row0, n_q, k_row0, n_k, tq):
    bsz = qt.shape[0]
    kblk = k_row0 // n_k
    q_tile0 = q_row0 // tq
    n_q_tiles = n_q // tq
    assert kblk * n_k == k_row0 and q_tile0 * tq == q_row0 and n_q_tiles * tq == n_q
    assert ATT_KV_HEADS == 2
    return pl.pallas_call(
        _attn_kernel,
        grid=(bsz, n_q_tiles),
        in_specs=[
            pl.BlockSpec((1, ATT_W, tq), lambda b, j: (b, 0, q_tile0 + j)),
            pl.BlockSpec((1, n_k, KV_W), lambda b, j: (b, kblk, 0)),
            pl.BlockSpec((1, KV_W, n_k), lambda b, j: (b, 0, kblk)),
        ],
        out_specs=pl.BlockSpec((1, tq, ATT_W), lambda b, j: (b, j, 0)),
        out_shape=jax.ShapeDtypeStruct((bsz, n_q_tiles * tq, ATT_W), BF16),
        scratch_shapes=[pltpu.VMEM((2, n_k, min(tq, ATT_SUB)), F32), pltpu.VMEM((2, n_k, min(tq, ATT_SUB)), BF16)],
        compiler_params=_params("arbitrary", "arbitrary"),
        name="attention",
    )(qt, k, vt)


def _ssd_kernel(xbc_ref, z_ref, dt_ref, dtb_ref, alog_ref, dsk_ref, nw_ref, bd_ref,
                o_ref, hf_ref, hb_ref, hbe_ref, *, n_lat_blocks):
    q = SSD_CHUNK
    nh = SSD_HEADS
    phase = pl.program_id(1)
    step = pl.program_id(2)
    block = jnp.where(phase == 0, n_lat_blocks - step, jnp.where(step < 1, n_lat_blocks, step - 1))

    lane = lax.broadcasted_iota(jnp.int32, (1, LANES), 1)
    a_row = jnp.where(lane < 2 * nh, -jnp.exp(alog_ref[...]), 0.0)
    tt = lax.broadcasted_iota(jnp.int32, (q, q), 0)
    ss = lax.broadcasted_iota(jnp.int32, (q, q), 1)
    lower = (ss <= tt)
    upper = (ss >= tt)
    lmat = jnp.where(lower, 1.0, 0.0).astype(BF16)
    umat = jnp.where(upper, 1.0, 0.0).astype(BF16)

    @pl.when(step == 0)
    def _():
        hf_ref[...] = jnp.zeros_like(hf_ref)
        hb_ref[...] = jnp.zeros_like(hb_ref)

    def load_chunk(ci):
        rows = slice(ci * q, (ci + 1) * q)
        xbc = xbc_ref[0, rows, :]
        dt = _softplus(dt_ref[0, rows, :] + dtb_ref[...])
        return rows, xbc, dt, dt * a_row

    def heads_x(x):
        return [x[:, hd * SSD_HEAD_DIM:(hd + 1) * SSD_HEAD_DIM].astype(BF16) for hd in range(nh)]

    def backward_prepare(ci):
        _, xbc, dt, a = load_chunk(ci)
        bt = xbc[:, SSD_W:SSD_W + SSD_GROUPS * SSD_STATE].T
        dt_t = dt.T
        suf_t = _dot3_left(a.T, lmat)
        terms = []
        for hd in range(nh):
            g = hd // SSD_HEADS_PER_GROUP
            row_b = suf_t[nh + hd:nh + hd + 1, :]
            total = row_b[:, 0:1]
            w_t = jnp.exp(total - row_b) * dt_t[nh + hd:nh + hd + 1, :]
            terms.append((jnp.exp(total), (bt[g * SSD_STATE:(g + 1) * SSD_STATE, :] * w_t).astype(BF16)))
        return terms, heads_x(xbc[:, :SSD_W])

    def backward_state(ci, prepared):
        terms, xh = prepared
        chunk = block * SSD_BLOCK + ci
        for hd in range(nh):
            decay, bw = terms[hd]
            prev = hb_ref[hd]
            hbe_ref[chunk, hd] = prev
            hb_ref[hd] = prev * decay + _dot(bw, xh[hd])

    def forward_prepare(ci):
        rows, xbc, dt, a = load_chunk(ci)
        x = xbc[:, :SSD_W]
        bt = xbc[:, SSD_W:SSD_W + SSD_GROUPS * SSD_STATE].T
        cmat = xbc[:, SSD_W + SSD_GROUPS * SSD_STATE:]
        a_t = a.T
        pre = _dot3_right(lmat, a)
        pre_t = _dot3_left(a_t, umat)
        cgs = [cmat[:, g * SSD_STATE:(g + 1) * SSD_STATE].astype(BF16) for g in range(SSD_GROUPS)]
        btgs = [bt[g * SSD_STATE:(g + 1) * SSD_STATE, :] for g in range(SSD_GROUPS)]
        cbs = [_dot(cgs[g], btgs[g].astype(BF16)) for g in range(SSD_GROUPS)]
        return dict(rows=rows, x=x, xh=heads_x(x), a=a, a_t=a_t, dt_t=dt.T, pre=pre, pre_t=pre_t,
                    cgs=cgs, btgs=btgs, cbs=cbs)

    def forward_mix(p):
        pre, pre_t, dt_t = p["pre"], p["pre_t"], p["dt_t"]
        suf = pre[q - 1:q, :] - pre + p["a"]
        suf_t = pre_t[:, q - 1:q] - pre_t + p["a_t"]
        neg_inf = jnp.float32(-jnp.inf)
        p["y_in"], p["scale"], p["upd"] = [], [], []
        for hd in range(nh):
            g = hd // SSD_HEADS_PER_GROUP
            colf = pre[:, hd:hd + 1]
            rowf = pre_t[hd:hd + 1, :]
            colb = suf[:, nh + hd:nh + hd + 1]
            rowb = suf_t[nh + hd:nh + hd + 1, :]
            wf = jnp.exp(jnp.where(lower, colf - rowf, neg_inf)) * dt_t[hd:hd + 1, :]
            wb = jnp.exp(jnp.where(upper, colb - rowb, neg_inf)) * dt_t[nh + hd:nh + hd + 1, :]
            p["y_in"].append(_dot((p["cbs"][g] * (wf + wb)).astype(BF16), p["xh"][hd]))
            p["scale"].append((jnp.exp(colf), jnp.exp(colb)))
            total = rowf[:, q - 1:q]
            w_t = jnp.exp(total - rowf) * dt_t[hd:hd + 1, :]
            p["upd"].append((jnp.exp(total), (p["btgs"][g] * w_t).astype(BF16)))

    def forward_state(ci, p):
        chunk = block * SSD_BLOCK + ci
        rows, x = p["rows"], p["x"]
        ys = []
        for hd in range(nh):
            cg = p["cgs"][hd // SSD_HEADS_PER_GROUP]
            hf = hf_ref[hd]
            ef, eb = p["scale"][hd]
            y = p["y_in"][hd] + _dot(cg, hf.astype(BF16)) * ef + _dot(cg, hbe_ref[chunk, hd].astype(BF16)) * eb
            decay, bw = p["upd"][hd]
            hf_ref[hd] = hf * decay + _dot(bw, p["xh"][hd])
            ys.append(y)
        y = jnp.concatenate(ys, axis=-1) + x * dsk_ref[...]
        gz = y * _silu(z_ref[0, rows, :])
        ms = _dot((gz * gz).astype(BF16), bd_ref[...]) * (1.0 / (SSD_W // SSD_GROUPS))
        o_ref[0, rows, :] = (gz * lax.rsqrt(ms + EPS) * nw_ref[...]).astype(o_ref.dtype)

    @pl.when(phase == 0)
    def _():
        prepared = [backward_prepare(ci) for ci in range(SSD_BLOCK)]
        for ci in reversed(range(SSD_BLOCK)):
            backward_state(ci, prepared[ci])

    @pl.when(phase == 1)
    def _():
        prepared = [forward_prepare(ci) for ci in range(SSD_BLOCK)]
        for p in prepared:
            forward_mix(p)
        for ci in range(SSD_BLOCK):
            forward_state(ci, prepared[ci])


def _ssd(xbc, z, dt_raw, dt_bias, a_log, d_skip, norm_w, bd, n_lat):
    bsz, t, _ = z.shape
    rows = SSD_BLOCK * SSD_CHUNK
    n_lat_blocks = n_lat // rows
    n_blocks = t // rows
    assert n_blocks == n_lat_blocks + 1 and n_lat_blocks * rows == n_lat

    def block_of(p, s):
        return jnp.where(p == 0, n_lat_blocks - s, jnp.where(s < 1, n_lat_blocks, s - 1))

    def out_block(p, s):
        return jnp.where(p == 0, n_lat_blocks, block_of(p, s))

    const = lambda shape: pl.BlockSpec(shape, lambda b, p, s: tuple(0 for _ in shape))
    return pl.pallas_call(
        functools.partial(_ssd_kernel, n_lat_blocks=n_lat_blocks),
        grid=(bsz, 2, n_blocks),
        in_specs=[
            pl.BlockSpec((1, rows, SSD_CONV_CH), lambda b, p, s: (b, block_of(p, s), 0)),
            pl.BlockSpec((1, rows, SSD_W), lambda b, p, s: (b, block_of(p, s), 0)),
            pl.BlockSpec((1, rows, LANES), lambda b, p, s: (b, block_of(p, s), 0)),
            const((1, LANES)),
            const((1, LANES)),
            const((1, SSD_W)),
            const((1, SSD_W)),
            const((SSD_W, SSD_W)),
        ],
        out_specs=pl.BlockSpec((1, rows, SSD_W), lambda b, p, s: (b, out_block(p, s), 0)),
        out_shape=jax.ShapeDtypeStruct((bsz, t, SSD_W), BF16),
        scratch_shapes=[
            pltpu.VMEM((SSD_HEADS, SSD_STATE, SSD_HEAD_DIM), F32),
            pltpu.VMEM((SSD_HEADS, SSD_STATE, SSD_HEAD_DIM), F32),
            pltpu.VMEM((n_blocks * SSD_BLOCK, SSD_HEADS, SSD_STATE, SSD_HEAD_DIM), F32),
        ],
        compiler_params=_params("arbitrary", "arbitrary", "arbitrary"),
        name="ssd",
    )(xbc, z, dt_raw, dt_bias, a_log, d_skip, norm_w, bd)


def _hyfilt_kernel(feat_ref, win_ref, w1_ref, b1_ref, fr_ref, w2_ref, b2_ref, w3_ref, o_ref):
    hp = lax.Precision.HIGHEST
    fr = fr_ref[...]
    h1 = jnp.sin(fr * (jnp.dot(feat_ref[...], w1_ref[...], precision=hp, preferred_element_type=F32) + b1_ref[...]))
    h2 = jnp.sin(fr * (jnp.dot(h1, w2_ref[...], precision=hp, preferred_element_type=F32) + b2_ref[...]))
    h = _dot_x3(h2, w3_ref[...])
    win = win_ref[...]
    first_tile = pl.program_id(0) == 0
    row = lax.broadcasted_iota(jnp.int32, win.shape, 0)
    for order in range(HY_ORDER):
        for direction in range(2):
            c0 = (direction * HY_ORDER + order) * HY_W
            f = h[:, c0:c0 + HY_W] * win
            if direction == 1:
                f = jnp.where(first_tile & (row == 0), 0.0, f)
            o_ref[order * 2 + direction] = f


def _hy_positional(length):
    n = np.arange(length, dtype=np.float64)
    t = n / max(length - 1, 1)
    bands = np.linspace(1e-4, HY_BANDS - 1, HY_BANDS)
    wpos = (2 * math.pi / length) * n
    feats = np.concatenate([t[:, None], np.cos(wpos[:, None] * bands), -np.sin(wpos[:, None] * bands)], axis=-1)
    feats = np.pad(feats, ((0, 0), (0, LANES - HY_POS_DIM)))
    deltas = np.abs(np.linspace(math.log(HY_TARGET) / HY_SLOW_DECAY, math.log(HY_TARGET) / HY_FAST_DECAY, HY_W))
    window = np.exp(-t[:, None] * deltas)
    return jnp.asarray(feats, F32), jnp.asarray(window, F32)


def _hyena_filters(length, w1, b1, freq, w2, b2, w3):
    feats, window = _hy_positional(length)
    tl = min(length, 512)
    hid = HY_FILTER_HID
    w1p = jnp.pad(w1, ((0, LANES - HY_POS_DIM), (0, 0)))
    const = lambda shape: pl.BlockSpec(shape, lambda i: tuple(0 for _ in shape))
    return pl.pallas_call(
        _hyfilt_kernel,
        grid=(length // tl,),
        in_specs=[
            pl.BlockSpec((tl, LANES), lambda i: (i, 0)),
            pl.BlockSpec((tl, HY_W), lambda i: (i, 0)),
            const((LANES, hid)), const((1, hid)), const((1, hid)),
            const((hid, hid)), const((1, hid)), const((hid, 2 * HY_ORDER * HY_W)),
        ],
        out_specs=pl.BlockSpec((2 * HY_ORDER, tl, HY_W), lambda i: (0, i, 0)),
        out_shape=jax.ShapeDtypeStruct((2 * HY_ORDER, length, HY_W), F32),
        compiler_params=_params("arbitrary"),
        name="hyena_filters",
    )(feats, window, w1p, b1.reshape(1, hid), freq.reshape(1, hid), w2, b2.reshape(1, hid), w3)


def _dft_tables(length):
    n = 2 * length
    n2 = HY_N2
    n1 = n // n2
    half = n1 // 2
    k1 = np.arange(half, dtype=np.float64) + 0.5
    idx = (n2 * np.arange(half)[None, None, :] + np.arange(n2)[:, None, None])
    ang = 2 * np.pi * k1[None, :, None] * idx / n
    fa = np.concatenate([np.cos(ang), -np.sin(ang)], axis=1)
    kk = np.arange(n2, dtype=np.float64)
    angb = 2 * np.pi * kk[:, None] * kk[None, :] / n2
    cr, sr = np.cos(angb), np.sin(angb)
    fb = np.block([[cr, sr], [-sr, cr]])
    fbi = np.block([[cr, -sr], [sr, cr]])
    idxo = (n2 * np.arange(half)[None, :, None] + np.arange(n2)[:, None, None])
    ango = 2 * np.pi * k1[None, None, :] * idxo / n
    ga = np.concatenate([np.cos(ango), -np.sin(ango)], axis=2) * (2.0 / n)
    f32 = lambda a: jnp.asarray(a, F32).astype(BF16)
    return f32(fa), f32(fb), f32(fbi), f32(ga)


def _seq_pitch(n2n):
    return n2n + SUBLANES


def _to_pitched(dst_ref, val, n2n):
    pitch = _seq_pitch(n2n)
    for i in range(val.shape[0] // n2n):
        dst_ref[i * pitch:i * pitch + n2n, :] = val[i * n2n:(i + 1) * n2n, :]


def _hy_forward_a(src_ref, a_ref, fa_ref):
    n2n, two_n1, half = fa_ref.shape
    zp, ap = _seq_pitch(n2n), _seq_pitch(two_n1)

    def body(n2, carry):
        zs = src_ref[pl.ds(n2, half, stride=zp), :].astype(BF16)
        a_ref[pl.ds(pl.multiple_of(n2 * ap, SUBLANES), two_n1), :] = _dot(fa_ref[n2], zs)
        return carry

    lax.fori_loop(0, n2n, body, 0, unroll=HY_UNROLL)


def _hy_kspec_kernel(hf_ref, hb_ref, fa_ref, fb_ref, o_ref, af_ref, ab_ref, hfp_ref, hbp_ref):
    n2n, two_n1, _ = fa_ref.shape
    n1 = two_n1 // 2
    ap = _seq_pitch(two_n1)
    _to_pitched(hfp_ref, hf_ref[0], n2n)
    _to_pitched(hbp_ref, hb_ref[0], n2n)
    _hy_forward_a(hfp_ref, af_ref, fa_ref)
    _hy_forward_a(hbp_ref, ab_ref, fa_ref)
    fb = fb_ref[...]

    def body(k1, carry):
        def spectrum(a_ref):
            ar = a_ref[pl.ds(k1, n2n, stride=ap), :]
            ai = a_ref[pl.ds(n1 + k1, n2n, stride=ap), :]
            return _dot(fb, jnp.concatenate([ar, ai], axis=0).astype(BF16))

        xf, xb = spectrum(af_ref), spectrum(ab_ref)
        o_ref[0, k1] = jnp.concatenate([xf[:n2n] + xb[:n2n], xf[n2n:] - xb[n2n:]], axis=0)
        return carry

    lax.fori_loop(0, n1, body, 0, unroll=HY_UNROLL)


def _hy_kspec(filt, tables):
    fa, fb, _, _ = tables
    _, length, _ = filt.shape
    n2n, two_n1, _ = fa.shape
    n1 = two_n1 // 2
    nh = HY_W // LANES
    return pl.pallas_call(
        _hy_kspec_kernel,
        grid=(HY_ORDER, nh),
        in_specs=[
            pl.BlockSpec((1, length, LANES), lambda o, h: (2 * o, 0, h)),
            pl.BlockSpec((1, length, LANES), lambda o, h: (2 * o + 1, 0, h)),
            pl.BlockSpec(fa.shape, lambda o, h: (0, 0, 0), pipeline_mode=pl.Buffered(1)),
            pl.BlockSpec(fb.shape, lambda o, h: (0, 0)),
        ],
        out_specs=pl.BlockSpec((1, n1, 2 * n2n, LANES), lambda o, h: (o, 0, 0, h)),
        out_shape=jax.ShapeDtypeStruct((HY_ORDER, n1, 2 * n2n, HY_W), F32),
        scratch_shapes=[pltpu.VMEM((n2n * _seq_pitch(two_n1), LANES), F32),
                        pltpu.VMEM((n2n * _seq_pitch(two_n1), LANES), F32),
                        pltpu.VMEM((length // n2n * _seq_pitch(n2n), LANES), F32),
                        pltpu.VMEM((length // n2n * _seq_pitch(n2n), LANES), F32)],
        compiler_params=_params("arbitrary", "arbitrary"),
        name="hyena_kspec",
    )(filt, filt, fa, fb)


def _hy_conv_kernel(z_ref, g_ref, fa_ref, fb_ref, fbi_ref, ga_ref, k_ref, bias_ref,
                    o_ref, a_ref, c_ref, zc_ref, gc_ref):
    n2n, two_n1, half = fa_ref.shape
    n1 = two_n1 // 2
    zp, ap = _seq_pitch(n2n), _seq_pitch(two_n1)
    _to_pitched(zc_ref, z_ref[0], n2n)
    _to_pitched(gc_ref, g_ref[0], n2n)
    _hy_forward_a(zc_ref, a_ref, fa_ref)
    fb = fb_ref[...]
    fbi = fbi_ref[...]

    def body_b(grp, carry):
        k1s = [grp * HY_UNROLL + i for i in range(HY_UNROLL)]
        sl = [(pl.ds(k1, n2n, stride=ap), pl.ds(n1 + k1, n2n, stride=ap)) for k1 in k1s]
        xs = [_dot(fb, jnp.concatenate([a_ref[re, :], a_ref[im, :]], axis=0).astype(BF16)) for re, im in sl]
        ys = []
        for k1, x in zip(k1s, xs):
            kk = k_ref[0, k1]
            xr, xi = x[:n2n], x[n2n:]
            kr, ki = kk[:n2n], kk[n2n:]
            ys.append(jnp.concatenate([xr * kr - xi * ki, xr * ki + xi * kr], axis=0).astype(BF16))
        for (re, im), y in zip(sl, ys):
            c = _dot(fbi, y)
            c_ref[re, :] = c[:n2n]
            c_ref[im, :] = c[n2n:]
        return carry

    lax.fori_loop(0, n1 // HY_UNROLL, body_b, 0)
    bias = bias_ref[0]

    def body_c(m2, carry):
        rows = pl.ds(m2, half, stride=zp)
        c = c_ref[pl.ds(pl.multiple_of(m2 * ap, SUBLANES), two_n1), :].astype(BF16)
        y = _dot(ga_ref[m2], c)
        a_ref[rows, :] = gc_ref[rows, :] * (y + zc_ref[rows, :] * bias)
        return carry

    lax.fori_loop(0, n2n, body_c, 0, unroll=HY_UNROLL)
    for i in range(half):
        o_ref[0, i * n2n:(i + 1) * n2n, :] = a_ref[i * zp:i * zp + n2n, :]


def _hy_conv(z_arr, z_blk, g_arr, g_blk, kspec, order, bias3, tables, n_lat):
    fa, fb, fbi, ga = tables
    bsz = z_arr.shape[0]
    n2n, two_n1, half = fa.shape
    n1 = two_n1 // 2
    nh = HY_W // LANES
    assert n1 % HY_UNROLL == 0 and n2n % HY_UNROLL == 0
    resident = lambda shape: pl.BlockSpec(shape, lambda h, b_: tuple(0 for _ in shape), pipeline_mode=pl.Buffered(1))
    return pl.pallas_call(
        _hy_conv_kernel,
        grid=(nh, bsz),
        in_specs=[
            pl.BlockSpec((1, n_lat, LANES), lambda h, b_: (b_, 0, z_blk + h)),
            pl.BlockSpec((1, n_lat, LANES), lambda h, b_: (b_, 0, g_blk + h)),
            resident(fa.shape), resident(fb.shape), resident(fbi.shape), resident(ga.shape),
            pl.BlockSpec((1, n1, 2 * n2n, LANES), lambda h, b_: (order, 0, 0, h), pipeline_mode=pl.Buffered(1)),
            pl.BlockSpec((1, 1, LANES), lambda h, b_: (order, 0, h)),
        ],
        out_specs=pl.BlockSpec((1, n_lat, LANES), lambda h, b_: (b_, 0, h)),
        out_shape=jax.ShapeDtypeStruct((bsz, n_lat, HY_W), F32),
        scratch_shapes=[pltpu.VMEM((n2n * _seq_pitch(two_n1), LANES), F32),
                        pltpu.VMEM((n2n * _seq_pitch(two_n1), LANES), F32),
                        pltpu.VMEM((half * _seq_pitch(n2n), LANES), F32),
                        pltpu.VMEM((half * _seq_pitch(n2n), LANES), F32)],
        compiler_params=_params("arbitrary", "arbitrary"),
        name="hyena_conv",
    )(z_arr, g_arr, fa, fb, fbi, ga, kspec, bias3)


def _hyena_ctx_kernel(u_ref, filt_ref, bias_ref, fd_ref, gd_ref, o_ref, *, n_ctx):
    fd = fd_ref[...]
    gd = gd_ref[...]
    nn = 2 * n_ctx
    u = u_ref[0]
    v, x1, x2 = u[:, :HY_W], u[:, HY_W:2 * HY_W], u[:, 2 * HY_W:]

    def conv(zin, order):
        hf = _dot(fd, filt_ref[2 * order].astype(BF16))
        hb = _dot(fd, filt_ref[2 * order + 1].astype(BF16))
        kr, ki = hf[:nn] + hb[:nn], hf[nn:] - hb[nn:]
        zz = _dot(fd, zin.astype(BF16))
        zr, zi = zz[:nn], zz[nn:]
        y = jnp.concatenate([zr * kr - zi * ki, zr * ki + zi * kr], axis=0).astype(BF16)
        return _dot(gd, y) + zin * bias_ref[order]

    y1 = x1 * conv(v, 0)
    o_ref[0] = x2 * conv(y1, 1)


def _hyena_ctx(hy, filt, bias, n_lat, n_ctx):
    bsz = hy.shape[0]
    nn = 2 * n_ctx
    k = np.arange(nn, dtype=np.float64)
    t = np.arange(n_ctx, dtype=np.float64)
    ang = 2 * np.pi * k[:, None] * t[None, :] / nn
    fd = jnp.asarray(np.concatenate([np.cos(ang), -np.sin(ang)], axis=0), F32).astype(BF16)
    gd = jnp.asarray(np.concatenate([np.cos(ang.T), -np.sin(ang.T)], axis=1) / nn, F32).astype(BF16)
    blk = n_lat // n_ctx
    assert blk * n_ctx == n_lat
    return pl.pallas_call(
        functools.partial(_hyena_ctx_kernel, n_ctx=n_ctx),
        grid=(bsz,),
        in_specs=[
            pl.BlockSpec((1, n_ctx, 3 * HY_W), lambda b: (b, blk, 0)),
            pl.BlockSpec((2 * HY_ORDER, n_ctx, HY_W), lambda b: (0, 0, 0)),
            pl.BlockSpec((HY_ORDER, 1, HY_W), lambda b: (0, 0, 0)),
            pl.BlockSpec((2 * nn, n_ctx), lambda b: (0, 0)),
            pl.BlockSpec((n_ctx, 2 * nn), lambda b: (0, 0)),
        ],
        out_specs=pl.BlockSpec((1, n_ctx, HY_W), lambda b: (b, 0, 0)),
        out_shape=jax.ShapeDtypeStruct((bsz, n_ctx, HY_W), F32),
        compiler_params=_params("arbitrary"),
        name="hyena_ctx",
    )(hy, filt, bias, fd, gd)


def _hyena_latent(hy, filt, bias, n_lat, tables):
    kspec = _hy_kspec(filt, tables)
    bias3 = bias.reshape(HY_ORDER, 1, HY_W)
    nb = HY_W // LANES
    y1 = _hy_conv(hy, 0, hy, nb, kspec, 0, bias3, tables, n_lat)
    return _hy_conv(y1, 0, hy, 2 * nb, kspec, 1, bias3, tables, n_lat)


def _outproj_kernel(x_ref, att_ref, ssd_ref, hy_ref, gt_ref, wa_ref, ws_ref, wh_ref, o_ref):
    mix = _dot(att_ref[0], wa_ref[0]) + _dot(ssd_ref[0], ws_ref[0]) + _dot(hy_ref[0].astype(BF16), wh_ref[0])
    o_ref[0] = x_ref[0] + gt_ref[0] * mix


def _outproj(x, att, ssd, ssd_row0, hy, mods, mod_row, w_out, layer, tm):
    bsz, n, _ = att.shape
    d = x.shape[-1]
    ssd_blk0 = ssd_row0 // tm
    assert ssd_blk0 * tm == ssd_row0 and n % tm == 0
    tok = lambda w: pl.BlockSpec((1, tm, w), lambda j, b: (b, j, 0))
    return pl.pallas_call(
        _outproj_kernel,
        grid=(n // tm, bsz),
        in_specs=[
            tok(d), tok(ATT_W),
            pl.BlockSpec((1, tm, SSD_W), lambda j, b: (b, ssd_blk0 + j, 0)),
            tok(HY_W),
            pl.BlockSpec((1, 1, d), lambda j, b: (mod_row(b), 0, 2)),
            pl.BlockSpec((1, ATT_W, d), lambda j, b: (layer, 0, 0)),
            pl.BlockSpec((1, SSD_W, d), lambda j, b: (layer, ATT_W // SSD_W, 0)),
            pl.BlockSpec((1, HY_W, d), lambda j, b: (layer, (ATT_W + SSD_W) // HY_W, 0)),
        ],
        out_specs=tok(d),
        out_shape=jax.ShapeDtypeStruct((bsz, n, d), F32),
        compiler_params=_params("arbitrary", "arbitrary"),
        name="outproj",
    )(x, att, ssd, hy, mods, w_out, w_out, w_out)


def _route(logits_t, bias_col):
    scores = jax.nn.sigmoid(logits_t)
    sel = scores + bias_col
    neg_inf = jnp.float32(-jnp.inf)
    rows = [sel[e:e + 1, :] for e in range(N_EXPERTS)]
    grp = []
    for g in range(N_GROUPS):
        r = rows[g * EXPERTS_PER_GROUP:(g + 1) * EXPERTS_PER_GROUP]
        top = functools.reduce(jnp.maximum, r)
        taken = None
        rest = []
        for ri in r:
            is_top = (ri == top) if taken is None else (ri == top) & jnp.logical_not(taken)
            rest.append(jnp.where(is_top, neg_inf, ri))
            taken = is_top if taken is None else taken | is_top
        grp.append(top + functools.reduce(jnp.maximum, rest))
    best = jnp.zeros(grp[0].shape, jnp.int32)
    cur = grp[0]
    for g in range(1, N_GROUPS):
        upd = grp[g] > cur
        best = jnp.where(upd, g, best)
        cur = jnp.where(upd, grp[g], cur)
    picked = []
    for e in range(N_EXPERTS):
        g, i = divmod(e, EXPERTS_PER_GROUP)
        rank = jnp.zeros(best.shape, jnp.int32)
        for j in range(EXPERTS_PER_GROUP):
            if j == i:
                continue
            other = rows[g * EXPERTS_PER_GROUP + j]
            ahead = (other > rows[e]) | ((other == rows[e]) & (j < i))
            rank = rank + ahead.astype(jnp.int32)
        keep = (best == g) & (rank < 2)
        picked.append(jnp.where(keep, scores[e:e + 1, :], 0.0))
    total = functools.reduce(lambda u, w: u + w, picked)
    return jnp.concatenate(picked, axis=0) / total, best


def _ffn_input(x_ref, sh_ref, sc_ref, g_ref):
    x = x_ref[...].reshape(MOE_TILE, D_MODEL)
    ms = jnp.mean(x * x, axis=-1, keepdims=True)
    t = x * lax.rsqrt(ms + EPS) * g_ref[...]
    return x, t * (1.0 + sc_ref[0]) + sh_ref[0]


def _route_kernel(x_ref, sh_ref, sc_ref, g_ref, wr_ref, rb_ref, rt_ref, srow_ref, tbl_ref):
    tm = MOE_TILE
    _, t = _ffn_input(x_ref, sh_ref, sc_ref, g_ref)
    logits = _dot_x3(t, wr_ref[...])
    comb_t, best = _route(logits.T[:N_EXPERTS, :], rb_ref[...])

    member = [jnp.where(best == g, 1.0, 0.0) for g in range(N_GROUPS)]
    comb4 = functools.reduce(
        lambda u, w: u + w,
        [member[g] * comb_t[g * EXPERTS_PER_GROUP:(g + 1) * EXPERTS_PER_GROUP, :] for g in range(N_GROUPS)])
    masks = jnp.concatenate(member + [jnp.zeros((8 - N_GROUPS, tm), F32)], axis=0)
    earlier = jnp.where(lax.broadcasted_iota(jnp.int32, (tm, tm), 0) < lax.broadcasted_iota(jnp.int32, (tm, tm), 1),
                        1.0, 0.0).astype(BF16)
    ranks = _dot(masks.astype(BF16), earlier)
    cnt = jnp.sum(masks, axis=1, keepdims=True)
    padded = jnp.ceil(cnt * (1.0 / MOE_CHUNK)) * MOE_CHUNK
    lane = lax.broadcasted_iota(jnp.int32, (1, LANES), 1).astype(F32) * MOE_CHUNK
    start = jnp.zeros((1, 1), F32)
    slot = jnp.zeros((1, tm), F32)
    gid = jnp.zeros((1, LANES), jnp.int32)
    for g in range(N_GROUPS):
        slot = slot + member[g] * (start + ranks[g:g + 1, :])
        start = start + padded[g:g + 1, :]
        gid = gid + jnp.where(lane >= start, 1, 0)
    tbl_ref[0] = gid
    srow_ref[0] = slot.astype(jnp.int32)
    rt_t = jnp.concatenate([comb4, slot, jnp.zeros((LANES - EXPERTS_PER_GROUP - 1, tm), F32)], axis=0)
    rt_ref[...] = rt_t.T


def _experts_kernel(tbl_ref, x_ref, sh_ref, sc_ref, gt_ref, g_ref, rt_ref, srow_ref, wg_ref, wu_ref, wd_ref, gf_ref,
                    *rest, final, tile_of, n_grid):
    o_ref, xp_ref, yp_ref = rest[-3:]
    tm, ch = MOE_TILE, MOE_CHUNK
    n_slots = MOE_CHUNKS * ch
    tile = tile_of(*[pl.program_id(a) for a in range(n_grid)])
    x, t = _ffn_input(x_ref, sh_ref, sc_ref, g_ref)
    rt = rt_ref[...]
    slot_col = rt[:, EXPERTS_PER_GROUP:EXPERTS_PER_GROUP + 1].astype(jnp.int32)
    gather = jnp.where(lax.broadcasted_iota(jnp.int32, (n_slots, tm), 0) == srow_ref[0], 1.0, 0.0).astype(BF16)
    scatter = jnp.where(lax.broadcasted_iota(jnp.int32, (tm, n_slots), 1) == slot_col, 1.0, 0.0).astype(BF16)
    xp_ref[...] = _dot(gather, t.astype(BF16)).astype(BF16)
    w_slot = _dot3_right(gather, rt)

    def run_chunk(rows, grp):
        xc = xp_ref[rows, :]
        e0 = grp * EXPERTS_PER_GROUP
        gu_next = _dot(xc, wg_ref[0, e0]), _dot(xc, wu_ref[0, e0])
        acc = jnp.zeros((ch, D_MODEL), F32)
        for j in range(EXPERTS_PER_GROUP):
            gate, up = gu_next
            if j + 1 < EXPERTS_PER_GROUP:
                gu_next = _dot(xc, wg_ref[0, e0 + j + 1]), _dot(xc, wu_ref[0, e0 + j + 1])
            hid = _silu(gate) * up * w_slot[rows, j:j + 1]
            acc = acc + _dot(hid.astype(BF16), wd_ref[0, e0 + j])
        yp_ref[rows, :] = acc.astype(BF16)

    for c in range(MOE_CHUNKS):
        rows = slice(c * ch, (c + 1) * ch)
        grp = tbl_ref[tile, c]
        pl.when(grp < N_GROUPS)(functools.partial(run_chunk, rows, grp))

        @pl.when(grp >= N_GROUPS)
        def _():
            yp_ref[rows, :] = jnp.zeros((ch, D_MODEL), BF16)

    y = x + gt_ref[0] * _dot(scatter, yp_ref[...])
    if final:
        y = y * lax.rsqrt(jnp.mean(y * y, axis=-1, keepdims=True) + EPS) * gf_ref[...]
    o_ref[...] = y.reshape(o_ref.shape)


def _moe_tiles(x, mods, mod_row, g_ffn, w_router, router_bias, wgu, wd, layer, g_final, final, out_rows,
               grid, x_block, x_index, out_index, tile_of, n_tiles):
    bsz, _, d = x.shape
    tm = MOE_TILE
    ng = len(grid)
    const = lambda shape: pl.BlockSpec(shape, lambda *a: tuple(0 for _ in shape))
    mod = lambda col: pl.BlockSpec((1, 1, d), lambda *a: (mod_row(*a[:ng]), 0, col))
    xspec = pl.BlockSpec(x_block, lambda *a: x_index(*a[:ng]))
    rt, srow, tbl = pl.pallas_call(
        _route_kernel,
        grid=grid,
        in_specs=[xspec, mod(3), mod(4), const((1, d)), const((d, LANES)), const((N_EXPERTS, 1))],
        out_specs=[
            pl.BlockSpec((tm, LANES), lambda *a: (tile_of(*a), 0)),
            pl.BlockSpec((1, 1, tm), lambda *a: (tile_of(*a), 0, 0)),
            pl.BlockSpec((1, 1, LANES), lambda *a: (tile_of(*a), 0, 0)),
        ],
        out_shape=[
            jax.ShapeDtypeStruct((n_tiles * tm, LANES), F32),
            jax.ShapeDtypeStruct((n_tiles, 1, tm), jnp.int32),
            jax.ShapeDtypeStruct((n_tiles, 1, LANES), jnp.int32),
        ],
        compiler_params=_params(*["arbitrary"] * ng),
        name="moe_route",
    )(x, mods, mods, g_ffn, w_router, router_bias)

    resident = lambda shape: pl.BlockSpec(shape, lambda *a: (layer,) + tuple(0 for _ in shape[1:]),
                                          pipeline_mode=pl.Buffered(1))
    in_specs = [
        xspec, mod(3), mod(4), mod(5), const((1, d)),
        pl.BlockSpec((tm, LANES), lambda *a: (tile_of(*a[:ng]), 0)),
        pl.BlockSpec((1, 1, tm), lambda *a: (tile_of(*a[:ng]), 0, 0)),
        resident((1, N_EXPERTS, d, D_FF)),
        resident((1, N_EXPERTS, d, D_FF)),
        resident((1, N_EXPERTS, D_FF, d)),
        const((1, d)),
    ]
    args = [tbl.reshape(n_tiles, LANES), x, mods, mods, mods, g_ffn, rt, srow, *wgu, wd, g_final]
    return pl.pallas_call(
        functools.partial(_experts_kernel, final=final, tile_of=tile_of, n_grid=ng),
        grid_spec=pltpu.PrefetchScalarGridSpec(
            num_scalar_prefetch=1,
            grid=grid,
            in_specs=in_specs,
            out_specs=pl.BlockSpec(x_block, lambda *a: out_index(*a[:ng])),
            scratch_shapes=[pltpu.VMEM((MOE_CHUNKS * MOE_CHUNK, d), BF16), pltpu.VMEM((MOE_CHUNKS * MOE_CHUNK, d), BF16)],
        ),
        out_shape=jax.ShapeDtypeStruct((bsz, out_rows, d), F32),
        compiler_params=_params(*["arbitrary"] * ng),
        name="moe_experts",
    )(*args)


def _moe(x_lat, x_ctx, mods, g_ffn, w_router, router_bias, wgu, wd, layer, g_final, final):
    bsz, n_lat, d = x_lat.shape
    tm = MOE_TILE
    per_b = n_lat // tm
    common = (g_ffn, w_router, router_bias, wgu, wd, layer, g_final, final)
    lat_index = lambda b, j: (b, j, 0)
    out_lat = _moe_tiles(x_lat, mods, lambda b, j: b, *common, n_lat,
                         (bsz, per_b), (1, tm, d), lat_index, lat_index, lambda b, j: b * per_b + j, bsz * per_b)
    if x_ctx is None:
        return out_lat, None
    n_ctx = x_ctx.shape[1]
    nb = tm // n_ctx
    ctx_index = lambda i: (i, 0, 0)
    out_ctx = _moe_tiles(x_ctx, mods, lambda i: bsz, *common, n_ctx,
                         (bsz // nb,), (nb, n_ctx, d), ctx_index, ctx_index, lambda i: i, bsz // nb)
    return out_lat, out_ctx


def _block_ones(width, block):
    idx = np.arange(width) // block
    return jnp.asarray(idx[:, None] == idx[None, :], F32).astype(BF16)


def _rope_tables(n_lat, n_ctx):
    rows = n_lat // GRID_W
    row = np.repeat(np.arange(rows), GRID_W).astype(np.float64)
    col = np.tile(np.arange(GRID_W), rows).astype(np.float64)
    inv = ROPE_THETA ** (-np.arange(0, ROPE_AXIS_DIM, 2, dtype=np.float64) / ROPE_AXIS_DIM)
    ang = np.concatenate([row[:, None] * inv, col[:, None] * inv], axis=-1)
    ang = np.concatenate([ang, np.zeros((n_ctx, ang.shape[1]))], axis=0)
    cos = np.concatenate([np.cos(ang), np.cos(ang)], axis=-1)
    sin = np.concatenate([-np.sin(ang), np.sin(ang)], axis=-1)
    scale = Q_SCALE
    cs = np.tile(cos, (1, ATT_HEADS)) * scale
    sn = np.tile(sin, (1, ATT_HEADS)) * scale
    return jnp.asarray(cs, F32), jnp.asarray(sn, F32)


def kernel(x, c, ctx, c_ctx, w_mod, b_mod, g_mix, g_ffn, w_in, q_norm, k_norm, ssd_conv_w, ssd_conv_b,
           ssd_dt_bias, ssd_a_log, ssd_d, ssd_norm, hy_conv_w, hy_conv_b, hy_w1, hy_b1, hy_freq, hy_w2, hy_b2,
           hy_w3, hy_bias, w_out, w_router, router_bias, w_gate, w_up, w_down, g_final):
    bsz, n_lat, d = x.shape
    n_ctx = ctx.shape[1]
    depth = w_mod.shape[0]
    t = n_lat + n_ctx
    tm = TOKEN_TILE
    n_lat_tiles = n_lat // tm
    n_tiles = t // tm
    assert n_ctx == tm and n_lat % (HY_N2 * 8) == 0 and bsz < MOD_ROWS

    cvec = jnp.concatenate([c, c_ctx[None], jnp.zeros((MOD_ROWS - bsz - 1, d), F32)], axis=0)
    mods_all = _adaln(cvec, w_mod, b_mod)

    cs, sn = _rope_tables(n_lat, n_ctx)
    bd_head = _block_ones(ATT_W, HEAD_DIM)
    bd_ssd = _block_ones(SSD_W, SSD_W // SSD_GROUPS)
    tables = _dft_tables(n_lat)
    pad_row = lambda v: jnp.pad(v.reshape(1, -1), ((0, 0), (0, LANES - v.size)))
    w_out_b = w_out.astype(BF16)
    wgu_b = (w_gate.astype(BF16), w_up.astype(BF16))
    wd_b = w_down.astype(BF16)

    x_lat, x_ctx, ctx_blk = x, ctx, 0
    for i in range(depth):
        last = i == depth - 1
        mods = mods_all[i].reshape(MOD_ROWS, 1, 6 * d)
        q, kt, v, z, xbc, dt_raw, hy = _inproj(
            x_lat, x_ctx, ctx_blk, mods, g_mix[i].reshape(1, d), w_in, i,
            jnp.tile(q_norm[i], ATT_HEADS).reshape(1, ATT_W), jnp.tile(k_norm[i], ATT_KV_HEADS).reshape(1, KV_W),
            cs, sn, bd_head, jnp.concatenate([ssd_conv_w[i], hy_conv_w[i]], axis=-1),
            jnp.concatenate([ssd_conv_b[i], hy_conv_b[i]]).reshape(1, -1), n_lat_tiles)

        att = _attention(q, kt, v, 0, n_lat, 0, t, ATT_TILE)

        ssd = _ssd(xbc, z, dt_raw, pad_row(ssd_dt_bias[i]), pad_row(ssd_a_log[i]),
                   jnp.repeat(ssd_d[i], SSD_HEAD_DIM).reshape(1, SSD_W), ssd_norm[i].reshape(1, SSD_W),
                   bd_ssd, n_lat)

        hyp = (hy_w1[i], hy_b1[i], hy_freq[i], hy_w2[i], hy_b2[i], hy_w3[i])
        hy_l = _hyena_latent(hy, _hyena_filters(n_lat, *hyp), hy_bias[i], n_lat, tables)

        mid_lat = _outproj(x_lat, att, ssd, 0, hy_l, mods, lambda b: b, w_out_b, i, OUT_TILE)
        mid_ctx = None
        if not last:
            att_c = _attention(q, kt, v, n_lat, n_ctx, n_lat, n_ctx, n_ctx)
            hy_c = _hyena_ctx(hy, _hyena_filters(n_ctx, *hyp), hy_bias[i].reshape(HY_ORDER, 1, HY_W), n_lat, n_ctx)
            mid_ctx = _outproj(x_ctx, att_c, ssd, n_lat, hy_c, mods, lambda b: bsz, w_out_b, i, n_ctx)

        x_lat, x_ctx = _moe(mid_lat, mid_ctx, mods, g_ffn[i].reshape(1, d),
                            jnp.pad(w_router, ((0, 0), (0, LANES - N_EXPERTS))), router_bias.reshape(N_EXPERTS, 1),
                            wgu_b, wd_b, i, g_final.reshape(1, d), last)
    return x_lat
```

```python
import functools
import math

import jax
import jax.numpy as jnp
import numpy as np
from jax import lax
from jax.experimental import pallas as pl
from jax.experimental.pallas import tpu as pltpu

F32 = jnp.float32
BF16 = jnp.bfloat16

D_MODEL = 1024
GRID_W = 64
EPS = 1e-6

ATT_HEADS = 6
ATT_KV_HEADS = 2
HEAD_DIM = 64
ATT_W = ATT_HEADS * HEAD_DIM
KV_W = ATT_KV_HEADS * HEAD_DIM
ROPE_AXIS_DIM = HEAD_DIM // 2
ROPE_THETA = 10000.0
Q_SCALE = HEAD_DIM ** -0.5 * math.log2(math.e)

SSD_HEADS = 6
SSD_HEAD_DIM = 64
SSD_W = SSD_HEADS * SSD_HEAD_DIM
SSD_GROUPS = 2
SSD_STATE = 64
SSD_CHUNK = 128
SSD_CONV_CH = SSD_W + 2 * SSD_GROUPS * SSD_STATE
SSD_HEADS_PER_GROUP = SSD_HEADS // SSD_GROUPS
SSD_BLOCK = 2

HY_W = 256
HY_ORDER = 2
HY_BANDS = 16
HY_POS_DIM = 1 + 2 * HY_BANDS
HY_FILTER_HID = 64
HY_FAST_DECAY = 0.3
HY_SLOW_DECAY = 1.5
HY_TARGET = 1e-2
HY_N2 = 64
HY_UNROLL = 64

MIX_W = ATT_W + SSD_W + HY_W
N_EXPERTS = 16
N_GROUPS = 4
EXPERTS_PER_GROUP = N_EXPERTS // N_GROUPS
D_FF = 256

LANES = 128
SUBLANES = 8
TOKEN_TILE = 256
ATT_TILE = 1024
ATT_SUB = 512
OUT_TILE = 512
MOE_TILE = 512
MOE_CHUNK = 128
MOE_CHUNKS = MOE_TILE // MOE_CHUNK + N_GROUPS - 1
MOD_ROWS = 8
VMEM_LIMIT = 56 * 1024 * 1024

COL_Q = 0
COL_K = COL_Q + ATT_W
COL_V = COL_K + KV_W
COL_XBC = COL_V + KV_W
COL_HY = COL_XBC + SSD_CONV_CH
COL_Z = COL_HY + 3 * HY_W
COL_DT = COL_Z + SSD_W
COL_END = COL_DT + LANES


def _params(*sem):
    return pltpu.CompilerParams(dimension_semantics=sem, vmem_limit_bytes=VMEM_LIMIT)


def _silu(x):
    return x * jax.nn.sigmoid(x)


def _softplus(x):
    return jnp.maximum(x, 0.0) + jnp.log1p(jnp.exp(-jnp.abs(x)))


def _split3(x):
    hi = x.astype(BF16)
    r1 = x - hi.astype(F32)
    mid = r1.astype(BF16)
    lo = (r1 - mid.astype(F32)).astype(BF16)
    return hi, mid, lo


def _dot(a, b):
    return jnp.dot(a, b, preferred_element_type=F32)


def _dot_x3(a, b):
    a_hi, a_mid, _ = _split3(a)
    b_hi, b_mid, _ = _split3(b)
    return _dot(a_hi, b_hi) + (_dot(a_hi, b_mid) + _dot(a_mid, b_hi))


def _dot3_right(m_bf16, x):
    hi, mid, lo = _split3(x)
    return _dot(m_bf16, hi) + _dot(m_bf16, mid) + _dot(m_bf16, lo)


def _dot3_left(x, m_bf16):
    hi, mid, lo = _split3(x)
    return _dot(hi, m_bf16) + _dot(mid, m_bf16) + _dot(lo, m_bf16)


def _adaln_kernel(c_ref, w_ref, b_ref, o_ref):
    s = _silu(c_ref[...]).astype(BF16)
    o_ref[0] = _dot(s, w_ref[0].astype(BF16)) + b_ref[0]


def _adaln(cvec, w_mod, b_mod):
    depth, d, n = w_mod.shape
    bn = n // 4
    return pl.pallas_call(
        _adaln_kernel,
        grid=(depth, n // bn),
        in_specs=[
            pl.BlockSpec((MOD_ROWS, d), lambda i, j: (0, 0)),
            pl.BlockSpec((1, d, bn), lambda i, j: (i, 0, j)),
            pl.BlockSpec((1, 1, bn), lambda i, j: (i, 0, j)),
        ],
        out_specs=pl.BlockSpec((1, MOD_ROWS, bn), lambda i, j: (i, 0, j)),
        out_shape=jax.ShapeDtypeStruct((depth, MOD_ROWS, n), F32),
        compiler_params=_params("arbitrary", "arbitrary"),
        name="adaln",
    )(cvec, w_mod, b_mod.reshape(depth, 1, n))


def _head_rope(xn, cs, sn):
    width = xn.shape[-1]
    lane = lax.broadcasted_iota(jnp.int32, xn.shape, 1)
    first_half = (lane % HEAD_DIM) < (HEAD_DIM // 2)
    partner = jnp.where(first_half,
                        pltpu.roll(xn, width - HEAD_DIM // 2, 1),
                        pltpu.roll(xn, HEAD_DIM // 2, 1))
    return xn * cs + partner * sn


def _stream_specs(n_lat_tiles, ctx_blk, d):
    lat = pl.BlockSpec((1, TOKEN_TILE, d), lambda j, b: (b, jnp.minimum(j, n_lat_tiles - 1), 0))
    ctx = pl.BlockSpec((1, TOKEN_TILE, d), lambda j, b: (jnp.where(j >= n_lat_tiles, b, 0), ctx_blk, 0))
    return lat, ctx


def _stream_tile(xl_ref, xc_ref, n_lat_tiles):
    return jnp.where(pl.program_id(0) >= n_lat_tiles, xc_ref[0], xl_ref[0])


def _pack_w_in(w_ref, wb_ref):
    src_z = ATT_W + 2 * KV_W
    src_xbc = src_z + SSD_W
    src_dt = src_xbc + SSD_CONV_CH
    n_dt = 2 * SSD_HEADS
    wb_ref[:, COL_Q:COL_XBC] = w_ref[0, :, :src_z].astype(BF16)
    wb_ref[:, COL_XBC:COL_HY] = w_ref[0, :, src_xbc:src_dt].astype(BF16)
    tail = w_ref[0, :, src_dt:]
    wb_ref[:, COL_HY:COL_Z] = tail[:, n_dt:].astype(BF16)
    wb_ref[:, COL_Z:COL_DT] = w_ref[0, :, src_z:src_xbc].astype(BF16)
    head = tail[:, :LANES]
    lane = lax.broadcasted_iota(jnp.int32, head.shape, 1)
    wb_ref[:, COL_DT:COL_END] = jnp.where(lane < n_dt, head, 0.0).astype(BF16)


def _inproj_kernel(xl_ref, xc_ref, xp_ref, xn_ref, sh_ref, sc_ref, g_ref, w_ref, gq_ref, gk_ref, cs_ref, sn_ref,
                   bd_ref, cw_ref, cb_ref, qt_ref, k_ref, vt_ref, z_ref, xbc_ref, dt_ref, hy_ref, wb_ref,
                   *, n_lat_tiles):
    j = pl.program_id(0)

    @pl.when((j == 0) & (pl.program_id(1) == 0))
    def _():
        _pack_w_in(w_ref, wb_ref)

    tm, halo = TOKEN_TILE, SUBLANES
    x = jnp.concatenate([xp_ref[0], _stream_tile(xl_ref, xc_ref, n_lat_tiles), xn_ref[0]], axis=0)
    ms = jnp.mean(x * x, axis=-1, keepdims=True)
    h = x * lax.rsqrt(ms + EPS) * g_ref[...]
    h = h * (1.0 + sc_ref[0]) + sh_ref[0]
    hb = h[halo:halo + tm].astype(BF16)
    he = h.astype(BF16)
    us = [_dot(he, wb_ref[:, COL_XBC:COL_HY]), _dot(he, wb_ref[:, COL_HY:COL_Z])]
    qkv = _dot(hb, wb_ref[:, COL_Q:COL_XBC])
    bd = bd_ref[...]
    q, k = qkv[:, COL_Q:COL_K], qkv[:, COL_K:COL_V]
    ms_q = _dot((q * q).astype(BF16), bd) * (1.0 / HEAD_DIM)
    ms_k = _dot((k * k).astype(BF16), bd[:KV_W, :KV_W]) * (1.0 / HEAD_DIM)
    zdt = _dot(hb, wb_ref[:, COL_Z:COL_END])

    cs = cs_ref[...]
    sn = sn_ref[...]
    qt_ref[0] = _head_rope(q * lax.rsqrt(ms_q + EPS) * gq_ref[...], cs, sn).T.astype(qt_ref.dtype)
    k_ref[0] = _head_rope(k * lax.rsqrt(ms_k + EPS) * gk_ref[...],
                          cs[:, :KV_W] * (1.0 / Q_SCALE), sn[:, :KV_W] * (1.0 / Q_SCALE)).astype(k_ref.dtype)
    vt_ref[0] = qkv[:, COL_V:COL_XBC].T.astype(vt_ref.dtype)

    has_prev = (j >= 1) & (j < n_lat_tiles)
    has_next = j < n_lat_tiles - 1
    own = slice(halo, halo + tm)

    def conv3(u, c0, c1):
        u = jnp.concatenate([jnp.where(has_prev, u[:halo], 0.0), u[own], jnp.where(has_next, u[halo + tm:], 0.0)],
                            axis=0)
        cw = cw_ref[:, c0:c1]
        return (pltpu.roll(u, 1, 0)[own] * cw[0:1] + u[own] * cw[1:2]
                + pltpu.roll(u, u.shape[0] - 1, 0)[own] * cw[2:3] + cb_ref[:, c0:c1])

    xbc_ref[0] = _silu(conv3(us[0], 0, COL_HY - COL_XBC))
    hy_ref[0] = conv3(us[1], COL_HY - COL_XBC, COL_Z - COL_XBC)
    z_ref[0] = zdt[:, :COL_DT - COL_Z]
    dt_ref[0] = zdt[:, COL_DT - COL_Z:]


def _inproj(x_lat, x_ctx, ctx_blk, mods, g_mix, w_in, layer, gq, gk, cs, sn, bd, conv_w, conv_b, n_lat_tiles):
    bsz, _, d = x_lat.shape
    tm = TOKEN_TILE
    nt = n_lat_tiles + 1
    t = nt * tm
    per_tile = tm // SUBLANES
    n_conv = COL_Z - COL_XBC

    def mod_row(j, b):
        return jnp.where(j >= n_lat_tiles, bsz, b)

    lat_j = lambda j: jnp.minimum(j, n_lat_tiles - 1)
    prev_spec = pl.BlockSpec((1, SUBLANES, d), lambda j, b: (b, jnp.maximum(lat_j(j) * per_tile - 1, 0), 0))
    next_spec = pl.BlockSpec((1, SUBLANES, d),
                             lambda j, b: (b, jnp.minimum((lat_j(j) + 1) * per_tile, n_lat_tiles * per_tile - 1), 0))

    tok = lambda w: pl.BlockSpec((1, tm, w), lambda j, b: (b, j, 0))
    const = lambda shape: pl.BlockSpec(shape, lambda j, b: tuple(0 for _ in shape))
    outs = pl.pallas_call(
        functools.partial(_inproj_kernel, n_lat_tiles=n_lat_tiles),
        grid=(nt, bsz),
        in_specs=[
            *_stream_specs(n_lat_tiles, ctx_blk, d),
            prev_spec, next_spec,
            pl.BlockSpec((1, 1, d), lambda j, b: (mod_row(j, b), 0, 0)),
            pl.BlockSpec((1, 1, d), lambda j, b: (mod_row(j, b), 0, 1)),
            const((1, d)),
            pl.BlockSpec((1,) + w_in.shape[1:], lambda j, b: (layer, 0, 0), pipeline_mode=pl.Buffered(1)),
            const((1, ATT_W)),
            const((1, KV_W)),
            pl.BlockSpec((tm, ATT_W), lambda j, b: (j, 0)),
            pl.BlockSpec((tm, ATT_W), lambda j, b: (j, 0)),
            const((ATT_W, ATT_W)),
            const((3, n_conv)),
            const((1, n_conv)),
        ],
        out_specs=[
            pl.BlockSpec((1, ATT_W, tm), lambda j, b: (b, 0, j)),
            tok(KV_W),
            pl.BlockSpec((1, KV_W, tm), lambda j, b: (b, 0, j)),
            tok(SSD_W),
            tok(SSD_CONV_CH),
            tok(LANES),
            tok(3 * HY_W),
        ],
        out_shape=[
            jax.ShapeDtypeStruct((bsz, ATT_W, t), BF16),
            jax.ShapeDtypeStruct((bsz, t, KV_W), BF16),
            jax.ShapeDtypeStruct((bsz, KV_W, t), BF16),
            jax.ShapeDtypeStruct((bsz, t, SSD_W), F32),
            jax.ShapeDtypeStruct((bsz, t, SSD_CONV_CH), F32),
            jax.ShapeDtypeStruct((bsz, t, LANES), F32),
            jax.ShapeDtypeStruct((bsz, t, 3 * HY_W), F32),
        ],
        scratch_shapes=[pltpu.VMEM((d, COL_END), BF16)],
        compiler_params=_params("arbitrary", "arbitrary"),
        name="inproj",
    )(x_lat, x_ctx, x_lat, x_lat, mods, mods, g_mix, w_in, gq, gk, cs, sn, bd, conv_w, conv_b)
    return outs


def _attn_kernel(qt_ref, k_ref, vt_ref, o_ref, st_ref, pt_ref):
    k = k_ref[0]
    vt = vt_ref[0]
    rep = ATT_HEADS // ATT_KV_HEADS
    tq = qt_ref.shape[2]
    sub = min(tq, ATT_SUB)
    ones = jnp.ones((2 * SUBLANES, vt.shape[1]), vt.dtype)
    vtg = [jnp.concatenate([vt[g * HEAD_DIM:(g + 1) * HEAD_DIM, :], ones], axis=0) for g in range(ATT_KV_HEADS)]
    units = [(c0, hd) for c0 in range(0, tq, sub) for hd in range(ATT_HEADS)]

    def scores(u):
        c0, hd = units[u]
        qh = qt_ref[0, hd * HEAD_DIM:(hd + 1) * HEAD_DIM, c0:c0 + sub]
        zero = jnp.zeros_like(qh)
        w = jnp.concatenate([qh, zero] if hd < rep else [zero, qh], axis=0)
        st = _dot(k, w)
        st_ref[u % 2] = st
        return jnp.max(st, axis=0, keepdims=True)

    m_next = scores(0)
    outs = []
    for u, (c0, hd) in enumerate(units):
        m = m_next
        if u + 1 < len(units):
            m_next = scores(u + 1)
        pt_ref[u % 2] = jnp.exp2(st_ref[u % 2] - m).astype(BF16)
        ot = _dot(vtg[hd // rep], pt_ref[u % 2])
        outs.append(ot[:HEAD_DIM] / ot[HEAD_DIM:HEAD_DIM + 1])
        if hd == ATT_HEADS - 1:
            o_ref[0, c0:c0 + sub, :] = jnp.concatenate(outs, axis=0).T.astype(o_ref.dtype)
            outs = []


def _attention(qt, k, vt, q_row0, n_q, k_row0, n_k, tq):
    bsz = qt.shape[0]
    kblk = k_row0 // n_k
    q_tile0 = q_row0 // tq
    n_q_tiles = n_q // tq
    assert kblk * n_k == k_row0 and q_tile0 * tq == q_row0 and n_q_tiles * tq == n_q
    assert ATT_KV_HEADS == 2
    return pl.pallas_call(
        _attn_kernel,
        grid=(bsz, n_q_tiles),
        in_specs=[
            pl.BlockSpec((1, ATT_W, tq), lambda b, j: (b, 0, q_tile0 + j)),
            pl.BlockSpec((1, n_k, KV_W), lambda b, j: (b, kblk, 0)),
            pl.BlockSpec((1, KV_W, n_k), lambda b, j: (b, 0, kblk)),
        ],
        out_specs=pl.BlockSpec((1, tq, ATT_W), lambda b, j: (b, j, 0)),
        out_shape=jax.ShapeDtypeStruct((bsz, n_q_tiles * tq, ATT_W), BF16),
        scratch_shapes=[pltpu.VMEM((2, n_k, min(tq, ATT_SUB)), F32), pltpu.VMEM((2, n_k, min(tq, ATT_SUB)), BF16)],
        compiler_params=_params("arbitrary", "arbitrary"),
        name="attention",
    )(qt, k, vt)


def _ssd_kernel(xbc_ref, z_ref, dt_ref, dtb_ref, alog_ref, dsk_ref, nw_ref, bd_ref,
                o_ref, hf_ref, hb_ref, hbe_ref, *, n_lat_blocks):
    q = SSD_CHUNK
    nh = SSD_HEADS
    phase = pl.program_id(1)
    step = pl.program_id(2)
    block = jnp.where(phase == 0, n_lat_blocks - step, jnp.where(step < 1, n_lat_blocks, step - 1))

    lane = lax.broadcasted_iota(jnp.int32, (1, LANES), 1)
    a_row = jnp.where(lane < 2 * nh, -jnp.exp(alog_ref[...]), 0.0)
    tt = lax.broadcasted_iota(jnp.int32, (q, q), 0)
    ss = lax.broadcasted_iota(jnp.int32, (q, q), 1)
    lower = (ss <= tt)
    upper = (ss >= tt)
    lmat = jnp.where(lower, 1.0, 0.0).astype(BF16)
    umat = jnp.where(upper, 1.0, 0.0).astype(BF16)

    @pl.when(step == 0)
    def _():
        hf_ref[...] = jnp.zeros_like(hf_ref)
        hb_ref[...] = jnp.zeros_like(hb_ref)

    def load_chunk(ci):
        rows = slice(ci * q, (ci + 1) * q)
        xbc = xbc_ref[0, rows, :]
        dt = _softplus(dt_ref[0, rows, :] + dtb_ref[...])
        return rows, xbc, dt, dt * a_row

    def heads_x(x):
        return [x[:, hd * SSD_HEAD_DIM:(hd + 1) * SSD_HEAD_DIM].astype(BF16) for hd in range(nh)]

    def backward_prepare(ci):
        _, xbc, dt, a = load_chunk(ci)
        bt = xbc[:, SSD_W:SSD_W + SSD_GROUPS * SSD_STATE].T
        dt_t = dt.T
        suf_t = _dot3_left(a.T, lmat)
        terms = []
        for hd in range(nh):
            g = hd // SSD_HEADS_PER_GROUP
            row_b = suf_t[nh + hd:nh + hd + 1, :]
            total = row_b[:, 0:1]
            w_t = jnp.exp(total - row_b) * dt_t[nh + hd:nh + hd + 1, :]
            terms.append((jnp.exp(total), (bt[g * SSD_STATE:(g + 1) * SSD_STATE, :] * w_t).astype(BF16)))
        return terms, heads_x(xbc[:, :SSD_W])

    def backward_state(ci, prepared):
        terms, xh = prepared
        chunk = block * SSD_BLOCK + ci
        for hd in range(nh):
            decay, bw = terms[hd]
            prev = hb_ref[hd]
            hbe_ref[chunk, hd] = prev
            hb_ref[hd] = prev * decay + _dot(bw, xh[hd])

    def forward_prepare(ci):
        rows, xbc, dt, a = load_chunk(ci)
        x = xbc[:, :SSD_W]
        bt = xbc[:, SSD_W:SSD_W + SSD_GROUPS * SSD_STATE].T
        cmat = xbc[:, SSD_W + SSD_GROUPS * SSD_STATE:]
        a_t = a.T
        pre = _dot3_right(lmat, a)
        pre_t = _dot3_left(a_t, umat)
        cgs = [cmat[:, g * SSD_STATE:(g + 1) * SSD_STATE].astype(BF16) for g in range(SSD_GROUPS)]
        btgs = [bt[g * SSD_STATE:(g + 1) * SSD_STATE, :] for g in range(SSD_GROUPS)]
        cbs = [_dot(cgs[g], btgs[g].astype(BF16)) for g in range(SSD_GROUPS)]
        return dict(rows=rows, x=x, xh=heads_x(x), a=a, a_t=a_t, dt_t=dt.T, pre=pre, pre_t=pre_t,
                    cgs=cgs, btgs=btgs, cbs=cbs)

    def forward_mix(p):
        pre, pre_t, dt_t = p["pre"], p["pre_t"], p["dt_t"]
        suf = pre[q - 1:q, :] - pre + p["a"]
        suf_t = pre_t[:, q - 1:q] - pre_t + p["a_t"]
        neg_inf = jnp.float32(-jnp.inf)
        p["y_in"], p["scale"], p["upd"] = [], [], []
        for hd in range(nh):
            g = hd // SSD_HEADS_PER_GROUP
            colf = pre[:, hd:hd + 1]
            rowf = pre_t[hd:hd + 1, :]
            colb = suf[:, nh + hd:nh + hd + 1]
            rowb = suf_t[nh + hd:nh + hd + 1, :]
            wf = jnp.exp(jnp.where(lower, colf - rowf, neg_inf)) * dt_t[hd:hd + 1, :]
            wb = jnp.exp(jnp.where(upper, colb - rowb, neg_inf)) * dt_t[nh + hd:nh + hd + 1, :]
            p["y_in"].append(_dot((p["cbs"][g] * (wf + wb)).astype(BF16), p["xh"][hd]))
            p["scale"].append((jnp.exp(colf), jnp.exp(colb)))
            total = rowf[:, q - 1:q]
            w_t = jnp.exp(total - rowf) * dt_t[hd:hd + 1, :]
            p["upd"].append((jnp.exp(total), (p["btgs"][g] * w_t).astype(BF16)))

    def forward_state(ci, p):
        chunk = block * SSD_BLOCK + ci
        rows, x = p["rows"], p["x"]
        ys = []
        for hd in range(nh):
            cg = p["cgs"][hd // SSD_HEADS_PER_GROUP]
            hf = hf_ref[hd]
            ef, eb = p["scale"][hd]
            y = p["y_in"][hd] + _dot(cg, hf.astype(BF16)) * ef + _dot(cg, hbe_ref[chunk, hd].astype(BF16)) * eb
            decay, bw = p["upd"][hd]
            hf_ref[hd] = hf * decay + _dot(bw, p["xh"][hd])
            ys.append(y)
        y = jnp.concatenate(ys, axis=-1) + x * dsk_ref[...]
        gz = y * _silu(z_ref[0, rows, :])
        ms = _dot((gz * gz).astype(BF16), bd_ref[...]) * (1.0 / (SSD_W // SSD_GROUPS))
        o_ref[0, rows, :] = (gz * lax.rsqrt(ms + EPS) * nw_ref[...]).astype(o_ref.dtype)

    @pl.when(phase == 0)
    def _():
        prepared = [backward_prepare(ci) for ci in range(SSD_BLOCK)]
        for ci in reversed(range(SSD_BLOCK)):
            backward_state(ci, prepared[ci])

    @pl.when(phase == 1)
    def _():
        prepared = [forward_prepare(ci) for ci in range(SSD_BLOCK)]
        for p in prepared:
            forward_mix(p)
        for ci in range(SSD_BLOCK):
            forward_state(ci, prepared[ci])


def _ssd(xbc, z, dt_raw, dt_bias, a_log, d_skip, norm_w, bd, n_lat):
    bsz, t, _ = z.shape
    rows = SSD_BLOCK * SSD_CHUNK
    n_lat_blocks = n_lat // rows
    n_blocks = t // rows
    assert n_blocks == n_lat_blocks + 1 and n_lat_blocks * rows == n_lat

    def block_of(p, s):
        return jnp.where(p == 0, n_lat_blocks - s, jnp.where(s < 1, n_lat_blocks, s - 1))

    def out_block(p, s):
        return jnp.where(p == 0, n_lat_blocks, block_of(p, s))

    const = lambda shape: pl.BlockSpec(shape, lambda b, p, s: tuple(0 for _ in shape))
    return pl.pallas_call(
        functools.partial(_ssd_kernel, n_lat_blocks=n_lat_blocks),
        grid=(bsz, 2, n_blocks),
        in_specs=[
            pl.BlockSpec((1, rows, SSD_CONV_CH), lambda b, p, s: (b, block_of(p, s), 0)),
            pl.BlockSpec((1, rows, SSD_W), lambda b, p, s: (b, block_of(p, s), 0)),
            pl.BlockSpec((1, rows, LANES), lambda b, p, s: (b, block_of(p, s), 0)),
            const((1, LANES)),
            const((1, LANES)),
            const((1, SSD_W)),
            const((1, SSD_W)),
            const((SSD_W, SSD_W)),
        ],
        out_specs=pl.BlockSpec((1, rows, SSD_W), lambda b, p, s: (b, out_block(p, s), 0)),
        out_shape=jax.ShapeDtypeStruct((bsz, t, SSD_W), BF16),
        scratch_shapes=[
            pltpu.VMEM((SSD_HEADS, SSD_STATE, SSD_HEAD_DIM), F32),
            pltpu.VMEM((SSD_HEADS, SSD_STATE, SSD_HEAD_DIM), F32),
            pltpu.VMEM((n_blocks * SSD_BLOCK, SSD_HEADS, SSD_STATE, SSD_HEAD_DIM), F32),
        ],
        compiler_params=_params("arbitrary", "arbitrary", "arbitrary"),
        name="ssd",
    )(xbc, z, dt_raw, dt_bias, a_log, d_skip, norm_w, bd)


def _hyfilt_kernel(feat_ref, win_ref, w1_ref, b1_ref, fr_ref, w2_ref, b2_ref, w3_ref, o_ref):
    hp = lax.Precision.HIGHEST
    fr = fr_ref[...]
    h1 = jnp.sin(fr * (jnp.dot(feat_ref[...], w1_ref[...], precision=hp, preferred_element_type=F32) + b1_ref[...]))
    h2 = jnp.sin(fr * (jnp.dot(h1, w2_ref[...], precision=hp, preferred_element_type=F32) + b2_ref[...]))
    h = _dot_x3(h2, w3_ref[...])
    win = win_ref[...]
    first_tile = pl.program_id(0) == 0
    row = lax.broadcasted_iota(jnp.int32, win.shape, 0)
    for order in range(HY_ORDER):
        for direction in range(2):
            c0 = (direction * HY_ORDER + order) * HY_W
            f = h[:, c0:c0 + HY_W] * win
            if direction == 1:
                f = jnp.where(first_tile & (row == 0), 0.0, f)
            o_ref[order * 2 + direction] = f


def _hy_positional(length):
    n = np.arange(length, dtype=np.float64)
    t = n / max(length - 1, 1)
    bands = np.linspace(1e-4, HY_BANDS - 1, HY_BANDS)
    wpos = (2 * math.pi / length) * n
    feats = np.concatenate([t[:, None], np.cos(wpos[:, None] * bands), -np.sin(wpos[:, None] * bands)], axis=-1)
    feats = np.pad(feats, ((0, 0), (0, LANES - HY_POS_DIM)))
    deltas = np.abs(np.linspace(math.log(HY_TARGET) / HY_SLOW_DECAY, math.log(HY_TARGET) / HY_FAST_DECAY, HY_W))
    window = np.exp(-t[:, None] * deltas)
    return jnp.asarray(feats, F32), jnp.asarray(window, F32)


def _hyena_filters(length, w1, b1, freq, w2, b2, w3):
    feats, window = _hy_positional(length)
    tl = min(length, 512)
    hid = HY_FILTER_HID
    w1p = jnp.pad(w1, ((0, LANES - HY_POS_DIM), (0, 0)))
    const = lambda shape: pl.BlockSpec(shape, lambda i: tuple(0 for _ in shape))
    return pl.pallas_call(
        _hyfilt_kernel,
        grid=(length // tl,),
        in_specs=[
            pl.BlockSpec((tl, LANES), lambda i: (i, 0)),
            pl.BlockSpec((tl, HY_W), lambda i: (i, 0)),
            const((LANES, hid)), const((1, hid)), const((1, hid)),
            const((hid, hid)), const((1, hid)), const((hid, 2 * HY_ORDER * HY_W)),
        ],
        out_specs=pl.BlockSpec((2 * HY_ORDER, tl, HY_W), lambda i: (0, i, 0)),
        out_shape=jax.ShapeDtypeStruct((2 * HY_ORDER, length, HY_W), F32),
        compiler_params=_params("arbitrary"),
        name="hyena_filters",
    )(feats, window, w1p, b1.reshape(1, hid), freq.reshape(1, hid), w2, b2.reshape(1, hid), w3)


def _dft_tables(length):
    n = 2 * length
    n2 = HY_N2
    n1 = n // n2
    half = n1 // 2
    k1 = np.arange(half, dtype=np.float64) + 0.5
    idx = (n2 * np.arange(half)[None, None, :] + np.arange(n2)[:, None, None])
    ang = 2 * np.pi * k1[None, :, None] * idx / n
    fa = np.concatenate([np.cos(ang), -np.sin(ang)], axis=1)
    kk = np.arange(n2, dtype=np.float64)
    angb = 2 * np.pi * kk[:, None] * kk[None, :] / n2
    cr, sr = np.cos(angb), np.sin(angb)
    fb = np.block([[cr, sr], [-sr, cr]])
    fbi = np.block([[cr, -sr], [sr, cr]])
    idxo = (n2 * np.arange(half)[None, :, None] + np.arange(n2)[:, None, None])
    ango = 2 * np.pi * k1[None, None, :] * idxo / n
    ga = np.concatenate([np.cos(ango), -np.sin(ango)], axis=2) * (2.0 / n)
    f32 = lambda a: jnp.asarray(a, F32).astype(BF16)
    return f32(fa), f32(fb), f32(fbi), f32(ga)


def _seq_pitch(n2n):
    return n2n + SUBLANES


def _to_pitched(dst_ref, val, n2n):
    pitch = _seq_pitch(n2n)
    for i in range(val.shape[0] // n2n):
        dst_ref[i * pitch:i * pitch + n2n, :] = val[i * n2n:(i + 1) * n2n, :]


def _hy_forward_a(src_ref, a_ref, fa_ref):
    n2n, two_n1, half = fa_ref.shape
    zp, ap = _seq_pitch(n2n), _seq_pitch(two_n1)

    def body(n2, carry):
        zs = src_ref[pl.ds(n2, half, stride=zp), :].astype(BF16)
        a_ref[pl.ds(pl.multiple_of(n2 * ap, SUBLANES), two_n1), :] = _dot(fa_ref[n2], zs)
        return carry

    lax.fori_loop(0, n2n, body, 0, unroll=HY_UNROLL)


def _hy_kspec_kernel(hf_ref, hb_ref, fa_ref, fb_ref, o_ref, af_ref, ab_ref, hfp_ref, hbp_ref):
    n2n, two_n1, _ = fa_ref.shape
    n1 = two_n1 // 2
    ap = _seq_pitch(two_n1)
    _to_pitched(hfp_ref, hf_ref[0], n2n)
    _to_pitched(hbp_ref, hb_ref[0], n2n)
    _hy_forward_a(hfp_ref, af_ref, fa_ref)
    _hy_forward_a(hbp_ref, ab_ref, fa_ref)
    fb = fb_ref[...]

    def body(k1, carry):
        def spectrum(a_ref):
            ar = a_ref[pl.ds(k1, n2n, stride=ap), :]
            ai = a_ref[pl.ds(n1 + k1, n2n, stride=ap), :]
            return _dot(fb, jnp.concatenate([ar, ai], axis=0).astype(BF16))

        xf, xb = spectrum(af_ref), spectrum(ab_ref)
        o_ref[0, k1] = jnp.concatenate([xf[:n2n] + xb[:n2n], xf[n2n:] - xb[n2n:]], axis=0)
        return carry

    lax.fori_loop(0, n1, body, 0, unroll=HY_UNROLL)


def _hy_kspec(filt, tables):
    fa, fb, _, _ = tables
    _, length, _ = filt.shape
    n2n, two_n1, _ = fa.shape
    n1 = two_n1 // 2
    nh = HY_W // LANES
    return pl.pallas_call(
        _hy_kspec_kernel,
        grid=(HY_ORDER, nh),
        in_specs=[
            pl.BlockSpec((1, length, LANES), lambda o, h: (2 * o, 0, h)),
            pl.BlockSpec((1, length, LANES), lambda o, h: (2 * o + 1, 0, h)),
            pl.BlockSpec(fa.shape, lambda o, h: (0, 0, 0), pipeline_mode=pl.Buffered(1)),
            pl.BlockSpec(fb.shape, lambda o, h: (0, 0)),
        ],
        out_specs=pl.BlockSpec((1, n1, 2 * n2n, LANES), lambda o, h: (o, 0, 0, h)),
        out_shape=jax.ShapeDtypeStruct((HY_ORDER, n1, 2 * n2n, HY_W), F32),
        scratch_shapes=[pltpu.VMEM((n2n * _seq_pitch(two_n1), LANES), F32),
                        pltpu.VMEM((n2n * _seq_pitch(two_n1), LANES), F32),
                        pltpu.VMEM((length // n2n * _seq_pitch(n2n), LANES), F32),
                        pltpu.VMEM((length // n2n * _seq_pitch(n2n), LANES), F32)],
        compiler_params=_params("arbitrary", "arbitrary"),
        name="hyena_kspec",
    )(filt, filt, fa, fb)


def _hy_conv_kernel(z_ref, g_ref, fa_ref, fb_ref, fbi_ref, ga_ref, k_ref, bias_ref,
                    o_ref, a_ref, c_ref, zc_ref, gc_ref):
    n2n, two_n1, half = fa_ref.shape
    n1 = two_n1 // 2
    zp, ap = _seq_pitch(n2n), _seq_pitch(two_n1)
    _to_pitched(zc_ref, z_ref[0], n2n)
    _to_pitched(gc_ref, g_ref[0], n2n)
    _hy_forward_a(zc_ref, a_ref, fa_ref)
    fb = fb_ref[...]
    fbi = fbi_ref[...]

    def body_b(grp, carry):
        k1s = [grp * HY_UNROLL + i for i in range(HY_UNROLL)]
        sl = [(pl.ds(k1, n2n, stride=ap), pl.ds(n1 + k1, n2n, stride=ap)) for k1 in k1s]
        xs = [_dot(fb, jnp.concatenate([a_ref[re, :], a_ref[im, :]], axis=0).astype(BF16)) for re, im in sl]
        ys = []
        for k1, x in zip(k1s, xs):
            kk = k_ref[0, k1]
            xr, xi = x[:n2n], x[n2n:]
            kr, ki = kk[:n2n], kk[n2n:]
            ys.append(jnp.concatenate([xr * kr - xi * ki, xr * ki + xi * kr], axis=0).astype(BF16))
        for (re, im), y in zip(sl, ys):
            c = _dot(fbi, y)
            c_ref[re, :] = c[:n2n]
            c_ref[im, :] = c[n2n:]
        return carry

    lax.fori_loop(0, n1 // HY_UNROLL, body_b, 0)
    bias = bias_ref[0]

    def body_c(m2, carry):
        rows = pl.ds(m2, half, stride=zp)
        c = c_ref[pl.ds(pl.multiple_of(m2 * ap, SUBLANES), two_n1), :].astype(BF16)
        y = _dot(ga_ref[m2], c)
        a_ref[rows, :] = gc_ref[rows, :] * (y + zc_ref[rows, :] * bias)
        return carry

    lax.fori_loop(0, n2n, body_c, 0, unroll=HY_UNROLL)
    for i in range(half):
        o_ref[0, i * n2n:(i + 1) * n2n, :] = a_ref[i * zp:i * zp + n2n, :]


def _hy_conv(z_arr, z_blk, g_arr, g_blk, kspec, order, bias3, tables, n_lat):
    fa, fb, fbi, ga = tables
    bsz = z_arr.shape[0]
    n2n, two_n1, half = fa.shape
    n1 = two_n1 // 2
    nh = HY_W // LANES
    assert n1 % HY_UNROLL == 0 and n2n % HY_UNROLL == 0
    resident = lambda shape: pl.BlockSpec(shape, lambda h, b_: tuple(0 for _ in shape), pipeline_mode=pl.Buffered(1))
    return pl.pallas_call(
        _hy_conv_kernel,
        grid=(nh, bsz),
        in_specs=[
            pl.BlockSpec((1, n_lat, LANES), lambda h, b_: (b_, 0, z_blk + h)),
            pl.BlockSpec((1, n_lat, LANES), lambda h, b_: (b_, 0, g_blk + h)),
            resident(fa.shape), resident(fb.shape), resident(fbi.shape), resident(ga.shape),
            pl.BlockSpec((1, n1, 2 * n2n, LANES), lambda h, b_: (order, 0, 0, h), pipeline_mode=pl.Buffered(1)),
            pl.BlockSpec((1, 1, LANES), lambda h, b_: (order, 0, h)),
        ],
        out_specs=pl.BlockSpec((1, n_lat, LANES), lambda h, b_: (b_, 0, h)),
        out_shape=jax.ShapeDtypeStruct((bsz, n_lat, HY_W), F32),
        scratch_shapes=[pltpu.VMEM((n2n * _seq_pitch(two_n1), LANES), F32),
                        pltpu.VMEM((n2n * _seq_pitch(two_n1), LANES), F32),
                        pltpu.VMEM((half * _seq_pitch(n2n), LANES), F32),
                        pltpu.VMEM((half * _seq_pitch(n2n), LANES), F32)],
        compiler_params=_params("arbitrary", "arbitrary"),
        name="hyena_conv",
    )(z_arr, g_arr, fa, fb, fbi, ga, kspec, bias3)


def _hyena_ctx_kernel(u_ref, filt_ref, bias_ref, fd_ref, gd_ref, o_ref, *, n_ctx):
    fd = fd_ref[...]
    gd = gd_ref[...]
    nn = 2 * n_ctx
    u = u_ref[0]
    v, x1, x2 = u[:, :HY_W], u[:, HY_W:2 * HY_W], u[:, 2 * HY_W:]

    def conv(zin, order):
        hf = _dot(fd, filt_ref[2 * order].astype(BF16))
        hb = _dot(fd, filt_ref[2 * order + 1].astype(BF16))
        kr, ki = hf[:nn] + hb[:nn], hf[nn:] - hb[nn:]
        zz = _dot(fd, zin.astype(BF16))
        zr, zi = zz[:nn], zz[nn:]
        y = jnp.concatenate([zr * kr - zi * ki, zr * ki + zi * kr], axis=0).astype(BF16)
        return _dot(gd, y) + zin * bias_ref[order]

    y1 = x1 * conv(v, 0)
    o_ref[0] = x2 * conv(y1, 1)


def _hyena_ctx(hy, filt, bias, n_lat, n_ctx):
    bsz = hy.shape[0]
    nn = 2 * n_ctx
    k = np.arange(nn, dtype=np.float64)
    t = np.arange(n_ctx, dtype=np.float64)
    ang = 2 * np.pi * k[:, None] * t[None, :] / nn
    fd = jnp.asarray(np.concatenate([np.cos(ang), -np.sin(ang)], axis=0), F32).astype(BF16)
    gd = jnp.asarray(np.concatenate([np.cos(ang.T), -np.sin(ang.T)], axis=1) / nn, F32).astype(BF16)
    blk = n_lat // n_ctx
    assert blk * n_ctx == n_lat
    return pl.pallas_call(
        functools.partial(_hyena_ctx_kernel, n_ctx=n_ctx),
        grid=(bsz,),
        in_specs=[
            pl.BlockSpec((1, n_ctx, 3 * HY_W), lambda b: (b, blk, 0)),
            pl.BlockSpec((2 * HY_ORDER, n_ctx, HY_W), lambda b: (0, 0, 0)),
            pl.BlockSpec((HY_ORDER, 1, HY_W), lambda b: (0, 0, 0)),
            pl.BlockSpec((2 * nn, n_ctx), lambda b: (0, 0)),
            pl.BlockSpec((n_ctx, 2 * nn), lambda b: (0, 0)),
        ],
        out_specs=pl.BlockSpec((1, n_ctx, HY_W), lambda b: (b, 0, 0)),
        out_shape=jax.ShapeDtypeStruct((bsz, n_ctx, HY_W), F32),
        compiler_params=_params("arbitrary"),
        name="hyena_ctx",
    )(hy, filt, bias, fd, gd)


def _hyena_latent(hy, filt, bias, n_lat, tables):
    kspec = _hy_kspec(filt, tables)
    bias3 = bias.reshape(HY_ORDER, 1, HY_W)
    nb = HY_W // LANES
    y1 = _hy_conv(hy, 0, hy, nb, kspec, 0, bias3, tables, n_lat)
    return _hy_conv(y1, 0, hy, 2 * nb, kspec, 1, bias3, tables, n_lat)


def _outproj_kernel(x_ref, att_ref, ssd_ref, hy_ref, gt_ref, wa_ref, ws_ref, wh_ref, o_ref):
    mix = _dot(att_ref[0], wa_ref[0]) + _dot(ssd_ref[0], ws_ref[0]) + _dot(hy_ref[0].astype(BF16), wh_ref[0])
    o_ref[0] = x_ref[0] + gt_ref[0] * mix


def _outproj(x, att, ssd, ssd_row0, hy, mods, mod_row, w_out, layer, tm):
    bsz, n, _ = att.shape
    d = x.shape[-1]
    ssd_blk0 = ssd_row0 // tm
    assert ssd_blk0 * tm == ssd_row0 and n % tm == 0
    tok = lambda w: pl.BlockSpec((1, tm, w), lambda j, b: (b, j, 0))
    return pl.pallas_call(
        _outproj_kernel,
        grid=(n // tm, bsz),
        in_specs=[
            tok(d), tok(ATT_W),
            pl.BlockSpec((1, tm, SSD_W), lambda j, b: (b, ssd_blk0 + j, 0)),
            tok(HY_W),
            pl.BlockSpec((1, 1, d), lambda j, b: (mod_row(b), 0, 2)),
            pl.BlockSpec((1, ATT_W, d), lambda j, b: (layer, 0, 0)),
            pl.BlockSpec((1, SSD_W, d), lambda j, b: (layer, ATT_W // SSD_W, 0)),
            pl.BlockSpec((1, HY_W, d), lambda j, b: (layer, (ATT_W + SSD_W) // HY_W, 0)),
        ],
        out_specs=tok(d),
        out_shape=jax.ShapeDtypeStruct((bsz, n, d), F32),
        compiler_params=_params("arbitrary", "arbitrary"),
        name="outproj",
    )(x, att, ssd, hy, mods, w_out, w_out, w_out)


def _route(logits_t, bias_col):
    scores = jax.nn.sigmoid(logits_t)
    sel = scores + bias_col
    neg_inf = jnp.float32(-jnp.inf)
    rows = [sel[e:e + 1, :] for e in range(N_EXPERTS)]
    grp = []
    for g in range(N_GROUPS):
        r = rows[g * EXPERTS_PER_GROUP:(g + 1) * EXPERTS_PER_GROUP]
        top = functools.reduce(jnp.maximum, r)
        taken = None
        rest = []
        for ri in r:
            is_top = (ri == top) if taken is None else (ri == top) & jnp.logical_not(taken)
            rest.append(jnp.where(is_top, neg_inf, ri))
            taken = is_top if taken is None else taken | is_top
        grp.append(top + functools.reduce(jnp.maximum, rest))
    best = jnp.zeros(grp[0].shape, jnp.int32)
    cur = grp[0]
    for g in range(1, N_GROUPS):
        upd = grp[g] > cur
        best = jnp.where(upd, g, best)
        cur = jnp.where(upd, grp[g], cur)
    picked = []
    for e in range(N_EXPERTS):
        g, i = divmod(e, EXPERTS_PER_GROUP)
        rank = jnp.zeros(best.shape, jnp.int32)
        for j in range(EXPERTS_PER_GROUP):
            if j == i:
                continue
            other = rows[g * EXPERTS_PER_GROUP + j]
            ahead = (other > rows[e]) | ((other == rows[e]) & (j < i))
            rank = rank + ahead.astype(jnp.int32)
        keep = (best == g) & (rank < 2)
        picked.append(jnp.where(keep, scores[e:e + 1, :], 0.0))
    total = functools.reduce(lambda u, w: u + w, picked)
    return jnp.concatenate(picked, axis=0) / total, best


def _ffn_input(x_ref, sh_ref, sc_ref, g_ref):
    x = x_ref[...].reshape(MOE_TILE, D_MODEL)
    ms = jnp.mean(x * x, axis=-1, keepdims=True)
    t = x * lax.rsqrt(ms + EPS) * g_ref[...]
    return x, t * (1.0 + sc_ref[0]) + sh_ref[0]


def _route_kernel(x_ref, sh_ref, sc_ref, g_ref, wr_ref, rb_ref, rt_ref, srow_ref, tbl_ref):
    tm = MOE_TILE
    _, t = _ffn_input(x_ref, sh_ref, sc_ref, g_ref)
    logits = _dot_x3(t, wr_ref[...])
    comb_t, best = _route(logits.T[:N_EXPERTS, :], rb_ref[...])

    member = [jnp.where(best == g, 1.0, 0.0) for g in range(N_GROUPS)]
    comb4 = functools.reduce(
        lambda u, w: u + w,
        [member[g] * comb_t[g * EXPERTS_PER_GROUP:(g + 1) * EXPERTS_PER_GROUP, :] for g in range(N_GROUPS)])
    masks = jnp.concatenate(member + [jnp.zeros((8 - N_GROUPS, tm), F32)], axis=0)
    earlier = jnp.where(lax.broadcasted_iota(jnp.int32, (tm, tm), 0) < lax.broadcasted_iota(jnp.int32, (tm, tm), 1),
                        1.0, 0.0).astype(BF16)
    ranks = _dot(masks.astype(BF16), earlier)
    cnt = jnp.sum(masks, axis=1, keepdims=True)
    padded = jnp.ceil(cnt * (1.0 / MOE_CHUNK)) * MOE_CHUNK
    lane = lax.broadcasted_iota(jnp.int32, (1, LANES), 1).astype(F32) * MOE_CHUNK
    start = jnp.zeros((1, 1), F32)
    slot = jnp.zeros((1, tm), F32)
    gid = jnp.zeros((1, LANES), jnp.int32)
    for g in range(N_GROUPS):
        slot = slot + member[g] * (start + ranks[g:g + 1, :])
        start = start + padded[g:g + 1, :]
        gid = gid + jnp.where(lane >= start, 1, 0)
    tbl_ref[0] = gid
    srow_ref[0] = slot.astype(jnp.int32)
    rt_t = jnp.concatenate([comb4, slot, jnp.zeros((LANES - EXPERTS_PER_GROUP - 1, tm), F32)], axis=0)
    rt_ref[...] = rt_t.T


def _experts_kernel(tbl_ref, x_ref, sh_ref, sc_ref, gt_ref, g_ref, rt_ref, srow_ref, wg_ref, wu_ref, wd_ref, gf_ref,
                    *rest, final, tile_of, n_grid):
    o_ref, xp_ref, yp_ref = rest[-3:]
    tm, ch = MOE_TILE, MOE_CHUNK
    n_slots = MOE_CHUNKS * ch
    tile = tile_of(*[pl.program_id(a) for a in range(n_grid)])
    x, t = _ffn_input(x_ref, sh_ref, sc_ref, g_ref)
    rt = rt_ref[...]
    slot_col = rt[:, EXPERTS_PER_GROUP:EXPERTS_PER_GROUP + 1].astype(jnp.int32)
    gather = jnp.where(lax.broadcasted_iota(jnp.int32, (n_slots, tm), 0) == srow_ref[0], 1.0, 0.0).astype(BF16)
    scatter = jnp.where(lax.broadcasted_iota(jnp.int32, (tm, n_slots), 1) == slot_col, 1.0, 0.0).astype(BF16)
    xp_ref[...] = _dot(gather, t.astype(BF16)).astype(BF16)
    w_slot = _dot3_right(gather, rt)

    def run_chunk(rows, grp):
        xc = xp_ref[rows, :]
        e0 = grp * EXPERTS_PER_GROUP
        gu_next = _dot(xc, wg_ref[0, e0]), _dot(xc, wu_ref[0, e0])
        acc = jnp.zeros((ch, D_MODEL), F32)
        for j in range(EXPERTS_PER_GROUP):
            gate, up = gu_next
            if j + 1 < EXPERTS_PER_GROUP:
                gu_next = _dot(xc, wg_ref[0, e0 + j + 1]), _dot(xc, wu_ref[0, e0 + j + 1])
            hid = _silu(gate) * up * w_slot[rows, j:j + 1]
            acc = acc + _dot(hid.astype(BF16), wd_ref[0, e0 + j])
        yp_ref[rows, :] = acc.astype(BF16)

    for c in range(MOE_CHUNKS):
        rows = slice(c * ch, (c + 1) * ch)
        grp = tbl_ref[tile, c]
        pl.when(grp < N_GROUPS)(functools.partial(run_chunk, rows, grp))

        @pl.when(grp >= N_GROUPS)
        def _():
            yp_ref[rows, :] = jnp.zeros((ch, D_MODEL), BF16)

    y = x + gt_ref[0] * _dot(scatter, yp_ref[...])
    if final:
        y = y * lax.rsqrt(jnp.mean(y * y, axis=-1, keepdims=True) + EPS) * gf_ref[...]
    o_ref[...] = y.reshape(o_ref.shape)


def _moe_tiles(x, mods, mod_row, g_ffn, w_router, router_bias, wgu, wd, layer, g_final, final, out_rows,
               grid, x_block, x_index, out_index, tile_of, n_tiles):
    bsz, _, d = x.shape
    tm = MOE_TILE
    ng = len(grid)
    const = lambda shape: pl.BlockSpec(shape, lambda *a: tuple(0 for _ in shape))
    mod = lambda col: pl.BlockSpec((1, 1, d), lambda *a: (mod_row(*a[:ng]), 0, col))
    xspec = pl.BlockSpec(x_block, lambda *a: x_index(*a[:ng]))
    rt, srow, tbl = pl.pallas_call(
        _route_kernel,
        grid=grid,
        in_specs=[xspec, mod(3), mod(4), const((1, d)), const((d, LANES)), const((N_EXPERTS, 1))],
        out_specs=[
            pl.BlockSpec((tm, LANES), lambda *a: (tile_of(*a), 0)),
            pl.BlockSpec((1, 1, tm), lambda *a: (tile_of(*a), 0, 0)),
            pl.BlockSpec((1, 1, LANES), lambda *a: (tile_of(*a), 0, 0)),
        ],
        out_shape=[
            jax.ShapeDtypeStruct((n_tiles * tm, LANES), F32),
            jax.ShapeDtypeStruct((n_tiles, 1, tm), jnp.int32),
            jax.ShapeDtypeStruct((n_tiles, 1, LANES), jnp.int32),
        ],
        compiler_params=_params(*["arbitrary"] * ng),
        name="moe_route",
    )(x, mods, mods, g_ffn, w_router, router_bias)

    resident = lambda shape: pl.BlockSpec(shape, lambda *a: (layer,) + tuple(0 for _ in shape[1:]),
                                          pipeline_mode=pl.Buffered(1))
    in_specs = [
        xspec, mod(3), mod(4), mod(5), const((1, d)),
        pl.BlockSpec((tm, LANES), lambda *a: (tile_of(*a[:ng]), 0)),
        pl.BlockSpec((1, 1, tm), lambda *a: (tile_of(*a[:ng]), 0, 0)),
        resident((1, N_EXPERTS, d, D_FF)),
        resident((1, N_EXPERTS, d, D_FF)),
        resident((1, N_EXPERTS, D_FF, d)),
        const((1, d)),
    ]
    args = [tbl.reshape(n_tiles, LANES), x, mods, mods, mods, g_ffn, rt, srow, *wgu, wd, g_final]
    return pl.pallas_call(
        functools.partial(_experts_kernel, final=final, tile_of=tile_of, n_grid=ng),
        grid_spec=pltpu.PrefetchScalarGridSpec(
            num_scalar_prefetch=1,
            grid=grid,
            in_specs=in_specs,
            out_specs=pl.BlockSpec(x_block, lambda *a: out_index(*a[:ng])),
            scratch_shapes=[pltpu.VMEM((MOE_CHUNKS * MOE_CHUNK, d), BF16), pltpu.VMEM((MOE_CHUNKS * MOE_CHUNK, d), BF16)],
        ),
        out_shape=jax.ShapeDtypeStruct((bsz, out_rows, d), F32),
        compiler_params=_params(*["arbitrary"] * ng),
        name="moe_experts",
    )(*args)


def _moe(x_lat, x_ctx, mods, g_ffn, w_router, router_bias, wgu, wd, layer, g_final, final):
    bsz, n_lat, d = x_lat.shape
    tm = MOE_TILE
    per_b = n_lat // tm
    common = (g_ffn, w_router, router_bias, wgu, wd, layer, g_final, final)
    lat_index = lambda b, j: (b, j, 0)
    out_lat = _moe_tiles(x_lat, mods, lambda b, j: b, *common, n_lat,
                         (bsz, per_b), (1, tm, d), lat_index, lat_index, lambda b, j: b * per_b + j, bsz * per_b)
    if x_ctx is None:
        return out_lat, None
    n_ctx = x_ctx.shape[1]
    nb = tm // n_ctx
    ctx_index = lambda i: (i, 0, 0)
    out_ctx = _moe_tiles(x_ctx, mods, lambda i: bsz, *common, n_ctx,
                         (bsz // nb,), (nb, n_ctx, d), ctx_index, ctx_index, lambda i: i, bsz // nb)
    return out_lat, out_ctx


def _block_ones(width, block):
    idx = np.arange(width) // block
    return jnp.asarray(idx[:, None] == idx[None, :], F32).astype(BF16)


def _rope_tables(n_lat, n_ctx):
    rows = n_lat // GRID_W
    row = np.repeat(np.arange(rows), GRID_W).astype(np.float64)
    col = np.tile(np.arange(GRID_W), rows).astype(np.float64)
    inv = ROPE_THETA ** (-np.arange(0, ROPE_AXIS_DIM, 2, dtype=np.float64) / ROPE_AXIS_DIM)
    ang = np.concatenate([row[:, None] * inv, col[:, None] * inv], axis=-1)
    ang = np.concatenate([ang, np.zeros((n_ctx, ang.shape[1]))], axis=0)
    cos = np.concatenate([np.cos(ang), np.cos(ang)], axis=-1)
    sin = np.concatenate([-np.sin(ang), np.sin(ang)], axis=-1)
    scale = Q_SCALE
    cs = np.tile(cos, (1, ATT_HEADS)) * scale
    sn = np.tile(sin, (1, ATT_HEADS)) * scale
    return jnp.asarray(cs, F32), jnp.asarray(sn, F32)


def kernel(x, c, ctx, c_ctx, w_mod, b_mod, g_mix, g_ffn, w_in, q_norm, k_norm, ssd_conv_w, ssd_conv_b,
           ssd_dt_bias, ssd_a_log, ssd_d, ssd_norm, hy_conv_w, hy_conv_b, hy_w1, hy_b1, hy_freq, hy_w2, hy_b2,
           hy_w3, hy_bias, w_out, w_router, router_bias, w_gate, w_up, w_down, g_final):
    bsz, n_lat, d = x.shape
    n_ctx = ctx.shape[1]
    depth = w_mod.shape[0]
    t = n_lat + n_ctx
    tm = TOKEN_TILE
    n_lat_tiles = n_lat // tm
    n_tiles = t // tm
    assert n_ctx == tm and n_lat % (HY_N2 * 8) == 0 and bsz < MOD_ROWS

    cvec = jnp.concatenate([c, c_ctx[None], jnp.zeros((MOD_ROWS - bsz - 1, d), F32)], axis=0)
    mods_all = _adaln(cvec, w_mod, b_mod)

    cs, sn = _rope_tables(n_lat, n_ctx)
    bd_head = _block_ones(ATT_W, HEAD_DIM)
    bd_ssd = _block_ones(SSD_W, SSD_W // SSD_GROUPS)
    tables = _dft_tables(n_lat)
    pad_row = lambda v: jnp.pad(v.reshape(1, -1), ((0, 0), (0, LANES - v.size)))
    w_out_b = w_out.astype(BF16)
    wgu_b = (w_gate.astype(BF16), w_up.astype(BF16))
    wd_b = w_down.astype(BF16)

    x_lat, x_ctx, ctx_blk = x, ctx, 0
    for i in range(depth):
        last = i == depth - 1
        mods = mods_all[i].reshape(MOD_ROWS, 1, 6 * d)
        q, kt, v, z, xbc, dt_raw, hy = _inproj(
            x_lat, x_ctx, ctx_blk, mods, g_mix[i].reshape(1, d), w_in, i,
            jnp.tile(q_norm[i], ATT_HEADS).reshape(1, ATT_W), jnp.tile(k_norm[i], ATT_KV_HEADS).reshape(1, KV_W),
            cs, sn, bd_head, jnp.concatenate([ssd_conv_w[i], hy_conv_w[i]], axis=-1),
            jnp.concatenate([ssd_conv_b[i], hy_conv_b[i]]).reshape(1, -1), n_lat_tiles)

        att = _attention(q, kt, v, 0, n_lat, 0, t, ATT_TILE)

        ssd = _ssd(xbc, z, dt_raw, pad_row(ssd_dt_bias[i]), pad_row(ssd_a_log[i]),
                   jnp.repeat(ssd_d[i], SSD_HEAD_DIM).reshape(1, SSD_W), ssd_norm[i].reshape(1, SSD_W),
                   bd_ssd, n_lat)

        hyp = (hy_w1[i], hy_b1[i], hy_freq[i], hy_w2[i], hy_b2[i], hy_w3[i])
        hy_l = _hyena_latent(hy, _hyena_filters(n_lat, *hyp), hy_bias[i], n_lat, tables)

        mid_lat = _outproj(x_lat, att, ssd, 0, hy_l, mods, lambda b: b, w_out_b, i, OUT_TILE)
        mid_ctx = None
        if not last:
            att_c = _attention(q, kt, v, n_lat, n_ctx, n_lat, n_ctx, n_ctx)
            hy_c = _hyena_ctx(hy, _hyena_filters(n_ctx, *hyp), hy_bias[i].reshape(HY_ORDER, 1, HY_W), n_lat, n_ctx)
            mid_ctx = _outproj(x_ctx, att_c, ssd, n_lat, hy_c, mods, lambda b: bsz, w_out_b, i, n_ctx)

        x_lat, x_ctx = _moe(mid_lat, mid_ctx, mods, g_ffn[i].reshape(1, d),
                            jnp.pad(w_router, ((0, 0), (0, LANES - N_EXPERTS))), router_bias.reshape(N_EXPERTS, 1),
                            wgu_b, wd_b, i, g_final.reshape(1, d), last)
    return x_lat
```
